```python
import jax, jax.numpy as jnp
from jax import lax
import numpy as np

D_MODEL = 1024
BATCH = 16
SEQ = 256
DEPTH = 2
DEC_BATCH = 8
DEC_SEQ = 1024
PAST_LEN = 512

GRID_W = 64
N_BRANCH = 4
BR_W = D_MODEL // N_BRANCH
MLA_HEADS = 4
MLA_NOPE = 64
MLA_ROPE = 32
MLA_QK = MLA_NOPE + MLA_ROPE
MLA_V = BR_W // MLA_HEADS
MLA_Q_LORA = 256
MLA_KV_LORA = 128
SC_W = BR_W
SC_K = 3
NA_HEADS = 4
NA_HD = BR_W // NA_HEADS
NA_W = NA_HEADS * NA_HD
NA_WIN_R = 8
NA_WIN_C = 16
CF_W = BR_W
CF_K = 31
D_FF = 3584
N_EXPERTS = 8
TOP_K = 2
D_FF_E = 3584
N_DENSE = (DEPTH + 1) // 2
N_MOE = DEPTH // 2
ROPE_THETA = 10000.0
EPS = 1e-6
Q_BLOCK = 128
DENSE_SWEEP_KEYS = 2048
_S1 = MLA_Q_LORA
_S2 = _S1 + MLA_KV_LORA + MLA_ROPE
_S3 = _S2 + 3 * SC_W
_S4 = _S3 + 3 * NA_W
_S5 = _S4 + 2 * CF_W
IN_COLS = _S5 + N_BRANCH * D_MODEL
IN_SPLITS = (_S1, _S2, _S3, _S4, _S5)

kernel_name = 'hybrid_diffusion_mla_natten_conv_step'


def rmsnorm(x, g):
    xf = x.astype(jnp.float32)
    y = xf * lax.rsqrt(jnp.mean(xf * xf, axis=-1, keepdims=True) + EPS)
    return (y * g.astype(jnp.float32)).astype(x.dtype)


def layernorm(x, g, b):
    xf = x.astype(jnp.float32)
    mu = jnp.mean(xf, axis=-1, keepdims=True)
    var = jnp.mean(jnp.square(xf - mu), axis=-1, keepdims=True)
    y = (xf - mu) * lax.rsqrt(var + EPS)
    return (y * g.astype(jnp.float32) + b.astype(jnp.float32)).astype(x.dtype)


def dwconv(x, w):
    K = w.shape[0]
    return lax.conv_general_dilated(x, w[:, None, :].astype(x.dtype), window_strides=(1,),
                                    padding=[(K // 2, K // 2)], dimension_numbers=('NWC', 'WIO', 'NWC'),
                                    feature_group_count=x.shape[-1])


def axial_rope(x):
    S = x.shape[1]
    half = MLA_ROPE // 2
    nf = half // 2
    t = jnp.arange(S)
    inv = ROPE_THETA ** (-jnp.arange(nf, dtype=jnp.float32) / nf)

    def rot(xa, pos):
        ang = pos.astype(jnp.float32)[:, None] * inv[None, :]
        cos = jnp.cos(ang)[None, :, None, :].astype(x.dtype)
        sin = jnp.sin(ang)[None, :, None, :].astype(x.dtype)
        x1, x2 = xa[..., :nf], xa[..., nf:]
        return jnp.concatenate([x1 * cos - x2 * sin, x1 * sin + x2 * cos], axis=-1)

    return jnp.concatenate([rot(x[..., :half], t // GRID_W), rot(x[..., half:], t % GRID_W)], axis=-1)


def rope_tail(x):
    return jnp.concatenate([x[..., :MLA_NOPE], axial_rope(x[..., MLA_NOPE:])], axis=-1)


def _attn(q, k, v, scale):
    s = jnp.einsum('bqhd,bkhd->bhqk', q, k, preferred_element_type=jnp.float32) * scale
    p = jax.nn.softmax(s, axis=-1).astype(v.dtype)
    return jnp.einsum('bhqk,bkhd->bqhd', p, v)


def dense_attention(q, k, v, scale):
    Bn, Sq, H, dq = q.shape
    if k.shape[1] < DENSE_SWEEP_KEYS or Sq % Q_BLOCK:
        return _attn(q, k, v, scale)
    qb = jnp.moveaxis(q.reshape(Bn, Sq // Q_BLOCK, Q_BLOCK, H, dq), 1, 0)
    ob = lax.map(lambda qi: _attn(qi, k, v, scale), qb)
    return jnp.moveaxis(ob, 0, 1).reshape(Bn, Sq, H, v.shape[-1])


def mla_keys_values(ckv, kpe, w_ukv, g_k):
    Bn, L, _ = ckv.shape
    kv = (ckv @ w_ukv).reshape(Bn, L, MLA_HEADS, MLA_NOPE + MLA_V)
    k_pe = jnp.broadcast_to(kpe[:, :, None, :], (Bn, L, MLA_HEADS, MLA_ROPE))
    k = rmsnorm(jnp.concatenate([kv[..., :MLA_NOPE], k_pe], axis=-1), g_k)
    return k, kv[..., MLA_NOPE:]


def neighbourhood_attention(q, k, v, k_ctx, v_ctx, rpb):
    Bn, S, H, dh = q.shape
    R = S // GRID_W
    wr = min(NA_WIN_R, R)
    scale = dh ** -0.5
    qg = q.reshape(Bn, R, GRID_W, H, dh)
    kg = k.reshape(Bn, R, GRID_W, H, dh)
    vg = v.reshape(Bn, R, GRID_W, H, dh)
    rows = jnp.arange(R)
    row_idx = jnp.clip(rows - wr // 2, 0, R - wr)[:, None] + jnp.arange(wr)[None, :]
    k_rows = kg[:, row_idx]
    v_rows = vg[:, row_idx]
    cols = jnp.arange(GRID_W)
    col_start = jnp.clip(cols - NA_WIN_C // 2, 0, GRID_W - NA_WIN_C)
    col_ok = (cols[None, :] >= col_start[:, None]) & (cols[None, :] < col_start[:, None] + NA_WIN_C)
    dr = row_idx - rows[:, None]
    dc = jnp.clip(cols[None, :] - cols[:, None], -(NA_WIN_C - 1), NA_WIN_C - 1)
    bias = rpb[:, dr[:, None, :, None] + (NA_WIN_R - 1), dc[None, :, None, :] + (NA_WIN_C - 1)]
    s_loc = jnp.einsum('brchd,brikhd->bhrcik', qg, k_rows, preferred_element_type=jnp.float32) * scale
    s_loc = jnp.where(col_ok[:, None, :], s_loc + bias[None].astype(jnp.float32), -jnp.inf)
    s_ctx = jnp.einsum('brchd,blhd->bhrcl', qg, k_ctx, preferred_element_type=jnp.float32) * scale
    n_loc = wr * GRID_W
    s = jnp.concatenate([s_loc.reshape(Bn, H, R, GRID_W, n_loc), s_ctx], axis=-1)
    p = jax.nn.softmax(s, axis=-1).astype(v.dtype)
    p_loc = p[..., :n_loc].reshape(Bn, H, R, GRID_W, wr, GRID_W)
    o = (jnp.einsum('bhrcik,brikhd->brchd', p_loc, v_rows)
         + jnp.einsum('bhrcl,blhd->brchd', p[..., n_loc:], v_ctx))
    return o.reshape(Bn, S, H * dh)


def swiglu(h, w_g, w_u, w_d):
    return (jax.nn.silu(h @ w_g) * (h @ w_u)) @ w_d


def moe_swiglu(h, w_router, w_g, w_u, w_d):
    shp = h.shape
    t = h.reshape(-1, D_MODEL)
    logits = (t @ w_router).astype(jnp.float32)
    top_v, top_i = lax.top_k(logits, TOP_K)
    top_w = jax.nn.softmax(top_v, axis=-1)
    gate = jnp.sum(jax.nn.one_hot(top_i, N_EXPERTS, dtype=jnp.float32) * top_w[..., None], axis=1).astype(h.dtype)
    out = jnp.zeros_like(t)
    for e in range(N_EXPERTS):
        out = out + gate[:, e:e + 1] * swiglu(t, w_g[e], w_u[e], w_d[e])
    return out.reshape(shp)


def trunk_layer(x, cvec, l, P, ctx):
    latent = ctx is not None
    Bn, S, _ = x.shape
    mod = (jax.nn.silu(cvec) @ P['w_ada'][l] + P['b_ada'][l]).reshape(cvec.shape[0], 1, 6, D_MODEL)
    shift1, scale1, gate1, shift2, scale2, gate2 = (mod[:, :, i] for i in range(6))
    h = rmsnorm(x, P['g_norm1'][l]) * (1 + scale1) + shift1
    p_qa, p_kva, p_sc, p_na, p_cf, p_gate = jnp.split(h @ P['w_in'][l], IN_SPLITS, axis=-1)

    q = rmsnorm(p_qa, P['g_qa'][l]) @ P['w_uq'][l]
    q = rmsnorm(q.reshape(Bn, S, MLA_HEADS, MLA_QK), P['g_mla_q'][l])
    ckv = rmsnorm(p_kva[..., :MLA_KV_LORA], P['g_kva'][l])
    kpe = p_kva[..., MLA_KV_LORA:]
    k, v = mla_keys_values(ckv, kpe, P['w_ukv'][l], P['g_mla_k'][l])
    if latent:
        q = rope_tail(q)
        k = rope_tail(k)
        k_c, v_c = mla_keys_values(ctx[0], ctx[1], P['w_ukv'][l], P['g_mla_k'][l])
        k = jnp.concatenate([k_c, k], axis=1)
        v = jnp.concatenate([v_c, v], axis=1)
    o_a = dense_attention(q, k, v, MLA_QK ** -0.5).reshape(Bn, S, MLA_HEADS * MLA_V)

    b_g, c_g, xv = jnp.split(p_sc, 3, axis=-1)
    o_b = b_g * dwconv(c_g * xv, P['sc_w'][l])

    q_na, k_na, v_na = (z.reshape(Bn, S, NA_HEADS, NA_HD) for z in jnp.split(p_na, 3, axis=-1))
    q_na = rmsnorm(q_na, P['g_na_q'][l])
    k_na = rmsnorm(k_na, P['g_na_k'][l])
    if latent:
        o_c = neighbourhood_attention(q_na, k_na, v_na, ctx[2], ctx[3], P['na_rpb'][l])
    else:
        o_c = dense_attention(q_na, k_na, v_na, NA_HD ** -0.5).reshape(Bn, S, NA_W)

    a_cf, b_cf = jnp.split(p_cf, 2, axis=-1)
    u = dwconv(a_cf * jax.nn.sigmoid(b_cf), P['cf_w'][l]) + P['cf_b'][l]
    o_d = jax.nn.silu(layernorm(u, P['cf_ln_g'][l], P['cf_ln_b'][l]))

    branches = jnp.stack([o_a, o_b, o_c, o_d], axis=2)
    proj = jnp.einsum('bsnc,ncd->bsnd', branches, P['w_br'][l])
    gates = jax.nn.sigmoid(p_gate.reshape(Bn, S, N_BRANCH, D_MODEL))
    x = x + gate1 * (jnp.sum(gates * proj, axis=2) @ P['w_o'][l])

    h2 = rmsnorm(x, P['g_norm2'][l]) * (1 + scale2) + shift2
    if l % 2 == 0:
        j = l // 2
        f = swiglu(h2, P['w_ff_gate'][j], P['w_ff_up'][j], P['w_ff_down'][j])
    else:
        j = l // 2
        f = moe_swiglu(h2, P['w_router'][j], P['w_e_gate'][j], P['w_e_up'][j], P['w_e_down'][j])
    x = x + gate2 * f
    if latent:
        return x, None
    return x, (ckv, kpe, k_na, v_na)


def setup_inputs(seed: int = 0) -> dict:
    key = jax.random.key(seed)
    ks = iter(jax.random.split(key, 48))

    def nrm(shape, scale=1.0):
        return jax.random.normal(next(ks), shape, jnp.float32) * scale

    def gain(shape):
        return 1.0 + nrm(shape, 0.02)

    D = D_MODEL
    return {
        'x_prompt': nrm((BATCH, SEQ, D)),
        'x_sample': nrm((DEC_BATCH, DEC_SEQ, D)),
        'cache_mla_ckv': nrm((DEC_BATCH, DEPTH, PAST_LEN, MLA_KV_LORA)),
        'cache_mla_kpe': nrm((DEC_BATCH, DEPTH, PAST_LEN, MLA_ROPE)),
        'cache_na_k': nrm((DEC_BATCH, DEPTH, PAST_LEN, NA_HEADS, NA_HD)),
        'cache_na_v': nrm((DEC_BATCH, DEPTH, PAST_LEN, NA_HEADS, NA_HD)),
        'c': nrm((DEC_BATCH, D)),
        'c_ctx': nrm((D,)),
        'w_ada': nrm((DEPTH, D, 6 * D), D ** -0.5),
        'b_ada': nrm((DEPTH, 6 * D), 0.01),
        'g_norm1': gain((DEPTH, D)),
        'w_in': nrm((DEPTH, D, IN_COLS), D ** -0.5),
        'g_qa': gain((DEPTH, MLA_Q_LORA)),
        'w_uq': nrm((DEPTH, MLA_Q_LORA, MLA_HEADS * MLA_QK), MLA_Q_LORA ** -0.5),
        'g_kva': gain((DEPTH, MLA_KV_LORA)),
        'w_ukv': nrm((DEPTH, MLA_KV_LORA, MLA_HEADS * (MLA_NOPE + MLA_V)), MLA_KV_LORA ** -0.5),
        'g_mla_q': gain((DEPTH, MLA_QK)),
        'g_mla_k': gain((DEPTH, MLA_QK)),
        'sc_w': nrm((DEPTH, SC_K, SC_W), SC_K ** -0.5),
        'g_na_q': gain((DEPTH, NA_HD)),
        'g_na_k': gain((DEPTH, NA_HD)),
        'na_rpb': nrm((DEPTH, NA_HEADS, 2 * NA_WIN_R - 1, 2 * NA_WIN_C - 1), 0.1),
        'cf_w': nrm((DEPTH, CF_K, CF_W), CF_K ** -0.5),
        'cf_b': nrm((DEPTH, CF_W), 0.01),
        'cf_ln_g': gain((DEPTH, CF_W)),
        'cf_ln_b': nrm((DEPTH, CF_W), 0.01),
        'w_br': nrm((DEPTH, N_BRANCH, BR_W, D), BR_W ** -0.5),
        'w_o': nrm((DEPTH, D, D), D ** -0.5),
        'g_norm2': gain((DEPTH, D)),
        'w_ff_gate': nrm((N_DENSE, D, D_FF), D ** -0.5),
        'w_ff_up': nrm((N_DENSE, D, D_FF), D ** -0.5),
        'w_ff_down': nrm((N_DENSE, D_FF, D), D_FF ** -0.5),
        'w_router': nrm((N_MOE, D, N_EXPERTS), D ** -0.5),
        'w_e_gate': nrm((N_MOE, N_EXPERTS, D, D_FF_E), D ** -0.5),
        'w_e_up': nrm((N_MOE, N_EXPERTS, D, D_FF_E), D ** -0.5),
        'w_e_down': nrm((N_MOE, N_EXPERTS, D_FF_E, D), D_FF_E ** -0.5),
    }


def reference(x_prompt, x_sample, cache_mla_ckv, cache_mla_kpe, cache_na_k, cache_na_v, c, c_ctx,
              w_ada, b_ada, g_norm1, w_in, g_qa, w_uq, g_kva, w_ukv, g_mla_q, g_mla_k, sc_w,
              g_na_q, g_na_k, na_rpb, cf_w, cf_b, cf_ln_g, cf_ln_b, w_br, w_o, g_norm2,
              w_ff_gate, w_ff_up, w_ff_down, w_router, w_e_gate, w_e_up, w_e_down):
    P = dict(w_ada=w_ada, b_ada=b_ada, g_norm1=g_norm1, w_in=w_in, g_qa=g_qa, w_uq=w_uq, g_kva=g_kva,
             w_ukv=w_ukv, g_mla_q=g_mla_q, g_mla_k=g_mla_k, sc_w=sc_w, g_na_q=g_na_q, g_na_k=g_na_k,
             na_rpb=na_rpb, cf_w=cf_w, cf_b=cf_b, cf_ln_g=cf_ln_g, cf_ln_b=cf_ln_b, w_br=w_br, w_o=w_o,
             g_norm2=g_norm2, w_ff_gate=w_ff_gate, w_ff_up=w_ff_up, w_ff_down=w_ff_down,
             w_router=w_router, w_e_gate=w_e_gate, w_e_up=w_e_up, w_e_down=w_e_down)

    y_prompt = x_prompt
    cvec_ctx = c_ctx[None, :]
    ckv_l, kpe_l, nak_l, nav_l = [], [], [], []
    for l in range(DEPTH):
        y_prompt, st = trunk_layer(y_prompt, cvec_ctx, l, P, None)
        ckv_l.append(st[0])
        kpe_l.append(st[1])
        nak_l.append(st[2])
        nav_l.append(st[3])
    new_mla_ckv = jnp.stack(ckv_l, axis=1)
    new_mla_kpe = jnp.stack(kpe_l, axis=1)
    new_na_k = jnp.stack(nak_l, axis=1)
    new_na_v = jnp.stack(nav_l, axis=1)

    y_sample = x_sample
    for l in range(DEPTH):
        y_sample, _ = trunk_layer(y_sample, c, l, P,
                                  (cache_mla_ckv[:, l], cache_mla_kpe[:, l], cache_na_k[:, l], cache_na_v[:, l]))

    return (y_prompt, y_sample, new_mla_ckv, new_mla_kpe, new_na_k, new_na_v)
```

```python
import functools

import numpy as np
import jax
import jax.numpy as jnp
from jax import lax
from jax.experimental import pallas as pl
from jax.experimental.pallas import tpu as pltpu

F32 = jnp.float32
BF16 = jnp.bfloat16

D_MODEL = 1024
BATCH = 16
SEQ = 256
DEPTH = 2
DEC_BATCH = 8
DEC_SEQ = 1024
PAST_LEN = 512
GRID_W = 64
N_BRANCH = 4
BR_W = 256
MLA_HEADS = 4
MLA_NOPE = 64
MLA_ROPE = 32
MLA_QK = 96
MLA_V = 64
MLA_Q_LORA = 256
MLA_KV_LORA = 128
SC_K = 3
NA_HEADS = 4
NA_HD = 64
NA_WIN_R = 8
NA_WIN_C = 16
CF_K = 31
D_FF = 3584
N_EXPERTS = 8
ROPE_THETA = 10000.0
EPS = 1e-6

N_CTX_TOK = BATCH * SEQ
N_LAT_TOK = DEC_BATCH * DEC_SEQ
N_TOK = N_CTX_TOK + N_LAT_TOK
MOD_ROWS = 16

HEAD_PAD = 128
MLA_PAD_W = MLA_HEADS * HEAD_PAD

P_QA = 0
P_KVA = 256
P_KPE4 = 512
P_CF = 1024
P_SC = 1536
P_NA = 2304
P_GATE = 3072
P_COLS = 7168

VMEM_LIMIT = 56 * 1024 * 1024

NA_TQ = 256
NA_WIN_ROWS = 12
NA_WIN_KEYS = NA_WIN_ROWS * GRID_W


def _cparams(sem):
    return pltpu.CompilerParams(dimension_semantics=sem, vmem_limit_bytes=VMEM_LIMIT)


def _const_spec(shape):
    nd = len(shape)
    return pl.BlockSpec(shape, lambda *_: (0,) * nd)


def _mod_row(tok_start):
    return jnp.where(tok_start < N_CTX_TOK, 0, 1 + (tok_start - N_CTX_TOK) // DEC_SEQ)


def _silu(x):
    return x * jax.nn.sigmoid(x)


def _dot(a, b):
    return jnp.dot(a.astype(BF16), b.astype(BF16), preferred_element_type=F32)


def _dot_nt(a, b):
    return lax.dot_general(a.astype(BF16), b.astype(BF16), (((1,), (1,)), ((), ())),
                           preferred_element_type=F32)


def _mod_kernel(c_ref, w_ref, b_ref, o_ref):
    o_ref[...] = _dot(_silu(c_ref[...]), w_ref[...]) + b_ref[...]


def _modulation(cvec, w_ada, b_ada, l):
    tn = 1024
    return pl.pallas_call(
        _mod_kernel,
        out_shape=jax.ShapeDtypeStruct((MOD_ROWS, 6 * D_MODEL), F32),
        grid=(6 * D_MODEL // tn,),
        in_specs=[
            _const_spec((MOD_ROWS, D_MODEL)),
            pl.BlockSpec((None, D_MODEL, tn), lambda j: (l, 0, j)),
            pl.BlockSpec((None, 1, tn), lambda j: (l, 0, j)),
        ],
        out_specs=pl.BlockSpec((MOD_ROWS, tn), lambda j: (0, j)),
        compiler_params=_cparams(("arbitrary",)),
        name="modulation",
    )(cvec, w_ada, b_ada.reshape(DEPTH, 1, 6 * D_MODEL))


def _inproj_kernel(x_ref, mod_ref, g_ref, w_ref, o_ref, h_scr):
    @pl.when(pl.program_id(1) == 0)
    def _():
        x = x_ref[...]
        y = x * lax.rsqrt(jnp.mean(x * x, axis=-1, keepdims=True) + EPS) * g_ref[...]
        h_scr[...] = (y * (1.0 + mod_ref[1:2, :]) + mod_ref[0:1, :]).astype(BF16)

    o_ref[...] = jnp.dot(h_scr[...], w_ref[...], preferred_element_type=F32).astype(o_ref.dtype)


def _in_projection(x_all, mod3, g_norm1, w_in_p, l):
    tm, tn = 512, 1024
    return pl.pallas_call(
        _inproj_kernel,
        out_shape=jax.ShapeDtypeStruct((N_TOK, P_COLS), BF16),
        grid=(N_TOK // tm, P_COLS // tn),
        in_specs=[
            pl.BlockSpec((tm, D_MODEL), lambda i, j: (i, 0)),
            pl.BlockSpec((None, 6, D_MODEL), lambda i, j: (_mod_row(i * tm), 0, 0)),
            pl.BlockSpec((None, 1, D_MODEL), lambda i, j: (l, 0, 0)),
            pl.BlockSpec((D_MODEL, tn), lambda i, j: (0, j)),
        ],
        out_specs=pl.BlockSpec((tm, tn), lambda i, j: (i, j)),
        scratch_shapes=[pltpu.VMEM((tm, D_MODEL), BF16)],
        compiler_params=_cparams(("arbitrary", "arbitrary")),
        name="in_projection",
    )(x_all, mod3, g_norm1.reshape(DEPTH, 1, D_MODEL), w_in_p)


def _rope_tables():
    t = np.arange(DEC_SEQ)
    nf = MLA_ROPE // 4
    inv = (np.float32(ROPE_THETA) ** (-np.arange(nf, dtype=np.float32) / np.float32(nf))).astype(np.float32)
    ang_r = (t // GRID_W).astype(np.float32)[:, None] * inv[None, :]
    ang_c = (t % GRID_W).astype(np.float32)[:, None] * inv[None, :]
    c = np.zeros((DEC_SEQ, HEAD_PAD), np.float32)
    s1 = np.zeros((DEC_SEQ, HEAD_PAD), np.float32)
    s2 = np.zeros((DEC_SEQ, HEAD_PAD), np.float32)
    c[:, :MLA_NOPE] = 1.0
    for base, ang in ((MLA_NOPE, ang_r), (MLA_NOPE + 2 * nf, ang_c)):
        c[:, base:base + nf] = np.cos(ang)
        c[:, base + nf:base + 2 * nf] = np.cos(ang)
        s1[:, base:base + nf] = -np.sin(ang)
        s2[:, base + nf:base + 2 * nf] = np.sin(ang)
    return c, s1, s2


def _rope(x, c, s1, s2):
    nf = MLA_ROPE // 4
    return x * c + pltpu.roll(x, HEAD_PAD - nf, 1) * s1 + pltpu.roll(x, nf, 1) * s2


def _head_norm(xh, g):
    ms = jnp.sum(xh * xh, axis=-1, keepdims=True) * (1.0 / MLA_QK)
    return xh * lax.rsqrt(ms + EPS) * g


def _mla_kernel(latent, tq, *refs):
    if latent:
        (pqa_ref, pkva_ref, pkpe_ref, cckv_ref, ckpe_ref, rc_ref, rs1_ref, rs2_ref,
         gqa_ref, wuq_ref, gkva_ref, wk_ref, wv_ref, gq_ref, gk_ref,
         o_ref, k_scr, v_scr) = refs
    else:
        (pqa_ref, pkva_ref, pkpe_ref,
         gqa_ref, wuq_ref, gkva_ref, wk_ref, wv_ref, gq_ref, gk_ref,
         o_ref, ckv_ref, k_scr, v_scr) = refs
    qi = pl.program_id(1)
    n_past = PAST_LEN if latent else 0

    def put_kv(ckvn, kpe4, row0, rope):
        n = ckvn.shape[0]
        kk = _dot(ckvn, wk_ref[...]) + kpe4
        v_scr[row0:row0 + n, :] = _dot(ckvn, wv_ref[...]).astype(BF16)
        for h in range(MLA_HEADS):
            sl = slice(h * HEAD_PAD, (h + 1) * HEAD_PAD)
            kh = _head_norm(kk[:, sl], gk_ref[...])
            if rope:
                kh = _rope(kh, rc_ref[...], rs1_ref[...], rs2_ref[...])
            k_scr[row0:row0 + n, sl] = kh.astype(BF16)

    @pl.when(qi == 0)
    def _():
        if latent:
            put_kv(cckv_ref[...], ckpe_ref[...], 0, False)
        kva = pkva_ref[...].astype(F32)
        ckv = kva[:, :MLA_KV_LORA]
        ckvn = ckv * lax.rsqrt(jnp.mean(ckv * ckv, axis=-1, keepdims=True) + EPS) * gkva_ref[...]
        if not latent:
            ckv_ref[...] = ckvn
        put_kv(ckvn, pkpe_ref[...].astype(F32), n_past, latent)

    qa = pqa_ref[...].astype(F32)
    qan = qa * lax.rsqrt(jnp.mean(qa * qa, axis=-1, keepdims=True) + EPS) * gqa_ref[...]
    q = _dot(qan, wuq_ref[...])
    scale = MLA_QK ** -0.5
    for h in range(MLA_HEADS):
        sl = slice(h * HEAD_PAD, (h + 1) * HEAD_PAD)
        qh = _head_norm(q[:, sl], gq_ref[...])
        if latent:
            rows = pl.ds(pl.multiple_of(qi * tq, tq), tq)
            qh = _rope(qh, rc_ref[rows, :], rs1_ref[rows, :], rs2_ref[rows, :])
        s = _dot_nt(qh, k_scr[:, sl]) * scale
        m = jnp.max(s, axis=-1, keepdims=True)
        e = jnp.exp(s - m)
        den = jnp.sum(e, axis=-1, keepdims=True)
        o = _dot(e, v_scr[:, sl]) / den
        o_ref[:, sl] = o.astype(o_ref.dtype)


def _mla(p, l, latent, cache_ckv, cache_kpe4, W):
    if latent:
        nb, s, tq, row_off = DEC_BATCH, DEC_SEQ, 256, N_CTX_TOK
    else:
        nb, s, tq, row_off = BATCH, SEQ, 256, 0
    nq = s // tq
    sk = s + (PAST_LEN if latent else 0)
    in_specs = [
        pl.BlockSpec((tq, MLA_Q_LORA), lambda b, qi: (row_off // tq + b * nq + qi, P_QA // MLA_Q_LORA)),
        pl.BlockSpec((s, 256), lambda b, qi: (row_off // s + b, P_KVA // 256)),
        pl.BlockSpec((s, MLA_PAD_W), lambda b, qi: (row_off // s + b, P_KPE4 // MLA_PAD_W)),
    ]
    args = [p, p, p]
    if latent:
        in_specs += [
            pl.BlockSpec((None, None, PAST_LEN, MLA_KV_LORA), lambda b, qi: (b, l, 0, 0)),
            pl.BlockSpec((None, PAST_LEN, MLA_PAD_W), lambda b, qi: (b, 0, 0)),
            _const_spec((DEC_SEQ, HEAD_PAD)), _const_spec((DEC_SEQ, HEAD_PAD)), _const_spec((DEC_SEQ, HEAD_PAD)),
        ]
        args += [cache_ckv, cache_kpe4] + [jnp.asarray(t) for t in _rope_tables()]
    in_specs += [
        _const_spec((1, MLA_Q_LORA)), _const_spec((MLA_Q_LORA, MLA_PAD_W)), _const_spec((1, MLA_KV_LORA)),
        _const_spec((MLA_KV_LORA, MLA_PAD_W)), _const_spec((MLA_KV_LORA, MLA_PAD_W)),
        _const_spec((1, HEAD_PAD)), _const_spec((1, HEAD_PAD)),
    ]
    args += [W['g_qa'], W['w_uq_p'], W['g_kva'], W['w_ukv_k'], W['w_ukv_v'], W['g_mla_q_p'], W['g_mla_k_p']]
    out_shape = [jax.ShapeDtypeStruct((nb * s, MLA_PAD_W), BF16)]
    out_specs = [pl.BlockSpec((tq, MLA_PAD_W), lambda b, qi: (b * nq + qi, 0))]
    if not latent:
        out_shape.append(jax.ShapeDtypeStruct((nb, s, MLA_KV_LORA), F32))
        out_specs.append(pl.BlockSpec((None, s, MLA_KV_LORA), lambda b, qi: (b, 0, 0)))
    return pl.pallas_call(
        functools.partial(_mla_kernel, latent, tq),
        out_shape=out_shape,
        grid=(nb, nq),
        in_specs=in_specs,
        out_specs=out_specs,
        scratch_shapes=[pltpu.VMEM((sk, MLA_PAD_W), BF16), pltpu.VMEM((sk, MLA_PAD_W), BF16)],
        compiler_params=_cparams(("arbitrary", "arbitrary")),
        name="mla_latent" if latent else "mla_context",
    )(*args)


def _head_masks(width):
    lane = lax.broadcasted_iota(jnp.int32, (1, width), 1)
    return [(lane >= h * NA_HD) & (lane < (h + 1) * NA_HD) for h in range(NA_HEADS)]


def _group_norm64(x, g, masks):
    x2 = x * x
    inv = jnp.zeros_like(x)
    for m in masks:
        ms = jnp.sum(jnp.where(m, x2, 0.0), axis=-1, keepdims=True) * (1.0 / NA_HD)
        inv = jnp.where(m, lax.rsqrt(ms + EPS), inv)
    return x * inv * g


def _na_ctx_kernel(p_ref, gq_ref, gk_ref, o_ref, k_ref, v_ref):
    masks = _head_masks(BR_W)
    pna = p_ref[...].astype(F32)
    qn = _group_norm64(pna[:, :BR_W], gq_ref[...], masks)
    kn = _group_norm64(pna[:, BR_W:2 * BR_W], gk_ref[...], masks)
    v = pna[:, 2 * BR_W:]
    k_ref[...] = kn
    v_ref[...] = v
    scale = NA_HD ** -0.5
    acc = jnp.zeros((p_ref.shape[0], BR_W), F32)
    for m in masks:
        s = _dot_nt(jnp.where(m, qn, 0.0), kn) * scale
        mx = jnp.max(s, axis=-1, keepdims=True)
        e = jnp.exp(s - mx)
        den = jnp.sum(e, axis=-1, keepdims=True)
        acc = acc + jnp.where(m, _dot(e, v) / den, 0.0)
    o_ref[...] = acc.astype(o_ref.dtype)


def _na_context(p, W):
    s = SEQ
    return pl.pallas_call(
        _na_ctx_kernel,
        out_shape=[jax.ShapeDtypeStruct((N_CTX_TOK, BR_W), BF16),
                   jax.ShapeDtypeStruct((N_CTX_TOK, BR_W), F32),
                   jax.ShapeDtypeStruct((N_CTX_TOK, BR_W), F32)],
        grid=(BATCH,),
        in_specs=[pl.BlockSpec((s, 3 * BR_W), lambda b: (b, P_NA // (3 * BR_W))),
                  _const_spec((1, BR_W)), _const_spec((1, BR_W))],
        out_specs=[pl.BlockSpec((s, BR_W), lambda b: (b, 0))] * 3,
        compiler_params=_cparams(("arbitrary",)),
        name="na_context",
    )(p, W['g_na_q_t'], W['g_na_k_t'])


def _na_lat_kernel(p_ref, ck_ref, cv_ref, bias_ref, gq_ref, gk_ref, o_ref, q_scr, k_scr, v_scr, kc_scr, vc_scr):
    j = pl.program_id(1)
    masks = _head_masks(BR_W)

    @pl.when(j == 0)
    def _():
        pna = p_ref[...].astype(F32)
        q_scr[...] = _group_norm64(pna[:, :BR_W], gq_ref[...], masks).astype(BF16)
        k_scr[...] = _group_norm64(pna[:, BR_W:2 * BR_W], gk_ref[...], masks).astype(BF16)
        v_scr[...] = pna[:, 2 * BR_W:].astype(BF16)
        kc_scr[...] = ck_ref[...].astype(BF16)
        vc_scr[...] = cv_ref[...].astype(BF16)

    win0 = pl.multiple_of(jnp.where(j < 2, 0, DEC_SEQ - NA_WIN_KEYS), 256)
    q = q_scr[pl.ds(pl.multiple_of(j * NA_TQ, NA_TQ), NA_TQ), :]
    kw = k_scr[pl.ds(win0, NA_WIN_KEYS), :]
    vw = v_scr[pl.ds(win0, NA_WIN_KEYS), :]
    kc = kc_scr[...]
    vc = vc_scr[...]
    scale = NA_HD ** -0.5
    acc = jnp.zeros((NA_TQ, BR_W), F32)
    zero = jnp.zeros((), BF16)
    for h, m in enumerate(masks):
        qm = jnp.where(m, q, zero)
        s_loc = _dot_nt(qm, kw) * scale + bias_ref[h]
        s_ctx = _dot_nt(qm, kc) * scale
        mx = jnp.maximum(jnp.max(s_loc, axis=-1, keepdims=True), jnp.max(s_ctx, axis=-1, keepdims=True))
        e_loc = jnp.exp(s_loc - mx)
        e_ctx = jnp.exp(s_ctx - mx)
        den = jnp.sum(e_loc, axis=-1, keepdims=True) + jnp.sum(e_ctx, axis=-1, keepdims=True)
        o = (_dot(e_loc, vw) + _dot(e_ctx, vc)) / den
        acc = acc + jnp.where(m, o, 0.0)
    o_ref[...] = acc.astype(o_ref.dtype)


def _na_bias_table(rpb):
    nt = DEC_SEQ // NA_TQ
    qq = np.arange(NA_TQ)
    kk = np.arange(NA_WIN_KEYS)
    n_rows = DEC_SEQ // GRID_W
    dr_idx = np.zeros((nt, NA_TQ, NA_WIN_KEYS), np.int32)
    dc_idx = np.zeros((nt, NA_TQ, NA_WIN_KEYS), np.int32)
    ok = np.zeros((nt, NA_TQ, NA_WIN_KEYS), bool)
    for j in range(nt):
        win_row0 = 0 if j < nt // 2 else n_rows - NA_WIN_ROWS
        r = (j * NA_TQ + qq) // GRID_W
        cq = qq % GRID_W
        kr = win_row0 + kk // GRID_W
        ck = kk % GRID_W
        start = np.clip(r - NA_WIN_R // 2, 0, n_rows - NA_WIN_R)
        row_ok = (kr[None, :] >= start[:, None]) & (kr[None, :] < start[:, None] + NA_WIN_R)
        cstart = np.clip(cq - NA_WIN_C // 2, 0, GRID_W - NA_WIN_C)
        col_ok = (ck[None, :] >= cstart[:, None]) & (ck[None, :] < cstart[:, None] + NA_WIN_C)
        ok[j] = row_ok & col_ok
        dr_idx[j] = np.clip(kr[None, :] - r[:, None] + (NA_WIN_R - 1), 0, 2 * NA_WIN_R - 2)
        dc_idx[j] = np.clip(ck[None, :] - cq[:, None], -(NA_WIN_C - 1), NA_WIN_C - 1) + (NA_WIN_C - 1)
    bias = rpb[:, dr_idx, dc_idx]
    return jnp.where(ok[None], bias, -jnp.inf)


def _na_latent(p, l, cache_k, cache_v, bias, W):
    s = DEC_SEQ
    nt = s // NA_TQ
    row_off = N_CTX_TOK
    return pl.pallas_call(
        _na_lat_kernel,
        out_shape=jax.ShapeDtypeStruct((N_LAT_TOK, BR_W), BF16),
        grid=(DEC_BATCH, nt),
        in_specs=[pl.BlockSpec((s, 3 * BR_W), lambda b, j: (row_off // s + b, P_NA // (3 * BR_W))),
                  pl.BlockSpec((None, None, PAST_LEN, BR_W), lambda b, j: (b, l, 0, 0)),
                  pl.BlockSpec((None, None, PAST_LEN, BR_W), lambda b, j: (b, l, 0, 0)),
                  pl.BlockSpec((NA_HEADS, None, NA_TQ, NA_WIN_KEYS), lambda b, j: (0, j, 0, 0)),
                  _const_spec((1, BR_W)), _const_spec((1, BR_W))],
        out_specs=pl.BlockSpec((NA_TQ, BR_W), lambda b, j: (b * nt + j, 0)),
        scratch_shapes=[pltpu.VMEM((s, BR_W), BF16), pltpu.VMEM((s, BR_W), BF16), pltpu.VMEM((s, BR_W), BF16),
                        pltpu.VMEM((PAST_LEN, BR_W), BF16), pltpu.VMEM((PAST_LEN, BR_W), BF16)],
        compiler_params=_cparams(("arbitrary", "arbitrary")),
        name="na_latent",
    )(p, cache_k, cache_v, bias, W['g_na_q_t'], W['g_na_k_t'])


CONV_HALO = 16
CONV_CHUNK = 128


def _dwconv_from_pad(pad_ref, w_ref, ksize, s, emit):
    half = ksize // 2
    for c0 in range(0, s, CONV_CHUNK):
        acc = jnp.zeros((CONV_CHUNK, BR_W), F32)
        for k in range(ksize):
            r0 = CONV_HALO + c0 + k - half
            acc = acc + pad_ref[r0:r0 + CONV_CHUNK, :] * w_ref[k:k + 1, :]
        emit(c0, acc)


def _conv_kernel(s, sc_ref, cf_ref, scw_ref, cfw_ref, cfb_ref, lng_ref, lnb_ref, ob_ref, od_ref, pad_ref):
    zeros = jnp.zeros((CONV_HALO, BR_W), F32)
    pad_ref[0:CONV_HALO, :] = zeros
    pad_ref[CONV_HALO + s:2 * CONV_HALO + s, :] = zeros

    pad_ref[CONV_HALO:CONV_HALO + s, :] = (sc_ref[:, BR_W:2 * BR_W].astype(F32)
                                           * sc_ref[:, 2 * BR_W:3 * BR_W].astype(F32))

    def emit_b(c0, acc):
        ob_ref[c0:c0 + CONV_CHUNK, :] = (sc_ref[c0:c0 + CONV_CHUNK, 0:BR_W].astype(F32) * acc).astype(ob_ref.dtype)

    _dwconv_from_pad(pad_ref, scw_ref, SC_K, s, emit_b)

    pad_ref[CONV_HALO:CONV_HALO + s, :] = (cf_ref[:, 0:BR_W].astype(F32)
                                           * jax.nn.sigmoid(cf_ref[:, BR_W:2 * BR_W].astype(F32)))

    def emit_d(c0, acc):
        u = acc + cfb_ref[...]
        mu = jnp.mean(u, axis=-1, keepdims=True)
        d = u - mu
        var = jnp.mean(d * d, axis=-1, keepdims=True)
        y = d * lax.rsqrt(var + EPS) * lng_ref[...] + lnb_ref[...]
        od_ref[c0:c0 + CONV_CHUNK, :] = _silu(y).astype(od_ref.dtype)

    _dwconv_from_pad(pad_ref, cfw_ref, CF_K, s, emit_d)


def _convs(p, latent, W):
    if latent:
        nb, s, row_off = DEC_BATCH, DEC_SEQ, N_CTX_TOK
    else:
        nb, s, row_off = BATCH, SEQ, 0
    return pl.pallas_call(
        functools.partial(_conv_kernel, s),
        out_shape=[jax.ShapeDtypeStruct((nb * s, BR_W), BF16)] * 2,
        grid=(nb,),
        in_specs=[pl.BlockSpec((s, 3 * BR_W), lambda b: (row_off // s + b, P_SC // (3 * BR_W))),
                  pl.BlockSpec((s, 2 * BR_W), lambda b: (row_off // s + b, P_CF // (2 * BR_W))),
                  _const_spec((SC_K, BR_W)), _const_spec((CF_K, BR_W)),
                  _const_spec((1, BR_W)), _const_spec((1, BR_W)), _const_spec((1, BR_W))],
        out_specs=[pl.BlockSpec((s, BR_W), lambda b: (b, 0))] * 2,
        scratch_shapes=[pltpu.VMEM((s + 2 * CONV_HALO, BR_W), F32)],
        compiler_params=_cparams(("arbitrary",)),
        name="convs_latent" if latent else "convs_context",
    )(p, p, W['sc_w'], W['cf_w'], W['cf_b'], W['cf_ln_g'], W['cf_ln_b'])


def _merge_kernel(x_ref, mod_ref, oa_ref, ob_ref, oc_ref, od_ref, g0_ref, g1_ref, g2_ref, g3_ref,
                  wa_ref, wb_ref, wo_ref, y_ref):
    merged = jax.nn.sigmoid(g0_ref[...].astype(F32)) * jnp.dot(oa_ref[...], wa_ref[...], preferred_element_type=F32)
    for n, (o_ref, g_ref) in enumerate(((ob_ref, g1_ref), (oc_ref, g2_ref), (od_ref, g3_ref))):
        merged = merged + (jax.nn.sigmoid(g_ref[...].astype(F32))
                           * jnp.dot(o_ref[...], wb_ref[n], preferred_element_type=F32))
    y_ref[...] = x_ref[...] + mod_ref[2:3, :] * _dot(merged, wo_ref[...])


def _merge(x_all, mod3, p, o_a, o_b, o_c, o_d, W):
    tm = 512
    row = lambda i: (i, 0)
    gate_spec = lambda n: pl.BlockSpec((tm, D_MODEL), lambda i: (i, P_GATE // D_MODEL + n))
    return pl.pallas_call(
        _merge_kernel,
        out_shape=jax.ShapeDtypeStruct((N_TOK, D_MODEL), F32),
        grid=(N_TOK // tm,),
        in_specs=[pl.BlockSpec((tm, D_MODEL), row),
                  pl.BlockSpec((None, 6, D_MODEL), lambda i: (_mod_row(i * tm), 0, 0)),
                  pl.BlockSpec((tm, MLA_PAD_W), row),
                  pl.BlockSpec((tm, BR_W), row), pl.BlockSpec((tm, BR_W), row), pl.BlockSpec((tm, BR_W), row),
                  gate_spec(0), gate_spec(1), gate_spec(2), gate_spec(3),
                  _const_spec((MLA_PAD_W, D_MODEL)), _const_spec((3, BR_W, D_MODEL)),
                  _const_spec((D_MODEL, D_MODEL))],
        out_specs=pl.BlockSpec((tm, D_MODEL), row),
        compiler_params=_cparams(("arbitrary",)),
        name="merge",
    )(x_all, mod3, o_a, o_b, o_c, o_d, p, p, p, p, W['w_br_a'], W['w_br_bcd'], W['w_o'])


def _norm2(x, g, mod_ref):
    y = x * lax.rsqrt(jnp.mean(x * x, axis=-1, keepdims=True) + EPS) * g
    return y * (1.0 + mod_ref[4:5, :]) + mod_ref[3:4, :]


def _ffn_kernel(x_ref, mod_ref, g_ref, wg_ref, wu_ref, wd_ref, y_ref, h_scr, acc_scr):
    f = pl.program_id(1)

    @pl.when(f == 0)
    def _():
        h_scr[...] = _norm2(x_ref[...], g_ref[...], mod_ref).astype(BF16)
        acc_scr[...] = jnp.zeros_like(acc_scr)

    h = h_scr[...]
    a = _silu(_dot(h, wg_ref[...])) * _dot(h, wu_ref[...])
    acc_scr[...] += _dot(a, wd_ref[...])

    @pl.when(f == pl.num_programs(1) - 1)
    def _():
        y_ref[...] = x_ref[...] + mod_ref[5:6, :] * acc_scr[...]


def _ffn(x_all, mod3, g_norm2, w_g, w_u, w_d, l, j):
    tm, tf = 1024, 512
    return pl.pallas_call(
        _ffn_kernel,
        out_shape=jax.ShapeDtypeStruct((N_TOK, D_MODEL), F32),
        grid=(N_TOK // tm, D_FF // tf),
        in_specs=[pl.BlockSpec((tm, D_MODEL), lambda i, f: (i, 0)),
                  pl.BlockSpec((None, 6, D_MODEL), lambda i, f: (_mod_row(i * tm), 0, 0)),
                  pl.BlockSpec((None, 1, D_MODEL), lambda i, f: (l, 0, 0)),
                  pl.BlockSpec((None, D_MODEL, tf), lambda i, f: (j, 0, f)),
                  pl.BlockSpec((None, D_MODEL, tf), lambda i, f: (j, 0, f)),
                  pl.BlockSpec((None, tf, D_MODEL), lambda i, f: (j, f, 0))],
        out_specs=pl.BlockSpec((tm, D_MODEL), lambda i, f: (i, 0)),
        scratch_shapes=[pltpu.VMEM((tm, D_MODEL), BF16), pltpu.VMEM((tm, D_MODEL), F32)],
        compiler_params=_cparams(("arbitrary", "arbitrary")),
        name="ffn_dense",
    )(x_all, mod3, g_norm2.reshape(DEPTH, 1, D_MODEL), w_g, w_u, w_d)


ROUTER_LANES = 128


def _route(h, wr):
    logits = jnp.dot(h, wr, preferred_element_type=F32, precision=lax.Precision.HIGHEST)
    lane = lax.broadcasted_iota(jnp.int32, logits.shape, 1).astype(F32)
    neg = jnp.float32(-jnp.inf)
    logits = jnp.where(lane < N_EXPERTS, logits, neg)
    m1 = jnp.max(logits, axis=-1, keepdims=True)
    i1 = jnp.min(jnp.where(logits == m1, lane, float(ROUTER_LANES)), axis=-1, keepdims=True)
    rest = jnp.where(lane == i1, neg, logits)
    m2 = jnp.max(rest, axis=-1, keepdims=True)
    i2 = jnp.min(jnp.where(rest == m2, lane, float(ROUTER_LANES)), axis=-1, keepdims=True)
    e2 = jnp.exp(m2 - m1)
    w1 = 1.0 / (1.0 + e2)
    w2 = e2 / (1.0 + e2)
    return jnp.where(lane == i1, w1, 0.0) + jnp.where(lane == i2, w2, 0.0)


def _moe_kernel(x_ref, mod_ref, g_ref, wr_ref, wg_ref, wu_ref, wd_ref, y_ref, h_scr, acc_scr, gate_scr):
    e = pl.program_id(1)
    f = pl.program_id(2)

    @pl.when((e == 0) & (f == 0))
    def _():
        h = _norm2(x_ref[...], g_ref[...], mod_ref)
        h_scr[...] = h.astype(BF16)
        gate_scr[...] = _route(h, wr_ref[...])
        acc_scr[...] = jnp.zeros_like(acc_scr)

    h = h_scr[...]
    a = _silu(_dot(h, wg_ref[...])) * _dot(h, wu_ref[...])
    lane = lax.broadcasted_iota(jnp.int32, gate_scr.shape, 1)
    ge = jnp.sum(jnp.where(lane == e, gate_scr[...], 0.0), axis=-1, keepdims=True)
    acc_scr[...] += ge * _dot(a, wd_ref[...])

    @pl.when((e == pl.num_programs(1) - 1) & (f == pl.num_programs(2) - 1))
    def _():
        y_ref[...] = x_ref[...] + mod_ref[5:6, :] * acc_scr[...]


def _moe(x_all, mod3, g_norm2, w_router_p, w_g, w_u, w_d, l, j):
    tm, tf = 1024, 512
    return pl.pallas_call(
        _moe_kernel,
        out_shape=jax.ShapeDtypeStruct((N_TOK, D_MODEL), F32),
        grid=(N_TOK // tm, N_EXPERTS, D_FF // tf),
        in_specs=[pl.BlockSpec((tm, D_MODEL), lambda i, e, f: (i, 0)),
                  pl.BlockSpec((None, 6, D_MODEL), lambda i, e, f: (_mod_row(i * tm), 0, 0)),
                  pl.BlockSpec((None, 1, D_MODEL), lambda i, e, f: (l, 0, 0)),
                  pl.BlockSpec((None, D_MODEL, ROUTER_LANES), lambda i, e, f: (j, 0, 0)),
                  pl.BlockSpec((None, None, D_MODEL, tf), lambda i, e, f: (j, e, 0, f)),
                  pl.BlockSpec((None, None, D_MODEL, tf), lambda i, e, f: (j, e, 0, f)),
                  pl.BlockSpec((None, None, tf, D_MODEL), lambda i, e, f: (j, e, f, 0))],
        out_specs=pl.BlockSpec((tm, D_MODEL), lambda i, e, f: (i, 0)),
        scratch_shapes=[pltpu.VMEM((tm, D_MODEL), BF16), pltpu.VMEM((tm, D_MODEL), F32),
                        pltpu.VMEM((tm, ROUTER_LANES), F32)],
        compiler_params=_cparams(("arbitrary", "arbitrary", "arbitrary")),
        name="moe_dense",
    )(x_all, mod3, g_norm2.reshape(DEPTH, 1, D_MODEL), w_router_p, w_g, w_u, w_d)


def _pad_heads(w, per_head, lo):
    lead = w.shape[:-1]
    w = w.reshape(lead + (MLA_HEADS, per_head))
    w = jnp.pad(w, [(0, 0)] * len(lead) + [(0, 0), (lo, HEAD_PAD - lo - per_head)])
    return w.reshape(lead + (MLA_PAD_W,))


def _kpe_to_heads(kpe):
    z = jnp.pad(kpe, [(0, 0)] * (kpe.ndim - 1) + [(MLA_NOPE, HEAD_PAD - MLA_QK)])
    return jnp.tile(z, (1,) * (kpe.ndim - 1) + (MLA_HEADS,))


def _layer_weights(l, w_in, g_qa, w_uq, g_kva, w_ukv, g_mla_q, g_mla_k, sc_w, g_na_q, g_na_k,
                   cf_w, cf_b, cf_ln_g, cf_ln_b, w_br, w_o):
    wi = w_in[l]
    s1 = MLA_Q_LORA
    s2 = s1 + MLA_KV_LORA
    s3 = s2 + MLA_ROPE
    s4 = s3 + 3 * BR_W
    s5 = s4 + 3 * BR_W
    s6 = s5 + 2 * BR_W
    w_kpe = wi[:, s2:s3]
    w_in_p = jnp.concatenate([
        wi[:, :s3], jnp.zeros((D_MODEL, 256 - MLA_KV_LORA - MLA_ROPE), F32),
        _kpe_to_heads(w_kpe), wi[:, s5:s6], wi[:, s3:s4], wi[:, s4:s5], wi[:, s6:]], axis=1).astype(BF16)
    ukv = w_ukv[l].reshape(MLA_KV_LORA, MLA_HEADS, MLA_NOPE + MLA_V)
    W = dict(
        w_in_p=w_in_p,
        g_qa=g_qa[l][None], g_kva=g_kva[l][None],
        w_uq_p=_pad_heads(w_uq[l], MLA_QK, 0).astype(BF16),
        w_ukv_k=_pad_heads(ukv[:, :, :MLA_NOPE].reshape(MLA_KV_LORA, -1), MLA_NOPE, 0).astype(BF16),
        w_ukv_v=_pad_heads(ukv[:, :, MLA_NOPE:].reshape(MLA_KV_LORA, -1), MLA_V, 0).astype(BF16),
        g_mla_q_p=jnp.pad(g_mla_q[l], (0, HEAD_PAD - MLA_QK))[None],
        g_mla_k_p=jnp.pad(g_mla_k[l], (0, HEAD_PAD - MLA_QK))[None],
        g_na_q_t=jnp.tile(g_na_q[l], NA_HEADS)[None], g_na_k_t=jnp.tile(g_na_k[l], NA_HEADS)[None],
        sc_w=sc_w[l], cf_w=cf_w[l], cf_b=cf_b[l][None], cf_ln_g=cf_ln_g[l][None], cf_ln_b=cf_ln_b[l][None],
        w_br_a=jnp.pad(w_br[l, 0].reshape(MLA_HEADS, MLA_V, D_MODEL),
                       ((0, 0), (0, HEAD_PAD - MLA_V), (0, 0))).reshape(MLA_PAD_W, D_MODEL).astype(BF16),
        w_br_bcd=w_br[l, 1:].astype(BF16),
        w_o=w_o[l].astype(BF16),
    )
    return W


def kernel(x_prompt, x_sample, cache_mla_ckv, cache_mla_kpe, cache_na_k, cache_na_v, c, c_ctx, w_ada, b_ada, g_norm1, w_in, g_qa, w_uq, g_kva, w_ukv, g_mla_q, g_mla_k, sc_w, g_na_q, g_na_k, na_rpb, cf_w, cf_b, cf_ln_g, cf_ln_b, w_br, w_o, g_norm2, w_ff_gate, w_ff_up, w_ff_down, w_router, w_e_gate, w_e_up, w_e_down):
    x_all = jnp.concatenate([x_prompt.reshape(N_CTX_TOK, D_MODEL), x_sample.reshape(N_LAT_TOK, D_MODEL)], axis=0)
    cvec = jnp.concatenate([c_ctx[None, :], c, jnp.zeros((MOD_ROWS - 1 - DEC_BATCH, D_MODEL), F32)], axis=0)
    cache_na_k2 = cache_na_k.reshape(DEC_BATCH, DEPTH, PAST_LEN, BR_W)
    cache_na_v2 = cache_na_v.reshape(DEC_BATCH, DEPTH, PAST_LEN, BR_W)
    w_router_p = jnp.pad(w_router, ((0, 0), (0, 0), (0, ROUTER_LANES - N_EXPERTS)))

    ckv_l, kpe_l, nak_l, nav_l = [], [], [], []
    for l in range(DEPTH):
        W = _layer_weights(l, w_in, g_qa, w_uq, g_kva, w_ukv, g_mla_q, g_mla_k, sc_w, g_na_q, g_na_k,
                           cf_w, cf_b, cf_ln_g, cf_ln_b, w_br, w_o)
        mod3 = _modulation(cvec, w_ada, b_ada, l).reshape(MOD_ROWS, 6, D_MODEL)
        p = _in_projection(x_all, mod3, g_norm1, W['w_in_p'], l)

        oa_c, ckv_new = _mla(p, l, False, None, None, W)
        (oa_l,) = _mla(p, l, True, cache_mla_ckv, _kpe_to_heads(cache_mla_kpe[:, l]), W)
        oc_c, nak_new, nav_new = _na_context(p, W)
        oc_l = _na_latent(p, l, cache_na_k2, cache_na_v2, _na_bias_table(na_rpb[l]), W)
        ob_c, od_c = _convs(p, False, W)
        ob_l, od_l = _convs(p, True, W)

        cat = lambda a, b: jnp.concatenate([a, b], axis=0)
        x_all = _merge(x_all, mod3, p, cat(oa_c, oa_l), cat(ob_c, ob_l), cat(oc_c, oc_l), cat(od_c, od_l), W)

        if l % 2 == 0:
            x_all = _ffn(x_all, mod3, g_norm2, w_ff_gate, w_ff_up, w_ff_down, l, l // 2)
        else:
            x_all = _moe(x_all, mod3, g_norm2, w_router_p, w_e_gate, w_e_up, w_e_down, l, l // 2)

        ckv_l.append(ckv_new)
        kpe_l.append(p[:N_CTX_TOK, P_KVA + MLA_KV_LORA:P_KVA + MLA_KV_LORA + MLA_ROPE]
                     .astype(F32).reshape(BATCH, SEQ, MLA_ROPE))
        nak_l.append(nak_new.reshape(BATCH, SEQ, NA_HEADS, NA_HD))
        nav_l.append(nav_new.reshape(BATCH, SEQ, NA_HEADS, NA_HD))

    y_prompt = x_all[:N_CTX_TOK].reshape(BATCH, SEQ, D_MODEL)
    y_sample = x_all[N_CTX_TOK:].reshape(DEC_BATCH, DEC_SEQ, D_MODEL)
    return (y_prompt, y_sample, jnp.stack(ckv_l, axis=1), jnp.stack(kpe_l, axis=1),
            jnp.stack(nak_l, axis=1), jnp.stack(nav_l, axis=1))
```

```python
import functools

import numpy as np
import jax
import jax.numpy as jnp
from jax import lax
from jax.experimental import pallas as pl
from jax.experimental.pallas import tpu as pltpu

F32 = jnp.float32
BF16 = jnp.bfloat16

D_MODEL = 1024
BATCH = 16
SEQ = 256
DEPTH = 2
DEC_BATCH = 8
DEC_SEQ = 1024
PAST_LEN = 512
GRID_W = 64
N_BRANCH = 4
BR_W = 256
MLA_HEADS = 4
MLA_NOPE = 64
MLA_ROPE = 32
MLA_QK = 96
MLA_V = 64
MLA_Q_LORA = 256
MLA_KV_LORA = 128
SC_K = 3
NA_HEADS = 4
NA_HD = 64
NA_WIN_R = 8
NA_WIN_C = 16
CF_K = 31
D_FF = 3584
N_EXPERTS = 8
ROPE_THETA = 10000.0
EPS = 1e-6

N_CTX_TOK = BATCH * SEQ
N_LAT_TOK = DEC_BATCH * DEC_SEQ
N_TOK = N_CTX_TOK + N_LAT_TOK
MOD_ROWS = 16

HEAD_PAD = 128
MLA_PAD_W = MLA_HEADS * HEAD_PAD

P_QA = 0
P_KVA = 256
P_KPE4 = 512
P_CF = 1024
P_SC = 1536
P_NA = 2304
P_GATE = 3072
P_COLS = 7168

VMEM_LIMIT = 56 * 1024 * 1024

NA_TQ = 256
NA_WIN_ROWS = 12
NA_WIN_KEYS = NA_WIN_ROWS * GRID_W


def _cparams(sem):
    return pltpu.CompilerParams(dimension_semantics=sem, vmem_limit_bytes=VMEM_LIMIT)


def _const_spec(shape):
    nd = len(shape)
    return pl.BlockSpec(shape, lambda *_: (0,) * nd)


def _mod_row(tok_start):
    return jnp.where(tok_start < N_CTX_TOK, 0, 1 + (tok_start - N_CTX_TOK) // DEC_SEQ)


def _silu(x):
    return x * jax.nn.sigmoid(x)


def _dot(a, b):
    return jnp.dot(a.astype(BF16), b.astype(BF16), preferred_element_type=F32)


def _dot_nt(a, b):
    return lax.dot_general(a.astype(BF16), b.astype(BF16), (((1,), (1,)), ((), ())),
                           preferred_element_type=F32)


def _mod_kernel(c_ref, w_ref, b_ref, o_ref):
    o_ref[...] = _dot(_silu(c_ref[...]), w_ref[...]) + b_ref[...]


def _modulation(cvec, w_ada, b_ada, l):
    tn = 1024
    return pl.pallas_call(
        _mod_kernel,
        out_shape=jax.ShapeDtypeStruct((MOD_ROWS, 6 * D_MODEL), F32),
        grid=(6 * D_MODEL // tn,),
        in_specs=[
            _const_spec((MOD_ROWS, D_MODEL)),
            pl.BlockSpec((None, D_MODEL, tn), lambda j: (l, 0, j)),
            pl.BlockSpec((None, 1, tn), lambda j: (l, 0, j)),
        ],
        out_specs=pl.BlockSpec((MOD_ROWS, tn), lambda j: (0, j)),
        compiler_params=_cparams(("arbitrary",)),
        name="modulation",
    )(cvec, w_ada, b_ada.reshape(DEPTH, 1, 6 * D_MODEL))


def _inproj_kernel(x_ref, mod_ref, g_ref, w_ref, o_ref, h_scr):
    @pl.when(pl.program_id(1) == 0)
    def _():
        x = x_ref[...]
        y = x * lax.rsqrt(jnp.mean(x * x, axis=-1, keepdims=True) + EPS) * g_ref[...]
        h_scr[...] = (y * (1.0 + mod_ref[1:2, :]) + mod_ref[0:1, :]).astype(BF16)

    o_ref[...] = jnp.dot(h_scr[...], w_ref[...], preferred_element_type=F32).astype(o_ref.dtype)


def _in_projection(x_all, mod3, g_norm1, w_in_p, l):
    tm, tn = 512, 1024
    return pl.pallas_call(
        _inproj_kernel,
        out_shape=jax.ShapeDtypeStruct((N_TOK, P_COLS), BF16),
        grid=(N_TOK // tm, P_COLS // tn),
        in_specs=[
            pl.BlockSpec((tm, D_MODEL), lambda i, j: (i, 0)),
            pl.BlockSpec((None, 6, D_MODEL), lambda i, j: (_mod_row(i * tm), 0, 0)),
            pl.BlockSpec((None, 1, D_MODEL), lambda i, j: (l, 0, 0)),
            pl.BlockSpec((D_MODEL, tn), lambda i, j: (0, j)),
        ],
        out_specs=pl.BlockSpec((tm, tn), lambda i, j: (i, j)),
        scratch_shapes=[pltpu.VMEM((tm, D_MODEL), BF16)],
        compiler_params=_cparams(("arbitrary", "arbitrary")),
        name="in_projection",
    )(x_all, mod3, g_norm1.reshape(DEPTH, 1, D_MODEL), w_in_p)


def _rope_tables():
    t = np.arange(DEC_SEQ)
    nf = MLA_ROPE // 4
    inv = (np.float32(ROPE_THETA) ** (-np.arange(nf, dtype=np.float32) / np.float32(nf))).astype(np.float32)
    ang_r = (t // GRID_W).astype(np.float32)[:, None] * inv[None, :]
    ang_c = (t % GRID_W).astype(np.float32)[:, None] * inv[None, :]
    c = np.zeros((DEC_SEQ, HEAD_PAD), np.float32)
    s1 = np.zeros((DEC_SEQ, HEAD_PAD), np.float32)
    s2 = np.zeros((DEC_SEQ, HEAD_PAD), np.float32)
    c[:, :MLA_NOPE] = 1.0
    for base, ang in ((MLA_NOPE, ang_r), (MLA_NOPE + 2 * nf, ang_c)):
        c[:, base:base + nf] = np.cos(ang)
        c[:, base + nf:base + 2 * nf] = np.cos(ang)
        s1[:, base:base + nf] = -np.sin(ang)
        s2[:, base + nf:base + 2 * nf] = np.sin(ang)
    return c, s1, s2


def _rope(x, c, s1, s2):
    nf = MLA_ROPE // 4
    return x * c + pltpu.roll(x, HEAD_PAD - nf, 1) * s1 + pltpu.roll(x, nf, 1) * s2


def _head_norm(xh, g):
    ms = jnp.sum(xh * xh, axis=-1, keepdims=True) * (1.0 / MLA_QK)
    return xh * lax.rsqrt(ms + EPS) * g


def _mla_kernel(latent, tq, *refs):
    if latent:
        (pqa_ref, pkva_ref, pkpe_ref, cckv_ref, ckpe_ref, rc_ref, rs1_ref, rs2_ref,
         gqa_ref, wuq_ref, gkva_ref, wk_ref, wv_ref, gq_ref, gk_ref,
         o_ref, k_scr, v_scr) = refs
    else:
        (pqa_ref, pkva_ref, pkpe_ref,
         gqa_ref, wuq_ref, gkva_ref, wk_ref, wv_ref, gq_ref, gk_ref,
         o_ref, ckv_ref, k_scr, v_scr) = refs
    qi = pl.program_id(1)
    n_past = PAST_LEN if latent else 0

    def put_kv(ckvn, kpe4, row0, rope):
        n = ckvn.shape[0]
        kk = _dot(ckvn, wk_ref[...]) + kpe4
        v_scr[row0:row0 + n, :] = _dot(ckvn, wv_ref[...]).astype(BF16)
        for h in range(MLA_HEADS):
            sl = slice(h * HEAD_PAD, (h + 1) * HEAD_PAD)
            kh = _head_norm(kk[:, sl], gk_ref[...])
            if rope:
                kh = _rope(kh, rc_ref[...], rs1_ref[...], rs2_ref[...])
            k_scr[row0:row0 + n, sl] = kh.astype(BF16)

    @pl.when(qi == 0)
    def _():
        if latent:
            put_kv(cckv_ref[...], ckpe_ref[...], 0, False)
        kva = pkva_ref[...].astype(F32)
        ckv = kva[:, :MLA_KV_LORA]
        ckvn = ckv * lax.rsqrt(jnp.mean(ckv * ckv, axis=-1, keepdims=True) + EPS) * gkva_ref[...]
        if not latent:
            ckv_ref[...] = ckvn
        put_kv(ckvn, pkpe_ref[...].astype(F32), n_past, latent)

    qa = pqa_ref[...].astype(F32)
    qan = qa * lax.rsqrt(jnp.mean(qa * qa, axis=-1, keepdims=True) + EPS) * gqa_ref[...]
    q = _dot(qan, wuq_ref[...])
    scale = MLA_QK ** -0.5
    for h in range(MLA_HEADS):
        sl = slice(h * HEAD_PAD, (h + 1) * HEAD_PAD)
        qh = _head_norm(q[:, sl], gq_ref[...])
        if latent:
            rows = pl.ds(pl.multiple_of(qi * tq, tq), tq)
            qh = _rope(qh, rc_ref[rows, :], rs1_ref[rows, :], rs2_ref[rows, :])
        s = _dot_nt(qh, k_scr[:, sl]) * scale
        m = jnp.max(s, axis=-1, keepdims=True)
        e = jnp.exp(s - m)
        den = jnp.sum(e, axis=-1, keepdims=True)
        o = _dot(e, v_scr[:, sl]) / den
        o_ref[:, sl] = o.astype(o_ref.dtype)


def _mla(p, l, latent, cache_ckv, cache_kpe4, W):
    if latent:
        nb, s, tq, row_off = DEC_BATCH, DEC_SEQ, 256, N_CTX_TOK
    else:
        nb, s, tq, row_off = BATCH, SEQ, 256, 0
    nq = s // tq
    sk = s + (PAST_LEN if latent else 0)
    in_specs = [
        pl.BlockSpec((tq, MLA_Q_LORA), lambda b, qi: (row_off // tq + b * nq + qi, P_QA // MLA_Q_LORA)),
        pl.BlockSpec((s, 256), lambda b, qi: (row_off // s + b, P_KVA // 256)),
        pl.BlockSpec((s, MLA_PAD_W), lambda b, qi: (row_off // s + b, P_KPE4 // MLA_PAD_W)),
    ]
    args = [p, p, p]
    if latent:
        in_specs += [
            pl.BlockSpec((None, None, PAST_LEN, MLA_KV_LORA), lambda b, qi: (b, l, 0, 0)),
            pl.BlockSpec((None, PAST_LEN, MLA_PAD_W), lambda b, qi: (b, 0, 0)),
            _const_spec((DEC_SEQ, HEAD_PAD)), _const_spec((DEC_SEQ, HEAD_PAD)), _const_spec((DEC_SEQ, HEAD_PAD)),
        ]
        args += [cache_ckv, cache_kpe4] + [jnp.asarray(t) for t in _rope_tables()]
    in_specs += [
        _const_spec((1, MLA_Q_LORA)), _const_spec((MLA_Q_LORA, MLA_PAD_W)), _const_spec((1, MLA_KV_LORA)),
        _const_spec((MLA_KV_LORA, MLA_PAD_W)), _const_spec((MLA_KV_LORA, MLA_PAD_W)),
        _const_spec((1, HEAD_PAD)), _const_spec((1, HEAD_PAD)),
    ]
    args += [W['g_qa'], W['w_uq_p'], W['g_kva'], W['w_ukv_k'], W['w_ukv_v'], W['g_mla_q_p'], W['g_mla_k_p']]
    out_shape = [jax.ShapeDtypeStruct((nb * s, MLA_PAD_W), BF16)]
    out_specs = [pl.BlockSpec((tq, MLA_PAD_W), lambda b, qi: (b * nq + qi, 0))]
    if not latent:
        out_shape.append(jax.ShapeDtypeStruct((nb, s, MLA_KV_LORA), F32))
        out_specs.append(pl.BlockSpec((None, s, MLA_KV_LORA), lambda b, qi: (b, 0, 0)))
    return pl.pallas_call(
        functools.partial(_mla_kernel, latent, tq),
        out_shape=out_shape,
        grid=(nb, nq),
        in_specs=in_specs,
        out_specs=out_specs,
        scratch_shapes=[pltpu.VMEM((sk, MLA_PAD_W), BF16), pltpu.VMEM((sk, MLA_PAD_W), BF16)],
        compiler_params=_cparams(("arbitrary", "arbitrary")),
        name="mla_latent" if latent else "mla_context",
    )(*args)


def _head_masks(width):
    lane = lax.broadcasted_iota(jnp.int32, (1, width), 1)
    return [(lane >= h * NA_HD) & (lane < (h + 1) * NA_HD) for h in range(NA_HEADS)]


def _group_norm64(x, g, masks):
    x2 = x * x
    inv = jnp.zeros_like(x)
    for m in masks:
        ms = jnp.sum(jnp.where(m, x2, 0.0), axis=-1, keepdims=True) * (1.0 / NA_HD)
        inv = jnp.where(m, lax.rsqrt(ms + EPS), inv)
    return x * inv * g


def _na_ctx_kernel(p_ref, gq_ref, gk_ref, o_ref, k_ref, v_ref):
    masks = _head_masks(BR_W)
    pna = p_ref[...].astype(F32)
    qn = _group_norm64(pna[:, :BR_W], gq_ref[...], masks)
    kn = _group_norm64(pna[:, BR_W:2 * BR_W], gk_ref[...], masks)
    v = pna[:, 2 * BR_W:]
    k_ref[...] = kn
    v_ref[...] = v
    scale = NA_HD ** -0.5
    acc = jnp.zeros((p_ref.shape[0], BR_W), F32)
    for m in masks:
        s = _dot_nt(jnp.where(m, qn, 0.0), kn) * scale
        mx = jnp.max(s, axis=-1, keepdims=True)
        e = jnp.exp(s - mx)
        den = jnp.sum(e, axis=-1, keepdims=True)
        acc = acc + jnp.where(m, _dot(e, v) / den, 0.0)
    o_ref[...] = acc.astype(o_ref.dtype)


def _na_context(p, W):
    s = SEQ
    return pl.pallas_call(
        _na_ctx_kernel,
        out_shape=[jax.ShapeDtypeStruct((N_CTX_TOK, BR_W), BF16),
                   jax.ShapeDtypeStruct((N_CTX_TOK, BR_W), F32),
                   jax.ShapeDtypeStruct((N_CTX_TOK, BR_W), F32)],
        grid=(BATCH,),
        in_specs=[pl.BlockSpec((s, 3 * BR_W), lambda b: (b, P_NA // (3 * BR_W))),
                  _const_spec((1, BR_W)), _const_spec((1, BR_W))],
        out_specs=[pl.BlockSpec((s, BR_W), lambda b: (b, 0))] * 3,
        compiler_params=_cparams(("arbitrary",)),
        name="na_context",
    )(p, W['g_na_q_t'], W['g_na_k_t'])


def _na_lat_kernel(p_ref, ck_ref, cv_ref, bias_ref, gq_ref, gk_ref, o_ref, q_scr, k_scr, v_scr, kc_scr, vc_scr):
    j = pl.program_id(1)
    masks = _head_masks(BR_W)

    @pl.when(j == 0)
    def _():
        pna = p_ref[...].astype(F32)
        q_scr[...] = _group_norm64(pna[:, :BR_W], gq_ref[...], masks).astype(BF16)
        k_scr[...] = _group_norm64(pna[:, BR_W:2 * BR_W], gk_ref[...], masks).astype(BF16)
        v_scr[...] = pna[:, 2 * BR_W:].astype(BF16)
        kc_scr[...] = ck_ref[...].astype(BF16)
        vc_scr[...] = cv_ref[...].astype(BF16)

    win0 = pl.multiple_of(jnp.where(j < 2, 0, DEC_SEQ - NA_WIN_KEYS), 256)
    q = q_scr[pl.ds(pl.multiple_of(j * NA_TQ, NA_TQ), NA_TQ), :]
    kw = k_scr[pl.ds(win0, NA_WIN_KEYS), :]
    vw = v_scr[pl.ds(win0, NA_WIN_KEYS), :]
    kc = kc_scr[...]
    vc = vc_scr[...]
    scale = NA_HD ** -0.5
    acc = jnp.zeros((NA_TQ, BR_W), F32)
    zero = jnp.zeros((), BF16)
    for h, m in enumerate(masks):
        qm = jnp.where(m, q, zero)
        s_loc = _dot_nt(qm, kw) * scale + bias_ref[h]
        s_ctx = _dot_nt(qm, kc) * scale
        mx = jnp.maximum(jnp.max(s_loc, axis=-1, keepdims=True), jnp.max(s_ctx, axis=-1, keepdims=True))
        e_loc = jnp.exp(s_loc - mx)
        e_ctx = jnp.exp(s_ctx - mx)
        den = jnp.sum(e_loc, axis=-1, keepdims=True) + jnp.sum(e_ctx, axis=-1, keepdims=True)
        o = (_dot(e_loc, vw) + _dot(e_ctx, vc)) / den
        acc = acc + jnp.where(m, o, 0.0)
    o_ref[...] = acc.astype(o_ref.dtype)


def _na_bias_table(rpb):
    n_dr, n_dc = 2 * NA_WIN_R - 1, 2 * NA_WIN_C - 1
    cols = np.arange(GRID_W)
    dc = np.clip(cols[None, :] - cols[:, None], -(NA_WIN_C - 1), NA_WIN_C - 1) + (NA_WIN_C - 1)
    cstart = np.clip(cols - NA_WIN_C // 2, 0, GRID_W - NA_WIN_C)
    col_ok = (cols[None, :] >= cstart[:, None]) & (cols[None, :] < cstart[:, None] + NA_WIN_C)
    place = (dc.reshape(-1)[None, :] == np.arange(n_dc)[:, None]).astype(np.float32)
    blocks = jnp.dot(rpb.reshape(NA_HEADS * n_dr, n_dc), place, precision=lax.Precision.HIGHEST)
    blocks = jnp.where(col_ok.reshape(-1)[None, :], blocks, -jnp.inf).reshape(NA_HEADS, n_dr, GRID_W, GRID_W)
    neg = jnp.full((NA_HEADS, 1, GRID_W, GRID_W), -jnp.inf, F32)
    blocks = jnp.concatenate([blocks, neg], axis=1)

    nt = DEC_SEQ // NA_TQ
    n_rows = DEC_SEQ // GRID_W
    rows_per_tile = NA_TQ // GRID_W
    sel = np.full((nt, rows_per_tile, NA_WIN_ROWS), n_dr, np.int32)
    for j in range(nt):
        win_row0 = 0 if j < nt // 2 else n_rows - NA_WIN_ROWS
        for rq in range(rows_per_tile):
            r = j * rows_per_tile + rq
            start = min(max(r - NA_WIN_R // 2, 0), n_rows - NA_WIN_R)
            for kr in range(NA_WIN_ROWS):
                if start <= win_row0 + kr < start + NA_WIN_R:
                    sel[j, rq, kr] = win_row0 + kr - r + (NA_WIN_R - 1)
    tiles = []
    for j in range(nt):
        rows = [jnp.concatenate([blocks[:, sel[j, rq, kr]] for kr in range(NA_WIN_ROWS)], axis=-1)
                for rq in range(rows_per_tile)]
        tiles.append(jnp.concatenate(rows, axis=1))
    return jnp.stack(tiles, axis=1)


def _na_latent(p, l, cache_k, cache_v, bias, W):
    s = DEC_SEQ
    nt = s // NA_TQ
    row_off = N_CTX_TOK
    return pl.pallas_call(
        _na_lat_kernel,
        out_shape=jax.ShapeDtypeStruct((N_LAT_TOK, BR_W), BF16),
        grid=(DEC_BATCH, nt),
        in_specs=[pl.BlockSpec((s, 3 * BR_W), lambda b, j: (row_off // s + b, P_NA // (3 * BR_W))),
                  pl.BlockSpec((None, None, PAST_LEN, BR_W), lambda b, j: (b, l, 0, 0)),
                  pl.BlockSpec((None, None, PAST_LEN, BR_W), lambda b, j: (b, l, 0, 0)),
                  pl.BlockSpec((NA_HEADS, None, NA_TQ, NA_WIN_KEYS), lambda b, j: (0, j, 0, 0)),
                  _const_spec((1, BR_W)), _const_spec((1, BR_W))],
        out_specs=pl.BlockSpec((NA_TQ, BR_W), lambda b, j: (b * nt + j, 0)),
        scratch_shapes=[pltpu.VMEM((s, BR_W), BF16), pltpu.VMEM((s, BR_W), BF16), pltpu.VMEM((s, BR_W), BF16),
                        pltpu.VMEM((PAST_LEN, BR_W), BF16), pltpu.VMEM((PAST_LEN, BR_W), BF16)],
        compiler_params=_cparams(("arbitrary", "arbitrary")),
        name="na_latent",
    )(p, cache_k, cache_v, bias, W['g_na_q_t'], W['g_na_k_t'])


CONV_HALO = 16
CONV_CHUNK = 128


def _dwconv_from_pad(pad_ref, w_ref, ksize, s, emit):
    half = ksize // 2
    for c0 in range(0, s, CONV_CHUNK):
        acc = jnp.zeros((CONV_CHUNK, BR_W), F32)
        for k in range(ksize):
            r0 = CONV_HALO + c0 + k - half
            acc = acc + pad_ref[r0:r0 + CONV_CHUNK, :] * w_ref[k:k + 1, :]
        emit(c0, acc)


def _conv_kernel(s, sc_ref, cf_ref, scw_ref, cfw_ref, cfb_ref, lng_ref, lnb_ref, ob_ref, od_ref, pad_ref):
    zeros = jnp.zeros((CONV_HALO, BR_W), F32)
    pad_ref[0:CONV_HALO, :] = zeros
    pad_ref[CONV_HALO + s:2 * CONV_HALO + s, :] = zeros

    pad_ref[CONV_HALO:CONV_HALO + s, :] = (sc_ref[:, BR_W:2 * BR_W].astype(F32)
                                           * sc_ref[:, 2 * BR_W:3 * BR_W].astype(F32))

    def emit_b(c0, acc):
        ob_ref[c0:c0 + CONV_CHUNK, :] = (sc_ref[c0:c0 + CONV_CHUNK, 0:BR_W].astype(F32) * acc).astype(ob_ref.dtype)

    _dwconv_from_pad(pad_ref, scw_ref, SC_K, s, emit_b)

    pad_ref[CONV_HALO:CONV_HALO + s, :] = (cf_ref[:, 0:BR_W].astype(F32)
                                           * jax.nn.sigmoid(cf_ref[:, BR_W:2 * BR_W].astype(F32)))

    def emit_d(c0, acc):
        u = acc + cfb_ref[...]
        mu = jnp.mean(u, axis=-1, keepdims=True)
        d = u - mu
        var = jnp.mean(d * d, axis=-1, keepdims=True)
        y = d * lax.rsqrt(var + EPS) * lng_ref[...] + lnb_ref[...]
        od_ref[c0:c0 + CONV_CHUNK, :] = _silu(y).astype(od_ref.dtype)

    _dwconv_from_pad(pad_ref, cfw_ref, CF_K, s, emit_d)


def _convs(p, latent, W):
    if latent:
        nb, s, row_off = DEC_BATCH, DEC_SEQ, N_CTX_TOK
    else:
        nb, s, row_off = BATCH, SEQ, 0
    return pl.pallas_call(
        functools.partial(_conv_kernel, s),
        out_shape=[jax.ShapeDtypeStruct((nb * s, BR_W), BF16)] * 2,
        grid=(nb,),
        in_specs=[pl.BlockSpec((s, 3 * BR_W), lambda b: (row_off // s + b, P_SC // (3 * BR_W))),
                  pl.BlockSpec((s, 2 * BR_W), lambda b: (row_off // s + b, P_CF // (2 * BR_W))),
                  _const_spec((SC_K, BR_W)), _const_spec((CF_K, BR_W)),
                  _const_spec((1, BR_W)), _const_spec((1, BR_W)), _const_spec((1, BR_W))],
        out_specs=[pl.BlockSpec((s, BR_W), lambda b: (b, 0))] * 2,
        scratch_shapes=[pltpu.VMEM((s + 2 * CONV_HALO, BR_W), F32)],
        compiler_params=_cparams(("arbitrary",)),
        name="convs_latent" if latent else "convs_context",
    )(p, p, W['sc_w'], W['cf_w'], W['cf_b'], W['cf_ln_g'], W['cf_ln_b'])


def _merge_kernel(x_ref, mod_ref, oa_ref, ob_ref, oc_ref, od_ref, g0_ref, g1_ref, g2_ref, g3_ref,
                  wa_ref, wb_ref, wo_ref, y_ref):
    merged = jax.nn.sigmoid(g0_ref[...].astype(F32)) * jnp.dot(oa_ref[...], wa_ref[...], preferred_element_type=F32)
    for n, (o_ref, g_ref) in enumerate(((ob_ref, g1_ref), (oc_ref, g2_ref), (od_ref, g3_ref))):
        merged = merged + (jax.nn.sigmoid(g_ref[...].astype(F32))
                           * jnp.dot(o_ref[...], wb_ref[n], preferred_element_type=F32))
    y_ref[...] = x_ref[...] + mod_ref[2:3, :] * _dot(merged, wo_ref[...])


def _merge(x_all, mod3, p, o_a, o_b, o_c, o_d, W):
    tm = 512
    row = lambda i: (i, 0)
    gate_spec = lambda n: pl.BlockSpec((tm, D_MODEL), lambda i: (i, P_GATE // D_MODEL + n))
    return pl.pallas_call(
        _merge_kernel,
        out_shape=jax.ShapeDtypeStruct((N_TOK, D_MODEL), F32),
        grid=(N_TOK // tm,),
        in_specs=[pl.BlockSpec((tm, D_MODEL), row),
                  pl.BlockSpec((None, 6, D_MODEL), lambda i: (_mod_row(i * tm), 0, 0)),
                  pl.BlockSpec((tm, MLA_PAD_W), row),
                  pl.BlockSpec((tm, BR_W), row), pl.BlockSpec((tm, BR_W), row), pl.BlockSpec((tm, BR_W), row),
                  gate_spec(0), gate_spec(1), gate_spec(2), gate_spec(3),
                  _const_spec((MLA_PAD_W, D_MODEL)), _const_spec((3, BR_W, D_MODEL)),
                  _const_spec((D_MODEL, D_MODEL))],
        out_specs=pl.BlockSpec((tm, D_MODEL), row),
        compiler_params=_cparams(("arbitrary",)),
        name="merge",
    )(x_all, mod3, o_a, o_b, o_c, o_d, p, p, p, p, W['w_br_a'], W['w_br_bcd'], W['w_o'])


def _norm2(x, g, mod_ref):
    y = x * lax.rsqrt(jnp.mean(x * x, axis=-1, keepdims=True) + EPS) * g
    return y * (1.0 + mod_ref[4:5, :]) + mod_ref[3:4, :]


def _ffn_kernel(x_ref, mod_ref, g_ref, wg_ref, wu_ref, wd_ref, y_ref, h_scr, acc_scr):
    f = pl.program_id(1)

    @pl.when(f == 0)
    def _():
        h_scr[...] = _norm2(x_ref[...], g_ref[...], mod_ref).astype(BF16)
        acc_scr[...] = jnp.zeros_like(acc_scr)

    h = h_scr[...]
    a = _silu(_dot(h, wg_ref[...])) * _dot(h, wu_ref[...])
    acc_scr[...] += _dot(a, wd_ref[...])

    @pl.when(f == pl.num_programs(1) - 1)
    def _():
        y_ref[...] = x_ref[...] + mod_ref[5:6, :] * acc_scr[...]


def _ffn(x_all, mod3, g_norm2, w_g, w_u, w_d, l, j):
    tm, tf = 1024, 512
    return pl.pallas_call(
        _ffn_kernel,
        out_shape=jax.ShapeDtypeStruct((N_TOK, D_MODEL), F32),
        grid=(N_TOK // tm, D_FF // tf),
        in_specs=[pl.BlockSpec((tm, D_MODEL), lambda i, f: (i, 0)),
                  pl.BlockSpec((None, 6, D_MODEL), lambda i, f: (_mod_row(i * tm), 0, 0)),
                  pl.BlockSpec((None, 1, D_MODEL), lambda i, f: (l, 0, 0)),
                  pl.BlockSpec((None, D_MODEL, tf), lambda i, f: (j, 0, f)),
                  pl.BlockSpec((None, D_MODEL, tf), lambda i, f: (j, 0, f)),
                  pl.BlockSpec((None, tf, D_MODEL), lambda i, f: (j, f, 0))],
        out_specs=pl.BlockSpec((tm, D_MODEL), lambda i, f: (i, 0)),
        scratch_shapes=[pltpu.VMEM((tm, D_MODEL), BF16), pltpu.VMEM((tm, D_MODEL), F32)],
        compiler_params=_cparams(("arbitrary", "arbitrary")),
        name="ffn_dense",
    )(x_all, mod3, g_norm2.reshape(DEPTH, 1, D_MODEL), w_g, w_u, w_d)


ROUTER_LANES = 128


def _route(h, wr):
    logits = jnp.dot(h, wr, preferred_element_type=F32, precision=lax.Precision.HIGHEST)
    lane = lax.broadcasted_iota(jnp.int32, logits.shape, 1).astype(F32)
    neg = jnp.float32(-jnp.inf)
    logits = jnp.where(lane < N_EXPERTS, logits, neg)
    m1 = jnp.max(logits, axis=-1, keepdims=True)
    i1 = jnp.min(jnp.where(logits == m1, lane, float(ROUTER_LANES)), axis=-1, keepdims=True)
    rest = jnp.where(lane == i1, neg, logits)
    m2 = jnp.max(rest, axis=-1, keepdims=True)
    i2 = jnp.min(jnp.where(rest == m2, lane, float(ROUTER_LANES)), axis=-1, keepdims=True)
    e2 = jnp.exp(m2 - m1)
    w1 = 1.0 / (1.0 + e2)
    w2 = e2 / (1.0 + e2)
    return jnp.where(lane == i1, w1, 0.0) + jnp.where(lane == i2, w2, 0.0)


def _moe_kernel(x_ref, mod_ref, g_ref, wr_ref, wg_ref, wu_ref, wd_ref, y_ref, h_scr, acc_scr, gate_scr):
    e = pl.program_id(1)
    f = pl.program_id(2)

    @pl.when((e == 0) & (f == 0))
    def _():
        h = _norm2(x_ref[...], g_ref[...], mod_ref)
        h_scr[...] = h.astype(BF16)
        gate_scr[...] = _route(h, wr_ref[...])
        acc_scr[...] = jnp.zeros_like(acc_scr)

    h = h_scr[...]
    a = _silu(_dot(h, wg_ref[...])) * _dot(h, wu_ref[...])
    lane = lax.broadcasted_iota(jnp.int32, gate_scr.shape, 1)
    ge = jnp.sum(jnp.where(lane == e, gate_scr[...], 0.0), axis=-1, keepdims=True)
    acc_scr[...] += ge * _dot(a, wd_ref[...])

    @pl.when((e == pl.num_programs(1) - 1) & (f == pl.num_programs(2) - 1))
    def _():
        y_ref[...] = x_ref[...] + mod_ref[5:6, :] * acc_scr[...]


def _moe(x_all, mod3, g_norm2, w_router_p, w_g, w_u, w_d, l, j):
    tm, tf = 1024, 512
    return pl.pallas_call(
        _moe_kernel,
        out_shape=jax.ShapeDtypeStruct((N_TOK, D_MODEL), F32),
        grid=(N_TOK // tm, N_EXPERTS, D_FF // tf),
        in_specs=[pl.BlockSpec((tm, D_MODEL), lambda i, e, f: (i, 0)),
                  pl.BlockSpec((None, 6, D_MODEL), lambda i, e, f: (_mod_row(i * tm), 0, 0)),
                  pl.BlockSpec((None, 1, D_MODEL), lambda i, e, f: (l, 0, 0)),
                  pl.BlockSpec((None, D_MODEL, ROUTER_LANES), lambda i, e, f: (j, 0, 0)),
                  pl.BlockSpec((None, None, D_MODEL, tf), lambda i, e, f: (j, e, 0, f)),
                  pl.BlockSpec((None, None, D_MODEL, tf), lambda i, e, f: (j, e, 0, f)),
                  pl.BlockSpec((None, None, tf, D_MODEL), lambda i, e, f: (j, e, f, 0))],
        out_specs=pl.BlockSpec((tm, D_MODEL), lambda i, e, f: (i, 0)),
        scratch_shapes=[pltpu.VMEM((tm, D_MODEL), BF16), pltpu.VMEM((tm, D_MODEL), F32),
                        pltpu.VMEM((tm, ROUTER_LANES), F32)],
        compiler_params=_cparams(("arbitrary", "arbitrary", "arbitrary")),
        name="moe_dense",
    )(x_all, mod3, g_norm2.reshape(DEPTH, 1, D_MODEL), w_router_p, w_g, w_u, w_d)


def _pad_heads(w, per_head, lo):
    lead = w.shape[:-1]
    w = w.reshape(lead + (MLA_HEADS, per_head))
    w = jnp.pad(w, [(0, 0)] * len(lead) + [(0, 0), (lo, HEAD_PAD - lo - per_head)])
    return w.reshape(lead + (MLA_PAD_W,))


def _kpe_to_heads(kpe):
    z = jnp.pad(kpe, [(0, 0)] * (kpe.ndim - 1) + [(MLA_NOPE, HEAD_PAD - MLA_QK)])
    return jnp.tile(z, (1,) * (kpe.ndim - 1) + (MLA_HEADS,))


def _layer_weights(l, w_in, g_qa, w_uq, g_kva, w_ukv, g_mla_q, g_mla_k, sc_w, g_na_q, g_na_k,
                   cf_w, cf_b, cf_ln_g, cf_ln_b, w_br, w_o):
    wi = w_in[l]
    s1 = MLA_Q_LORA
    s2 = s1 + MLA_KV_LORA
    s3 = s2 + MLA_ROPE
    s4 = s3 + 3 * BR_W
    s5 = s4 + 3 * BR_W
    s6 = s5 + 2 * BR_W
    w_kpe = wi[:, s2:s3]
    w_in_p = jnp.concatenate([
        wi[:, :s3], jnp.zeros((D_MODEL, 256 - MLA_KV_LORA - MLA_ROPE), F32),
        _kpe_to_heads(w_kpe), wi[:, s5:s6], wi[:, s3:s4], wi[:, s4:s5], wi[:, s6:]], axis=1).astype(BF16)
    ukv = w_ukv[l].reshape(MLA_KV_LORA, MLA_HEADS, MLA_NOPE + MLA_V)
    W = dict(
        w_in_p=w_in_p,
        g_qa=g_qa[l][None], g_kva=g_kva[l][None],
        w_uq_p=_pad_heads(w_uq[l], MLA_QK, 0).astype(BF16),
        w_ukv_k=_pad_heads(ukv[:, :, :MLA_NOPE].reshape(MLA_KV_LORA, -1), MLA_NOPE, 0).astype(BF16),
        w_ukv_v=_pad_heads(ukv[:, :, MLA_NOPE:].reshape(MLA_KV_LORA, -1), MLA_V, 0).astype(BF16),
        g_mla_q_p=jnp.pad(g_mla_q[l], (0, HEAD_PAD - MLA_QK))[None],
        g_mla_k_p=jnp.pad(g_mla_k[l], (0, HEAD_PAD - MLA_QK))[None],
        g_na_q_t=jnp.tile(g_na_q[l], NA_HEADS)[None], g_na_k_t=jnp.tile(g_na_k[l], NA_HEADS)[None],
        sc_w=sc_w[l], cf_w=cf_w[l], cf_b=cf_b[l][None], cf_ln_g=cf_ln_g[l][None], cf_ln_b=cf_ln_b[l][None],
        w_br_a=jnp.pad(w_br[l, 0].reshape(MLA_HEADS, MLA_V, D_MODEL),
                       ((0, 0), (0, HEAD_PAD - MLA_V), (0, 0))).reshape(MLA_PAD_W, D_MODEL).astype(BF16),
        w_br_bcd=w_br[l, 1:].astype(BF16),
        w_o=w_o[l].astype(BF16),
    )
    return W


def kernel(x_prompt, x_sample, cache_mla_ckv, cache_mla_kpe, cache_na_k, cache_na_v, c, c_ctx, w_ada, b_ada, g_norm1, w_in, g_qa, w_uq, g_kva, w_ukv, g_mla_q, g_mla_k, sc_w, g_na_q, g_na_k, na_rpb, cf_w, cf_b, cf_ln_g, cf_ln_b, w_br, w_o, g_norm2, w_ff_gate, w_ff_up, w_ff_down, w_router, w_e_gate, w_e_up, w_e_down):
    x_all = jnp.concatenate([x_prompt.reshape(N_CTX_TOK, D_MODEL), x_sample.reshape(N_LAT_TOK, D_MODEL)], axis=0)
    cvec = jnp.concatenate([c_ctx[None, :], c, jnp.zeros((MOD_ROWS - 1 - DEC_BATCH, D_MODEL), F32)], axis=0)
    cache_na_k2 = cache_na_k.reshape(DEC_BATCH, DEPTH, PAST_LEN, BR_W)
    cache_na_v2 = cache_na_v.reshape(DEC_BATCH, DEPTH, PAST_LEN, BR_W)
    w_router_p = jnp.pad(w_router, ((0, 0), (0, 0), (0, ROUTER_LANES - N_EXPERTS)))

    ckv_l, kpe_l, nak_l, nav_l = [], [], [], []
    for l in range(DEPTH):
        W = _layer_weights(l, w_in, g_qa, w_uq, g_kva, w_ukv, g_mla_q, g_mla_k, sc_w, g_na_q, g_na_k,
                           cf_w, cf_b, cf_ln_g, cf_ln_b, w_br, w_o)
        mod3 = _modulation(cvec, w_ada, b_ada, l).reshape(MOD_ROWS, 6, D_MODEL)
        p = _in_projection(x_all, mod3, g_norm1, W['w_in_p'], l)

        oa_c, ckv_new = _mla(p, l, False, None, None, W)
        (oa_l,) = _mla(p, l, True, cache_mla_ckv, _kpe_to_heads(cache_mla_kpe[:, l]), W)
        oc_c, nak_new, nav_new = _na_context(p, W)
        oc_l = _na_latent(p, l, cache_na_k2, cache_na_v2, _na_bias_table(na_rpb[l]), W)
        ob_c, od_c = _convs(p, False, W)
        ob_l, od_l = _convs(p, True, W)

        cat = lambda a, b: jnp.concatenate([a, b], axis=0)
        x_all = _merge(x_all, mod3, p, cat(oa_c, oa_l), cat(ob_c, ob_l), cat(oc_c, oc_l), cat(od_c, od_l), W)

        if l % 2 == 0:
            x_all = _ffn(x_all, mod3, g_norm2, w_ff_gate, w_ff_up, w_ff_down, l, l // 2)
        else:
            x_all = _moe(x_all, mod3, g_norm2, w_router_p, w_e_gate, w_e_up, w_e_down, l, l // 2)

        ckv_l.append(ckv_new)
        kpe_l.append(p[:N_CTX_TOK, P_KVA + MLA_KV_LORA:P_KVA + MLA_KV_LORA + MLA_ROPE]
                     .astype(F32).reshape(BATCH, SEQ, MLA_ROPE))
        nak_l.append(nak_new.reshape(BATCH, SEQ, NA_HEADS, NA_HD))
        nav_l.append(nav_new.reshape(BATCH, SEQ, NA_HEADS, NA_HD))

    y_prompt = x_all[:N_CTX_TOK].reshape(BATCH, SEQ, D_MODEL)
    y_sample = x_all[N_CTX_TOK:].reshape(DEC_BATCH, DEC_SEQ, D_MODEL)
    return (y_prompt, y_sample, jnp.stack(ckv_l, axis=1), jnp.stack(kpe_l, axis=1),
            jnp.stack(nak_l, axis=1), jnp.stack(nav_l, axis=1))
```

```python
import functools

import numpy as np
import jax
import jax.numpy as jnp
from jax import lax
from jax.experimental import pallas as pl
from jax.experimental.pallas import tpu as pltpu

F32 = jnp.float32
BF16 = jnp.bfloat16

D_MODEL = 1024
BATCH = 16
SEQ = 256
DEPTH = 2
DEC_BATCH = 8
DEC_SEQ = 1024
PAST_LEN = 512
GRID_W = 64
N_BRANCH = 4
BR_W = 256
MLA_HEADS = 4
MLA_NOPE = 64
MLA_ROPE = 32
MLA_QK = 96
MLA_V = 64
MLA_Q_LORA = 256
MLA_KV_LORA = 128
SC_K = 3
NA_HEADS = 4
NA_HD = 64
NA_WIN_R = 8
NA_WIN_C = 16
CF_K = 31
D_FF = 3584
N_EXPERTS = 8
ROPE_THETA = 10000.0
EPS = 1e-6

N_CTX_TOK = BATCH * SEQ
N_LAT_TOK = DEC_BATCH * DEC_SEQ
N_TOK = N_CTX_TOK + N_LAT_TOK
MOD_ROWS = 16

HEAD_PAD = 128
MLA_PAD_W = MLA_HEADS * HEAD_PAD

P_QA = 0
P_KVA = 256
P_KPE4 = 512
P_CF = 1024
P_SC = 1536
P_NA = 2304
P_GATE = 3072
P_COLS = 7168

VMEM_LIMIT = 56 * 1024 * 1024

NA_TQ = 256
NA_WIN_ROWS = 12
NA_WIN_KEYS = NA_WIN_ROWS * GRID_W


def _cparams(sem):
    return pltpu.CompilerParams(dimension_semantics=sem, vmem_limit_bytes=VMEM_LIMIT)


def _const_spec(shape):
    nd = len(shape)
    return pl.BlockSpec(shape, lambda *_: (0,) * nd)


def _mod_row(tok_start):
    return jnp.where(tok_start < N_CTX_TOK, 0, 1 + (tok_start - N_CTX_TOK) // DEC_SEQ)


def _silu(x):
    return x * jax.nn.sigmoid(x)


def _dot(a, b):
    return jnp.dot(a.astype(BF16), b.astype(BF16), preferred_element_type=F32)


def _dot_nt(a, b):
    return lax.dot_general(a.astype(BF16), b.astype(BF16), (((1,), (1,)), ((), ())),
                           preferred_element_type=F32)


def _mod_kernel(c_ref, w_ref, b_ref, o_ref):
    o_ref[...] = _dot(_silu(c_ref[...]), w_ref[...]) + b_ref[...]


def _modulation(cvec, w_ada, b_ada, l):
    tn = 1024
    return pl.pallas_call(
        _mod_kernel,
        out_shape=jax.ShapeDtypeStruct((MOD_ROWS, 6 * D_MODEL), F32),
        grid=(6 * D_MODEL // tn,),
        in_specs=[
            _const_spec((MOD_ROWS, D_MODEL)),
            pl.BlockSpec((None, D_MODEL, tn), lambda j: (l, 0, j)),
            pl.BlockSpec((None, 1, tn), lambda j: (l, 0, j)),
        ],
        out_specs=pl.BlockSpec((MOD_ROWS, tn), lambda j: (0, j)),
        compiler_params=_cparams(("arbitrary",)),
        name="modulation",
    )(cvec, w_ada, b_ada.reshape(DEPTH, 1, 6 * D_MODEL))


def _inproj_kernel(x_ref, mod_ref, g_ref, w_ref, o_ref, h_scr):
    @pl.when(pl.program_id(1) == 0)
    def _():
        x = x_ref[...]
        y = x * lax.rsqrt(jnp.mean(x * x, axis=-1, keepdims=True) + EPS) * g_ref[...]
        h_scr[...] = (y * (1.0 + mod_ref[1:2, :]) + mod_ref[0:1, :]).astype(BF16)

    o_ref[...] = jnp.dot(h_scr[...], w_ref[...], preferred_element_type=F32).astype(o_ref.dtype)


def _in_projection(x_all, mod3, g_norm1, w_in_p, l):
    tm, tn = 512, 1024
    return pl.pallas_call(
        _inproj_kernel,
        out_shape=jax.ShapeDtypeStruct((N_TOK, P_COLS), BF16),
        grid=(N_TOK // tm, P_COLS // tn),
        in_specs=[
            pl.BlockSpec((tm, D_MODEL), lambda i, j: (i, 0)),
            pl.BlockSpec((None, 6, D_MODEL), lambda i, j: (_mod_row(i * tm), 0, 0)),
            pl.BlockSpec((None, 1, D_MODEL), lambda i, j: (l, 0, 0)),
            pl.BlockSpec((D_MODEL, tn), lambda i, j: (0, j)),
        ],
        out_specs=pl.BlockSpec((tm, tn), lambda i, j: (i, j)),
        scratch_shapes=[pltpu.VMEM((tm, D_MODEL), BF16)],
        compiler_params=_cparams(("arbitrary", "arbitrary")),
        name="in_projection",
    )(x_all, mod3, g_norm1.reshape(DEPTH, 1, D_MODEL), w_in_p)


def _rope_tables():
    t = np.arange(DEC_SEQ)
    nf = MLA_ROPE // 4
    inv = (np.float32(ROPE_THETA) ** (-np.arange(nf, dtype=np.float32) / np.float32(nf))).astype(np.float32)
    ang_r = (t // GRID_W).astype(np.float32)[:, None] * inv[None, :]
    ang_c = (t % GRID_W).astype(np.float32)[:, None] * inv[None, :]
    c = np.zeros((DEC_SEQ, HEAD_PAD), np.float32)
    s1 = np.zeros((DEC_SEQ, HEAD_PAD), np.float32)
    s2 = np.zeros((DEC_SEQ, HEAD_PAD), np.float32)
    c[:, :MLA_NOPE] = 1.0
    for base, ang in ((MLA_NOPE, ang_r), (MLA_NOPE + 2 * nf, ang_c)):
        c[:, base:base + nf] = np.cos(ang)
        c[:, base + nf:base + 2 * nf] = np.cos(ang)
        s1[:, base:base + nf] = -np.sin(ang)
        s2[:, base + nf:base + 2 * nf] = np.sin(ang)
    return c, s1, s2


def _rope(x, c, s1, s2):
    nf = MLA_ROPE // 4
    return x * c + pltpu.roll(x, HEAD_PAD - nf, 1) * s1 + pltpu.roll(x, nf, 1) * s2


def _head_norm(xh, g):
    ms = jnp.sum(xh * xh, axis=-1, keepdims=True) * (1.0 / MLA_QK)
    return xh * lax.rsqrt(ms + EPS) * g


def _mla_kernel(latent, tq, *refs):
    if latent:
        (pqa_ref, pkva_ref, pkpe_ref, cckv_ref, ckpe_ref, rc_ref, rs1_ref, rs2_ref,
         gqa_ref, wuq_ref, gkva_ref, wk_ref, wv_ref, gq_ref, gk_ref,
         o_ref, k_scr, v_scr) = refs
    else:
        (pqa_ref, pkva_ref, pkpe_ref,
         gqa_ref, wuq_ref, gkva_ref, wk_ref, wv_ref, gq_ref, gk_ref,
         o_ref, ckv_ref, k_scr, v_scr) = refs
    qi = pl.program_id(1)
    n_past = PAST_LEN if latent else 0

    def put_kv(ckvn, kpe4, row0, rope):
        n = ckvn.shape[0]
        kk = _dot(ckvn, wk_ref[...]) + kpe4
        v_scr[row0:row0 + n, :] = _dot(ckvn, wv_ref[...]).astype(BF16)
        for h in range(MLA_HEADS):
            sl = slice(h * HEAD_PAD, (h + 1) * HEAD_PAD)
            kh = _head_norm(kk[:, sl], gk_ref[...])
            if rope:
                kh = _rope(kh, rc_ref[...], rs1_ref[...], rs2_ref[...])
            k_scr[row0:row0 + n, sl] = kh.astype(BF16)

    @pl.when(qi == 0)
    def _():
        if latent:
            put_kv(cckv_ref[...], ckpe_ref[...], 0, False)
        kva = pkva_ref[...].astype(F32)
        ckv = kva[:, :MLA_KV_LORA]
        ckvn = ckv * lax.rsqrt(jnp.mean(ckv * ckv, axis=-1, keepdims=True) + EPS) * gkva_ref[...]
        if not latent:
            ckv_ref[...] = ckvn
        put_kv(ckvn, pkpe_ref[...].astype(F32), n_past, latent)

    qa = pqa_ref[...].astype(F32)
    qan = qa * lax.rsqrt(jnp.mean(qa * qa, axis=-1, keepdims=True) + EPS) * gqa_ref[...]
    q = _dot(qan, wuq_ref[...])
    scale = MLA_QK ** -0.5
    for h in range(MLA_HEADS):
        sl = slice(h * HEAD_PAD, (h + 1) * HEAD_PAD)
        qh = _head_norm(q[:, sl], gq_ref[...])
        if latent:
            rows = pl.ds(pl.multiple_of(qi * tq, tq), tq)
            qh = _rope(qh, rc_ref[rows, :], rs1_ref[rows, :], rs2_ref[rows, :])
        s = _dot_nt(qh, k_scr[:, sl]) * scale
        m = jnp.max(s, axis=-1, keepdims=True)
        e = jnp.exp(s - m)
        den = jnp.sum(e, axis=-1, keepdims=True)
        o = _dot(e, v_scr[:, sl]) / den
        o_ref[:, sl] = o.astype(o_ref.dtype)


def _mla(p, l, latent, cache_ckv, cache_kpe4, W):
    if latent:
        nb, s, tq, row_off = DEC_BATCH, DEC_SEQ, 256, N_CTX_TOK
    else:
        nb, s, tq, row_off = BATCH, SEQ, 256, 0
    nq = s // tq
    sk = s + (PAST_LEN if latent else 0)
    in_specs = [
        pl.BlockSpec((tq, MLA_Q_LORA), lambda b, qi: (row_off // tq + b * nq + qi, P_QA // MLA_Q_LORA)),
        pl.BlockSpec((s, 256), lambda b, qi: (row_off // s + b, P_KVA // 256)),
        pl.BlockSpec((s, MLA_PAD_W), lambda b, qi: (row_off // s + b, P_KPE4 // MLA_PAD_W)),
    ]
    args = [p, p, p]
    if latent:
        in_specs += [
            pl.BlockSpec((None, None, PAST_LEN, MLA_KV_LORA), lambda b, qi: (b, l, 0, 0)),
            pl.BlockSpec((None, PAST_LEN, MLA_PAD_W), lambda b, qi: (b, 0, 0)),
            _const_spec((DEC_SEQ, HEAD_PAD)), _const_spec((DEC_SEQ, HEAD_PAD)), _const_spec((DEC_SEQ, HEAD_PAD)),
        ]
        args += [cache_ckv, cache_kpe4] + [jnp.asarray(t) for t in _rope_tables()]
    in_specs += [
        _const_spec((1, MLA_Q_LORA)), _const_spec((MLA_Q_LORA, MLA_PAD_W)), _const_spec((1, MLA_KV_LORA)),
        _const_spec((MLA_KV_LORA, MLA_PAD_W)), _const_spec((MLA_KV_LORA, MLA_PAD_W)),
        _const_spec((1, HEAD_PAD)), _const_spec((1, HEAD_PAD)),
    ]
    args += [W['g_qa'], W['w_uq_p'], W['g_kva'], W['w_ukv_k'], W['w_ukv_v'], W['g_mla_q_p'], W['g_mla_k_p']]
    out_shape = [jax.ShapeDtypeStruct((nb * s, MLA_PAD_W), BF16)]
    out_specs = [pl.BlockSpec((tq, MLA_PAD_W), lambda b, qi: (b * nq + qi, 0))]
    if not latent:
        out_shape.append(jax.ShapeDtypeStruct((nb, s, MLA_KV_LORA), F32))
        out_specs.append(pl.BlockSpec((None, s, MLA_KV_LORA), lambda b, qi: (b, 0, 0)))
    return pl.pallas_call(
        functools.partial(_mla_kernel, latent, tq),
        out_shape=out_shape,
        grid=(nb, nq),
        in_specs=in_specs,
        out_specs=out_specs,
        scratch_shapes=[pltpu.VMEM((sk, MLA_PAD_W), BF16), pltpu.VMEM((sk, MLA_PAD_W), BF16)],
        compiler_params=_cparams(("arbitrary", "arbitrary")),
        name="mla_latent" if latent else "mla_context",
    )(*args)


def _head_masks(width):
    lane = lax.broadcasted_iota(jnp.int32, (1, width), 1)
    return [(lane >= h * NA_HD) & (lane < (h + 1) * NA_HD) for h in range(NA_HEADS)]


def _group_norm64(x, g, masks):
    x2 = x * x
    inv = jnp.zeros_like(x)
    for m in masks:
        ms = jnp.sum(jnp.where(m, x2, 0.0), axis=-1, keepdims=True) * (1.0 / NA_HD)
        inv = jnp.where(m, lax.rsqrt(ms + EPS), inv)
    return x * inv * g


def _na_ctx_kernel(p_ref, gq_ref, gk_ref, o_ref, k_ref, v_ref):
    masks = _head_masks(BR_W)
    pna = p_ref[...].astype(F32)
    qn = _group_norm64(pna[:, :BR_W], gq_ref[...], masks)
    kn = _group_norm64(pna[:, BR_W:2 * BR_W], gk_ref[...], masks)
    v = pna[:, 2 * BR_W:]
    k_ref[...] = kn
    v_ref[...] = v
    scale = NA_HD ** -0.5
    acc = jnp.zeros((p_ref.shape[0], BR_W), F32)
    for m in masks:
        s = _dot_nt(jnp.where(m, qn, 0.0), kn) * scale
        mx = jnp.max(s, axis=-1, keepdims=True)
        e = jnp.exp(s - mx)
        den = jnp.sum(e, axis=-1, keepdims=True)
        acc = acc + jnp.where(m, _dot(e, v) / den, 0.0)
    o_ref[...] = acc.astype(o_ref.dtype)


def _na_context(p, W):
    s = SEQ
    return pl.pallas_call(
        _na_ctx_kernel,
        out_shape=[jax.ShapeDtypeStruct((N_CTX_TOK, BR_W), BF16),
                   jax.ShapeDtypeStruct((N_CTX_TOK, BR_W), F32),
                   jax.ShapeDtypeStruct((N_CTX_TOK, BR_W), F32)],
        grid=(BATCH,),
        in_specs=[pl.BlockSpec((s, 3 * BR_W), lambda b: (b, P_NA // (3 * BR_W))),
                  _const_spec((1, BR_W)), _const_spec((1, BR_W))],
        out_specs=[pl.BlockSpec((s, BR_W), lambda b: (b, 0))] * 3,
        compiler_params=_cparams(("arbitrary",)),
        name="na_context",
    )(p, W['g_na_q_t'], W['g_na_k_t'])


def _na_lat_kernel(p_ref, ck_ref, cv_ref, bias_ref, gq_ref, gk_ref, o_ref, q_scr, k_scr, v_scr, kc_scr, vc_scr):
    j = pl.program_id(1)
    masks = _head_masks(BR_W)

    @pl.when(j == 0)
    def _():
        pna = p_ref[...].astype(F32)
        q_scr[...] = _group_norm64(pna[:, :BR_W], gq_ref[...], masks).astype(BF16)
        k_scr[...] = _group_norm64(pna[:, BR_W:2 * BR_W], gk_ref[...], masks).astype(BF16)
        v_scr[...] = pna[:, 2 * BR_W:].astype(BF16)
        kc_scr[...] = ck_ref[...].astype(BF16)
        vc_scr[...] = cv_ref[...].astype(BF16)

    win0 = pl.multiple_of(jnp.where(j < 2, 0, DEC_SEQ - NA_WIN_KEYS), 256)
    q = q_scr[pl.ds(pl.multiple_of(j * NA_TQ, NA_TQ), NA_TQ), :]
    kw = k_scr[pl.ds(win0, NA_WIN_KEYS), :]
    vw = v_scr[pl.ds(win0, NA_WIN_KEYS), :]
    kc = kc_scr[...]
    vc = vc_scr[...]
    scale = NA_HD ** -0.5
    acc = jnp.zeros((NA_TQ, BR_W), F32)
    zero = jnp.zeros((), BF16)
    for h, m in enumerate(masks):
        qm = jnp.where(m, q, zero)
        s_loc = _dot_nt(qm, kw) * scale + bias_ref[h]
        s_ctx = _dot_nt(qm, kc) * scale
        mx = jnp.maximum(jnp.max(s_loc, axis=-1, keepdims=True), jnp.max(s_ctx, axis=-1, keepdims=True))
        e_loc = jnp.exp(s_loc - mx)
        e_ctx = jnp.exp(s_ctx - mx)
        den = jnp.sum(e_loc, axis=-1, keepdims=True) + jnp.sum(e_ctx, axis=-1, keepdims=True)
        o = (_dot(e_loc, vw) + _dot(e_ctx, vc)) / den
        acc = acc + jnp.where(m, o, 0.0)
    o_ref[...] = acc.astype(o_ref.dtype)


def _na_bias_table(rpb):
    n_dr, n_dc = 2 * NA_WIN_R - 1, 2 * NA_WIN_C - 1
    cols = np.arange(GRID_W)
    dc = np.clip(cols[None, :] - cols[:, None], -(NA_WIN_C - 1), NA_WIN_C - 1) + (NA_WIN_C - 1)
    cstart = np.clip(cols - NA_WIN_C // 2, 0, GRID_W - NA_WIN_C)
    col_ok = (cols[None, :] >= cstart[:, None]) & (cols[None, :] < cstart[:, None] + NA_WIN_C)
    place = (dc.reshape(-1)[None, :] == np.arange(n_dc)[:, None]).astype(np.float32)
    blocks = jnp.dot(rpb.reshape(NA_HEADS * n_dr, n_dc), place, precision=lax.Precision.HIGHEST)
    blocks = jnp.where(col_ok.reshape(-1)[None, :], blocks, -jnp.inf).reshape(NA_HEADS, n_dr, GRID_W, GRID_W)
    neg = jnp.full((NA_HEADS, 1, GRID_W, GRID_W), -jnp.inf, F32)
    blocks = jnp.concatenate([blocks, neg], axis=1)

    nt = DEC_SEQ // NA_TQ
    n_rows = DEC_SEQ // GRID_W
    rows_per_tile = NA_TQ // GRID_W
    sel = np.full((nt, rows_per_tile, NA_WIN_ROWS), n_dr, np.int32)
    for j in range(nt):
        win_row0 = 0 if j < nt // 2 else n_rows - NA_WIN_ROWS
        for rq in range(rows_per_tile):
            r = j * rows_per_tile + rq
            start = min(max(r - NA_WIN_R // 2, 0), n_rows - NA_WIN_R)
            for kr in range(NA_WIN_ROWS):
                if start <= win_row0 + kr < start + NA_WIN_R:
                    sel[j, rq, kr] = win_row0 + kr - r + (NA_WIN_R - 1)
    tiles = []
    for j in range(nt):
        rows = [jnp.concatenate([blocks[:, sel[j, rq, kr]] for kr in range(NA_WIN_ROWS)], axis=-1)
                for rq in range(rows_per_tile)]
        tiles.append(jnp.concatenate(rows, axis=1))
    return jnp.stack(tiles, axis=1)


def _na_latent(p, l, cache_k, cache_v, bias, W):
    s = DEC_SEQ
    nt = s // NA_TQ
    row_off = N_CTX_TOK
    return pl.pallas_call(
        _na_lat_kernel,
        out_shape=jax.ShapeDtypeStruct((N_LAT_TOK, BR_W), BF16),
        grid=(DEC_BATCH, nt),
        in_specs=[pl.BlockSpec((s, 3 * BR_W), lambda b, j: (row_off // s + b, P_NA // (3 * BR_W))),
                  pl.BlockSpec((None, None, PAST_LEN, BR_W), lambda b, j: (b, l, 0, 0)),
                  pl.BlockSpec((None, None, PAST_LEN, BR_W), lambda b, j: (b, l, 0, 0)),
                  pl.BlockSpec((NA_HEADS, None, NA_TQ, NA_WIN_KEYS), lambda b, j: (0, j, 0, 0)),
                  _const_spec((1, BR_W)), _const_spec((1, BR_W))],
        out_specs=pl.BlockSpec((NA_TQ, BR_W), lambda b, j: (b * nt + j, 0)),
        scratch_shapes=[pltpu.VMEM((s, BR_W), BF16), pltpu.VMEM((s, BR_W), BF16), pltpu.VMEM((s, BR_W), BF16),
                        pltpu.VMEM((PAST_LEN, BR_W), BF16), pltpu.VMEM((PAST_LEN, BR_W), BF16)],
        compiler_params=_cparams(("arbitrary", "arbitrary")),
        name="na_latent",
    )(p, cache_k, cache_v, bias, W['g_na_q_t'], W['g_na_k_t'])


CONV_HALO = 16
CONV_CHUNK = 128


def _dwconv_from_pad(pad_ref, w_ref, ksize, s, emit):
    half = ksize // 2
    for c0 in range(0, s, CONV_CHUNK):
        acc = jnp.zeros((CONV_CHUNK, BR_W), F32)
        for k in range(ksize):
            r0 = CONV_HALO + c0 + k - half
            acc = acc + pad_ref[r0:r0 + CONV_CHUNK, :] * w_ref[k:k + 1, :]
        emit(c0, acc)


def _conv_kernel(s, sc_ref, cf_ref, scw_ref, cfw_ref, cfb_ref, lng_ref, lnb_ref, ob_ref, od_ref, pad_ref):
    zeros = jnp.zeros((CONV_HALO, BR_W), F32)
    pad_ref[0:CONV_HALO, :] = zeros
    pad_ref[CONV_HALO + s:2 * CONV_HALO + s, :] = zeros

    pad_ref[CONV_HALO:CONV_HALO + s, :] = (sc_ref[:, BR_W:2 * BR_W].astype(F32)
                                           * sc_ref[:, 2 * BR_W:3 * BR_W].astype(F32))

    def emit_b(c0, acc):
        ob_ref[c0:c0 + CONV_CHUNK, :] = (sc_ref[c0:c0 + CONV_CHUNK, 0:BR_W].astype(F32) * acc).astype(ob_ref.dtype)

    _dwconv_from_pad(pad_ref, scw_ref, SC_K, s, emit_b)

    pad_ref[CONV_HALO:CONV_HALO + s, :] = (cf_ref[:, 0:BR_W].astype(F32)
                                           * jax.nn.sigmoid(cf_ref[:, BR_W:2 * BR_W].astype(F32)))

    def emit_d(c0, acc):
        u = acc + cfb_ref[...]
        mu = jnp.mean(u, axis=-1, keepdims=True)
        d = u - mu
        var = jnp.mean(d * d, axis=-1, keepdims=True)
        y = d * lax.rsqrt(var + EPS) * lng_ref[...] + lnb_ref[...]
        od_ref[c0:c0 + CONV_CHUNK, :] = _silu(y).astype(od_ref.dtype)

    _dwconv_from_pad(pad_ref, cfw_ref, CF_K, s, emit_d)


def _convs(p, latent, W):
    if latent:
        nb, s, row_off = DEC_BATCH, DEC_SEQ, N_CTX_TOK
    else:
        nb, s, row_off = BATCH, SEQ, 0
    return pl.pallas_call(
        functools.partial(_conv_kernel, s),
        out_shape=[jax.ShapeDtypeStruct((nb * s, BR_W), BF16)] * 2,
        grid=(nb,),
        in_specs=[pl.BlockSpec((s, 3 * BR_W), lambda b: (row_off // s + b, P_SC // (3 * BR_W))),
                  pl.BlockSpec((s, 2 * BR_W), lambda b: (row_off // s + b, P_CF // (2 * BR_W))),
                  _const_spec((SC_K, BR_W)), _const_spec((CF_K, BR_W)),
                  _const_spec((1, BR_W)), _const_spec((1, BR_W)), _const_spec((1, BR_W))],
        out_specs=[pl.BlockSpec((s, BR_W), lambda b: (b, 0))] * 2,
        scratch_shapes=[pltpu.VMEM((s + 2 * CONV_HALO, BR_W), F32)],
        compiler_params=_cparams(("arbitrary",)),
        name="convs_latent" if latent else "convs_context",
    )(p, p, W['sc_w'], W['cf_w'], W['cf_b'], W['cf_ln_g'], W['cf_ln_b'])


def _merge_kernel(x_ref, mod_ref, oa_ref, ob_ref, oc_ref, od_ref, g0_ref, g1_ref, g2_ref, g3_ref,
                  wa_ref, wb_ref, wo_ref, y_ref):
    merged = jax.nn.sigmoid(g0_ref[...].astype(F32)) * jnp.dot(oa_ref[...], wa_ref[...], preferred_element_type=F32)
    for n, (o_ref, g_ref) in enumerate(((ob_ref, g1_ref), (oc_ref, g2_ref), (od_ref, g3_ref))):
        merged = merged + (jax.nn.sigmoid(g_ref[...].astype(F32))
                           * jnp.dot(o_ref[...], wb_ref[n], preferred_element_type=F32))
    y_ref[...] = x_ref[...] + mod_ref[2:3, :] * _dot(merged, wo_ref[...])


def _merge(x_all, mod3, p, o_a, o_b, o_c, o_d, W):
    tm = 512
    row = lambda i: (i, 0)
    gate_spec = lambda n: pl.BlockSpec((tm, D_MODEL), lambda i: (i, P_GATE // D_MODEL + n))
    return pl.pallas_call(
        _merge_kernel,
        out_shape=jax.ShapeDtypeStruct((N_TOK, D_MODEL), F32),
        grid=(N_TOK // tm,),
        in_specs=[pl.BlockSpec((tm, D_MODEL), row),
                  pl.BlockSpec((None, 6, D_MODEL), lambda i: (_mod_row(i * tm), 0, 0)),
                  pl.BlockSpec((tm, MLA_PAD_W), row),
                  pl.BlockSpec((tm, BR_W), row), pl.BlockSpec((tm, BR_W), row), pl.BlockSpec((tm, BR_W), row),
                  gate_spec(0), gate_spec(1), gate_spec(2), gate_spec(3),
                  _const_spec((MLA_PAD_W, D_MODEL)), _const_spec((3, BR_W, D_MODEL)),
                  _const_spec((D_MODEL, D_MODEL))],
        out_specs=pl.BlockSpec((tm, D_MODEL), row),
        compiler_params=_cparams(("arbitrary",)),
        name="merge",
    )(x_all, mod3, o_a, o_b, o_c, o_d, p, p, p, p, W['w_br_a'], W['w_br_bcd'], W['w_o'])


def _norm2(x, g, mod_ref):
    y = x * lax.rsqrt(jnp.mean(x * x, axis=-1, keepdims=True) + EPS) * g
    return y * (1.0 + mod_ref[4:5, :]) + mod_ref[3:4, :]


def _ffn_kernel(x_ref, mod_ref, g_ref, wg_ref, wu_ref, wd_ref, y_ref, h_scr, acc_scr):
    f = pl.program_id(1)

    @pl.when(f == 0)
    def _():
        h_scr[...] = _norm2(x_ref[...], g_ref[...], mod_ref).astype(BF16)
        acc_scr[...] = jnp.zeros_like(acc_scr)

    h = h_scr[...]
    a = _silu(_dot(h, wg_ref[...])) * _dot(h, wu_ref[...])
    acc_scr[...] += _dot(a, wd_ref[...])

    @pl.when(f == pl.num_programs(1) - 1)
    def _():
        y_ref[...] = x_ref[...] + mod_ref[5:6, :] * acc_scr[...]


def _ffn(x_all, mod3, g_norm2, w_g, w_u, w_d, l, j):
    tm, tf = 1024, 512
    return pl.pallas_call(
        _ffn_kernel,
        out_shape=jax.ShapeDtypeStruct((N_TOK, D_MODEL), F32),
        grid=(N_TOK // tm, D_FF // tf),
        in_specs=[pl.BlockSpec((tm, D_MODEL), lambda i, f: (i, 0)),
                  pl.BlockSpec((None, 6, D_MODEL), lambda i, f: (_mod_row(i * tm), 0, 0)),
                  pl.BlockSpec((None, 1, D_MODEL), lambda i, f: (l, 0, 0)),
                  pl.BlockSpec((None, D_MODEL, tf), lambda i, f: (j, 0, f)),
                  pl.BlockSpec((None, D_MODEL, tf), lambda i, f: (j, 0, f)),
                  pl.BlockSpec((None, tf, D_MODEL), lambda i, f: (j, f, 0))],
        out_specs=pl.BlockSpec((tm, D_MODEL), lambda i, f: (i, 0)),
        scratch_shapes=[pltpu.VMEM((tm, D_MODEL), BF16), pltpu.VMEM((tm, D_MODEL), F32)],
        compiler_params=_cparams(("arbitrary", "arbitrary")),
        name="ffn_dense",
    )(x_all, mod3, g_norm2.reshape(DEPTH, 1, D_MODEL), w_g, w_u, w_d)


ROUTER_LANES = 128
ROUTE_TM = 512
MOE_TM = 1024
MOE_SUB = 256
MOE_TF = 512
MOE_ROWS = 2 * N_TOK + N_EXPERTS * MOE_TM
MOE_TILES = MOE_ROWS // MOE_TM
ROW_DMA_TM = 256
R_I1, R_I2, R_W1, R_W2, R_RANK1, R_RANK2 = range(6)


def _route_top2(h, wr):
    logits = jnp.dot(h, wr, preferred_element_type=F32, precision=lax.Precision.HIGHEST)
    lane = lax.broadcasted_iota(jnp.int32, logits.shape, 1).astype(F32)
    neg = jnp.float32(-jnp.inf)
    logits = jnp.where(lane < N_EXPERTS, logits, neg)
    m1 = jnp.max(logits, axis=-1, keepdims=True)
    i1 = jnp.min(jnp.where(logits == m1, lane, float(ROUTER_LANES)), axis=-1, keepdims=True)
    rest = jnp.where(lane == i1, neg, logits)
    m2 = jnp.max(rest, axis=-1, keepdims=True)
    i2 = jnp.min(jnp.where(rest == m2, lane, float(ROUTER_LANES)), axis=-1, keepdims=True)
    e2 = jnp.exp(m2 - m1)
    return lane, i1, i2, 1.0 / (1.0 + e2), e2 / (1.0 + e2)


def _route_kernel(x_ref, mod_ref, g_ref, wr_ref, route_ref, cnt_ref, carry_scr):
    @pl.when(pl.program_id(0) == 0)
    def _():
        carry_scr[...] = jnp.zeros_like(carry_scr)

    h = _norm2(x_ref[...], g_ref[...], mod_ref)
    lane, i1, i2, w1, w2 = _route_top2(h, wr_ref[...])
    tm = h.shape[0]
    oh1 = lane == i1
    oh2 = lane == i2
    oh = jnp.where(oh1, 1.0, 0.0) + jnp.where(oh2, 1.0, 0.0)
    r = lax.broadcasted_iota(jnp.int32, (tm, tm), 0)
    c = lax.broadcasted_iota(jnp.int32, (tm, tm), 1)
    lower = jnp.where(r > c, 1.0, 0.0).astype(BF16)
    before = jnp.dot(lower, oh.astype(BF16), preferred_element_type=F32) + carry_scr[...]
    rank1 = jnp.sum(jnp.where(oh1, before, 0.0), axis=-1, keepdims=True)
    rank2 = jnp.sum(jnp.where(oh2, before, 0.0), axis=-1, keepdims=True)
    carry_scr[...] += jnp.sum(oh, axis=0, keepdims=True)
    rec = jnp.zeros_like(lane)
    for k, v in ((R_I1, i1), (R_I2, i2), (R_W1, w1), (R_W2, w2), (R_RANK1, rank1), (R_RANK2, rank2)):
        rec = jnp.where(lane == float(k), v, rec)
    route_ref[...] = rec
    cnt_ref[...] = carry_scr[...]


def _moe_route(x_all, mod3, g_norm2, w_router_p, l, j):
    tm = ROUTE_TM
    return pl.pallas_call(
        _route_kernel,
        out_shape=[jax.ShapeDtypeStruct((N_TOK, ROUTER_LANES), F32),
                   jax.ShapeDtypeStruct((1, ROUTER_LANES), F32)],
        grid=(N_TOK // tm,),
        in_specs=[pl.BlockSpec((tm, D_MODEL), lambda i: (i, 0)),
                  pl.BlockSpec((None, 6, D_MODEL), lambda i: (_mod_row(i * tm), 0, 0)),
                  pl.BlockSpec((None, 1, D_MODEL), lambda i: (l, 0, 0)),
                  pl.BlockSpec((None, D_MODEL, ROUTER_LANES), lambda i: (j, 0, 0))],
        out_specs=[pl.BlockSpec((tm, ROUTER_LANES), lambda i: (i, 0)),
                   pl.BlockSpec((1, ROUTER_LANES), lambda i: (0, 0))],
        scratch_shapes=[pltpu.VMEM((1, ROUTER_LANES), F32)],
        compiler_params=_cparams(("arbitrary",)),
        name="moe_route",
    )(x_all, mod3, g_norm2.reshape(DEPTH, 1, D_MODEL), w_router_p)


def _row_copies(n, pos_refs, base, make):
    def body(r, carry):
        for pos_ref in pos_refs:
            make(r, pos_ref[base + r]).start()
        return carry
    lax.fori_loop(0, n, body, 0, unroll=8)


def _dispatch_kernel(pos1_ref, pos2_ref, x_ref, mod_ref, g_ref, xs_init_ref, xs_ref, h_scr, sem):
    del xs_init_ref
    tm = h_scr.shape[0]
    h_scr[...] = _norm2(x_ref[...], g_ref[...], mod_ref)
    base = pl.program_id(0) * tm

    def make(r, pos):
        return pltpu.make_async_copy(h_scr.at[pl.ds(r, 1), :], xs_ref.at[pl.ds(pos, 1), :], sem)

    _row_copies(tm, (pos1_ref, pos2_ref), base, make)
    for _ in range(2):
        pltpu.make_async_copy(h_scr, xs_ref.at[pl.ds(0, tm), :], sem).wait()


def _moe_dispatch(pos1, pos2, x_all, mod3, g_norm2, l):
    tm = ROW_DMA_TM
    xs_init = jnp.zeros((MOE_ROWS, D_MODEL), F32)
    return pl.pallas_call(
        _dispatch_kernel,
        out_shape=jax.ShapeDtypeStruct((MOE_ROWS, D_MODEL), F32),
        grid_spec=pltpu.PrefetchScalarGridSpec(
            num_scalar_prefetch=2,
            grid=(N_TOK // tm,),
            in_specs=[pl.BlockSpec((tm, D_MODEL), lambda i, p1, p2: (i, 0)),
                      pl.BlockSpec((None, 6, D_MODEL), lambda i, p1, p2: (_mod_row(i * tm), 0, 0)),
                      pl.BlockSpec((None, 1, D_MODEL), lambda i, p1, p2: (l, 0, 0)),
                      pl.BlockSpec(memory_space=pl.ANY)],
            out_specs=pl.BlockSpec(memory_space=pl.ANY),
            scratch_shapes=[pltpu.VMEM((tm, D_MODEL), F32), pltpu.SemaphoreType.DMA],
        ),
        input_output_aliases={5: 0},
        compiler_params=_cparams(("arbitrary",)),
        name="moe_dispatch",
    )(pos1, pos2, x_all, mod3, g_norm2.reshape(DEPTH, 1, D_MODEL), xs_init)


def _gmm_kernel(te_ref, tr_ref, xs_ref, wg_ref, wu_ref, wd_ref, y_ref, wg_scr, wu_scr, wd_scr):
    del te_ref
    g = pl.program_id(0)
    f = pl.program_id(1)
    rows = tr_ref[g]

    @pl.when(f == 0)
    def _():
        y_ref[...] = jnp.zeros_like(y_ref)

    @pl.when(rows > 0)
    def _():
        wg_scr[...] = wg_ref[...].astype(BF16)
        wu_scr[...] = wu_ref[...].astype(BF16)
        wd_scr[...] = wd_ref[...].astype(BF16)

    for s in range(MOE_TM // MOE_SUB):
        @pl.when(rows > s * MOE_SUB)
        def _():
            sl = slice(s * MOE_SUB, (s + 1) * MOE_SUB)
            x = xs_ref[sl, :].astype(BF16)
            a = (_silu(jnp.dot(x, wg_scr[...], preferred_element_type=F32))
                 * jnp.dot(x, wu_scr[...], preferred_element_type=F32))
            y_ref[sl, :] += jnp.dot(a.astype(BF16), wd_scr[...], preferred_element_type=F32)


def _moe_gmm(tile_expert, tile_rows, xs, w_g, w_u, w_d, j):
    tm, tf = MOE_TM, MOE_TF
    nf = D_FF // tf

    def f_eff(g, f, tr):
        return jnp.where(tr[g] > 0, f, nf - 1)

    return pl.pallas_call(
        _gmm_kernel,
        out_shape=jax.ShapeDtypeStruct((MOE_ROWS, D_MODEL), F32),
        grid_spec=pltpu.PrefetchScalarGridSpec(
            num_scalar_prefetch=2,
            grid=(MOE_TILES, nf),
            in_specs=[pl.BlockSpec((tm, D_MODEL), lambda g, f, te, tr: (g, 0)),
                      pl.BlockSpec((None, None, D_MODEL, tf), lambda g, f, te, tr: (j, te[g], 0, f_eff(g, f, tr))),
                      pl.BlockSpec((None, None, D_MODEL, tf), lambda g, f, te, tr: (j, te[g], 0, f_eff(g, f, tr))),
                      pl.BlockSpec((None, None, tf, D_MODEL), lambda g, f, te, tr: (j, te[g], f_eff(g, f, tr), 0))],
            out_specs=pl.BlockSpec((tm, D_MODEL), lambda g, f, te, tr: (g, 0)),
            scratch_shapes=[pltpu.VMEM((D_MODEL, tf), BF16), pltpu.VMEM((D_MODEL, tf), BF16),
                            pltpu.VMEM((tf, D_MODEL), BF16)],
        ),
        compiler_params=_cparams(("arbitrary", "arbitrary")),
        name="moe_experts",
    )(tile_expert, tile_rows, xs, w_g, w_u, w_d)


def _combine_kernel(pos1_ref, pos2_ref, x_ref, mod_ref, route_ref, y_ref, o_ref, g1_scr, g2_scr, sem):
    tm = g1_scr.shape[0]
    base = pl.program_id(0) * tm

    def make1(r, pos):
        return pltpu.make_async_copy(y_ref.at[pl.ds(pos, 1), :], g1_scr.at[pl.ds(r, 1), :], sem)

    def make2(r, pos):
        return pltpu.make_async_copy(y_ref.at[pl.ds(pos, 1), :], g2_scr.at[pl.ds(r, 1), :], sem)

    _row_copies(tm, (pos1_ref,), base, make1)
    _row_copies(tm, (pos2_ref,), base, make2)
    pltpu.make_async_copy(y_ref.at[pl.ds(0, tm), :], g1_scr, sem).wait()
    pltpu.make_async_copy(y_ref.at[pl.ds(0, tm), :], g2_scr, sem).wait()
    rec = route_ref[...]
    w1 = rec[:, R_W1:R_W1 + 1]
    w2 = rec[:, R_W2:R_W2 + 1]
    o_ref[...] = x_ref[...] + mod_ref[5:6, :] * (w1 * g1_scr[...] + w2 * g2_scr[...])


def _moe_combine(pos1, pos2, x_all, mod3, route, y):
    tm = ROW_DMA_TM
    return pl.pallas_call(
        _combine_kernel,
        out_shape=jax.ShapeDtypeStruct((N_TOK, D_MODEL), F32),
        grid_spec=pltpu.PrefetchScalarGridSpec(
            num_scalar_prefetch=2,
            grid=(N_TOK // tm,),
            in_specs=[pl.BlockSpec((tm, D_MODEL), lambda i, p1, p2: (i, 0)),
                      pl.BlockSpec((None, 6, D_MODEL), lambda i, p1, p2: (_mod_row(i * tm), 0, 0)),
                      pl.BlockSpec((tm, ROUTER_LANES), lambda i, p1, p2: (i, 0)),
                      pl.BlockSpec(memory_space=pl.ANY)],
            out_specs=pl.BlockSpec((tm, D_MODEL), lambda i, p1, p2: (i, 0)),
            scratch_shapes=[pltpu.VMEM((tm, D_MODEL), F32), pltpu.VMEM((tm, D_MODEL), F32),
                            pltpu.SemaphoreType.DMA],
        ),
        compiler_params=_cparams(("arbitrary",)),
        name="moe_combine",
    )(pos1, pos2, x_all, mod3, route, y)


def _moe(x_all, mod3, g_norm2, w_router_p, w_g, w_u, w_d, l, j):
    route, cnt = _moe_route(x_all, mod3, g_norm2, w_router_p, l, j)
    cnt = cnt[0, :N_EXPERTS].astype(jnp.int32)
    padded = (cnt + MOE_TM - 1) // MOE_TM * MOE_TM
    ends = jnp.cumsum(padded)
    offs = ends - padded
    experts = jnp.arange(N_EXPERTS, dtype=jnp.int32)

    def positions(i_lane, rank_lane):
        e = route[:, i_lane].astype(jnp.int32)
        off = jnp.sum(jnp.where(e[:, None] == experts[None, :], offs[None, :], 0), axis=1)
        return off + route[:, rank_lane].astype(jnp.int32)

    pos1 = positions(R_I1, R_RANK1)
    pos2 = positions(R_I2, R_RANK2)
    tile_start = jnp.arange(MOE_TILES, dtype=jnp.int32) * MOE_TM
    last_tile = jnp.maximum(ends[-1] - MOE_TM, 0)
    owner_start = jnp.minimum(tile_start, last_tile)
    tile_expert = jnp.minimum(jnp.sum(owner_start[:, None] >= ends[None, :], axis=1), N_EXPERTS - 1).astype(jnp.int32)
    group_end = jnp.sum(jnp.where(tile_expert[:, None] == experts[None, :], (offs + cnt)[None, :], 0), axis=1)
    tile_rows = jnp.where(tile_start < ends[-1], jnp.clip(group_end - tile_start, 0, MOE_TM), 0).astype(jnp.int32)

    xs = _moe_dispatch(pos1, pos2, x_all, mod3, g_norm2, l)
    y = _moe_gmm(tile_expert, tile_rows, xs, w_g, w_u, w_d, j)
    return _moe_combine(pos1, pos2, x_all, mod3, route, y)


def _pad_heads(w, per_head, lo):
    lead = w.shape[:-1]
    w = w.reshape(lead + (MLA_HEADS, per_head))
    w = jnp.pad(w, [(0, 0)] * len(lead) + [(0, 0), (lo, HEAD_PAD - lo - per_head)])
    return w.reshape(lead + (MLA_PAD_W,))


def _kpe_to_heads(kpe):
    z = jnp.pad(kpe, [(0, 0)] * (kpe.ndim - 1) + [(MLA_NOPE, HEAD_PAD - MLA_QK)])
    return jnp.tile(z, (1,) * (kpe.ndim - 1) + (MLA_HEADS,))


def _layer_weights(l, w_in, g_qa, w_uq, g_kva, w_ukv, g_mla_q, g_mla_k, sc_w, g_na_q, g_na_k,
                   cf_w, cf_b, cf_ln_g, cf_ln_b, w_br, w_o):
    wi = w_in[l]
    s1 = MLA_Q_LORA
    s2 = s1 + MLA_KV_LORA
    s3 = s2 + MLA_ROPE
    s4 = s3 + 3 * BR_W
    s5 = s4 + 3 * BR_W
    s6 = s5 + 2 * BR_W
    w_kpe = wi[:, s2:s3]
    w_in_p = jnp.concatenate([
        wi[:, :s3], jnp.zeros((D_MODEL, 256 - MLA_KV_LORA - MLA_ROPE), F32),
        _kpe_to_heads(w_kpe), wi[:, s5:s6], wi[:, s3:s4], wi[:, s4:s5], wi[:, s6:]], axis=1).astype(BF16)
    ukv = w_ukv[l].reshape(MLA_KV_LORA, MLA_HEADS, MLA_NOPE + MLA_V)
    W = dict(
        w_in_p=w_in_p,
        g_qa=g_qa[l][None], g_kva=g_kva[l][None],
        w_uq_p=_pad_heads(w_uq[l], MLA_QK, 0).astype(BF16),
        w_ukv_k=_pad_heads(ukv[:, :, :MLA_NOPE].reshape(MLA_KV_LORA, -1), MLA_NOPE, 0).astype(BF16),
        w_ukv_v=_pad_heads(ukv[:, :, MLA_NOPE:].reshape(MLA_KV_LORA, -1), MLA_V, 0).astype(BF16),
        g_mla_q_p=jnp.pad(g_mla_q[l], (0, HEAD_PAD - MLA_QK))[None],
        g_mla_k_p=jnp.pad(g_mla_k[l], (0, HEAD_PAD - MLA_QK))[None],
        g_na_q_t=jnp.tile(g_na_q[l], NA_HEADS)[None], g_na_k_t=jnp.tile(g_na_k[l], NA_HEADS)[None],
        sc_w=sc_w[l], cf_w=cf_w[l], cf_b=cf_b[l][None], cf_ln_g=cf_ln_g[l][None], cf_ln_b=cf_ln_b[l][None],
        w_br_a=jnp.pad(w_br[l, 0].reshape(MLA_HEADS, MLA_V, D_MODEL),
                       ((0, 0), (0, HEAD_PAD - MLA_V), (0, 0))).reshape(MLA_PAD_W, D_MODEL).astype(BF16),
        w_br_bcd=w_br[l, 1:].astype(BF16),
        w_o=w_o[l].astype(BF16),
    )
    return W


def kernel(x_prompt, x_sample, cache_mla_ckv, cache_mla_kpe, cache_na_k, cache_na_v, c, c_ctx, w_ada, b_ada, g_norm1, w_in, g_qa, w_uq, g_kva, w_ukv, g_mla_q, g_mla_k, sc_w, g_na_q, g_na_k, na_rpb, cf_w, cf_b, cf_ln_g, cf_ln_b, w_br, w_o, g_norm2, w_ff_gate, w_ff_up, w_ff_down, w_router, w_e_gate, w_e_up, w_e_down):
    x_all = jnp.concatenate([x_prompt.reshape(N_CTX_TOK, D_MODEL), x_sample.reshape(N_LAT_TOK, D_MODEL)], axis=0)
    cvec = jnp.concatenate([c_ctx[None, :], c, jnp.zeros((MOD_ROWS - 1 - DEC_BATCH, D_MODEL), F32)], axis=0)
    cache_na_k2 = cache_na_k.reshape(DEC_BATCH, DEPTH, PAST_LEN, BR_W)
    cache_na_v2 = cache_na_v.reshape(DEC_BATCH, DEPTH, PAST_LEN, BR_W)
    w_router_p = jnp.pad(w_router, ((0, 0), (0, 0), (0, ROUTER_LANES - N_EXPERTS)))

    ckv_l, kpe_l, nak_l, nav_l = [], [], [], []
    for l in range(DEPTH):
        W = _layer_weights(l, w_in, g_qa, w_uq, g_kva, w_ukv, g_mla_q, g_mla_k, sc_w, g_na_q, g_na_k,
                           cf_w, cf_b, cf_ln_g, cf_ln_b, w_br, w_o)
        mod3 = _modulation(cvec, w_ada, b_ada, l).reshape(MOD_ROWS, 6, D_MODEL)
        p = _in_projection(x_all, mod3, g_norm1, W['w_in_p'], l)

        oa_c, ckv_new = _mla(p, l, False, None, None, W)
        (oa_l,) = _mla(p, l, True, cache_mla_ckv, _kpe_to_heads(cache_mla_kpe[:, l]), W)
        oc_c, nak_new, nav_new = _na_context(p, W)
        oc_l = _na_latent(p, l, cache_na_k2, cache_na_v2, _na_bias_table(na_rpb[l]), W)
        ob_c, od_c = _convs(p, False, W)
        ob_l, od_l = _convs(p, True, W)

        cat = lambda a, b: jnp.concatenate([a, b], axis=0)
        x_all = _merge(x_all, mod3, p, cat(oa_c, oa_l), cat(ob_c, ob_l), cat(oc_c, oc_l), cat(od_c, od_l), W)

        if l % 2 == 0:
            x_all = _ffn(x_all, mod3, g_norm2, w_ff_gate, w_ff_up, w_ff_down, l, l // 2)
        else:
            x_all = _moe(x_all, mod3, g_norm2, w_router_p, w_e_gate, w_e_up, w_e_down, l, l // 2)

        ckv_l.append(ckv_new)
        kpe_l.append(p[:N_CTX_TOK, P_KVA + MLA_KV_LORA:P_KVA + MLA_KV_LORA + MLA_ROPE]
                     .astype(F32).reshape(BATCH, SEQ, MLA_ROPE))
        nak_l.append(nak_new.reshape(BATCH, SEQ, NA_HEADS, NA_HD))
        nav_l.append(nav_new.reshape(BATCH, SEQ, NA_HEADS, NA_HD))

    y_prompt = x_all[:N_CTX_TOK].reshape(BATCH, SEQ, D_MODEL)
    y_sample = x_all[N_CTX_TOK:].reshape(DEC_BATCH, DEC_SEQ, D_MODEL)
    return (y_prompt, y_sample, jnp.stack(ckv_l, axis=1), jnp.stack(kpe_l, axis=1),
            jnp.stack(nak_l, axis=1), jnp.stack(nav_l, axis=1))
```

```python
import functools

import numpy as np
import jax
import jax.numpy as jnp
from jax import lax
from jax.experimental import pallas as pl
from jax.experimental.pallas import tpu as pltpu

F32 = jnp.float32
BF16 = jnp.bfloat16

D_MODEL = 1024
BATCH = 16
SEQ = 256
DEPTH = 2
DEC_BATCH = 8
DEC_SEQ = 1024
PAST_LEN = 512
GRID_W = 64
N_BRANCH = 4
BR_W = 256
MLA_HEADS = 4
MLA_NOPE = 64
MLA_ROPE = 32
MLA_QK = 96
MLA_V = 64
MLA_Q_LORA = 256
MLA_KV_LORA = 128
SC_K = 3
NA_HEADS = 4
NA_HD = 64
NA_WIN_R = 8
NA_WIN_C = 16
CF_K = 31
D_FF = 3584
N_EXPERTS = 8
ROPE_THETA = 10000.0
EPS = 1e-6

N_CTX_TOK = BATCH * SEQ
N_LAT_TOK = DEC_BATCH * DEC_SEQ
N_TOK = N_CTX_TOK + N_LAT_TOK
MOD_ROWS = 16

HEAD_PAD = 128
MLA_PAD_W = MLA_HEADS * HEAD_PAD

P_QA = 0
P_KVA = 256
P_KPE4 = 512
P_HEAD = 1024
P_SC = 1024
P_NA = 1792
P_CF = 2560
P_GATE = 3072
P_COLS = 7168

VMEM_LIMIT = 56 * 1024 * 1024

NA_TQ = 256
NA_WIN_ROWS = 12
NA_WIN_KEYS = NA_WIN_ROWS * GRID_W


def _cparams(sem):
    return pltpu.CompilerParams(dimension_semantics=sem, vmem_limit_bytes=VMEM_LIMIT)


def _const_spec(shape):
    nd = len(shape)
    return pl.BlockSpec(shape, lambda *_: (0,) * nd)


def _mod_row(tok_start):
    return jnp.where(tok_start < N_CTX_TOK, 0, 1 + (tok_start - N_CTX_TOK) // DEC_SEQ)


def _sigmoid(x):
    return 0.5 * jnp.tanh(0.5 * x) + 0.5


def _silu(x):
    return x * _sigmoid(x)


def _ctx_or_lat(is_ctx, c_ref, l_ref):
    return jnp.where(is_ctx, c_ref[...], l_ref[...])


def _dot(a, b):
    return jnp.dot(a.astype(BF16), b.astype(BF16), preferred_element_type=F32)


def _dot_nt(a, b):
    return lax.dot_general(a.astype(BF16), b.astype(BF16), (((1,), (1,)), ((), ())),
                           preferred_element_type=F32)


def _mod_kernel(c_ref, w_ref, b_ref, o_ref):
    o_ref[...] = _dot(_silu(c_ref[...]), w_ref[...]) + b_ref[...]


def _modulation(cvec, w_ada, b_ada, l):
    tn = 1024
    return pl.pallas_call(
        _mod_kernel,
        out_shape=jax.ShapeDtypeStruct((MOD_ROWS, 6 * D_MODEL), F32),
        grid=(6 * D_MODEL // tn,),
        in_specs=[
            _const_spec((MOD_ROWS, D_MODEL)),
            pl.BlockSpec((None, D_MODEL, tn), lambda j: (l, 0, j)),
            pl.BlockSpec((None, 1, tn), lambda j: (l, 0, j)),
        ],
        out_specs=pl.BlockSpec((MOD_ROWS, tn), lambda j: (0, j)),
        compiler_params=_cparams(("arbitrary",)),
        name="modulation",
    )(cvec, w_ada, b_ada.reshape(DEPTH, 1, 6 * D_MODEL))


def _pair_specs(tm, width, buffers=None):
    nc = N_CTX_TOK // tm
    mode = {} if buffers is None else dict(pipeline_mode=pl.Buffered(buffers))
    return [pl.BlockSpec((tm, width), lambda i, *_: (jnp.minimum(i, nc - 1), 0), **mode),
            pl.BlockSpec((tm, width), lambda i, *_: (jnp.maximum(i - nc, 0), 0), **mode)]


def _pair_shapes(width, dtype):
    return [jax.ShapeDtypeStruct((N_CTX_TOK, width), dtype), jax.ShapeDtypeStruct((N_LAT_TOK, width), dtype)]


def _store_pair(is_ctx, c_ref, l_ref, val):
    @pl.when(is_ctx)
    def _():
        c_ref[...] = val

    @pl.when(jnp.logical_not(is_ctx))
    def _():
        l_ref[...] = val


def _x_specs(x, tm, buffers=None):
    if len(x) == 2:
        return _pair_specs(tm, D_MODEL, buffers)
    mode = {} if buffers is None else dict(pipeline_mode=pl.Buffered(buffers))
    return [pl.BlockSpec((tm, D_MODEL), lambda i, *_: (i, 0), **mode)]


def _x_shapes(split):
    return _pair_shapes(D_MODEL, F32) if split else [jax.ShapeDtypeStruct((N_TOK, D_MODEL), F32)]


def _x_load(x_refs):
    if len(x_refs) == 1:
        return x_refs[0][...]
    tm = x_refs[0].shape[0]
    return _ctx_or_lat(pl.program_id(0) < N_CTX_TOK // tm, *x_refs)


def _x_store(y_refs, val):
    if len(y_refs) == 1:
        y_refs[0][...] = val
    else:
        tm = y_refs[0].shape[0]
        _store_pair(pl.program_id(0) < N_CTX_TOK // tm, *y_refs, val)


def _inproj_kernel(nx, *refs):
    x_refs, (mod_ref, g_ref, wh_ref, wt_ref, o_ref, h_scr) = refs[:nx], refs[nx:]
    j = pl.program_id(1)

    @pl.when(j == 0)
    def _():
        x = _x_load(x_refs)
        y = x * lax.rsqrt(jnp.mean(x * x, axis=-1, keepdims=True) + EPS) * g_ref[...]
        h_scr[...] = (y * (1.0 + mod_ref[1:2, :]) + mod_ref[0:1, :]).astype(BF16)
        o_ref[...] = jnp.dot(h_scr[...], wh_ref[...], preferred_element_type=F32).astype(o_ref.dtype)

    @pl.when(j > 0)
    def _():
        o_ref[...] = jnp.dot(h_scr[...], wt_ref[...], preferred_element_type=F32).astype(o_ref.dtype)


def _in_projection(x, mod3, g_norm1, w_head, w_tail, l):
    tm, tn = 1024, 1024
    return pl.pallas_call(
        functools.partial(_inproj_kernel, len(x)),
        out_shape=jax.ShapeDtypeStruct((N_TOK, P_COLS), BF16),
        grid=(N_TOK // tm, P_COLS // tn),
        in_specs=_x_specs(x, tm) + [
            pl.BlockSpec((None, 6, D_MODEL), lambda i, j: (_mod_row(i * tm), 0, 0)),
            pl.BlockSpec((None, 1, D_MODEL), lambda i, j: (l, 0, 0)),
            _const_spec((D_MODEL, P_HEAD)),
            pl.BlockSpec((D_MODEL, tn), lambda i, j: (0, jnp.maximum(j - 1, 0))),
        ],
        out_specs=pl.BlockSpec((tm, tn), lambda i, j: (i, j)),
        scratch_shapes=[pltpu.VMEM((tm, D_MODEL), BF16)],
        compiler_params=_cparams(("arbitrary", "arbitrary")),
        name="in_projection",
    )(*x, mod3, g_norm1.reshape(DEPTH, 1, D_MODEL), w_head, w_tail)


def _rope_tables():
    t = np.arange(DEC_SEQ)
    nf = MLA_ROPE // 4
    inv = (np.float32(ROPE_THETA) ** (-np.arange(nf, dtype=np.float32) / np.float32(nf))).astype(np.float32)
    ang_r = (t // GRID_W).astype(np.float32)[:, None] * inv[None, :]
    ang_c = (t % GRID_W).astype(np.float32)[:, None] * inv[None, :]
    c = np.zeros((DEC_SEQ, HEAD_PAD), np.float32)
    s1 = np.zeros((DEC_SEQ, HEAD_PAD), np.float32)
    s2 = np.zeros((DEC_SEQ, HEAD_PAD), np.float32)
    c[:, :MLA_NOPE] = 1.0
    for base, ang in ((MLA_NOPE, ang_r), (MLA_NOPE + 2 * nf, ang_c)):
        c[:, base:base + nf] = np.cos(ang)
        c[:, base + nf:base + 2 * nf] = np.cos(ang)
        s1[:, base:base + nf] = -np.sin(ang)
        s2[:, base + nf:base + 2 * nf] = np.sin(ang)
    return c, s1, s2


def _rope(x, c, s1, s2):
    nf = MLA_ROPE // 4
    return x * c + pltpu.roll(x, HEAD_PAD - nf, 1) * s1 + pltpu.roll(x, nf, 1) * s2


def _head_norm(xh, g):
    ms = jnp.sum(xh * xh, axis=-1, keepdims=True) * (1.0 / MLA_QK)
    return xh * lax.rsqrt(ms + EPS) * g


def _mla_kernel(latent, tq, *refs):
    if latent:
        (pqa_ref, pkva_ref, pkpe_ref, cckv_ref, ckpe_ref, rc_ref, rs1_ref, rs2_ref,
         gqa_ref, wuq_ref, gkva_ref, wk_ref, wv_ref, gq_ref, gk_ref,
         o_ref, k_scr, v_scr) = refs
    else:
        (pqa_ref, pkva_ref, pkpe_ref,
         gqa_ref, wuq_ref, gkva_ref, wk_ref, wv_ref, gq_ref, gk_ref,
         o_ref, ckv_ref, kpe_ref, k_scr, v_scr) = refs
    qi = pl.program_id(1)
    n_past = PAST_LEN if latent else 0

    def put_kv(ckvn, kpe4, row0, rope):
        n = ckvn.shape[0]
        kk = _dot(ckvn, wk_ref[...]) + kpe4
        v_scr[row0:row0 + n, :] = _dot(ckvn, wv_ref[...]).astype(BF16)
        for h in range(MLA_HEADS):
            sl = slice(h * HEAD_PAD, (h + 1) * HEAD_PAD)
            kh = _head_norm(kk[:, sl], gk_ref[...])
            if rope:
                kh = _rope(kh, rc_ref[...], rs1_ref[...], rs2_ref[...])
            k_scr[row0:row0 + n, sl] = kh.astype(BF16)

    @pl.when(qi == 0)
    def _():
        if latent:
            put_kv(cckv_ref[...], ckpe_ref[...], 0, False)
        kva = pkva_ref[...].astype(F32)
        ckv = kva[:, :MLA_KV_LORA]
        ckvn = ckv * lax.rsqrt(jnp.mean(ckv * ckv, axis=-1, keepdims=True) + EPS) * gkva_ref[...]
        if not latent:
            ckv_ref[...] = ckvn
            kpe_ref[...] = kva[:, MLA_KV_LORA:MLA_KV_LORA + MLA_ROPE]
        put_kv(ckvn, pkpe_ref[...].astype(F32), n_past, latent)

    qa = pqa_ref[...].astype(F32)
    qan = qa * lax.rsqrt(jnp.mean(qa * qa, axis=-1, keepdims=True) + EPS) * gqa_ref[...]
    q = _dot(qan, wuq_ref[...])
    scale = MLA_QK ** -0.5
    for h in range(MLA_HEADS):
        sl = slice(h * HEAD_PAD, (h + 1) * HEAD_PAD)
        qh = _head_norm(q[:, sl], gq_ref[...])
        if latent:
            rows = pl.ds(pl.multiple_of(qi * tq, tq), tq)
            qh = _rope(qh, rc_ref[rows, :], rs1_ref[rows, :], rs2_ref[rows, :])
        s = _dot_nt(qh, k_scr[:, sl]) * scale
        m = jnp.max(s, axis=-1, keepdims=True)
        e = jnp.exp(s - m)
        den = jnp.sum(e, axis=-1, keepdims=True)
        o = _dot(e, v_scr[:, sl]) / den
        o_ref[:, sl] = o.astype(o_ref.dtype)


def _mla(p, l, latent, cache_ckv, cache_kpe4, W):
    if latent:
        nb, s, tq, row_off = DEC_BATCH, DEC_SEQ, 256, N_CTX_TOK
    else:
        nb, s, tq, row_off = BATCH, SEQ, 256, 0
    nq = s // tq
    sk = s + (PAST_LEN if latent else 0)
    in_specs = [
        pl.BlockSpec((tq, MLA_Q_LORA), lambda b, qi: (row_off // tq + b * nq + qi, P_QA // MLA_Q_LORA)),
        pl.BlockSpec((s, 256), lambda b, qi: (row_off // s + b, P_KVA // 256)),
        pl.BlockSpec((s, MLA_PAD_W), lambda b, qi: (row_off // s + b, P_KPE4 // MLA_PAD_W)),
    ]
    args = [p, p, p]
    if latent:
        in_specs += [
            pl.BlockSpec((None, None, PAST_LEN, MLA_KV_LORA), lambda b, qi: (b, l, 0, 0)),
            pl.BlockSpec((None, PAST_LEN, MLA_PAD_W), lambda b, qi: (b, 0, 0)),
            _const_spec((DEC_SEQ, HEAD_PAD)), _const_spec((DEC_SEQ, HEAD_PAD)), _const_spec((DEC_SEQ, HEAD_PAD)),
        ]
        args += [cache_ckv, cache_kpe4] + [jnp.asarray(t) for t in _rope_tables()]
    in_specs += [
        _const_spec((1, MLA_Q_LORA)), _const_spec((MLA_Q_LORA, MLA_PAD_W)), _const_spec((1, MLA_KV_LORA)),
        _const_spec((MLA_KV_LORA, MLA_PAD_W)), _const_spec((MLA_KV_LORA, MLA_PAD_W)),
        _const_spec((1, HEAD_PAD)), _const_spec((1, HEAD_PAD)),
    ]
    args += [W['g_qa'], W['w_uq_p'], W['g_kva'], W['w_ukv_k'], W['w_ukv_v'], W['g_mla_q_p'], W['g_mla_k_p']]
    out_shape = [jax.ShapeDtypeStruct((nb * s, MLA_PAD_W), BF16)]
    out_specs = [pl.BlockSpec((tq, MLA_PAD_W), lambda b, qi: (b * nq + qi, 0))]
    if not latent:
        out_shape.append(jax.ShapeDtypeStruct((nb, s, MLA_KV_LORA), F32))
        out_specs.append(pl.BlockSpec((None, s, MLA_KV_LORA), lambda b, qi: (b, 0, 0)))
        out_shape.append(jax.ShapeDtypeStruct((nb, s, MLA_ROPE), F32))
        out_specs.append(pl.BlockSpec((None, s, MLA_ROPE), lambda b, qi: (b, 0, 0)))
    return pl.pallas_call(
        functools.partial(_mla_kernel, latent, tq),
        out_shape=out_shape,
        grid=(nb, nq),
        in_specs=in_specs,
        out_specs=out_specs,
        scratch_shapes=[pltpu.VMEM((sk, MLA_PAD_W), BF16), pltpu.VMEM((sk, MLA_PAD_W), BF16)],
        compiler_params=_cparams(("arbitrary", "arbitrary")),
        name="mla_latent" if latent else "mla_context",
    )(*args)


def _head_masks(width):
    lane = lax.broadcasted_iota(jnp.int32, (1, width), 1)
    return [(lane >= h * NA_HD) & (lane < (h + 1) * NA_HD) for h in range(NA_HEADS)]


def _group_norm64(x, g, masks):
    x2 = x * x
    inv = jnp.zeros_like(x)
    for m in masks:
        ms = jnp.sum(jnp.where(m, x2, 0.0), axis=-1, keepdims=True) * (1.0 / NA_HD)
        inv = jnp.where(m, lax.rsqrt(ms + EPS), inv)
    return x * inv * g


def _p_blocks(col0, n, rows, row_fn):
    def spec(k):
        return pl.BlockSpec((rows, BR_W), lambda *g: (row_fn(*g), col0 // BR_W + k))
    return [spec(k) for k in range(n)]


def _na_ctx_kernel(pq_ref, pk_ref, pv_ref, gq_ref, gk_ref, o_ref, k_ref, v_ref):
    masks = _head_masks(BR_W)
    qn = _group_norm64(pq_ref[...].astype(F32), gq_ref[...], masks)
    kn = _group_norm64(pk_ref[...].astype(F32), gk_ref[...], masks)
    v = pv_ref[...].astype(F32)
    k_ref[...] = kn
    v_ref[...] = v
    scale = NA_HD ** -0.5
    acc = jnp.zeros((pq_ref.shape[0], BR_W), F32)
    for m in masks:
        s = _dot_nt(jnp.where(m, qn, 0.0), kn) * scale
        mx = jnp.max(s, axis=-1, keepdims=True)
        e = jnp.exp(s - mx)
        den = jnp.sum(e, axis=-1, keepdims=True)
        acc = acc + jnp.where(m, _dot(e, v) / den, 0.0)
    o_ref[...] = acc.astype(o_ref.dtype)


def _na_context(p, W):
    s = SEQ
    return pl.pallas_call(
        _na_ctx_kernel,
        out_shape=[jax.ShapeDtypeStruct((N_CTX_TOK, BR_W), BF16),
                   jax.ShapeDtypeStruct((N_CTX_TOK, BR_W), F32),
                   jax.ShapeDtypeStruct((N_CTX_TOK, BR_W), F32)],
        grid=(BATCH,),
        in_specs=_p_blocks(P_NA, 3, s, lambda b: b) + [_const_spec((1, BR_W)), _const_spec((1, BR_W))],
        out_specs=[pl.BlockSpec((s, BR_W), lambda b: (b, 0))] * 3,
        compiler_params=_cparams(("arbitrary",)),
        name="na_context",
    )(p, p, p, W['g_na_q_t'], W['g_na_k_t'])


def _na_lat_kernel(pq_ref, pk_ref, pv_ref, ck_ref, cv_ref, bias_ref, gq_ref, gk_ref, o_ref,
                   q_scr, k_scr, v_scr, kc_scr, vc_scr):
    j = pl.program_id(1)
    masks = _head_masks(BR_W)

    @pl.when(j == 0)
    def _():
        q_scr[...] = _group_norm64(pq_ref[...].astype(F32), gq_ref[...], masks).astype(BF16)
        k_scr[...] = _group_norm64(pk_ref[...].astype(F32), gk_ref[...], masks).astype(BF16)
        v_scr[...] = pv_ref[...]
        kc_scr[...] = ck_ref[...].astype(BF16)
        vc_scr[...] = cv_ref[...].astype(BF16)

    win0 = pl.multiple_of(jnp.where(j < 2, 0, DEC_SEQ - NA_WIN_KEYS), 256)
    q = q_scr[pl.ds(pl.multiple_of(j * NA_TQ, NA_TQ), NA_TQ), :]
    kw = k_scr[pl.ds(win0, NA_WIN_KEYS), :]
    vw = v_scr[pl.ds(win0, NA_WIN_KEYS), :]
    kc = kc_scr[...]
    vc = vc_scr[...]
    scale = NA_HD ** -0.5
    acc = jnp.zeros((NA_TQ, BR_W), F32)
    zero = jnp.zeros((), BF16)
    for h, m in enumerate(masks):
        qm = jnp.where(m, q, zero)
        s_loc = _dot_nt(qm, kw) * scale + bias_ref[h]
        s_ctx = _dot_nt(qm, kc) * scale
        mx = jnp.maximum(jnp.max(s_loc, axis=-1, keepdims=True), jnp.max(s_ctx, axis=-1, keepdims=True))
        e_loc = jnp.exp(s_loc - mx)
        e_ctx = jnp.exp(s_ctx - mx)
        den = jnp.sum(e_loc, axis=-1, keepdims=True) + jnp.sum(e_ctx, axis=-1, keepdims=True)
        o = (_dot(e_loc, vw) + _dot(e_ctx, vc)) / den
        acc = acc + jnp.where(m, o, 0.0)
    o_ref[...] = acc.astype(o_ref.dtype)


def _na_bias_table(rpb):
    n_dr, n_dc = 2 * NA_WIN_R - 1, 2 * NA_WIN_C - 1
    cols = np.arange(GRID_W)
    dc = np.clip(cols[None, :] - cols[:, None], -(NA_WIN_C - 1), NA_WIN_C - 1) + (NA_WIN_C - 1)
    cstart = np.clip(cols - NA_WIN_C // 2, 0, GRID_W - NA_WIN_C)
    col_ok = (cols[None, :] >= cstart[:, None]) & (cols[None, :] < cstart[:, None] + NA_WIN_C)
    place = (dc.reshape(-1)[None, :] == np.arange(n_dc)[:, None]).astype(np.float32)
    blocks = jnp.dot(rpb.reshape(NA_HEADS * n_dr, n_dc), place, precision=lax.Precision.HIGHEST)
    blocks = jnp.where(col_ok.reshape(-1)[None, :], blocks, -jnp.inf).reshape(NA_HEADS, n_dr, GRID_W, GRID_W)
    neg = jnp.full((NA_HEADS, 1, GRID_W, GRID_W), -jnp.inf, F32)
    blocks = jnp.concatenate([blocks, neg], axis=1)

    nt = DEC_SEQ // NA_TQ
    n_rows = DEC_SEQ // GRID_W
    rows_per_tile = NA_TQ // GRID_W
    sel = np.full((nt, rows_per_tile, NA_WIN_ROWS), n_dr, np.int32)
    for j in range(nt):
        win_row0 = 0 if j < nt // 2 else n_rows - NA_WIN_ROWS
        for rq in range(rows_per_tile):
            r = j * rows_per_tile + rq
            start = min(max(r - NA_WIN_R // 2, 0), n_rows - NA_WIN_R)
            for kr in range(NA_WIN_ROWS):
                if start <= win_row0 + kr < start + NA_WIN_R:
                    sel[j, rq, kr] = win_row0 + kr - r + (NA_WIN_R - 1)
    tiles = []
    for j in range(nt):
        rows = [jnp.concatenate([blocks[:, sel[j, rq, kr]] for kr in range(NA_WIN_ROWS)], axis=-1)
                for rq in range(rows_per_tile)]
        tiles.append(jnp.concatenate(rows, axis=1))
    return jnp.stack(tiles, axis=1)


def _na_latent(p, l, cache_k, cache_v, bias, W):
    s = DEC_SEQ
    nt = s // NA_TQ
    row_off = N_CTX_TOK
    return pl.pallas_call(
        _na_lat_kernel,
        out_shape=jax.ShapeDtypeStruct((N_LAT_TOK, BR_W), BF16),
        grid=(DEC_BATCH, nt),
        in_specs=_p_blocks(P_NA, 3, s, lambda b, j: row_off // s + b) + [
                  pl.BlockSpec((None, None, PAST_LEN, BR_W), lambda b, j: (b, l, 0, 0)),
                  pl.BlockSpec((None, None, PAST_LEN, BR_W), lambda b, j: (b, l, 0, 0)),
                  pl.BlockSpec((NA_HEADS, None, NA_TQ, NA_WIN_KEYS), lambda b, j: (0, j, 0, 0)),
                  _const_spec((1, BR_W)), _const_spec((1, BR_W))],
        out_specs=pl.BlockSpec((NA_TQ, BR_W), lambda b, j: (b * nt + j, 0)),
        scratch_shapes=[pltpu.VMEM((s, BR_W), BF16), pltpu.VMEM((s, BR_W), BF16), pltpu.VMEM((s, BR_W), BF16),
                        pltpu.VMEM((PAST_LEN, BR_W), BF16), pltpu.VMEM((PAST_LEN, BR_W), BF16)],
        compiler_params=_cparams(("arbitrary", "arbitrary")),
        name="na_latent",
    )(p, p, p, cache_k, cache_v, bias, W['g_na_q_t'], W['g_na_k_t'])


CONV_HALO = 16
CONV_CHUNK = 128


def _dwconv_from_pad(pad_ref, w_ref, ksize, s, emit):
    half = ksize // 2
    for c0 in range(0, s, CONV_CHUNK):
        acc = jnp.zeros((CONV_CHUNK, BR_W), F32)
        for k in range(ksize):
            r0 = CONV_HALO + c0 + k - half
            acc = acc + pad_ref[r0:r0 + CONV_CHUNK, :] * w_ref[k:k + 1, :]
        emit(c0, acc)


def _conv_kernel(s, scb_ref, scc_ref, scx_ref, cfa_ref, cfb2_ref, scw_ref, cfw_ref, cfb_ref, lng_ref, lnb_ref,
                 ob_ref, od_ref, pad_ref):
    zeros = jnp.zeros((CONV_HALO, BR_W), F32)
    pad_ref[0:CONV_HALO, :] = zeros
    pad_ref[CONV_HALO + s:2 * CONV_HALO + s, :] = zeros

    pad_ref[CONV_HALO:CONV_HALO + s, :] = scc_ref[...].astype(F32) * scx_ref[...].astype(F32)

    def emit_b(c0, acc):
        ob_ref[c0:c0 + CONV_CHUNK, :] = (scb_ref[c0:c0 + CONV_CHUNK, :].astype(F32) * acc).astype(ob_ref.dtype)

    _dwconv_from_pad(pad_ref, scw_ref, SC_K, s, emit_b)

    pad_ref[CONV_HALO:CONV_HALO + s, :] = cfa_ref[...].astype(F32) * _sigmoid(cfb2_ref[...].astype(F32))

    def emit_d(c0, acc):
        u = acc + cfb_ref[...]
        mu = jnp.mean(u, axis=-1, keepdims=True)
        d = u - mu
        var = jnp.mean(d * d, axis=-1, keepdims=True)
        y = d * lax.rsqrt(var + EPS) * lng_ref[...] + lnb_ref[...]
        od_ref[c0:c0 + CONV_CHUNK, :] = _silu(y).astype(od_ref.dtype)

    _dwconv_from_pad(pad_ref, cfw_ref, CF_K, s, emit_d)


def _convs(p, latent, W):
    if latent:
        nb, s, row_off = DEC_BATCH, DEC_SEQ, N_CTX_TOK
    else:
        nb, s, row_off = BATCH, SEQ, 0
    return pl.pallas_call(
        functools.partial(_conv_kernel, s),
        out_shape=[jax.ShapeDtypeStruct((nb * s, BR_W), BF16)] * 2,
        grid=(nb,),
        in_specs=(_p_blocks(P_SC, 3, s, lambda b: row_off // s + b)
                  + _p_blocks(P_CF, 2, s, lambda b: row_off // s + b)
                  + [_const_spec((SC_K, BR_W)), _const_spec((CF_K, BR_W)),
                     _const_spec((1, BR_W)), _const_spec((1, BR_W)), _const_spec((1, BR_W))]),
        out_specs=[pl.BlockSpec((s, BR_W), lambda b: (b, 0))] * 2,
        scratch_shapes=[pltpu.VMEM((s + 2 * CONV_HALO, BR_W), F32)],
        compiler_params=_cparams(("arbitrary",)),
        name="convs_latent" if latent else "convs_context",
    )(p, p, p, p, p, W['sc_w'], W['cf_w'], W['cf_b'], W['cf_ln_g'], W['cf_ln_b'])


def _merge_kernel(nc, nx, *refs):
    x_refs = refs[:nx]
    (mod_ref, oac_ref, oal_ref, obc_ref, obl_ref, occ_ref, ocl_ref, odc_ref, odl_ref,
     g0_ref, g1_ref, g2_ref, g3_ref, wa_ref, wb_ref, wo_ref, y_ref) = refs[nx:]
    is_ctx = pl.program_id(0) < nc
    merged = (_sigmoid(g0_ref[...].astype(F32))
              * jnp.dot(_ctx_or_lat(is_ctx, oac_ref, oal_ref), wa_ref[...], preferred_element_type=F32))
    branches = ((obc_ref, obl_ref, g1_ref), (occ_ref, ocl_ref, g2_ref), (odc_ref, odl_ref, g3_ref))
    for n, (oc_ref, ol_ref, g_ref) in enumerate(branches):
        merged = merged + (_sigmoid(g_ref[...].astype(F32))
                           * jnp.dot(_ctx_or_lat(is_ctx, oc_ref, ol_ref), wb_ref[n], preferred_element_type=F32))
    y_ref[...] = _x_load(x_refs) + mod_ref[2:3, :] * _dot(merged, wo_ref[...])


def _merge(x, mod3, p, o_a, o_b, o_c, o_d, W):
    tm = 512
    gate_spec = lambda n: pl.BlockSpec((tm, D_MODEL), lambda i: (i, P_GATE // D_MODEL + n))
    return pl.pallas_call(
        functools.partial(_merge_kernel, N_CTX_TOK // tm, len(x)),
        out_shape=jax.ShapeDtypeStruct((N_TOK, D_MODEL), F32),
        grid=(N_TOK // tm,),
        in_specs=(_x_specs(x, tm)
                  + [pl.BlockSpec((None, 6, D_MODEL), lambda i: (_mod_row(i * tm), 0, 0))]
                  + _pair_specs(tm, MLA_PAD_W) + _pair_specs(tm, BR_W) + _pair_specs(tm, BR_W) + _pair_specs(tm, BR_W)
                  + [gate_spec(0), gate_spec(1), gate_spec(2), gate_spec(3),
                     _const_spec((MLA_PAD_W, D_MODEL)), _const_spec((3, BR_W, D_MODEL)),
                     _const_spec((D_MODEL, D_MODEL))]),
        out_specs=pl.BlockSpec((tm, D_MODEL), lambda i: (i, 0)),
        compiler_params=_cparams(("arbitrary",)),
        name="merge",
    )(*x, mod3, *o_a, *o_b, *o_c, *o_d, p, p, p, p, W['w_br_a'], W['w_br_bcd'], W['w_o'])


def _norm2(x, g, mod_ref):
    y = x * lax.rsqrt(jnp.mean(x * x, axis=-1, keepdims=True) + EPS) * g
    return y * (1.0 + mod_ref[4:5, :]) + mod_ref[3:4, :]


def _ffn_kernel(x_ref, mod_ref, g_ref, wg_ref, wu_ref, wd_ref, y_ref, h_scr, acc_scr):
    f = pl.program_id(1)

    @pl.when(f == 0)
    def _():
        h_scr[...] = _norm2(x_ref[...], g_ref[...], mod_ref).astype(BF16)
        acc_scr[...] = jnp.zeros_like(acc_scr)

    h = h_scr[...]
    a = _silu(_dot(h, wg_ref[...])) * _dot(h, wu_ref[...])
    acc_scr[...] += _dot(a, wd_ref[...])

    @pl.when(f == pl.num_programs(1) - 1)
    def _():
        y_ref[...] = x_ref[...] + mod_ref[5:6, :] * acc_scr[...]


def _ffn(x_all, mod3, g_norm2, w_g, w_u, w_d, l, j):
    tm, tf = 1024, 512
    return pl.pallas_call(
        _ffn_kernel,
        out_shape=jax.ShapeDtypeStruct((N_TOK, D_MODEL), F32),
        grid=(N_TOK // tm, D_FF // tf),
        in_specs=[pl.BlockSpec((tm, D_MODEL), lambda i, f: (i, 0)),
                  pl.BlockSpec((None, 6, D_MODEL), lambda i, f: (_mod_row(i * tm), 0, 0)),
                  pl.BlockSpec((None, 1, D_MODEL), lambda i, f: (l, 0, 0)),
                  pl.BlockSpec((None, D_MODEL, tf), lambda i, f: (j, 0, f)),
                  pl.BlockSpec((None, D_MODEL, tf), lambda i, f: (j, 0, f)),
                  pl.BlockSpec((None, tf, D_MODEL), lambda i, f: (j, f, 0))],
        out_specs=pl.BlockSpec((tm, D_MODEL), lambda i, f: (i, 0)),
        scratch_shapes=[pltpu.VMEM((tm, D_MODEL), BF16), pltpu.VMEM((tm, D_MODEL), F32)],
        compiler_params=_cparams(("arbitrary", "arbitrary")),
        name="ffn_dense",
    )(x_all, mod3, g_norm2.reshape(DEPTH, 1, D_MODEL), w_g, w_u, w_d)


ROUTER_LANES = 128
ROUTE_TM = 512
MOE_TM = 1024
MOE_SUB = 256
MOE_TF = 512
MOE_ROWS = 2 * N_TOK + N_EXPERTS * MOE_TM
MOE_TILES = MOE_ROWS // MOE_TM
ROW_DMA_TM = 256
R_I1, R_I2, R_W1, R_W2, R_RANK1, R_RANK2 = range(6)


def _route_top2(h, wr):
    logits = jnp.dot(h, wr, preferred_element_type=F32, precision=lax.Precision.HIGHEST)
    lane = lax.broadcasted_iota(jnp.int32, logits.shape, 1).astype(F32)
    neg = jnp.float32(-jnp.inf)
    logits = jnp.where(lane < N_EXPERTS, logits, neg)
    m1 = jnp.max(logits, axis=-1, keepdims=True)
    i1 = jnp.min(jnp.where(logits == m1, lane, float(ROUTER_LANES)), axis=-1, keepdims=True)
    rest = jnp.where(lane == i1, neg, logits)
    m2 = jnp.max(rest, axis=-1, keepdims=True)
    i2 = jnp.min(jnp.where(rest == m2, lane, float(ROUTER_LANES)), axis=-1, keepdims=True)
    e2 = jnp.exp(m2 - m1)
    return lane, i1, i2, 1.0 / (1.0 + e2), e2 / (1.0 + e2)


def _route_kernel(x_ref, mod_ref, g_ref, wr_ref, route_ref, cnt_ref, carry_scr):
    @pl.when(pl.program_id(0) == 0)
    def _():
        carry_scr[...] = jnp.zeros_like(carry_scr)

    h = _norm2(x_ref[...], g_ref[...], mod_ref)
    lane, i1, i2, w1, w2 = _route_top2(h, wr_ref[...])
    tm = h.shape[0]
    oh1 = lane == i1
    oh2 = lane == i2
    oh = jnp.where(oh1, 1.0, 0.0) + jnp.where(oh2, 1.0, 0.0)
    r = lax.broadcasted_iota(jnp.int32, (tm, tm), 0)
    c = lax.broadcasted_iota(jnp.int32, (tm, tm), 1)
    lower = jnp.where(r > c, 1.0, 0.0).astype(BF16)
    before = jnp.dot(lower, oh.astype(BF16), preferred_element_type=F32) + carry_scr[...]
    rank1 = jnp.sum(jnp.where(oh1, before, 0.0), axis=-1, keepdims=True)
    rank2 = jnp.sum(jnp.where(oh2, before, 0.0), axis=-1, keepdims=True)
    carry_scr[...] += jnp.sum(oh, axis=0, keepdims=True)
    rec = jnp.zeros_like(lane)
    for k, v in ((R_I1, i1), (R_I2, i2), (R_W1, w1), (R_W2, w2), (R_RANK1, rank1), (R_RANK2, rank2)):
        rec = jnp.where(lane == float(k), v, rec)
    route_ref[...] = rec
    cnt_ref[...] = carry_scr[...]


def _moe_route(x_all, mod3, g_norm2, w_router_p, l, j):
    tm = ROUTE_TM
    return pl.pallas_call(
        _route_kernel,
        out_shape=[jax.ShapeDtypeStruct((N_TOK, ROUTER_LANES), F32),
                   jax.ShapeDtypeStruct((1, ROUTER_LANES), F32)],
        grid=(N_TOK // tm,),
        in_specs=[pl.BlockSpec((tm, D_MODEL), lambda i: (i, 0)),
                  pl.BlockSpec((None, 6, D_MODEL), lambda i: (_mod_row(i * tm), 0, 0)),
                  pl.BlockSpec((None, 1, D_MODEL), lambda i: (l, 0, 0)),
                  pl.BlockSpec((None, D_MODEL, ROUTER_LANES), lambda i: (j, 0, 0))],
        out_specs=[pl.BlockSpec((tm, ROUTER_LANES), lambda i: (i, 0)),
                   pl.BlockSpec((1, ROUTER_LANES), lambda i: (0, 0))],
        scratch_shapes=[pltpu.VMEM((1, ROUTER_LANES), F32)],
        compiler_params=_cparams(("arbitrary",)),
        name="moe_route",
    )(x_all, mod3, g_norm2.reshape(DEPTH, 1, D_MODEL), w_router_p)


def _row_copies(n, pos_refs, base, make):
    def body(r, carry):
        for pos_ref in pos_refs:
            make(r, pos_ref[base + r]).start()
        return carry
    lax.fori_loop(0, n, body, 0, unroll=8)


def _dispatch_kernel(pos1_ref, pos2_ref, x_ref, mod_ref, g_ref, xs_init_ref, xs_ref, h_scr, sem):
    del xs_init_ref
    tm = h_scr.shape[0]
    h_scr[...] = _norm2(x_ref[...], g_ref[...], mod_ref)
    base = pl.program_id(0) * tm

    def make(r, pos):
        return pltpu.make_async_copy(h_scr.at[pl.ds(r, 1), :], xs_ref.at[pl.ds(pos, 1), :], sem)

    _row_copies(tm, (pos1_ref, pos2_ref), base, make)
    for _ in range(2):
        pltpu.make_async_copy(h_scr, xs_ref.at[pl.ds(0, tm), :], sem).wait()


def _moe_dispatch(pos1, pos2, x_all, mod3, g_norm2, l):
    tm = ROW_DMA_TM
    xs_init = jnp.zeros((MOE_ROWS, D_MODEL), F32)
    return pl.pallas_call(
        _dispatch_kernel,
        out_shape=jax.ShapeDtypeStruct((MOE_ROWS, D_MODEL), F32),
        grid_spec=pltpu.PrefetchScalarGridSpec(
            num_scalar_prefetch=2,
            grid=(N_TOK // tm,),
            in_specs=[pl.BlockSpec((tm, D_MODEL), lambda i, p1, p2: (i, 0)),
                      pl.BlockSpec((None, 6, D_MODEL), lambda i, p1, p2: (_mod_row(i * tm), 0, 0)),
                      pl.BlockSpec((None, 1, D_MODEL), lambda i, p1, p2: (l, 0, 0)),
                      pl.BlockSpec(memory_space=pl.ANY)],
            out_specs=pl.BlockSpec(memory_space=pl.ANY),
            scratch_shapes=[pltpu.VMEM((tm, D_MODEL), F32), pltpu.SemaphoreType.DMA],
        ),
        input_output_aliases={5: 0},
        compiler_params=_cparams(("arbitrary",)),
        name="moe_dispatch",
    )(pos1, pos2, x_all, mod3, g_norm2.reshape(DEPTH, 1, D_MODEL), xs_init)


def _gmm_kernel(te_ref, tr_ref, xs_ref, wg_ref, wu_ref, wd_ref, y_ref, wg_scr, wu_scr, wd_scr):
    del te_ref
    g = pl.program_id(0)
    f = pl.program_id(1)
    rows = tr_ref[g]

    @pl.when(f == 0)
    def _():
        y_ref[...] = jnp.zeros_like(y_ref)

    def sub_tile(s, wg, wu, wd):
        sl = slice(s * MOE_SUB, (s + 1) * MOE_SUB)
        x = xs_ref[sl, :].astype(BF16)
        a = (_silu(jnp.dot(x, wg, preferred_element_type=F32)) * jnp.dot(x, wu, preferred_element_type=F32))
        y_ref[sl, :] += jnp.dot(a.astype(BF16), wd, preferred_element_type=F32)

    @pl.when(rows > 0)
    def _():
        wg = wg_ref[...].astype(BF16)
        wu = wu_ref[...].astype(BF16)
        wd = wd_ref[...].astype(BF16)
        wg_scr[...] = wg
        wu_scr[...] = wu
        wd_scr[...] = wd
        sub_tile(0, wg, wu, wd)

    for s in range(1, MOE_TM // MOE_SUB):
        @pl.when(rows > s * MOE_SUB)
        def _():
            sub_tile(s, wg_scr[...], wu_scr[...], wd_scr[...])


def _moe_gmm(tile_expert, tile_rows, xs, w_g, w_u, w_d, j):
    tm, tf = MOE_TM, MOE_TF
    nf = D_FF // tf

    def f_eff(g, f, tr):
        return jnp.where(tr[g] > 0, f, nf - 1)

    return pl.pallas_call(
        _gmm_kernel,
        out_shape=jax.ShapeDtypeStruct((MOE_ROWS, D_MODEL), F32),
        grid_spec=pltpu.PrefetchScalarGridSpec(
            num_scalar_prefetch=2,
            grid=(MOE_TILES, nf),
            in_specs=[pl.BlockSpec((tm, D_MODEL), lambda g, f, te, tr: (g, 0)),
                      pl.BlockSpec((None, None, D_MODEL, tf), lambda g, f, te, tr: (j, te[g], 0, f_eff(g, f, tr))),
                      pl.BlockSpec((None, None, D_MODEL, tf), lambda g, f, te, tr: (j, te[g], 0, f_eff(g, f, tr))),
                      pl.BlockSpec((None, None, tf, D_MODEL), lambda g, f, te, tr: (j, te[g], f_eff(g, f, tr), 0))],
            out_specs=pl.BlockSpec((tm, D_MODEL), lambda g, f, te, tr: (g, 0)),
            scratch_shapes=[pltpu.VMEM((D_MODEL, tf), BF16), pltpu.VMEM((D_MODEL, tf), BF16),
                            pltpu.VMEM((tf, D_MODEL), BF16)],
        ),
        compiler_params=_cparams(("arbitrary", "arbitrary")),
        name="moe_experts",
    )(tile_expert, tile_rows, xs, w_g, w_u, w_d)


def _combine_kernel(n_out, pos1_ref, pos2_ref, x_ref, mod_ref, route_ref, y_ref, *refs):
    o_refs, (g1_scr, g2_scr, sems) = refs[:n_out], refs[n_out:]
    tm = g1_scr.shape[1]
    i = pl.program_id(0)
    n = pl.num_programs(0)

    def start_gathers(tile, slot):
        def make1(r, pos):
            return pltpu.make_async_copy(y_ref.at[pl.ds(pos, 1), :], g1_scr.at[slot, pl.ds(r, 1), :], sems.at[slot])

        def make2(r, pos):
            return pltpu.make_async_copy(y_ref.at[pl.ds(pos, 1), :], g2_scr.at[slot, pl.ds(r, 1), :], sems.at[slot])

        _row_copies(tm, (pos1_ref,), tile * tm, make1)
        _row_copies(tm, (pos2_ref,), tile * tm, make2)

    @pl.when(i == 0)
    def _():
        start_gathers(0, 0)

    slot = i % 2

    @pl.when(i + 1 < n)
    def _():
        start_gathers(i + 1, 1 - slot)

    pltpu.make_async_copy(y_ref.at[pl.ds(0, tm), :], g1_scr.at[slot], sems.at[slot]).wait()
    pltpu.make_async_copy(y_ref.at[pl.ds(0, tm), :], g2_scr.at[slot], sems.at[slot]).wait()
    rec = route_ref[...]
    w1 = rec[:, R_W1:R_W1 + 1]
    w2 = rec[:, R_W2:R_W2 + 1]
    _x_store(o_refs, x_ref[...] + mod_ref[5:6, :] * (w1 * g1_scr[slot] + w2 * g2_scr[slot]))


def _moe_combine(pos1, pos2, x_all, mod3, route, y, split_out):
    tm = ROW_DMA_TM
    return pl.pallas_call(
        functools.partial(_combine_kernel, 2 if split_out else 1),
        out_shape=_x_shapes(split_out),
        grid_spec=pltpu.PrefetchScalarGridSpec(
            num_scalar_prefetch=2,
            grid=(N_TOK // tm,),
            in_specs=[pl.BlockSpec((tm, D_MODEL), lambda i, p1, p2: (i, 0)),
                      pl.BlockSpec((None, 6, D_MODEL), lambda i, p1, p2: (_mod_row(i * tm), 0, 0)),
                      pl.BlockSpec((tm, ROUTER_LANES), lambda i, p1, p2: (i, 0)),
                      pl.BlockSpec(memory_space=pl.ANY)],
            out_specs=_x_specs((None,) * (2 if split_out else 1), tm),
            scratch_shapes=[pltpu.VMEM((2, tm, D_MODEL), F32), pltpu.VMEM((2, tm, D_MODEL), F32),
                            pltpu.SemaphoreType.DMA((2,))],
        ),
        compiler_params=_cparams(("arbitrary",)),
        name="moe_combine",
    )(pos1, pos2, x_all, mod3, route, y)


def _moe(x_all, mod3, g_norm2, w_router_p, w_g, w_u, w_d, l, j, split_out):
    route, cnt = _moe_route(x_all, mod3, g_norm2, w_router_p, l, j)
    cnt = cnt[0, :N_EXPERTS].astype(jnp.int32)
    padded = (cnt + MOE_TM - 1) // MOE_TM * MOE_TM
    ends = jnp.cumsum(padded)
    offs = ends - padded
    experts = jnp.arange(N_EXPERTS, dtype=jnp.int32)

    def positions(i_lane, rank_lane):
        e = route[:, i_lane].astype(jnp.int32)
        off = jnp.sum(jnp.where(e[:, None] == experts[None, :], offs[None, :], 0), axis=1)
        return off + route[:, rank_lane].astype(jnp.int32)

    pos1 = positions(R_I1, R_RANK1)
    pos2 = positions(R_I2, R_RANK2)
    tile_start = jnp.arange(MOE_TILES, dtype=jnp.int32) * MOE_TM
    last_tile = jnp.maximum(ends[-1] - MOE_TM, 0)
    owner_start = jnp.minimum(tile_start, last_tile)
    tile_expert = jnp.minimum(jnp.sum(owner_start[:, None] >= ends[None, :], axis=1), N_EXPERTS - 1).astype(jnp.int32)
    group_end = jnp.sum(jnp.where(tile_expert[:, None] == experts[None, :], (offs + cnt)[None, :], 0), axis=1)
    tile_rows = jnp.where(tile_start < ends[-1], jnp.clip(group_end - tile_start, 0, MOE_TM), 0).astype(jnp.int32)

    xs = _moe_dispatch(pos1, pos2, x_all, mod3, g_norm2, l)
    y = _moe_gmm(tile_expert, tile_rows, xs, w_g, w_u, w_d, j)
    return tuple(_moe_combine(pos1, pos2, x_all, mod3, route, y, split_out))


def _pad_heads(w, per_head, lo):
    lead = w.shape[:-1]
    w = w.reshape(lead + (MLA_HEADS, per_head))
    w = jnp.pad(w, [(0, 0)] * len(lead) + [(0, 0), (lo, HEAD_PAD - lo - per_head)])
    return w.reshape(lead + (MLA_PAD_W,))


def _kpe_to_heads(kpe):
    z = jnp.pad(kpe, [(0, 0)] * (kpe.ndim - 1) + [(MLA_NOPE, HEAD_PAD - MLA_QK)])
    return jnp.tile(z, (1,) * (kpe.ndim - 1) + (MLA_HEADS,))


def _layer_weights(l, w_in, g_qa, w_uq, g_kva, w_ukv, g_mla_q, g_mla_k, sc_w, g_na_q, g_na_k,
                   cf_w, cf_b, cf_ln_g, cf_ln_b, w_br, w_o):
    s2 = MLA_Q_LORA + MLA_KV_LORA
    s3 = s2 + MLA_ROPE
    w_lo = w_in[l, :, :s3]
    w_head = jnp.concatenate([
        w_lo, jnp.zeros((D_MODEL, 256 - MLA_KV_LORA - MLA_ROPE), F32), _kpe_to_heads(w_lo[:, s2:s3])],
        axis=1).astype(BF16)
    ukv = w_ukv[l].reshape(MLA_KV_LORA, MLA_HEADS, MLA_NOPE + MLA_V)
    W = dict(
        w_head=w_head,
        w_tail=w_in[l, :, s3:].astype(BF16),
        g_qa=g_qa[l][None], g_kva=g_kva[l][None],
        w_uq_p=_pad_heads(w_uq[l], MLA_QK, 0).astype(BF16),
        w_ukv_k=_pad_heads(ukv[:, :, :MLA_NOPE].reshape(MLA_KV_LORA, -1), MLA_NOPE, 0).astype(BF16),
        w_ukv_v=_pad_heads(ukv[:, :, MLA_NOPE:].reshape(MLA_KV_LORA, -1), MLA_V, 0).astype(BF16),
        g_mla_q_p=jnp.pad(g_mla_q[l], (0, HEAD_PAD - MLA_QK))[None],
        g_mla_k_p=jnp.pad(g_mla_k[l], (0, HEAD_PAD - MLA_QK))[None],
        g_na_q_t=jnp.tile(g_na_q[l], NA_HEADS)[None], g_na_k_t=jnp.tile(g_na_k[l], NA_HEADS)[None],
        sc_w=sc_w[l], cf_w=cf_w[l], cf_b=cf_b[l][None], cf_ln_g=cf_ln_g[l][None], cf_ln_b=cf_ln_b[l][None],
        w_br_a=jnp.pad(w_br[l, 0].reshape(MLA_HEADS, MLA_V, D_MODEL),
                       ((0, 0), (0, HEAD_PAD - MLA_V), (0, 0))).reshape(MLA_PAD_W, D_MODEL).astype(BF16),
        w_br_bcd=w_br[l, 1:].astype(BF16),
        w_o=w_o[l].astype(BF16),
    )
    return W


def kernel(x_prompt, x_sample, cache_mla_ckv, cache_mla_kpe, cache_na_k, cache_na_v, c, c_ctx, w_ada, b_ada, g_norm1, w_in, g_qa, w_uq, g_kva, w_ukv, g_mla_q, g_mla_k, sc_w, g_na_q, g_na_k, na_rpb, cf_w, cf_b, cf_ln_g, cf_ln_b, w_br, w_o, g_norm2, w_ff_gate, w_ff_up, w_ff_down, w_router, w_e_gate, w_e_up, w_e_down):
    x = (x_prompt.reshape(N_CTX_TOK, D_MODEL), x_sample.reshape(N_LAT_TOK, D_MODEL))
    cvec = jnp.concatenate([c_ctx[None, :], c, jnp.zeros((MOD_ROWS - 1 - DEC_BATCH, D_MODEL), F32)], axis=0)
    cache_na_k2 = cache_na_k.reshape(DEC_BATCH, DEPTH, PAST_LEN, BR_W)
    cache_na_v2 = cache_na_v.reshape(DEC_BATCH, DEPTH, PAST_LEN, BR_W)
    w_router_p = jnp.pad(w_router, ((0, 0), (0, 0), (0, ROUTER_LANES - N_EXPERTS)))

    ckv_l, kpe_l, nak_l, nav_l = [], [], [], []
    for l in range(DEPTH):
        W = _layer_weights(l, w_in, g_qa, w_uq, g_kva, w_ukv, g_mla_q, g_mla_k, sc_w, g_na_q, g_na_k,
                           cf_w, cf_b, cf_ln_g, cf_ln_b, w_br, w_o)
        mod3 = _modulation(cvec, w_ada, b_ada, l).reshape(MOD_ROWS, 6, D_MODEL)
        p = _in_projection(x, mod3, g_norm1, W['w_head'], W['w_tail'], l)

        oa_c, ckv_new, kpe_new = _mla(p, l, False, None, None, W)
        (oa_l,) = _mla(p, l, True, cache_mla_ckv, _kpe_to_heads(cache_mla_kpe[:, l]), W)
        oc_c, nak_new, nav_new = _na_context(p, W)
        oc_l = _na_latent(p, l, cache_na_k2, cache_na_v2, _na_bias_table(na_rpb[l]), W)
        ob_c, od_c = _convs(p, False, W)
        ob_l, od_l = _convs(p, True, W)

        x_all = _merge(x, mod3, p, (oa_c, oa_l), (ob_c, ob_l), (oc_c, oc_l), (od_c, od_l), W)

        last = l == DEPTH - 1
        if l % 2 == 0:
            x = (_ffn(x_all, mod3, g_norm2, w_ff_gate, w_ff_up, w_ff_down, l, l // 2),)
        else:
            x = _moe(x_all, mod3, g_norm2, w_router_p, w_e_gate, w_e_up, w_e_down, l, l // 2, last)

        ckv_l.append(ckv_new)
        kpe_l.append(kpe_new)
        nak_l.append(nak_new.reshape(BATCH, SEQ, NA_HEADS, NA_HD))
        nav_l.append(nav_new.reshape(BATCH, SEQ, NA_HEADS, NA_HD))

    if len(x) == 1:
        x = (x[0][:N_CTX_TOK], x[0][N_CTX_TOK:])
    y_prompt = x[0].reshape(BATCH, SEQ, D_MODEL)
    y_sample = x[1].reshape(DEC_BATCH, DEC_SEQ, D_MODEL)
    return (y_prompt, y_sample, jnp.stack(ckv_l, axis=1), jnp.stack(kpe_l, axis=1),
            jnp.stack(nak_l, axis=1), jnp.stack(nav_l, axis=1))
```

```python
import functools

import numpy as np
import jax
import jax.numpy as jnp
from jax import lax
from jax.experimental import pallas as pl
from jax.experimental.pallas import tpu as pltpu

F32 = jnp.float32
BF16 = jnp.bfloat16

D_MODEL = 1024
BATCH = 16
SEQ = 256
DEPTH = 2
DEC_BATCH = 8
DEC_SEQ = 1024
PAST_LEN = 512
GRID_W = 64
N_BRANCH = 4
BR_W = 256
MLA_HEADS = 4
MLA_NOPE = 64
MLA_ROPE = 32
MLA_QK = 96
MLA_V = 64
MLA_Q_LORA = 256
MLA_KV_LORA = 128
SC_K = 3
NA_HEADS = 4
NA_HD = 64
NA_WIN_R = 8
NA_WIN_C = 16
CF_K = 31
D_FF = 3584
N_EXPERTS = 8
ROPE_THETA = 10000.0
EPS = 1e-6

N_CTX_TOK = BATCH * SEQ
N_LAT_TOK = DEC_BATCH * DEC_SEQ
N_TOK = N_CTX_TOK + N_LAT_TOK
MOD_ROWS = 16

HEAD_PAD = 128
MLA_PAD_W = MLA_HEADS * HEAD_PAD

P_QA = 0
P_KVA = 256
P_KPE4 = 512
P_HEAD = 1024
P_SC = 1024
P_NA = 1792
P_CF = 2560
P_GATE = 3072
P_COLS = 7168

VMEM_LIMIT = 56 * 1024 * 1024

NA_TQ = 256
NA_WIN_ROWS = 12
NA_WIN_KEYS = NA_WIN_ROWS * GRID_W


def _cparams(sem):
    return pltpu.CompilerParams(dimension_semantics=sem, vmem_limit_bytes=VMEM_LIMIT)


def _const_spec(shape):
    nd = len(shape)
    return pl.BlockSpec(shape, lambda *_: (0,) * nd)


def _mod_row(tok_start):
    return jnp.where(tok_start < N_CTX_TOK, 0, 1 + (tok_start - N_CTX_TOK) // DEC_SEQ)


def _sigmoid(x):
    return 0.5 * jnp.tanh(0.5 * x) + 0.5


def _silu(x):
    return x * _sigmoid(x)


def _ctx_or_lat(is_ctx, c_ref, l_ref):
    return jnp.where(is_ctx, c_ref[...], l_ref[...])


def _dot(a, b):
    return jnp.dot(a.astype(BF16), b.astype(BF16), preferred_element_type=F32)


def _dot_nt(a, b):
    return lax.dot_general(a.astype(BF16), b.astype(BF16), (((1,), (1,)), ((), ())),
                           preferred_element_type=F32)


def _mod_kernel(c_ref, w_ref, b_ref, o_ref):
    o_ref[...] = _dot(_silu(c_ref[...]), w_ref[...]) + b_ref[...]


def _modulation(cvec, w_ada, b_ada, l):
    tn = 1024
    return pl.pallas_call(
        _mod_kernel,
        out_shape=jax.ShapeDtypeStruct((MOD_ROWS, 6 * D_MODEL), F32),
        grid=(6 * D_MODEL // tn,),
        in_specs=[
            _const_spec((MOD_ROWS, D_MODEL)),
            pl.BlockSpec((None, D_MODEL, tn), lambda j: (l, 0, j)),
            pl.BlockSpec((None, 1, tn), lambda j: (l, 0, j)),
        ],
        out_specs=pl.BlockSpec((MOD_ROWS, tn), lambda j: (0, j)),
        compiler_params=_cparams(("arbitrary",)),
        name="modulation",
    )(cvec, w_ada, b_ada.reshape(DEPTH, 1, 6 * D_MODEL))


def _pair_specs(tm, width, buffers=None):
    nc = N_CTX_TOK // tm
    mode = {} if buffers is None else dict(pipeline_mode=pl.Buffered(buffers))
    return [pl.BlockSpec((tm, width), lambda i, *_: (jnp.minimum(i, nc - 1), 0), **mode),
            pl.BlockSpec((tm, width), lambda i, *_: (jnp.maximum(i - nc, 0), 0), **mode)]


def _pair_shapes(width, dtype):
    return [jax.ShapeDtypeStruct((N_CTX_TOK, width), dtype), jax.ShapeDtypeStruct((N_LAT_TOK, width), dtype)]


def _store_pair(is_ctx, c_ref, l_ref, val):
    @pl.when(is_ctx)
    def _():
        c_ref[...] = val

    @pl.when(jnp.logical_not(is_ctx))
    def _():
        l_ref[...] = val


def _x_specs(x, tm, buffers=None):
    if len(x) == 2:
        return _pair_specs(tm, D_MODEL, buffers)
    mode = {} if buffers is None else dict(pipeline_mode=pl.Buffered(buffers))
    return [pl.BlockSpec((tm, D_MODEL), lambda i, *_: (i, 0), **mode)]


def _x_shapes(split):
    return _pair_shapes(D_MODEL, F32) if split else [jax.ShapeDtypeStruct((N_TOK, D_MODEL), F32)]


def _x_load(x_refs):
    if len(x_refs) == 1:
        return x_refs[0][...]
    tm = x_refs[0].shape[0]
    return _ctx_or_lat(pl.program_id(0) < N_CTX_TOK // tm, *x_refs)


def _x_store(y_refs, val):
    if len(y_refs) == 1:
        y_refs[0][...] = val
    else:
        tm = y_refs[0].shape[0]
        _store_pair(pl.program_id(0) < N_CTX_TOK // tm, *y_refs, val)


def _inproj_kernel(nx, *refs):
    x_refs, (mod_ref, g_ref, wh_ref, wt_ref, o_ref, h_scr) = refs[:nx], refs[nx:]
    j = pl.program_id(1)

    @pl.when(j == 0)
    def _():
        x = _x_load(x_refs)
        y = x * lax.rsqrt(jnp.mean(x * x, axis=-1, keepdims=True) + EPS) * g_ref[...]
        h_scr[...] = (y * (1.0 + mod_ref[1:2, :]) + mod_ref[0:1, :]).astype(BF16)
        o_ref[...] = jnp.dot(h_scr[...], wh_ref[...], preferred_element_type=F32).astype(o_ref.dtype)

    @pl.when(j > 0)
    def _():
        o_ref[...] = jnp.dot(h_scr[...], wt_ref[...], preferred_element_type=F32).astype(o_ref.dtype)


def _in_projection(x, mod3, g_norm1, w_head, w_tail, l):
    tm, tn = 1024, 1024
    return pl.pallas_call(
        functools.partial(_inproj_kernel, len(x)),
        out_shape=jax.ShapeDtypeStruct((N_TOK, P_COLS), BF16),
        grid=(N_TOK // tm, P_COLS // tn),
        in_specs=_x_specs(x, tm) + [
            pl.BlockSpec((None, 6, D_MODEL), lambda i, j: (_mod_row(i * tm), 0, 0)),
            pl.BlockSpec((None, 1, D_MODEL), lambda i, j: (l, 0, 0)),
            _const_spec((D_MODEL, P_HEAD)),
            pl.BlockSpec((D_MODEL, tn), lambda i, j: (0, jnp.maximum(j - 1, 0))),
        ],
        out_specs=pl.BlockSpec((tm, tn), lambda i, j: (i, j)),
        scratch_shapes=[pltpu.VMEM((tm, D_MODEL), BF16)],
        compiler_params=_cparams(("arbitrary", "arbitrary")),
        name="in_projection",
    )(*x, mod3, g_norm1.reshape(DEPTH, 1, D_MODEL), w_head, w_tail)


def _rope_tables():
    t = np.arange(DEC_SEQ)
    nf = MLA_ROPE // 4
    inv = (np.float32(ROPE_THETA) ** (-np.arange(nf, dtype=np.float32) / np.float32(nf))).astype(np.float32)
    ang_r = (t // GRID_W).astype(np.float32)[:, None] * inv[None, :]
    ang_c = (t % GRID_W).astype(np.float32)[:, None] * inv[None, :]
    c = np.zeros((DEC_SEQ, HEAD_PAD), np.float32)
    s1 = np.zeros((DEC_SEQ, HEAD_PAD), np.float32)
    s2 = np.zeros((DEC_SEQ, HEAD_PAD), np.float32)
    c[:, :MLA_NOPE] = 1.0
    for base, ang in ((MLA_NOPE, ang_r), (MLA_NOPE + 2 * nf, ang_c)):
        c[:, base:base + nf] = np.cos(ang)
        c[:, base + nf:base + 2 * nf] = np.cos(ang)
        s1[:, base:base + nf] = -np.sin(ang)
        s2[:, base + nf:base + 2 * nf] = np.sin(ang)
    return c, s1, s2


def _rope(x, c, s1, s2):
    nf = MLA_ROPE // 4
    return x * c + pltpu.roll(x, HEAD_PAD - nf, 1) * s1 + pltpu.roll(x, nf, 1) * s2


def _head_norm(xh, g):
    ms = jnp.sum(xh * xh, axis=-1, keepdims=True) * (1.0 / MLA_QK)
    return xh * lax.rsqrt(ms + EPS) * g


def _mla_kernel(latent, tq, *refs):
    if latent:
        (pqa_ref, pkva_ref, pkpe_ref, cckv_ref, ckpe_ref, rc_ref, rs1_ref, rs2_ref,
         gqa_ref, wuq_ref, gkva_ref, wk_ref, wv_ref, gq_ref, gk_ref,
         o_ref, k_scr, v_scr) = refs
    else:
        (pqa_ref, pkva_ref, pkpe_ref,
         gqa_ref, wuq_ref, gkva_ref, wk_ref, wv_ref, gq_ref, gk_ref,
         o_ref, ckv_ref, kpe_ref, k_scr, v_scr) = refs
    qi = pl.program_id(1)
    n_past = PAST_LEN if latent else 0

    def put_kv(ckvn, kpe4, row0, rope):
        n = ckvn.shape[0]
        kk = _dot(ckvn, wk_ref[...]) + kpe4
        v_scr[row0:row0 + n, :] = _dot(ckvn, wv_ref[...]).astype(BF16)
        for h in range(MLA_HEADS):
            sl = slice(h * HEAD_PAD, (h + 1) * HEAD_PAD)
            kh = _head_norm(kk[:, sl], gk_ref[...])
            if rope:
                kh = _rope(kh, rc_ref[...], rs1_ref[...], rs2_ref[...])
            k_scr[row0:row0 + n, sl] = kh.astype(BF16)

    @pl.when(qi == 0)
    def _():
        if latent:
            put_kv(cckv_ref[...], ckpe_ref[...], 0, False)
        kva = pkva_ref[...].astype(F32)
        ckv = kva[:, :MLA_KV_LORA]
        ckvn = ckv * lax.rsqrt(jnp.mean(ckv * ckv, axis=-1, keepdims=True) + EPS) * gkva_ref[...]
        if not latent:
            ckv_ref[...] = ckvn
            kpe_ref[...] = kva[:, MLA_KV_LORA:MLA_KV_LORA + MLA_ROPE]
        put_kv(ckvn, pkpe_ref[...].astype(F32), n_past, latent)

    qa = pqa_ref[...].astype(F32)
    qan = qa * lax.rsqrt(jnp.mean(qa * qa, axis=-1, keepdims=True) + EPS) * gqa_ref[...]
    q = _dot(qan, wuq_ref[...])
    scale = MLA_QK ** -0.5
    for h in range(MLA_HEADS):
        sl = slice(h * HEAD_PAD, (h + 1) * HEAD_PAD)
        qh = _head_norm(q[:, sl], gq_ref[...])
        if latent:
            rows = pl.ds(pl.multiple_of(qi * tq, tq), tq)
            qh = _rope(qh, rc_ref[rows, :], rs1_ref[rows, :], rs2_ref[rows, :])
        s = _dot_nt(qh, k_scr[:, sl]) * scale
        m = jnp.max(s, axis=-1, keepdims=True)
        e = jnp.exp(s - m)
        den = jnp.sum(e, axis=-1, keepdims=True)
        o = _dot(e, v_scr[:, sl]) / den
        o_ref[:, sl] = o.astype(o_ref.dtype)


def _mla(p, l, latent, cache_ckv, cache_kpe4, W):
    if latent:
        nb, s, tq, row_off = DEC_BATCH, DEC_SEQ, 256, N_CTX_TOK
    else:
        nb, s, tq, row_off = BATCH, SEQ, 256, 0
    nq = s // tq
    sk = s + (PAST_LEN if latent else 0)
    in_specs = [
        pl.BlockSpec((tq, MLA_Q_LORA), lambda b, qi: (row_off // tq + b * nq + qi, P_QA // MLA_Q_LORA)),
        pl.BlockSpec((s, 256), lambda b, qi: (row_off // s + b, P_KVA // 256)),
        pl.BlockSpec((s, MLA_PAD_W), lambda b, qi: (row_off // s + b, P_KPE4 // MLA_PAD_W)),
    ]
    args = [p, p, p]
    if latent:
        in_specs += [
            pl.BlockSpec((None, None, PAST_LEN, MLA_KV_LORA), lambda b, qi: (b, l, 0, 0)),
            pl.BlockSpec((None, PAST_LEN, MLA_PAD_W), lambda b, qi: (b, 0, 0)),
            _const_spec((DEC_SEQ, HEAD_PAD)), _const_spec((DEC_SEQ, HEAD_PAD)), _const_spec((DEC_SEQ, HEAD_PAD)),
        ]
        args += [cache_ckv, cache_kpe4] + [jnp.asarray(t) for t in _rope_tables()]
    in_specs += [
        _const_spec((1, MLA_Q_LORA)), _const_spec((MLA_Q_LORA, MLA_PAD_W)), _const_spec((1, MLA_KV_LORA)),
        _const_spec((MLA_KV_LORA, MLA_PAD_W)), _const_spec((MLA_KV_LORA, MLA_PAD_W)),
        _const_spec((1, HEAD_PAD)), _const_spec((1, HEAD_PAD)),
    ]
    args += [W['g_qa'], W['w_uq_p'], W['g_kva'], W['w_ukv_k'], W['w_ukv_v'], W['g_mla_q_p'], W['g_mla_k_p']]
    out_shape = [jax.ShapeDtypeStruct((nb * s, MLA_PAD_W), BF16)]
    out_specs = [pl.BlockSpec((tq, MLA_PAD_W), lambda b, qi: (b * nq + qi, 0))]
    if not latent:
        out_shape.append(jax.ShapeDtypeStruct((nb, s, MLA_KV_LORA), F32))
        out_specs.append(pl.BlockSpec((None, s, MLA_KV_LORA), lambda b, qi: (b, 0, 0)))
        out_shape.append(jax.ShapeDtypeStruct((nb, s, MLA_ROPE), F32))
        out_specs.append(pl.BlockSpec((None, s, MLA_ROPE), lambda b, qi: (b, 0, 0)))
    return pl.pallas_call(
        functools.partial(_mla_kernel, latent, tq),
        out_shape=out_shape,
        grid=(nb, nq),
        in_specs=in_specs,
        out_specs=out_specs,
        scratch_shapes=[pltpu.VMEM((sk, MLA_PAD_W), BF16), pltpu.VMEM((sk, MLA_PAD_W), BF16)],
        compiler_params=_cparams(("arbitrary", "arbitrary")),
        name="mla_latent" if latent else "mla_context",
    )(*args)


def _head_masks(width):
    lane = lax.broadcasted_iota(jnp.int32, (1, width), 1)
    return [(lane >= h * NA_HD) & (lane < (h + 1) * NA_HD) for h in range(NA_HEADS)]


def _group_norm64(x, g, masks):
    x2 = x * x
    inv = jnp.zeros_like(x)
    for m in masks:
        ms = jnp.sum(jnp.where(m, x2, 0.0), axis=-1, keepdims=True) * (1.0 / NA_HD)
        inv = jnp.where(m, lax.rsqrt(ms + EPS), inv)
    return x * inv * g


def _p_blocks(col0, n, rows, row_fn):
    def spec(k):
        return pl.BlockSpec((rows, BR_W), lambda *g: (row_fn(*g), col0 // BR_W + k))
    return [spec(k) for k in range(n)]


def _na_ctx_kernel(pq_ref, pk_ref, pv_ref, gq_ref, gk_ref, o_ref, k_ref, v_ref):
    masks = _head_masks(BR_W)
    qn = _group_norm64(pq_ref[...].astype(F32), gq_ref[...], masks)
    kn = _group_norm64(pk_ref[...].astype(F32), gk_ref[...], masks)
    v = pv_ref[...].astype(F32)
    k_ref[...] = kn
    v_ref[...] = v
    scale = NA_HD ** -0.5
    acc = jnp.zeros((pq_ref.shape[0], BR_W), F32)
    for m in masks:
        s = _dot_nt(jnp.where(m, qn, 0.0), kn) * scale
        mx = jnp.max(s, axis=-1, keepdims=True)
        e = jnp.exp(s - mx)
        den = jnp.sum(e, axis=-1, keepdims=True)
        acc = acc + jnp.where(m, _dot(e, v) / den, 0.0)
    o_ref[...] = acc.astype(o_ref.dtype)


def _na_context(p, W):
    s = SEQ
    return pl.pallas_call(
        _na_ctx_kernel,
        out_shape=[jax.ShapeDtypeStruct((N_CTX_TOK, BR_W), BF16),
                   jax.ShapeDtypeStruct((N_CTX_TOK, BR_W), F32),
                   jax.ShapeDtypeStruct((N_CTX_TOK, BR_W), F32)],
        grid=(BATCH,),
        in_specs=_p_blocks(P_NA, 3, s, lambda b: b) + [_const_spec((1, BR_W)), _const_spec((1, BR_W))],
        out_specs=[pl.BlockSpec((s, BR_W), lambda b: (b, 0))] * 3,
        compiler_params=_cparams(("arbitrary",)),
        name="na_context",
    )(p, p, p, W['g_na_q_t'], W['g_na_k_t'])


def _na_lat_kernel(pq_ref, pk_ref, pv_ref, ck_ref, cv_ref, bias_ref, gq_ref, gk_ref, o_ref,
                   q_scr, k_scr, v_scr, kc_scr, vc_scr):
    j = pl.program_id(1)
    masks = _head_masks(BR_W)

    @pl.when(j == 0)
    def _():
        q_scr[...] = _group_norm64(pq_ref[...].astype(F32), gq_ref[...], masks).astype(BF16)
        k_scr[...] = _group_norm64(pk_ref[...].astype(F32), gk_ref[...], masks).astype(BF16)
        v_scr[...] = pv_ref[...]
        kc_scr[...] = ck_ref[...].astype(BF16)
        vc_scr[...] = cv_ref[...].astype(BF16)

    win0 = pl.multiple_of(jnp.where(j < 2, 0, DEC_SEQ - NA_WIN_KEYS), 256)
    q = q_scr[pl.ds(pl.multiple_of(j * NA_TQ, NA_TQ), NA_TQ), :]
    kw = k_scr[pl.ds(win0, NA_WIN_KEYS), :]
    vw = v_scr[pl.ds(win0, NA_WIN_KEYS), :]
    kc = kc_scr[...]
    vc = vc_scr[...]
    scale = NA_HD ** -0.5
    acc = jnp.zeros((NA_TQ, BR_W), F32)
    zero = jnp.zeros((), BF16)
    for h, m in enumerate(masks):
        qm = jnp.where(m, q, zero)
        s_loc = _dot_nt(qm, kw) * scale + bias_ref[h]
        s_ctx = _dot_nt(qm, kc) * scale
        mx = jnp.maximum(jnp.max(s_loc, axis=-1, keepdims=True), jnp.max(s_ctx, axis=-1, keepdims=True))
        e_loc = jnp.exp(s_loc - mx)
        e_ctx = jnp.exp(s_ctx - mx)
        den = jnp.sum(e_loc, axis=-1, keepdims=True) + jnp.sum(e_ctx, axis=-1, keepdims=True)
        o = (_dot(e_loc, vw) + _dot(e_ctx, vc)) / den
        acc = acc + jnp.where(m, o, 0.0)
    o_ref[...] = acc.astype(o_ref.dtype)


def _na_bias_table(rpb):
    n_dr, n_dc = 2 * NA_WIN_R - 1, 2 * NA_WIN_C - 1
    cols = np.arange(GRID_W)
    dc = np.clip(cols[None, :] - cols[:, None], -(NA_WIN_C - 1), NA_WIN_C - 1) + (NA_WIN_C - 1)
    cstart = np.clip(cols - NA_WIN_C // 2, 0, GRID_W - NA_WIN_C)
    col_ok = (cols[None, :] >= cstart[:, None]) & (cols[None, :] < cstart[:, None] + NA_WIN_C)
    place = (dc.reshape(-1)[None, :] == np.arange(n_dc)[:, None]).astype(np.float32)
    blocks = jnp.dot(rpb.reshape(NA_HEADS * n_dr, n_dc), place, precision=lax.Precision.HIGHEST)
    blocks = jnp.where(col_ok.reshape(-1)[None, :], blocks, -jnp.inf).reshape(NA_HEADS, n_dr, GRID_W, GRID_W)
    neg = jnp.full((NA_HEADS, 1, GRID_W, GRID_W), -jnp.inf, F32)
    blocks = jnp.concatenate([blocks, neg], axis=1)

    nt = DEC_SEQ // NA_TQ
    n_rows = DEC_SEQ // GRID_W
    rows_per_tile = NA_TQ // GRID_W
    sel = np.full((nt, rows_per_tile, NA_WIN_ROWS), n_dr, np.int32)
    for j in range(nt):
        win_row0 = 0 if j < nt // 2 else n_rows - NA_WIN_ROWS
        for rq in range(rows_per_tile):
            r = j * rows_per_tile + rq
            start = min(max(r - NA_WIN_R // 2, 0), n_rows - NA_WIN_R)
            for kr in range(NA_WIN_ROWS):
                if start <= win_row0 + kr < start + NA_WIN_R:
                    sel[j, rq, kr] = win_row0 + kr - r + (NA_WIN_R - 1)
    tiles = []
    for j in range(nt):
        rows = [jnp.concatenate([blocks[:, sel[j, rq, kr]] for kr in range(NA_WIN_ROWS)], axis=-1)
                for rq in range(rows_per_tile)]
        tiles.append(jnp.concatenate(rows, axis=1))
    return jnp.stack(tiles, axis=1)


def _na_latent(p, l, cache_k, cache_v, bias, W):
    s = DEC_SEQ
    nt = s // NA_TQ
    row_off = N_CTX_TOK
    return pl.pallas_call(
        _na_lat_kernel,
        out_shape=jax.ShapeDtypeStruct((N_LAT_TOK, BR_W), BF16),
        grid=(DEC_BATCH, nt),
        in_specs=_p_blocks(P_NA, 3, s, lambda b, j: row_off // s + b) + [
                  pl.BlockSpec((None, None, PAST_LEN, BR_W), lambda b, j: (b, l, 0, 0)),
                  pl.BlockSpec((None, None, PAST_LEN, BR_W), lambda b, j: (b, l, 0, 0)),
                  pl.BlockSpec((NA_HEADS, None, NA_TQ, NA_WIN_KEYS), lambda b, j: (0, j, 0, 0)),
                  _const_spec((1, BR_W)), _const_spec((1, BR_W))],
        out_specs=pl.BlockSpec((NA_TQ, BR_W), lambda b, j: (b * nt + j, 0)),
        scratch_shapes=[pltpu.VMEM((s, BR_W), BF16), pltpu.VMEM((s, BR_W), BF16), pltpu.VMEM((s, BR_W), BF16),
                        pltpu.VMEM((PAST_LEN, BR_W), BF16), pltpu.VMEM((PAST_LEN, BR_W), BF16)],
        compiler_params=_cparams(("arbitrary", "arbitrary")),
        name="na_latent",
    )(p, p, p, cache_k, cache_v, bias, W['g_na_q_t'], W['g_na_k_t'])


CONV_HALO = 16
CONV_CHUNK = 128


def _dwconv_from_pad(pad_ref, w_ref, ksize, s, emit):
    half = ksize // 2
    for c0 in range(0, s, CONV_CHUNK):
        acc = jnp.zeros((CONV_CHUNK, BR_W), F32)
        for k in range(ksize):
            r0 = CONV_HALO + c0 + k - half
            acc = acc + pad_ref[r0:r0 + CONV_CHUNK, :] * w_ref[k:k + 1, :]
        emit(c0, acc)


def _conv_kernel(s, scb_ref, scc_ref, scx_ref, cfa_ref, cfb2_ref, scw_ref, cfw_ref, cfb_ref, lng_ref, lnb_ref,
                 ob_ref, od_ref, pad_ref):
    zeros = jnp.zeros((CONV_HALO, BR_W), F32)
    pad_ref[0:CONV_HALO, :] = zeros
    pad_ref[CONV_HALO + s:2 * CONV_HALO + s, :] = zeros

    pad_ref[CONV_HALO:CONV_HALO + s, :] = scc_ref[...].astype(F32) * scx_ref[...].astype(F32)

    def emit_b(c0, acc):
        ob_ref[c0:c0 + CONV_CHUNK, :] = (scb_ref[c0:c0 + CONV_CHUNK, :].astype(F32) * acc).astype(ob_ref.dtype)

    _dwconv_from_pad(pad_ref, scw_ref, SC_K, s, emit_b)

    pad_ref[CONV_HALO:CONV_HALO + s, :] = cfa_ref[...].astype(F32) * _sigmoid(cfb2_ref[...].astype(F32))

    def emit_d(c0, acc):
        u = acc + cfb_ref[...]
        mu = jnp.mean(u, axis=-1, keepdims=True)
        d = u - mu
        var = jnp.mean(d * d, axis=-1, keepdims=True)
        y = d * lax.rsqrt(var + EPS) * lng_ref[...] + lnb_ref[...]
        od_ref[c0:c0 + CONV_CHUNK, :] = _silu(y).astype(od_ref.dtype)

    _dwconv_from_pad(pad_ref, cfw_ref, CF_K, s, emit_d)


def _convs(p, latent, W):
    if latent:
        nb, s, row_off = DEC_BATCH, DEC_SEQ, N_CTX_TOK
    else:
        nb, s, row_off = BATCH, SEQ, 0
    return pl.pallas_call(
        functools.partial(_conv_kernel, s),
        out_shape=[jax.ShapeDtypeStruct((nb * s, BR_W), BF16)] * 2,
        grid=(nb,),
        in_specs=(_p_blocks(P_SC, 3, s, lambda b: row_off // s + b)
                  + _p_blocks(P_CF, 2, s, lambda b: row_off // s + b)
                  + [_const_spec((SC_K, BR_W)), _const_spec((CF_K, BR_W)),
                     _const_spec((1, BR_W)), _const_spec((1, BR_W)), _const_spec((1, BR_W))]),
        out_specs=[pl.BlockSpec((s, BR_W), lambda b: (b, 0))] * 2,
        scratch_shapes=[pltpu.VMEM((s + 2 * CONV_HALO, BR_W), F32)],
        compiler_params=_cparams(("arbitrary",)),
        name="convs_latent" if latent else "convs_context",
    )(p, p, p, p, p, W['sc_w'], W['cf_w'], W['cf_b'], W['cf_ln_g'], W['cf_ln_b'])


def _merge_kernel(nc, nx, *refs):
    x_refs = refs[:nx]
    (mod_ref, oac_ref, oal_ref, obc_ref, obl_ref, occ_ref, ocl_ref, odc_ref, odl_ref,
     g0_ref, g1_ref, g2_ref, g3_ref, wa_ref, wb_ref, wo_ref, y_ref) = refs[nx:]
    is_ctx = pl.program_id(0) < nc
    merged = (_sigmoid(g0_ref[...]).astype(F32)
              * jnp.dot(_ctx_or_lat(is_ctx, oac_ref, oal_ref), wa_ref[...], preferred_element_type=F32))
    branches = ((obc_ref, obl_ref, g1_ref), (occ_ref, ocl_ref, g2_ref), (odc_ref, odl_ref, g3_ref))
    for n, (oc_ref, ol_ref, g_ref) in enumerate(branches):
        merged = merged + (_sigmoid(g_ref[...]).astype(F32)
                           * jnp.dot(_ctx_or_lat(is_ctx, oc_ref, ol_ref), wb_ref[n], preferred_element_type=F32))
    y_ref[...] = _x_load(x_refs) + mod_ref[2:3, :] * _dot(merged, wo_ref[...])


def _merge(x, mod3, p, o_a, o_b, o_c, o_d, W):
    tm = 512
    gate_spec = lambda n: pl.BlockSpec((tm, D_MODEL), lambda i: (i, P_GATE // D_MODEL + n))
    return pl.pallas_call(
        functools.partial(_merge_kernel, N_CTX_TOK // tm, len(x)),
        out_shape=jax.ShapeDtypeStruct((N_TOK, D_MODEL), F32),
        grid=(N_TOK // tm,),
        in_specs=(_x_specs(x, tm)
                  + [pl.BlockSpec((None, 6, D_MODEL), lambda i: (_mod_row(i * tm), 0, 0))]
                  + _pair_specs(tm, MLA_PAD_W) + _pair_specs(tm, BR_W) + _pair_specs(tm, BR_W) + _pair_specs(tm, BR_W)
                  + [gate_spec(0), gate_spec(1), gate_spec(2), gate_spec(3),
                     _const_spec((MLA_PAD_W, D_MODEL)), _const_spec((3, BR_W, D_MODEL)),
                     _const_spec((D_MODEL, D_MODEL))]),
        out_specs=pl.BlockSpec((tm, D_MODEL), lambda i: (i, 0)),
        compiler_params=_cparams(("arbitrary",)),
        name="merge",
    )(*x, mod3, *o_a, *o_b, *o_c, *o_d, p, p, p, p, W['w_br_a'], W['w_br_bcd'], W['w_o'])


def _norm2(x, g, mod_ref):
    y = x * lax.rsqrt(jnp.mean(x * x, axis=-1, keepdims=True) + EPS) * g
    return y * (1.0 + mod_ref[4:5, :]) + mod_ref[3:4, :]


def _ffn_kernel(x_ref, mod_ref, g_ref, wg_ref, wu_ref, wd_ref, y_ref, h_scr, acc_scr):
    f = pl.program_id(1)

    @pl.when(f == 0)
    def _():
        h_scr[...] = _norm2(x_ref[...], g_ref[...], mod_ref).astype(BF16)
        acc_scr[...] = jnp.zeros_like(acc_scr)

    h = h_scr[...]
    a = _silu(_dot(h, wg_ref[...])) * _dot(h, wu_ref[...])
    acc_scr[...] += _dot(a, wd_ref[...])

    @pl.when(f == pl.num_programs(1) - 1)
    def _():
        y_ref[...] = x_ref[...] + mod_ref[5:6, :] * acc_scr[...]


def _ffn(x_all, mod3, g_norm2, w_g, w_u, w_d, l, j):
    tm, tf = 1024, 512
    return pl.pallas_call(
        _ffn_kernel,
        out_shape=jax.ShapeDtypeStruct((N_TOK, D_MODEL), F32),
        grid=(N_TOK // tm, D_FF // tf),
        in_specs=[pl.BlockSpec((tm, D_MODEL), lambda i, f: (i, 0)),
                  pl.BlockSpec((None, 6, D_MODEL), lambda i, f: (_mod_row(i * tm), 0, 0)),
                  pl.BlockSpec((None, 1, D_MODEL), lambda i, f: (l, 0, 0)),
                  pl.BlockSpec((None, D_MODEL, tf), lambda i, f: (j, 0, f)),
                  pl.BlockSpec((None, D_MODEL, tf), lambda i, f: (j, 0, f)),
                  pl.BlockSpec((None, tf, D_MODEL), lambda i, f: (j, f, 0))],
        out_specs=pl.BlockSpec((tm, D_MODEL), lambda i, f: (i, 0)),
        scratch_shapes=[pltpu.VMEM((tm, D_MODEL), BF16), pltpu.VMEM((tm, D_MODEL), F32)],
        compiler_params=_cparams(("arbitrary", "arbitrary")),
        name="ffn_dense",
    )(x_all, mod3, g_norm2.reshape(DEPTH, 1, D_MODEL), w_g, w_u, w_d)


ROUTER_LANES = 128
ROUTE_TM = 512
MOE_TM = 1024
MOE_SUB = 256
MOE_TF = 512
MOE_ROWS = 2 * N_TOK + N_EXPERTS * MOE_TM
MOE_TILES = MOE_ROWS // MOE_TM
ROW_DMA_TM = 256
R_I1, R_I2, R_W1, R_W2, R_RANK1, R_RANK2 = range(6)


def _route_top2(h, wr):
    logits = jnp.dot(h, wr, preferred_element_type=F32, precision=lax.Precision.HIGHEST)
    lane = lax.broadcasted_iota(jnp.int32, logits.shape, 1).astype(F32)
    neg = jnp.float32(-jnp.inf)
    logits = jnp.where(lane < N_EXPERTS, logits, neg)
    m1 = jnp.max(logits, axis=-1, keepdims=True)
    i1 = jnp.min(jnp.where(logits == m1, lane, float(ROUTER_LANES)), axis=-1, keepdims=True)
    rest = jnp.where(lane == i1, neg, logits)
    m2 = jnp.max(rest, axis=-1, keepdims=True)
    i2 = jnp.min(jnp.where(rest == m2, lane, float(ROUTER_LANES)), axis=-1, keepdims=True)
    e2 = jnp.exp(m2 - m1)
    return lane, i1, i2, 1.0 / (1.0 + e2), e2 / (1.0 + e2)


def _route_kernel(x_ref, mod_ref, g_ref, wr_ref, route_ref, cnt_ref, carry_scr):
    @pl.when(pl.program_id(0) == 0)
    def _():
        carry_scr[...] = jnp.zeros_like(carry_scr)

    h = _norm2(x_ref[...], g_ref[...], mod_ref)
    lane, i1, i2, w1, w2 = _route_top2(h, wr_ref[...])
    tm = h.shape[0]
    oh1 = lane == i1
    oh2 = lane == i2
    oh = jnp.where(oh1, 1.0, 0.0) + jnp.where(oh2, 1.0, 0.0)
    r = lax.broadcasted_iota(jnp.int32, (tm, tm), 0)
    c = lax.broadcasted_iota(jnp.int32, (tm, tm), 1)
    lower = jnp.where(r > c, 1.0, 0.0).astype(BF16)
    before = jnp.dot(lower, oh.astype(BF16), preferred_element_type=F32) + carry_scr[...]
    rank1 = jnp.sum(jnp.where(oh1, before, 0.0), axis=-1, keepdims=True)
    rank2 = jnp.sum(jnp.where(oh2, before, 0.0), axis=-1, keepdims=True)
    carry_scr[...] += jnp.sum(oh, axis=0, keepdims=True)
    rec = jnp.zeros_like(lane)
    for k, v in ((R_I1, i1), (R_I2, i2), (R_W1, w1), (R_W2, w2), (R_RANK1, rank1), (R_RANK2, rank2)):
        rec = jnp.where(lane == float(k), v, rec)
    route_ref[...] = rec
    cnt_ref[...] = carry_scr[...]


def _moe_route(x_all, mod3, g_norm2, w_router_p, l, j):
    tm = ROUTE_TM
    return pl.pallas_call(
        _route_kernel,
        out_shape=[jax.ShapeDtypeStruct((N_TOK, ROUTER_LANES), F32),
                   jax.ShapeDtypeStruct((1, ROUTER_LANES), F32)],
        grid=(N_TOK // tm,),
        in_specs=[pl.BlockSpec((tm, D_MODEL), lambda i: (i, 0)),
                  pl.BlockSpec((None, 6, D_MODEL), lambda i: (_mod_row(i * tm), 0, 0)),
                  pl.BlockSpec((None, 1, D_MODEL), lambda i: (l, 0, 0)),
                  pl.BlockSpec((None, D_MODEL, ROUTER_LANES), lambda i: (j, 0, 0))],
        out_specs=[pl.BlockSpec((tm, ROUTER_LANES), lambda i: (i, 0)),
                   pl.BlockSpec((1, ROUTER_LANES), lambda i: (0, 0))],
        scratch_shapes=[pltpu.VMEM((1, ROUTER_LANES), F32)],
        compiler_params=_cparams(("arbitrary",)),
        name="moe_route",
    )(x_all, mod3, g_norm2.reshape(DEPTH, 1, D_MODEL), w_router_p)


def _row_copies(n, pos_refs, base, make):
    def body(r, carry):
        for pos_ref in pos_refs:
            make(r, pos_ref[base + r]).start()
        return carry
    lax.fori_loop(0, n, body, 0, unroll=8)


def _dispatch_kernel(pos1_ref, pos2_ref, x_ref, mod_ref, g_ref, xs_init_ref, xs_ref, h_scr, sem):
    del xs_init_ref
    tm = h_scr.shape[0]
    h_scr[...] = _norm2(x_ref[...], g_ref[...], mod_ref)
    base = pl.program_id(0) * tm

    def make(r, pos):
        return pltpu.make_async_copy(h_scr.at[pl.ds(r, 1), :], xs_ref.at[pl.ds(pos, 1), :], sem)

    _row_copies(tm, (pos1_ref, pos2_ref), base, make)
    for _ in range(2):
        pltpu.make_async_copy(h_scr, xs_ref.at[pl.ds(0, tm), :], sem).wait()


def _moe_dispatch(pos1, pos2, x_all, mod3, g_norm2, l):
    tm = ROW_DMA_TM
    xs_init = jnp.zeros((MOE_ROWS, D_MODEL), F32)
    return pl.pallas_call(
        _dispatch_kernel,
        out_shape=jax.ShapeDtypeStruct((MOE_ROWS, D_MODEL), F32),
        grid_spec=pltpu.PrefetchScalarGridSpec(
            num_scalar_prefetch=2,
            grid=(N_TOK // tm,),
            in_specs=[pl.BlockSpec((tm, D_MODEL), lambda i, p1, p2: (i, 0)),
                      pl.BlockSpec((None, 6, D_MODEL), lambda i, p1, p2: (_mod_row(i * tm), 0, 0)),
                      pl.BlockSpec((None, 1, D_MODEL), lambda i, p1, p2: (l, 0, 0)),
                      pl.BlockSpec(memory_space=pl.ANY)],
            out_specs=pl.BlockSpec(memory_space=pl.ANY),
            scratch_shapes=[pltpu.VMEM((tm, D_MODEL), F32), pltpu.SemaphoreType.DMA],
        ),
        input_output_aliases={5: 0},
        compiler_params=_cparams(("arbitrary",)),
        name="moe_dispatch",
    )(pos1, pos2, x_all, mod3, g_norm2.reshape(DEPTH, 1, D_MODEL), xs_init)


def _gmm_kernel(te_ref, tr_ref, xs_ref, wg_ref, wu_ref, wd_ref, y_ref, wg_scr, wu_scr, wd_scr):
    del te_ref
    g = pl.program_id(0)
    f = pl.program_id(1)
    rows = tr_ref[g]

    @pl.when(f == 0)
    def _():
        y_ref[...] = jnp.zeros_like(y_ref)

    def sub_tile(s, wg, wu, wd):
        sl = slice(s * MOE_SUB, (s + 1) * MOE_SUB)
        x = xs_ref[sl, :].astype(BF16)
        a = (_silu(jnp.dot(x, wg, preferred_element_type=F32)) * jnp.dot(x, wu, preferred_element_type=F32))
        y_ref[sl, :] += jnp.dot(a.astype(BF16), wd, preferred_element_type=F32)

    @pl.when(rows == MOE_TM)
    def _():
        x = xs_ref[...].astype(BF16)
        a = _silu(_dot(x, wg_ref[...])) * _dot(x, wu_ref[...])
        y_ref[...] += _dot(a, wd_ref[...])

    @pl.when((rows > 0) & (rows < MOE_TM))
    def _():
        wg = wg_ref[...].astype(BF16)
        wu = wu_ref[...].astype(BF16)
        wd = wd_ref[...].astype(BF16)
        wg_scr[...] = wg
        wu_scr[...] = wu
        wd_scr[...] = wd
        sub_tile(0, wg, wu, wd)

    for s in range(1, MOE_TM // MOE_SUB):
        @pl.when((rows > s * MOE_SUB) & (rows < MOE_TM))
        def _():
            sub_tile(s, wg_scr[...], wu_scr[...], wd_scr[...])


def _moe_gmm(tile_expert, tile_rows, xs, w_g, w_u, w_d, j):
    tm, tf = MOE_TM, MOE_TF
    nf = D_FF // tf

    def f_eff(g, f, tr):
        return jnp.where(tr[g] > 0, f, nf - 1)

    return pl.pallas_call(
        _gmm_kernel,
        out_shape=jax.ShapeDtypeStruct((MOE_ROWS, D_MODEL), F32),
        grid_spec=pltpu.PrefetchScalarGridSpec(
            num_scalar_prefetch=2,
            grid=(MOE_TILES, nf),
            in_specs=[pl.BlockSpec((tm, D_MODEL), lambda g, f, te, tr: (g, 0)),
                      pl.BlockSpec((None, None, D_MODEL, tf), lambda g, f, te, tr: (j, te[g], 0, f_eff(g, f, tr))),
                      pl.BlockSpec((None, None, D_MODEL, tf), lambda g, f, te, tr: (j, te[g], 0, f_eff(g, f, tr))),
                      pl.BlockSpec((None, None, tf, D_MODEL), lambda g, f, te, tr: (j, te[g], f_eff(g, f, tr), 0))],
            out_specs=pl.BlockSpec((tm, D_MODEL), lambda g, f, te, tr: (g, 0)),
            scratch_shapes=[pltpu.VMEM((D_MODEL, tf), BF16), pltpu.VMEM((D_MODEL, tf), BF16),
                            pltpu.VMEM((tf, D_MODEL), BF16)],
        ),
        compiler_params=_cparams(("arbitrary", "arbitrary")),
        name="moe_experts",
    )(tile_expert, tile_rows, xs, w_g, w_u, w_d)


def _combine_kernel(n_out, pos1_ref, pos2_ref, x_ref, mod_ref, route_ref, y_ref, *refs):
    o_refs, (g1_scr, g2_scr, sems) = refs[:n_out], refs[n_out:]
    tm = g1_scr.shape[1]
    i = pl.program_id(0)
    n = pl.num_programs(0)

    def start_gathers(tile, slot):
        def make1(r, pos):
            return pltpu.make_async_copy(y_ref.at[pl.ds(pos, 1), :], g1_scr.at[slot, pl.ds(r, 1), :], sems.at[slot])

        def make2(r, pos):
            return pltpu.make_async_copy(y_ref.at[pl.ds(pos, 1), :], g2_scr.at[slot, pl.ds(r, 1), :], sems.at[slot])

        _row_copies(tm, (pos1_ref,), tile * tm, make1)
        _row_copies(tm, (pos2_ref,), tile * tm, make2)

    @pl.when(i == 0)
    def _():
        start_gathers(0, 0)

    slot = i % 2

    @pl.when(i + 1 < n)
    def _():
        start_gathers(i + 1, 1 - slot)

    pltpu.make_async_copy(y_ref.at[pl.ds(0, tm), :], g1_scr.at[slot], sems.at[slot]).wait()
    pltpu.make_async_copy(y_ref.at[pl.ds(0, tm), :], g2_scr.at[slot], sems.at[slot]).wait()
    rec = route_ref[...]
    w1 = rec[:, R_W1:R_W1 + 1]
    w2 = rec[:, R_W2:R_W2 + 1]
    _x_store(o_refs, x_ref[...] + mod_ref[5:6, :] * (w1 * g1_scr[slot] + w2 * g2_scr[slot]))


def _moe_combine(pos1, pos2, x_all, mod3, route, y, split_out):
    tm = ROW_DMA_TM
    return pl.pallas_call(
        functools.partial(_combine_kernel, 2 if split_out else 1),
        out_shape=_x_shapes(split_out),
        grid_spec=pltpu.PrefetchScalarGridSpec(
            num_scalar_prefetch=2,
            grid=(N_TOK // tm,),
            in_specs=[pl.BlockSpec((tm, D_MODEL), lambda i, p1, p2: (i, 0)),
                      pl.BlockSpec((None, 6, D_MODEL), lambda i, p1, p2: (_mod_row(i * tm), 0, 0)),
                      pl.BlockSpec((tm, ROUTER_LANES), lambda i, p1, p2: (i, 0)),
                      pl.BlockSpec(memory_space=pl.ANY)],
            out_specs=_x_specs((None,) * (2 if split_out else 1), tm),
            scratch_shapes=[pltpu.VMEM((2, tm, D_MODEL), F32), pltpu.VMEM((2, tm, D_MODEL), F32),
                            pltpu.SemaphoreType.DMA((2,))],
        ),
        compiler_params=_cparams(("arbitrary",)),
        name="moe_combine",
    )(pos1, pos2, x_all, mod3, route, y)


def _moe(x_all, mod3, g_norm2, w_router_p, w_g, w_u, w_d, l, j, split_out):
    route, cnt = _moe_route(x_all, mod3, g_norm2, w_router_p, l, j)
    cnt = cnt[0, :N_EXPERTS].astype(jnp.int32)
    padded = (cnt + MOE_TM - 1) // MOE_TM * MOE_TM
    ends = jnp.cumsum(padded)
    offs = ends - padded
    experts = jnp.arange(N_EXPERTS, dtype=jnp.int32)

    def positions(i_lane, rank_lane):
        e = route[:, i_lane].astype(jnp.int32)
        off = jnp.sum(jnp.where(e[:, None] == experts[None, :], offs[None, :], 0), axis=1)
        return off + route[:, rank_lane].astype(jnp.int32)

    pos1 = positions(R_I1, R_RANK1)
    pos2 = positions(R_I2, R_RANK2)
    tile_start = jnp.arange(MOE_TILES, dtype=jnp.int32) * MOE_TM
    last_tile = jnp.maximum(ends[-1] - MOE_TM, 0)
    owner_start = jnp.minimum(tile_start, last_tile)
    tile_expert = jnp.minimum(jnp.sum(owner_start[:, None] >= ends[None, :], axis=1), N_EXPERTS - 1).astype(jnp.int32)
    group_end = jnp.sum(jnp.where(tile_expert[:, None] == experts[None, :], (offs + cnt)[None, :], 0), axis=1)
    tile_rows = jnp.where(tile_start < ends[-1], jnp.clip(group_end - tile_start, 0, MOE_TM), 0).astype(jnp.int32)

    xs = _moe_dispatch(pos1, pos2, x_all, mod3, g_norm2, l)
    y = _moe_gmm(tile_expert, tile_rows, xs, w_g, w_u, w_d, j)
    return tuple(_moe_combine(pos1, pos2, x_all, mod3, route, y, split_out))


def _pad_heads(w, per_head, lo):
    lead = w.shape[:-1]
    w = w.reshape(lead + (MLA_HEADS, per_head))
    w = jnp.pad(w, [(0, 0)] * len(lead) + [(0, 0), (lo, HEAD_PAD - lo - per_head)])
    return w.reshape(lead + (MLA_PAD_W,))


def _kpe_to_heads(kpe):
    z = jnp.pad(kpe, [(0, 0)] * (kpe.ndim - 1) + [(MLA_NOPE, HEAD_PAD - MLA_QK)])
    return jnp.tile(z, (1,) * (kpe.ndim - 1) + (MLA_HEADS,))


def _layer_weights(l, w_in, g_qa, w_uq, g_kva, w_ukv, g_mla_q, g_mla_k, sc_w, g_na_q, g_na_k,
                   cf_w, cf_b, cf_ln_g, cf_ln_b, w_br, w_o):
    s2 = MLA_Q_LORA + MLA_KV_LORA
    s3 = s2 + MLA_ROPE
    w_lo = w_in[l, :, :s3]
    w_head = jnp.concatenate([
        w_lo, jnp.zeros((D_MODEL, 256 - MLA_KV_LORA - MLA_ROPE), F32), _kpe_to_heads(w_lo[:, s2:s3])],
        axis=1).astype(BF16)
    ukv = w_ukv[l].reshape(MLA_KV_LORA, MLA_HEADS, MLA_NOPE + MLA_V)
    W = dict(
        w_head=w_head,
        w_tail=w_in[l, :, s3:].astype(BF16),
        g_qa=g_qa[l][None], g_kva=g_kva[l][None],
        w_uq_p=_pad_heads(w_uq[l], MLA_QK, 0).astype(BF16),
        w_ukv_k=_pad_heads(ukv[:, :, :MLA_NOPE].reshape(MLA_KV_LORA, -1), MLA_NOPE, 0).astype(BF16),
        w_ukv_v=_pad_heads(ukv[:, :, MLA_NOPE:].reshape(MLA_KV_LORA, -1), MLA_V, 0).astype(BF16),
        g_mla_q_p=jnp.pad(g_mla_q[l], (0, HEAD_PAD - MLA_QK))[None],
        g_mla_k_p=jnp.pad(g_mla_k[l], (0, HEAD_PAD - MLA_QK))[None],
        g_na_q_t=jnp.tile(g_na_q[l], NA_HEADS)[None], g_na_k_t=jnp.tile(g_na_k[l], NA_HEADS)[None],
        sc_w=sc_w[l], cf_w=cf_w[l], cf_b=cf_b[l][None], cf_ln_g=cf_ln_g[l][None], cf_ln_b=cf_ln_b[l][None],
        w_br_a=jnp.pad(w_br[l, 0].reshape(MLA_HEADS, MLA_V, D_MODEL),
                       ((0, 0), (0, HEAD_PAD - MLA_V), (0, 0))).reshape(MLA_PAD_W, D_MODEL).astype(BF16),
        w_br_bcd=w_br[l, 1:].astype(BF16),
        w_o=w_o[l].astype(BF16),
    )
    return W


def kernel(x_prompt, x_sample, cache_mla_ckv, cache_mla_kpe, cache_na_k, cache_na_v, c, c_ctx, w_ada, b_ada, g_norm1, w_in, g_qa, w_uq, g_kva, w_ukv, g_mla_q, g_mla_k, sc_w, g_na_q, g_na_k, na_rpb, cf_w, cf_b, cf_ln_g, cf_ln_b, w_br, w_o, g_norm2, w_ff_gate, w_ff_up, w_ff_down, w_router, w_e_gate, w_e_up, w_e_down):
    x = (x_prompt.reshape(N_CTX_TOK, D_MODEL), x_sample.reshape(N_LAT_TOK, D_MODEL))
    cvec = jnp.concatenate([c_ctx[None, :], c, jnp.zeros((MOD_ROWS - 1 - DEC_BATCH, D_MODEL), F32)], axis=0)
    cache_na_k2 = cache_na_k.reshape(DEC_BATCH, DEPTH, PAST_LEN, BR_W)
    cache_na_v2 = cache_na_v.reshape(DEC_BATCH, DEPTH, PAST_LEN, BR_W)
    w_router_p = jnp.pad(w_router, ((0, 0), (0, 0), (0, ROUTER_LANES - N_EXPERTS)))

    ckv_l, kpe_l, nak_l, nav_l = [], [], [], []
    for l in range(DEPTH):
        W = _layer_weights(l, w_in, g_qa, w_uq, g_kva, w_ukv, g_mla_q, g_mla_k, sc_w, g_na_q, g_na_k,
                           cf_w, cf_b, cf_ln_g, cf_ln_b, w_br, w_o)
        mod3 = _modulation(cvec, w_ada, b_ada, l).reshape(MOD_ROWS, 6, D_MODEL)
        p = _in_projection(x, mod3, g_norm1, W['w_head'], W['w_tail'], l)

        oa_c, ckv_new, kpe_new = _mla(p, l, False, None, None, W)
        (oa_l,) = _mla(p, l, True, cache_mla_ckv, _kpe_to_heads(cache_mla_kpe[:, l]), W)
        oc_c, nak_new, nav_new = _na_context(p, W)
        oc_l = _na_latent(p, l, cache_na_k2, cache_na_v2, _na_bias_table(na_rpb[l]), W)
        ob_c, od_c = _convs(p, False, W)
        ob_l, od_l = _convs(p, True, W)

        x_all = _merge(x, mod3, p, (oa_c, oa_l), (ob_c, ob_l), (oc_c, oc_l), (od_c, od_l), W)

        last = l == DEPTH - 1
        if l % 2 == 0:
            x = (_ffn(x_all, mod3, g_norm2, w_ff_gate, w_ff_up, w_ff_down, l, l // 2),)
        else:
            x = _moe(x_all, mod3, g_norm2, w_router_p, w_e_gate, w_e_up, w_e_down, l, l // 2, last)

        ckv_l.append(ckv_new)
        kpe_l.append(kpe_new)
        nak_l.append(nak_new.reshape(BATCH, SEQ, NA_HEADS, NA_HD))
        nav_l.append(nav_new.reshape(BATCH, SEQ, NA_HEADS, NA_HD))

    if len(x) == 1:
        x = (x[0][:N_CTX_TOK], x[0][N_CTX_TOK:])
    y_prompt = x[0].reshape(BATCH, SEQ, D_MODEL)
    y_sample = x[1].reshape(DEC_BATCH, DEC_SEQ, D_MODEL)
    return (y_prompt, y_sample, jnp.stack(ckv_l, axis=1), jnp.stack(kpe_l, axis=1),
            jnp.stack(nak_l, axis=1), jnp.stack(nav_l, axis=1))
```

```python
import functools

import numpy as np
import jax
import jax.numpy as jnp
from jax import lax
from jax.experimental import pallas as pl
from jax.experimental.pallas import tpu as pltpu

F32 = jnp.float32
BF16 = jnp.bfloat16

D_MODEL = 1024
BATCH = 16
SEQ = 256
DEPTH = 2
DEC_BATCH = 8
DEC_SEQ = 1024
PAST_LEN = 512
GRID_W = 64
N_BRANCH = 4
BR_W = 256
MLA_HEADS = 4
MLA_NOPE = 64
MLA_ROPE = 32
MLA_QK = 96
MLA_V = 64
MLA_Q_LORA = 256
MLA_KV_LORA = 128
SC_K = 3
NA_HEADS = 4
NA_HD = 64
NA_WIN_R = 8
NA_WIN_C = 16
CF_K = 31
D_FF = 3584
N_EXPERTS = 8
ROPE_THETA = 10000.0
EPS = 1e-6

N_CTX_TOK = BATCH * SEQ
N_LAT_TOK = DEC_BATCH * DEC_SEQ
N_TOK = N_CTX_TOK + N_LAT_TOK
MOD_ROWS = 16

HEAD_PAD = 128
MLA_PAD_W = MLA_HEADS * HEAD_PAD

P_QA = 0
P_KVA = 256
P_KPE4 = 512
P_HEAD = 1024
P_SC = 1024
P_NA = 1792
P_CF = 2560
P_GATE = 3072
P_COLS = 7168

VMEM_LIMIT = 56 * 1024 * 1024

NA_TQ = 256
NA_WIN_ROWS = 12
NA_WIN_KEYS = NA_WIN_ROWS * GRID_W


def _cparams(sem):
    return pltpu.CompilerParams(dimension_semantics=sem, vmem_limit_bytes=VMEM_LIMIT)


def _const_spec(shape):
    nd = len(shape)
    return pl.BlockSpec(shape, lambda *_: (0,) * nd)


def _mod_row(tok_start):
    return jnp.where(tok_start < N_CTX_TOK, 0, 1 + (tok_start - N_CTX_TOK) // DEC_SEQ)


def _sigmoid(x):
    return 0.5 * jnp.tanh(0.5 * x) + 0.5


def _silu(x):
    return x * _sigmoid(x)


def _ctx_or_lat(is_ctx, c_ref, l_ref):
    return jnp.where(is_ctx, c_ref[...], l_ref[...])


def _dot(a, b):
    return jnp.dot(a.astype(BF16), b.astype(BF16), preferred_element_type=F32)


def _dot_nt(a, b):
    return lax.dot_general(a.astype(BF16), b.astype(BF16), (((1,), (1,)), ((), ())),
                           preferred_element_type=F32)


def _mod_kernel(c_ref, w_ref, b_ref, o_ref):
    o_ref[...] = _dot(_silu(c_ref[...]), w_ref[...]) + b_ref[...]


def _modulation(cvec, w_ada, b_ada, l):
    tn = 1024
    return pl.pallas_call(
        _mod_kernel,
        out_shape=jax.ShapeDtypeStruct((MOD_ROWS, 6 * D_MODEL), F32),
        grid=(6 * D_MODEL // tn,),
        in_specs=[
            _const_spec((MOD_ROWS, D_MODEL)),
            pl.BlockSpec((None, D_MODEL, tn), lambda j: (l, 0, j)),
            pl.BlockSpec((None, 1, tn), lambda j: (l, 0, j)),
        ],
        out_specs=pl.BlockSpec((MOD_ROWS, tn), lambda j: (0, j)),
        compiler_params=_cparams(("arbitrary",)),
        name="modulation",
    )(cvec, w_ada, b_ada.reshape(DEPTH, 1, 6 * D_MODEL))


def _pair_specs(tm, width, buffers=None):
    nc = N_CTX_TOK // tm
    mode = {} if buffers is None else dict(pipeline_mode=pl.Buffered(buffers))
    return [pl.BlockSpec((tm, width), lambda i, *_: (jnp.minimum(i, nc - 1), 0), **mode),
            pl.BlockSpec((tm, width), lambda i, *_: (jnp.maximum(i - nc, 0), 0), **mode)]


def _pair_shapes(width, dtype):
    return [jax.ShapeDtypeStruct((N_CTX_TOK, width), dtype), jax.ShapeDtypeStruct((N_LAT_TOK, width), dtype)]


def _store_pair(is_ctx, c_ref, l_ref, val):
    @pl.when(is_ctx)
    def _():
        c_ref[...] = val

    @pl.when(jnp.logical_not(is_ctx))
    def _():
        l_ref[...] = val


def _x_specs(x, tm, buffers=None):
    if len(x) == 2:
        return _pair_specs(tm, D_MODEL, buffers)
    mode = {} if buffers is None else dict(pipeline_mode=pl.Buffered(buffers))
    return [pl.BlockSpec((tm, D_MODEL), lambda i, *_: (i, 0), **mode)]


def _x_shapes(split):
    return _pair_shapes(D_MODEL, F32) if split else [jax.ShapeDtypeStruct((N_TOK, D_MODEL), F32)]


def _x_load(x_refs):
    if len(x_refs) == 1:
        return x_refs[0][...]
    tm = x_refs[0].shape[0]
    return _ctx_or_lat(pl.program_id(0) < N_CTX_TOK // tm, *x_refs)


def _x_store(y_refs, val):
    if len(y_refs) == 1:
        y_refs[0][...] = val
    else:
        tm = y_refs[0].shape[0]
        _store_pair(pl.program_id(0) < N_CTX_TOK // tm, *y_refs, val)


def _inproj_kernel(nx, *refs):
    x_refs, (mod_ref, g_ref, wh_ref, wt_ref, o_ref, h_scr) = refs[:nx], refs[nx:]
    j = pl.program_id(1)

    @pl.when(j == 0)
    def _():
        x = _x_load(x_refs)
        y = x * lax.rsqrt(jnp.mean(x * x, axis=-1, keepdims=True) + EPS) * g_ref[...]
        h_scr[...] = (y * (1.0 + mod_ref[1:2, :]) + mod_ref[0:1, :]).astype(BF16)
        o_ref[...] = jnp.dot(h_scr[...], wh_ref[...], preferred_element_type=F32).astype(o_ref.dtype)

    @pl.when(j > 0)
    def _():
        o_ref[...] = jnp.dot(h_scr[...], wt_ref[...], preferred_element_type=F32).astype(o_ref.dtype)


def _in_projection(x, mod3, g_norm1, w_head, w_tail, l):
    tm, tn = 1024, 1024
    return pl.pallas_call(
        functools.partial(_inproj_kernel, len(x)),
        out_shape=jax.ShapeDtypeStruct((N_TOK, P_COLS), BF16),
        grid=(N_TOK // tm, P_COLS // tn),
        in_specs=_x_specs(x, tm) + [
            pl.BlockSpec((None, 6, D_MODEL), lambda i, j: (_mod_row(i * tm), 0, 0)),
            pl.BlockSpec((None, 1, D_MODEL), lambda i, j: (l, 0, 0)),
            _const_spec((D_MODEL, P_HEAD)),
            pl.BlockSpec((D_MODEL, tn), lambda i, j: (0, jnp.maximum(j - 1, 0))),
        ],
        out_specs=pl.BlockSpec((tm, tn), lambda i, j: (i, j)),
        scratch_shapes=[pltpu.VMEM((tm, D_MODEL), BF16)],
        compiler_params=_cparams(("arbitrary", "arbitrary")),
        name="in_projection",
    )(*x, mod3, g_norm1.reshape(DEPTH, 1, D_MODEL), w_head, w_tail)


def _rope_tables():
    t = np.arange(DEC_SEQ)
    nf = MLA_ROPE // 4
    inv = (np.float32(ROPE_THETA) ** (-np.arange(nf, dtype=np.float32) / np.float32(nf))).astype(np.float32)
    ang_r = (t // GRID_W).astype(np.float32)[:, None] * inv[None, :]
    ang_c = (t % GRID_W).astype(np.float32)[:, None] * inv[None, :]
    c = np.zeros((DEC_SEQ, HEAD_PAD), np.float32)
    s1 = np.zeros((DEC_SEQ, HEAD_PAD), np.float32)
    s2 = np.zeros((DEC_SEQ, HEAD_PAD), np.float32)
    c[:, :MLA_NOPE] = 1.0
    for base, ang in ((MLA_NOPE, ang_r), (MLA_NOPE + 2 * nf, ang_c)):
        c[:, base:base + nf] = np.cos(ang)
        c[:, base + nf:base + 2 * nf] = np.cos(ang)
        s1[:, base:base + nf] = -np.sin(ang)
        s2[:, base + nf:base + 2 * nf] = np.sin(ang)
    return c, s1, s2


def _rope(x, c, s1, s2):
    nf = MLA_ROPE // 4
    return x * c + pltpu.roll(x, HEAD_PAD - nf, 1) * s1 + pltpu.roll(x, nf, 1) * s2


def _head_norm(xh, g):
    ms = jnp.sum(xh * xh, axis=-1, keepdims=True) * (1.0 / MLA_QK)
    return xh * lax.rsqrt(ms + EPS) * g


def _mla_kernel(latent, tq, *refs):
    if latent:
        (pqa_ref, pkva_ref, pkpe_ref, cckv_ref, ckpe_ref, rc_ref, rs1_ref, rs2_ref,
         gqa_ref, wuq_ref, gkva_ref, wk_ref, wv_ref, gq_ref, gk_ref,
         o_ref, k_scr, v_scr) = refs
    else:
        (pqa_ref, pkva_ref, pkpe_ref,
         gqa_ref, wuq_ref, gkva_ref, wk_ref, wv_ref, gq_ref, gk_ref,
         o_ref, ckv_ref, kpe_ref, k_scr, v_scr) = refs
    qi = pl.program_id(1)
    n_past = PAST_LEN if latent else 0

    def put_kv(ckvn, kpe4, row0, rope):
        n = ckvn.shape[0]
        kk = _dot(ckvn, wk_ref[...]) + kpe4
        v_scr[row0:row0 + n, :] = _dot(ckvn, wv_ref[...]).astype(BF16)
        for h in range(MLA_HEADS):
            sl = slice(h * HEAD_PAD, (h + 1) * HEAD_PAD)
            kh = _head_norm(kk[:, sl], gk_ref[...])
            if rope:
                kh = _rope(kh, rc_ref[...], rs1_ref[...], rs2_ref[...])
            k_scr[row0:row0 + n, sl] = kh.astype(BF16)

    @pl.when(qi == 0)
    def _():
        if latent:
            put_kv(cckv_ref[...], ckpe_ref[...], 0, False)
        kva = pkva_ref[...].astype(F32)
        ckv = kva[:, :MLA_KV_LORA]
        ckvn = ckv * lax.rsqrt(jnp.mean(ckv * ckv, axis=-1, keepdims=True) + EPS) * gkva_ref[...]
        if not latent:
            ckv_ref[...] = ckvn
            kpe_ref[...] = kva[:, MLA_KV_LORA:MLA_KV_LORA + MLA_ROPE]
        put_kv(ckvn, pkpe_ref[...].astype(F32), n_past, latent)

    qa = pqa_ref[...].astype(F32)
    qan = qa * lax.rsqrt(jnp.mean(qa * qa, axis=-1, keepdims=True) + EPS) * gqa_ref[...]
    q = _dot(qan, wuq_ref[...])
    scale = MLA_QK ** -0.5
    for h in range(MLA_HEADS):
        sl = slice(h * HEAD_PAD, (h + 1) * HEAD_PAD)
        qh = _head_norm(q[:, sl], gq_ref[...])
        if latent:
            rows = pl.ds(pl.multiple_of(qi * tq, tq), tq)
            qh = _rope(qh, rc_ref[rows, :], rs1_ref[rows, :], rs2_ref[rows, :])
        s = _dot_nt(qh, k_scr[:, sl]) * scale
        m = jnp.max(s, axis=-1, keepdims=True)
        e = jnp.exp(s - m)
        den = jnp.sum(e, axis=-1, keepdims=True)
        o = _dot(e, v_scr[:, sl]) / den
        o_ref[:, sl] = o.astype(o_ref.dtype)


def _mla(p, l, latent, cache_ckv, cache_kpe4, W):
    if latent:
        nb, s, tq, row_off = DEC_BATCH, DEC_SEQ, 256, N_CTX_TOK
    else:
        nb, s, tq, row_off = BATCH, SEQ, 256, 0
    nq = s // tq
    sk = s + (PAST_LEN if latent else 0)
    in_specs = [
        pl.BlockSpec((tq, MLA_Q_LORA), lambda b, qi: (row_off // tq + b * nq + qi, P_QA // MLA_Q_LORA)),
        pl.BlockSpec((s, 256), lambda b, qi: (row_off // s + b, P_KVA // 256)),
        pl.BlockSpec((s, MLA_PAD_W), lambda b, qi: (row_off // s + b, P_KPE4 // MLA_PAD_W)),
    ]
    args = [p, p, p]
    if latent:
        in_specs += [
            pl.BlockSpec((None, None, PAST_LEN, MLA_KV_LORA), lambda b, qi: (b, l, 0, 0)),
            pl.BlockSpec((None, PAST_LEN, MLA_PAD_W), lambda b, qi: (b, 0, 0)),
            _const_spec((DEC_SEQ, HEAD_PAD)), _const_spec((DEC_SEQ, HEAD_PAD)), _const_spec((DEC_SEQ, HEAD_PAD)),
        ]
        args += [cache_ckv, cache_kpe4] + [jnp.asarray(t) for t in _rope_tables()]
    in_specs += [
        _const_spec((1, MLA_Q_LORA)), _const_spec((MLA_Q_LORA, MLA_PAD_W)), _const_spec((1, MLA_KV_LORA)),
        _const_spec((MLA_KV_LORA, MLA_PAD_W)), _const_spec((MLA_KV_LORA, MLA_PAD_W)),
        _const_spec((1, HEAD_PAD)), _const_spec((1, HEAD_PAD)),
    ]
    args += [W['g_qa'], W['w_uq_p'], W['g_kva'], W['w_ukv_k'], W['w_ukv_v'], W['g_mla_q_p'], W['g_mla_k_p']]
    out_shape = [jax.ShapeDtypeStruct((nb * s, MLA_PAD_W), BF16)]
    out_specs = [pl.BlockSpec((tq, MLA_PAD_W), lambda b, qi: (b * nq + qi, 0))]
    if not latent:
        out_shape.append(jax.ShapeDtypeStruct((nb, s, MLA_KV_LORA), F32))
        out_specs.append(pl.BlockSpec((None, s, MLA_KV_LORA), lambda b, qi: (b, 0, 0)))
        out_shape.append(jax.ShapeDtypeStruct((nb, s, MLA_ROPE), F32))
        out_specs.append(pl.BlockSpec((None, s, MLA_ROPE), lambda b, qi: (b, 0, 0)))
    return pl.pallas_call(
        functools.partial(_mla_kernel, latent, tq),
        out_shape=out_shape,
        grid=(nb, nq),
        in_specs=in_specs,
        out_specs=out_specs,
        scratch_shapes=[pltpu.VMEM((sk, MLA_PAD_W), BF16), pltpu.VMEM((sk, MLA_PAD_W), BF16)],
        compiler_params=_cparams(("arbitrary", "arbitrary")),
        name="mla_latent" if latent else "mla_context",
    )(*args)


def _head_masks(width):
    lane = lax.broadcasted_iota(jnp.int32, (1, width), 1)
    return [(lane >= h * NA_HD) & (lane < (h + 1) * NA_HD) for h in range(NA_HEADS)]


def _group_norm64(x, g, masks):
    x2 = x * x
    inv = jnp.zeros_like(x)
    for m in masks:
        ms = jnp.sum(jnp.where(m, x2, 0.0), axis=-1, keepdims=True) * (1.0 / NA_HD)
        inv = jnp.where(m, lax.rsqrt(ms + EPS), inv)
    return x * inv * g


def _p_blocks(col0, n, rows, row_fn):
    def spec(k):
        return pl.BlockSpec((rows, BR_W), lambda *g: (row_fn(*g), col0 // BR_W + k))
    return [spec(k) for k in range(n)]


def _na_ctx_kernel(pq_ref, pk_ref, pv_ref, gq_ref, gk_ref, o_ref, k_ref, v_ref):
    masks = _head_masks(BR_W)
    qn = _group_norm64(pq_ref[...].astype(F32), gq_ref[...], masks)
    kn = _group_norm64(pk_ref[...].astype(F32), gk_ref[...], masks)
    v = pv_ref[...].astype(F32)
    k_ref[...] = kn
    v_ref[...] = v
    scale = NA_HD ** -0.5
    acc = jnp.zeros((pq_ref.shape[0], BR_W), F32)
    for m in masks:
        s = _dot_nt(jnp.where(m, qn, 0.0), kn) * scale
        mx = jnp.max(s, axis=-1, keepdims=True)
        e = jnp.exp(s - mx)
        den = jnp.sum(e, axis=-1, keepdims=True)
        acc = acc + jnp.where(m, _dot(e, v) / den, 0.0)
    o_ref[...] = acc.astype(o_ref.dtype)


def _na_context(p, W):
    s = SEQ
    return pl.pallas_call(
        _na_ctx_kernel,
        out_shape=[jax.ShapeDtypeStruct((N_CTX_TOK, BR_W), BF16),
                   jax.ShapeDtypeStruct((N_CTX_TOK, BR_W), F32),
                   jax.ShapeDtypeStruct((N_CTX_TOK, BR_W), F32)],
        grid=(BATCH,),
        in_specs=_p_blocks(P_NA, 3, s, lambda b: b) + [_const_spec((1, BR_W)), _const_spec((1, BR_W))],
        out_specs=[pl.BlockSpec((s, BR_W), lambda b: (b, 0))] * 3,
        compiler_params=_cparams(("arbitrary",)),
        name="na_context",
    )(p, p, p, W['g_na_q_t'], W['g_na_k_t'])


def _na_lat_kernel(pq_ref, pk_ref, pv_ref, ck_ref, cv_ref, bias_ref, gq_ref, gk_ref, o_ref,
                   q_scr, k_scr, v_scr, kc_scr, vc_scr):
    j = pl.program_id(1)
    masks = _head_masks(BR_W)

    @pl.when(j == 0)
    def _():
        q_scr[...] = _group_norm64(pq_ref[...].astype(F32), gq_ref[...], masks).astype(BF16)
        k_scr[...] = _group_norm64(pk_ref[...].astype(F32), gk_ref[...], masks).astype(BF16)
        v_scr[...] = pv_ref[...]
        kc_scr[...] = ck_ref[...].astype(BF16)
        vc_scr[...] = cv_ref[...].astype(BF16)

    win0 = pl.multiple_of(jnp.where(j < 2, 0, DEC_SEQ - NA_WIN_KEYS), 256)
    q = q_scr[pl.ds(pl.multiple_of(j * NA_TQ, NA_TQ), NA_TQ), :]
    kw = k_scr[pl.ds(win0, NA_WIN_KEYS), :]
    vw = v_scr[pl.ds(win0, NA_WIN_KEYS), :]
    kc = kc_scr[...]
    vc = vc_scr[...]
    scale = NA_HD ** -0.5
    acc = jnp.zeros((NA_TQ, BR_W), F32)
    zero = jnp.zeros((), BF16)
    for h, m in enumerate(masks):
        qm = jnp.where(m, q, zero)
        s_loc = _dot_nt(qm, kw) * scale + bias_ref[h]
        s_ctx = _dot_nt(qm, kc) * scale
        mx = jnp.maximum(jnp.max(s_loc, axis=-1, keepdims=True), jnp.max(s_ctx, axis=-1, keepdims=True))
        e_loc = jnp.exp(s_loc - mx)
        e_ctx = jnp.exp(s_ctx - mx)
        den = jnp.sum(e_loc, axis=-1, keepdims=True) + jnp.sum(e_ctx, axis=-1, keepdims=True)
        o = (_dot(e_loc, vw) + _dot(e_ctx, vc)) / den
        acc = acc + jnp.where(m, o, 0.0)
    o_ref[...] = acc.astype(o_ref.dtype)


def _na_bias_table(rpb):
    n_dr, n_dc = 2 * NA_WIN_R - 1, 2 * NA_WIN_C - 1
    cols = np.arange(GRID_W)
    dc = np.clip(cols[None, :] - cols[:, None], -(NA_WIN_C - 1), NA_WIN_C - 1) + (NA_WIN_C - 1)
    cstart = np.clip(cols - NA_WIN_C // 2, 0, GRID_W - NA_WIN_C)
    col_ok = (cols[None, :] >= cstart[:, None]) & (cols[None, :] < cstart[:, None] + NA_WIN_C)
    place = (dc.reshape(-1)[None, :] == np.arange(n_dc)[:, None]).astype(np.float32)
    blocks = jnp.dot(rpb.reshape(NA_HEADS * n_dr, n_dc), place, precision=lax.Precision.HIGHEST)
    blocks = jnp.where(col_ok.reshape(-1)[None, :], blocks, -jnp.inf).reshape(NA_HEADS, n_dr, GRID_W, GRID_W)
    neg = jnp.full((NA_HEADS, 1, GRID_W, GRID_W), -jnp.inf, F32)
    blocks = jnp.concatenate([blocks, neg], axis=1)

    nt = DEC_SEQ // NA_TQ
    n_rows = DEC_SEQ // GRID_W
    rows_per_tile = NA_TQ // GRID_W
    sel = np.full((nt, rows_per_tile, NA_WIN_ROWS), n_dr, np.int32)
    for j in range(nt):
        win_row0 = 0 if j < nt // 2 else n_rows - NA_WIN_ROWS
        for rq in range(rows_per_tile):
            r = j * rows_per_tile + rq
            start = min(max(r - NA_WIN_R // 2, 0), n_rows - NA_WIN_R)
            for kr in range(NA_WIN_ROWS):
                if start <= win_row0 + kr < start + NA_WIN_R:
                    sel[j, rq, kr] = win_row0 + kr - r + (NA_WIN_R - 1)
    tiles = []
    for j in range(nt):
        rows = [jnp.concatenate([blocks[:, sel[j, rq, kr]] for kr in range(NA_WIN_ROWS)], axis=-1)
                for rq in range(rows_per_tile)]
        tiles.append(jnp.concatenate(rows, axis=1))
    return jnp.stack(tiles, axis=1)


def _na_latent(p, l, cache_k, cache_v, bias, W):
    s = DEC_SEQ
    nt = s // NA_TQ
    row_off = N_CTX_TOK
    return pl.pallas_call(
        _na_lat_kernel,
        out_shape=jax.ShapeDtypeStruct((N_LAT_TOK, BR_W), BF16),
        grid=(DEC_BATCH, nt),
        in_specs=_p_blocks(P_NA, 3, s, lambda b, j: row_off // s + b) + [
                  pl.BlockSpec((None, None, PAST_LEN, BR_W), lambda b, j: (b, l, 0, 0)),
                  pl.BlockSpec((None, None, PAST_LEN, BR_W), lambda b, j: (b, l, 0, 0)),
                  pl.BlockSpec((NA_HEADS, None, NA_TQ, NA_WIN_KEYS), lambda b, j: (0, j, 0, 0)),
                  _const_spec((1, BR_W)), _const_spec((1, BR_W))],
        out_specs=pl.BlockSpec((NA_TQ, BR_W), lambda b, j: (b * nt + j, 0)),
        scratch_shapes=[pltpu.VMEM((s, BR_W), BF16), pltpu.VMEM((s, BR_W), BF16), pltpu.VMEM((s, BR_W), BF16),
                        pltpu.VMEM((PAST_LEN, BR_W), BF16), pltpu.VMEM((PAST_LEN, BR_W), BF16)],
        compiler_params=_cparams(("arbitrary", "arbitrary")),
        name="na_latent",
    )(p, p, p, cache_k, cache_v, bias, W['g_na_q_t'], W['g_na_k_t'])


CONV_HALO = 16
CONV_CHUNK = 128


def _dwconv_from_pad(pad_ref, w_ref, ksize, s, emit):
    half = ksize // 2
    for c0 in range(0, s, CONV_CHUNK):
        acc = jnp.zeros((CONV_CHUNK, BR_W), F32)
        for k in range(ksize):
            r0 = CONV_HALO + c0 + k - half
            acc = acc + pad_ref[r0:r0 + CONV_CHUNK, :] * w_ref[k:k + 1, :]
        emit(c0, acc)


def _conv_kernel(s, scb_ref, scc_ref, scx_ref, cfa_ref, cfb2_ref, scw_ref, cfw_ref, cfb_ref, lng_ref, lnb_ref,
                 ob_ref, od_ref, pad_ref):
    zeros = jnp.zeros((CONV_HALO, BR_W), F32)
    pad_ref[0:CONV_HALO, :] = zeros
    pad_ref[CONV_HALO + s:2 * CONV_HALO + s, :] = zeros

    pad_ref[CONV_HALO:CONV_HALO + s, :] = scc_ref[...].astype(F32) * scx_ref[...].astype(F32)

    def emit_b(c0, acc):
        ob_ref[c0:c0 + CONV_CHUNK, :] = (scb_ref[c0:c0 + CONV_CHUNK, :].astype(F32) * acc).astype(ob_ref.dtype)

    _dwconv_from_pad(pad_ref, scw_ref, SC_K, s, emit_b)

    pad_ref[CONV_HALO:CONV_HALO + s, :] = cfa_ref[...].astype(F32) * _sigmoid(cfb2_ref[...].astype(F32))

    def emit_d(c0, acc):
        u = acc + cfb_ref[...]
        mu = jnp.mean(u, axis=-1, keepdims=True)
        d = u - mu
        var = jnp.mean(d * d, axis=-1, keepdims=True)
        y = d * lax.rsqrt(var + EPS) * lng_ref[...] + lnb_ref[...]
        od_ref[c0:c0 + CONV_CHUNK, :] = _silu(y).astype(od_ref.dtype)

    _dwconv_from_pad(pad_ref, cfw_ref, CF_K, s, emit_d)


def _convs(p, latent, W):
    if latent:
        nb, s, row_off = DEC_BATCH, DEC_SEQ, N_CTX_TOK
    else:
        nb, s, row_off = BATCH, SEQ, 0
    return pl.pallas_call(
        functools.partial(_conv_kernel, s),
        out_shape=[jax.ShapeDtypeStruct((nb * s, BR_W), BF16)] * 2,
        grid=(nb,),
        in_specs=(_p_blocks(P_SC, 3, s, lambda b: row_off // s + b)
                  + _p_blocks(P_CF, 2, s, lambda b: row_off // s + b)
                  + [_const_spec((SC_K, BR_W)), _const_spec((CF_K, BR_W)),
                     _const_spec((1, BR_W)), _const_spec((1, BR_W)), _const_spec((1, BR_W))]),
        out_specs=[pl.BlockSpec((s, BR_W), lambda b: (b, 0))] * 2,
        scratch_shapes=[pltpu.VMEM((s + 2 * CONV_HALO, BR_W), F32)],
        compiler_params=_cparams(("arbitrary",)),
        name="convs_latent" if latent else "convs_context",
    )(p, p, p, p, p, W['sc_w'], W['cf_w'], W['cf_b'], W['cf_ln_g'], W['cf_ln_b'])


def _merge_kernel(nc, nx, *refs):
    x_refs = refs[:nx]
    (mod_ref, oac_ref, oal_ref, obc_ref, obl_ref, occ_ref, ocl_ref, odc_ref, odl_ref,
     g0_ref, g1_ref, g2_ref, g3_ref, wa_ref, wb_ref, wo_ref, y_ref) = refs[nx:]
    is_ctx = pl.program_id(0) < nc
    merged = (_sigmoid(g0_ref[...]).astype(F32)
              * jnp.dot(_ctx_or_lat(is_ctx, oac_ref, oal_ref), wa_ref[...], preferred_element_type=F32))
    branches = ((obc_ref, obl_ref, g1_ref), (occ_ref, ocl_ref, g2_ref), (odc_ref, odl_ref, g3_ref))
    for n, (oc_ref, ol_ref, g_ref) in enumerate(branches):
        merged = merged + (_sigmoid(g_ref[...]).astype(F32)
                           * jnp.dot(_ctx_or_lat(is_ctx, oc_ref, ol_ref), wb_ref[n], preferred_element_type=F32))
    y_ref[...] = _x_load(x_refs) + mod_ref[2:3, :] * _dot(merged, wo_ref[...])


def _merge(x, mod3, p, o_a, o_b, o_c, o_d, W):
    tm = 512
    gate_spec = lambda n: pl.BlockSpec((tm, D_MODEL), lambda i: (i, P_GATE // D_MODEL + n))
    return pl.pallas_call(
        functools.partial(_merge_kernel, N_CTX_TOK // tm, len(x)),
        out_shape=jax.ShapeDtypeStruct((N_TOK, D_MODEL), F32),
        grid=(N_TOK // tm,),
        in_specs=(_x_specs(x, tm)
                  + [pl.BlockSpec((None, 6, D_MODEL), lambda i: (_mod_row(i * tm), 0, 0))]
                  + _pair_specs(tm, MLA_PAD_W) + _pair_specs(tm, BR_W) + _pair_specs(tm, BR_W) + _pair_specs(tm, BR_W)
                  + [gate_spec(0), gate_spec(1), gate_spec(2), gate_spec(3),
                     _const_spec((MLA_PAD_W, D_MODEL)), _const_spec((3, BR_W, D_MODEL)),
                     _const_spec((D_MODEL, D_MODEL))]),
        out_specs=pl.BlockSpec((tm, D_MODEL), lambda i: (i, 0)),
        compiler_params=_cparams(("arbitrary",)),
        name="merge",
    )(*x, mod3, *o_a, *o_b, *o_c, *o_d, p, p, p, p, W['w_br_a'], W['w_br_bcd'], W['w_o'])


def _norm2(x, g, mod_ref):
    y = x * lax.rsqrt(jnp.mean(x * x, axis=-1, keepdims=True) + EPS) * g
    return y * (1.0 + mod_ref[4:5, :]) + mod_ref[3:4, :]


def _ffn_kernel(x_ref, mod_ref, g_ref, wg_ref, wu_ref, wd_ref, y_ref, h_scr, acc_scr):
    f = pl.program_id(1)

    @pl.when(f == 0)
    def _():
        h_scr[...] = _norm2(x_ref[...], g_ref[...], mod_ref).astype(BF16)
        acc_scr[...] = jnp.zeros_like(acc_scr)

    h = h_scr[...]
    a = _silu(_dot(h, wg_ref[...])) * _dot(h, wu_ref[...])
    acc_scr[...] += _dot(a, wd_ref[...])

    @pl.when(f == pl.num_programs(1) - 1)
    def _():
        y_ref[...] = x_ref[...] + mod_ref[5:6, :] * acc_scr[...]


def _ffn(x_all, mod3, g_norm2, w_g, w_u, w_d, l, j):
    tm, tf = 1024, 512
    return pl.pallas_call(
        _ffn_kernel,
        out_shape=jax.ShapeDtypeStruct((N_TOK, D_MODEL), F32),
        grid=(N_TOK // tm, D_FF // tf),
        in_specs=[pl.BlockSpec((tm, D_MODEL), lambda i, f: (i, 0)),
                  pl.BlockSpec((None, 6, D_MODEL), lambda i, f: (_mod_row(i * tm), 0, 0)),
                  pl.BlockSpec((None, 1, D_MODEL), lambda i, f: (l, 0, 0)),
                  pl.BlockSpec((None, D_MODEL, tf), lambda i, f: (j, 0, f)),
                  pl.BlockSpec((None, D_MODEL, tf), lambda i, f: (j, 0, f)),
                  pl.BlockSpec((None, tf, D_MODEL), lambda i, f: (j, f, 0))],
        out_specs=pl.BlockSpec((tm, D_MODEL), lambda i, f: (i, 0)),
        scratch_shapes=[pltpu.VMEM((tm, D_MODEL), BF16), pltpu.VMEM((tm, D_MODEL), F32)],
        compiler_params=_cparams(("arbitrary", "arbitrary")),
        name="ffn_dense",
    )(x_all, mod3, g_norm2.reshape(DEPTH, 1, D_MODEL), w_g, w_u, w_d)


ROUTER_LANES = 128
ROUTE_TM = 512
MOE_TM = 1024
MOE_SUB = 256
MOE_TF = 512
RUN_ALIGN = 8
MOE_TILES = -(-(2 * N_TOK + (N_TOK // ROUTE_TM) * N_EXPERTS * RUN_ALIGN + N_EXPERTS * MOE_TM) // MOE_TM)
MOE_ROWS = MOE_TILES * MOE_TM
ROW_DMA_TM = 256
R_I1, R_I2, R_W1, R_W2, R_RANK1, R_RANK2 = range(6)


def _route_top2(h, wr):
    logits = jnp.dot(h, wr, preferred_element_type=F32, precision=lax.Precision.HIGHEST)
    lane = lax.broadcasted_iota(jnp.int32, logits.shape, 1).astype(F32)
    neg = jnp.float32(-jnp.inf)
    logits = jnp.where(lane < N_EXPERTS, logits, neg)
    m1 = jnp.max(logits, axis=-1, keepdims=True)
    i1 = jnp.min(jnp.where(logits == m1, lane, float(ROUTER_LANES)), axis=-1, keepdims=True)
    rest = jnp.where(lane == i1, neg, logits)
    m2 = jnp.max(rest, axis=-1, keepdims=True)
    i2 = jnp.min(jnp.where(rest == m2, lane, float(ROUTER_LANES)), axis=-1, keepdims=True)
    e2 = jnp.exp(m2 - m1)
    return lane, i1, i2, 1.0 / (1.0 + e2), e2 / (1.0 + e2)


def _route_kernel(x_ref, mod_ref, g_ref, wr_ref, route_ref, cnt_ref, tcarry_ref, carry_scr):
    @pl.when(pl.program_id(0) == 0)
    def _():
        carry_scr[...] = jnp.zeros_like(carry_scr)

    tcarry_ref[...] = carry_scr[...]

    h = _norm2(x_ref[...], g_ref[...], mod_ref)
    lane, i1, i2, w1, w2 = _route_top2(h, wr_ref[...])
    tm = h.shape[0]
    oh1 = lane == i1
    oh2 = lane == i2
    oh = jnp.where(oh1, 1.0, 0.0) + jnp.where(oh2, 1.0, 0.0)
    r = lax.broadcasted_iota(jnp.int32, (tm, tm), 0)
    c = lax.broadcasted_iota(jnp.int32, (tm, tm), 1)
    lower = jnp.where(r > c, 1.0, 0.0).astype(BF16)
    before = jnp.dot(lower, oh.astype(BF16), preferred_element_type=F32)
    rank1 = jnp.sum(jnp.where(oh1, before, 0.0), axis=-1, keepdims=True)
    rank2 = jnp.sum(jnp.where(oh2, before, 0.0), axis=-1, keepdims=True)
    carry_scr[...] += jnp.sum(oh, axis=0, keepdims=True)
    rec = jnp.zeros_like(lane)
    for k, v in ((R_I1, i1), (R_I2, i2), (R_W1, w1), (R_W2, w2), (R_RANK1, rank1), (R_RANK2, rank2)):
        rec = jnp.where(lane == float(k), v, rec)
    route_ref[...] = rec
    cnt_ref[...] = carry_scr[...]


def _moe_route(x_all, mod3, g_norm2, w_router_p, l, j):
    tm = ROUTE_TM
    return pl.pallas_call(
        _route_kernel,
        out_shape=[jax.ShapeDtypeStruct((N_TOK, ROUTER_LANES), F32),
                   jax.ShapeDtypeStruct((1, ROUTER_LANES), F32),
                   jax.ShapeDtypeStruct((N_TOK // tm, 1, ROUTER_LANES), F32)],
        grid=(N_TOK // tm,),
        in_specs=[pl.BlockSpec((tm, D_MODEL), lambda i: (i, 0)),
                  pl.BlockSpec((None, 6, D_MODEL), lambda i: (_mod_row(i * tm), 0, 0)),
                  pl.BlockSpec((None, 1, D_MODEL), lambda i: (l, 0, 0)),
                  pl.BlockSpec((None, D_MODEL, ROUTER_LANES), lambda i: (j, 0, 0))],
        out_specs=[pl.BlockSpec((tm, ROUTER_LANES), lambda i: (i, 0)),
                   pl.BlockSpec((1, ROUTER_LANES), lambda i: (0, 0)),
                   pl.BlockSpec((None, 1, ROUTER_LANES), lambda i: (i, 0, 0))],
        scratch_shapes=[pltpu.VMEM((1, ROUTER_LANES), F32)],
        compiler_params=_cparams(("arbitrary",)),
        name="moe_route",
    )(x_all, mod3, g_norm2.reshape(DEPTH, 1, D_MODEL), w_router_p)


def _row_copies(n, pos_refs, base, make):
    def body(r, carry):
        for pos_ref in pos_refs:
            make(r, pos_ref[base + r]).start()
        return carry
    lax.fori_loop(0, n, body, 0, unroll=8)


ZERO_ROWS = 512
N_GAPS = N_EXPERTS + 1


def _bit_chunks(n_units, max_units, make, wait=False):
    for b in range(max_units.bit_length()):
        units = 1 << b

        @pl.when((n_units & units) != 0)
        def _():
            cp = make(pl.multiple_of((n_units & (units - 1)) * RUN_ALIGN, RUN_ALIGN), units * RUN_ALIGN)
            if wait:
                cp.wait()
            else:
                cp.start()


def _dispatch_kernel(n_ref, lo_ref, dst_ref, gap0_ref, gapn_ref, x_ref, mod_ref, g_ref, route_ref, lovec_ref,
                     xs_ref, z_scr, zero_scr, sem, zsem):
    i = pl.program_id(0)
    tm = x_ref.shape[0]
    h = _norm2(x_ref[...], g_ref[...], mod_ref).astype(BF16)
    rec = route_ref[...]
    lane = lax.broadcasted_iota(jnp.int32, rec.shape, 1).astype(F32)
    lo_row = lovec_ref[...]

    def local_pos(i_lane, r_lane):
        lp = (jnp.sum(jnp.where(lane == rec[:, i_lane:i_lane + 1], lo_row, 0.0), axis=-1, keepdims=True)
              + rec[:, r_lane:r_lane + 1])
        return jnp.transpose(jnp.broadcast_to(lp, (tm, ROUTER_LANES)))[0:1, :]

    slot = lax.broadcasted_iota(jnp.int32, (z_scr.shape[0], tm), 0).astype(F32)
    perm = jnp.where((slot == local_pos(R_I1, R_RANK1)) | (slot == local_pos(R_I2, R_RANK2)), 1.0, 0.0)
    z_scr[...] = jnp.dot(perm.astype(BF16), h, preferred_element_type=F32)

    for wait in (False, True):
        for e in range(N_EXPERTS):
            k = i * N_EXPERTS + e
            src0 = lo_ref[k]
            dst0 = dst_ref[k]

            def make(off, rows, src0=src0, dst0=dst0):
                return pltpu.make_async_copy(z_scr.at[pl.ds(pl.multiple_of(src0 + off, RUN_ALIGN), rows), :],
                                             xs_ref.at[pl.ds(pl.multiple_of(dst0 + off, RUN_ALIGN), rows), :], sem)

            _bit_chunks(n_ref[k], tm // RUN_ALIGN, make, wait)

    @pl.when(i == pl.num_programs(0) - 1)
    def _():
        zero_scr[...] = jnp.zeros_like(zero_scr)
        for wait in (False, True):
            for g in range(N_GAPS):
                start = gap0_ref[g]
                n = gapn_ref[g]
                max_rows = MOE_TM if g < N_EXPERTS else MOE_ROWS - 2 * N_TOK
                for c in range(max_rows // ZERO_ROWS):
                    @pl.when(n >= (c + 1) * ZERO_ROWS)
                    def _():
                        cp = pltpu.make_async_copy(
                            zero_scr,
                            xs_ref.at[pl.ds(pl.multiple_of(start + c * ZERO_ROWS, RUN_ALIGN), ZERO_ROWS), :], zsem)
                        if wait:
                            cp.wait()
                        else:
                            cp.start()

                def make(off, rows, start=start, n=n):
                    tail0 = start + (n // ZERO_ROWS) * ZERO_ROWS
                    return pltpu.make_async_copy(
                        zero_scr.at[pl.ds(0, rows), :],
                        xs_ref.at[pl.ds(pl.multiple_of(tail0 + off, RUN_ALIGN), rows), :], zsem)

                _bit_chunks((n % ZERO_ROWS) // RUN_ALIGN, ZERO_ROWS // RUN_ALIGN - 1, make, wait)


def _moe_dispatch(n_tile, lo_tile, dst_tile, gap0, gapn, x_all, mod3, g_norm2, route, lo_vec, l):
    tm = ROUTE_TM
    spec = lambda shape, fn: pl.BlockSpec(shape, lambda i, *_: fn(i))
    return pl.pallas_call(
        _dispatch_kernel,
        out_shape=jax.ShapeDtypeStruct((MOE_ROWS, D_MODEL), F32),
        grid_spec=pltpu.PrefetchScalarGridSpec(
            num_scalar_prefetch=5,
            grid=(N_TOK // tm,),
            in_specs=[spec((tm, D_MODEL), lambda i: (i, 0)),
                      spec((None, 6, D_MODEL), lambda i: (_mod_row(i * tm), 0, 0)),
                      spec((None, 1, D_MODEL), lambda i: (l, 0, 0)),
                      spec((tm, ROUTER_LANES), lambda i: (i, 0)),
                      spec((None, 1, ROUTER_LANES), lambda i: (i, 0, 0))],
            out_specs=pl.BlockSpec(memory_space=pl.ANY),
            scratch_shapes=[pltpu.VMEM((2 * tm + N_EXPERTS * RUN_ALIGN, D_MODEL), F32),
                            pltpu.VMEM((ZERO_ROWS, D_MODEL), F32),
                            pltpu.SemaphoreType.DMA, pltpu.SemaphoreType.DMA],
        ),
        compiler_params=_cparams(("arbitrary",)),
        name="moe_dispatch",
    )(n_tile, lo_tile, dst_tile, gap0, gapn, x_all, mod3, g_norm2.reshape(DEPTH, 1, D_MODEL), route, lo_vec)


def _gmm_kernel(te_ref, tr_ref, xs_ref, wg_ref, wu_ref, wd_ref, y_ref, wg_scr, wu_scr, wd_scr):
    del te_ref
    g = pl.program_id(0)
    f = pl.program_id(1)
    rows = tr_ref[g]

    @pl.when(f == 0)
    def _():
        y_ref[...] = jnp.zeros_like(y_ref)

    def sub_tile(s, wg, wu, wd):
        sl = slice(s * MOE_SUB, (s + 1) * MOE_SUB)
        x = xs_ref[sl, :].astype(BF16)
        a = (_silu(jnp.dot(x, wg, preferred_element_type=F32)) * jnp.dot(x, wu, preferred_element_type=F32))
        y_ref[sl, :] += jnp.dot(a.astype(BF16), wd, preferred_element_type=F32)

    @pl.when(rows == MOE_TM)
    def _():
        x = xs_ref[...].astype(BF16)
        a = _silu(_dot(x, wg_ref[...])) * _dot(x, wu_ref[...])
        y_ref[...] += _dot(a, wd_ref[...])

    @pl.when((rows > 0) & (rows < MOE_TM))
    def _():
        wg = wg_ref[...].astype(BF16)
        wu = wu_ref[...].astype(BF16)
        wd = wd_ref[...].astype(BF16)
        wg_scr[...] = wg
        wu_scr[...] = wu
        wd_scr[...] = wd
        sub_tile(0, wg, wu, wd)

    for s in range(1, MOE_TM // MOE_SUB):
        @pl.when((rows > s * MOE_SUB) & (rows < MOE_TM))
        def _():
            sub_tile(s, wg_scr[...], wu_scr[...], wd_scr[...])


def _moe_gmm(tile_expert, tile_rows, xs, w_g, w_u, w_d, j):
    tm, tf = MOE_TM, MOE_TF
    nf = D_FF // tf

    def f_eff(g, f, tr):
        return jnp.where(tr[g] > 0, f, nf - 1)

    return pl.pallas_call(
        _gmm_kernel,
        out_shape=jax.ShapeDtypeStruct((MOE_ROWS, D_MODEL), F32),
        grid_spec=pltpu.PrefetchScalarGridSpec(
            num_scalar_prefetch=2,
            grid=(MOE_TILES, nf),
            in_specs=[pl.BlockSpec((tm, D_MODEL), lambda g, f, te, tr: (g, 0)),
                      pl.BlockSpec((None, None, D_MODEL, tf), lambda g, f, te, tr: (j, te[g], 0, f_eff(g, f, tr))),
                      pl.BlockSpec((None, None, D_MODEL, tf), lambda g, f, te, tr: (j, te[g], 0, f_eff(g, f, tr))),
                      pl.BlockSpec((None, None, tf, D_MODEL), lambda g, f, te, tr: (j, te[g], f_eff(g, f, tr), 0))],
            out_specs=pl.BlockSpec((tm, D_MODEL), lambda g, f, te, tr: (g, 0)),
            scratch_shapes=[pltpu.VMEM((D_MODEL, tf), BF16), pltpu.VMEM((D_MODEL, tf), BF16),
                            pltpu.VMEM((tf, D_MODEL), BF16)],
        ),
        compiler_params=_cparams(("arbitrary", "arbitrary")),
        name="moe_experts",
    )(tile_expert, tile_rows, xs, w_g, w_u, w_d)


def _combine_kernel(n_out, pos1_ref, pos2_ref, x_ref, mod_ref, route_ref, y_ref, *refs):
    o_refs, (g1_scr, g2_scr, sems) = refs[:n_out], refs[n_out:]
    tm = g1_scr.shape[1]
    i = pl.program_id(0)
    n = pl.num_programs(0)

    def start_gathers(tile, slot):
        def make1(r, pos):
            return pltpu.make_async_copy(y_ref.at[pl.ds(pos, 1), :], g1_scr.at[slot, pl.ds(r, 1), :], sems.at[slot])

        def make2(r, pos):
            return pltpu.make_async_copy(y_ref.at[pl.ds(pos, 1), :], g2_scr.at[slot, pl.ds(r, 1), :], sems.at[slot])

        _row_copies(tm, (pos1_ref,), tile * tm, make1)
        _row_copies(tm, (pos2_ref,), tile * tm, make2)

    @pl.when(i == 0)
    def _():
        start_gathers(0, 0)

    slot = i % 2

    @pl.when(i + 1 < n)
    def _():
        start_gathers(i + 1, 1 - slot)

    pltpu.make_async_copy(y_ref.at[pl.ds(0, tm), :], g1_scr.at[slot], sems.at[slot]).wait()
    pltpu.make_async_copy(y_ref.at[pl.ds(0, tm), :], g2_scr.at[slot], sems.at[slot]).wait()
    rec = route_ref[...]
    w1 = rec[:, R_W1:R_W1 + 1]
    w2 = rec[:, R_W2:R_W2 + 1]
    _x_store(o_refs, x_ref[...] + mod_ref[5:6, :] * (w1 * g1_scr[slot] + w2 * g2_scr[slot]))


def _moe_combine(pos1, pos2, x_all, mod3, route, y, split_out):
    tm = ROW_DMA_TM
    return pl.pallas_call(
        functools.partial(_combine_kernel, 2 if split_out else 1),
        out_shape=_x_shapes(split_out),
        grid_spec=pltpu.PrefetchScalarGridSpec(
            num_scalar_prefetch=2,
            grid=(N_TOK // tm,),
            in_specs=[pl.BlockSpec((tm, D_MODEL), lambda i, p1, p2: (i, 0)),
                      pl.BlockSpec((None, 6, D_MODEL), lambda i, p1, p2: (_mod_row(i * tm), 0, 0)),
                      pl.BlockSpec((tm, ROUTER_LANES), lambda i, p1, p2: (i, 0)),
                      pl.BlockSpec(memory_space=pl.ANY)],
            out_specs=_x_specs((None,) * (2 if split_out else 1), tm),
            scratch_shapes=[pltpu.VMEM((2, tm, D_MODEL), F32), pltpu.VMEM((2, tm, D_MODEL), F32),
                            pltpu.SemaphoreType.DMA((2,))],
        ),
        compiler_params=_cparams(("arbitrary",)),
        name="moe_combine",
    )(pos1, pos2, x_all, mod3, route, y)


def _moe(x_all, mod3, g_norm2, w_router_p, w_g, w_u, w_d, l, j, split_out):
    route, cnt, tcarry = _moe_route(x_all, mod3, g_norm2, w_router_p, l, j)
    cnt = cnt[0, :N_EXPERTS].astype(jnp.int32)
    carry = tcarry[:, 0, :N_EXPERTS].astype(jnp.int32)
    n_tile = jnp.concatenate([carry[1:], cnt[None, :]], axis=0) - carry
    n_tile = (n_tile + RUN_ALIGN - 1) // RUN_ALIGN * RUN_ALIGN
    lo_tile = jnp.cumsum(n_tile, axis=1) - n_tile
    carry = jnp.cumsum(n_tile, axis=0) - n_tile
    cnt = jnp.sum(n_tile, axis=0)
    padded = (cnt + MOE_TM - 1) // MOE_TM * MOE_TM
    ends = jnp.cumsum(padded)
    offs = ends - padded
    dst_tile = offs[None, :] + carry
    experts = jnp.arange(N_EXPERTS, dtype=jnp.int32)
    gap0 = jnp.concatenate([offs + cnt, ends[-1:]])
    gapn = jnp.concatenate([padded - cnt, MOE_ROWS - ends[-1:]])
    lo_vec = jnp.pad(lo_tile.astype(F32), ((0, 0), (0, ROUTER_LANES - N_EXPERTS)))[:, None, :]

    def positions(i_lane, rank_lane):
        e = route[:, i_lane].astype(jnp.int32)
        base = jnp.repeat(dst_tile, ROUTE_TM, axis=0)
        off = jnp.sum(jnp.where(e[:, None] == experts[None, :], base, 0), axis=1)
        return off + route[:, rank_lane].astype(jnp.int32)

    pos1 = positions(R_I1, R_RANK1)
    pos2 = positions(R_I2, R_RANK2)
    tile_start = jnp.arange(MOE_TILES, dtype=jnp.int32) * MOE_TM
    last_tile = jnp.maximum(ends[-1] - MOE_TM, 0)
    owner_start = jnp.minimum(tile_start, last_tile)
    tile_expert = jnp.minimum(jnp.sum(owner_start[:, None] >= ends[None, :], axis=1), N_EXPERTS - 1).astype(jnp.int32)
    group_end = jnp.sum(jnp.where(tile_expert[:, None] == experts[None, :], (offs + cnt)[None, :], 0), axis=1)
    tile_rows = jnp.where(tile_start < ends[-1], jnp.clip(group_end - tile_start, 0, MOE_TM), 0).astype(jnp.int32)

    i32 = lambda a: a.reshape(-1).astype(jnp.int32)
    xs = _moe_dispatch(i32(n_tile // RUN_ALIGN), i32(lo_tile), i32(dst_tile), i32(gap0), i32(gapn),
                       x_all, mod3, g_norm2, route, lo_vec, l)
    y = _moe_gmm(tile_expert, tile_rows, xs, w_g, w_u, w_d, j)
    return tuple(_moe_combine(pos1, pos2, x_all, mod3, route, y, split_out))


def _pad_heads(w, per_head, lo):
    lead = w.shape[:-1]
    w = w.reshape(lead + (MLA_HEADS, per_head))
    w = jnp.pad(w, [(0, 0)] * len(lead) + [(0, 0), (lo, HEAD_PAD - lo - per_head)])
    return w.reshape(lead + (MLA_PAD_W,))


def _kpe_to_heads(kpe):
    z = jnp.pad(kpe, [(0, 0)] * (kpe.ndim - 1) + [(MLA_NOPE, HEAD_PAD - MLA_QK)])
    return jnp.tile(z, (1,) * (kpe.ndim - 1) + (MLA_HEADS,))


def _layer_weights(l, w_in, g_qa, w_uq, g_kva, w_ukv, g_mla_q, g_mla_k, sc_w, g_na_q, g_na_k,
                   cf_w, cf_b, cf_ln_g, cf_ln_b, w_br, w_o):
    s2 = MLA_Q_LORA + MLA_KV_LORA
    s3 = s2 + MLA_ROPE
    w_lo = w_in[l, :, :s3]
    w_head = jnp.concatenate([
        w_lo, jnp.zeros((D_MODEL, 256 - MLA_KV_LORA - MLA_ROPE), F32), _kpe_to_heads(w_lo[:, s2:s3])],
        axis=1).astype(BF16)
    ukv = w_ukv[l].reshape(MLA_KV_LORA, MLA_HEADS, MLA_NOPE + MLA_V)
    W = dict(
        w_head=w_head,
        w_tail=w_in[l, :, s3:].astype(BF16),
        g_qa=g_qa[l][None], g_kva=g_kva[l][None],
        w_uq_p=_pad_heads(w_uq[l], MLA_QK, 0).astype(BF16),
        w_ukv_k=_pad_heads(ukv[:, :, :MLA_NOPE].reshape(MLA_KV_LORA, -1), MLA_NOPE, 0).astype(BF16),
        w_ukv_v=_pad_heads(ukv[:, :, MLA_NOPE:].reshape(MLA_KV_LORA, -1), MLA_V, 0).astype(BF16),
        g_mla_q_p=jnp.pad(g_mla_q[l], (0, HEAD_PAD - MLA_QK))[None],
        g_mla_k_p=jnp.pad(g_mla_k[l], (0, HEAD_PAD - MLA_QK))[None],
        g_na_q_t=jnp.tile(g_na_q[l], NA_HEADS)[None], g_na_k_t=jnp.tile(g_na_k[l], NA_HEADS)[None],
        sc_w=sc_w[l], cf_w=cf_w[l], cf_b=cf_b[l][None], cf_ln_g=cf_ln_g[l][None], cf_ln_b=cf_ln_b[l][None],
        w_br_a=jnp.pad(w_br[l, 0].reshape(MLA_HEADS, MLA_V, D_MODEL),
                       ((0, 0), (0, HEAD_PAD - MLA_V), (0, 0))).reshape(MLA_PAD_W, D_MODEL).astype(BF16),
        w_br_bcd=w_br[l, 1:].astype(BF16),
        w_o=w_o[l].astype(BF16),
    )
    return W


def kernel(x_prompt, x_sample, cache_mla_ckv, cache_mla_kpe, cache_na_k, cache_na_v, c, c_ctx, w_ada, b_ada, g_norm1, w_in, g_qa, w_uq, g_kva, w_ukv, g_mla_q, g_mla_k, sc_w, g_na_q, g_na_k, na_rpb, cf_w, cf_b, cf_ln_g, cf_ln_b, w_br, w_o, g_norm2, w_ff_gate, w_ff_up, w_ff_down, w_router, w_e_gate, w_e_up, w_e_down):
    x = (x_prompt.reshape(N_CTX_TOK, D_MODEL), x_sample.reshape(N_LAT_TOK, D_MODEL))
    cvec = jnp.concatenate([c_ctx[None, :], c, jnp.zeros((MOD_ROWS - 1 - DEC_BATCH, D_MODEL), F32)], axis=0)
    cache_na_k2 = cache_na_k.reshape(DEC_BATCH, DEPTH, PAST_LEN, BR_W)
    cache_na_v2 = cache_na_v.reshape(DEC_BATCH, DEPTH, PAST_LEN, BR_W)
    w_router_p = jnp.pad(w_router, ((0, 0), (0, 0), (0, ROUTER_LANES - N_EXPERTS)))

    ckv_l, kpe_l, nak_l, nav_l = [], [], [], []
    for l in range(DEPTH):
        W = _layer_weights(l, w_in, g_qa, w_uq, g_kva, w_ukv, g_mla_q, g_mla_k, sc_w, g_na_q, g_na_k,
                           cf_w, cf_b, cf_ln_g, cf_ln_b, w_br, w_o)
        mod3 = _modulation(cvec, w_ada, b_ada, l).reshape(MOD_ROWS, 6, D_MODEL)
        p = _in_projection(x, mod3, g_norm1, W['w_head'], W['w_tail'], l)

        oa_c, ckv_new, kpe_new = _mla(p, l, False, None, None, W)
        (oa_l,) = _mla(p, l, True, cache_mla_ckv, _kpe_to_heads(cache_mla_kpe[:, l]), W)
        oc_c, nak_new, nav_new = _na_context(p, W)
        oc_l = _na_latent(p, l, cache_na_k2, cache_na_v2, _na_bias_table(na_rpb[l]), W)
        ob_c, od_c = _convs(p, False, W)
        ob_l, od_l = _convs(p, True, W)

        x_all = _merge(x, mod3, p, (oa_c, oa_l), (ob_c, ob_l), (oc_c, oc_l), (od_c, od_l), W)

        last = l == DEPTH - 1
        if l % 2 == 0:
            x = (_ffn(x_all, mod3, g_norm2, w_ff_gate, w_ff_up, w_ff_down, l, l // 2),)
        else:
            x = _moe(x_all, mod3, g_norm2, w_router_p, w_e_gate, w_e_up, w_e_down, l, l // 2, last)

        ckv_l.append(ckv_new)
        kpe_l.append(kpe_new)
        nak_l.append(nak_new.reshape(BATCH, SEQ, NA_HEADS, NA_HD))
        nav_l.append(nav_new.reshape(BATCH, SEQ, NA_HEADS, NA_HD))

    if len(x) == 1:
        x = (x[0][:N_CTX_TOK], x[0][N_CTX_TOK:])
    y_prompt = x[0].reshape(BATCH, SEQ, D_MODEL)
    y_sample = x[1].reshape(DEC_BATCH, DEC_SEQ, D_MODEL)
    return (y_prompt, y_sample, jnp.stack(ckv_l, axis=1), jnp.stack(kpe_l, axis=1),
            jnp.stack(nak_l, axis=1), jnp.stack(nav_l, axis=1))
```

```python
import functools

import numpy as np
import jax
import jax.numpy as jnp
from jax import lax
from jax.experimental import pallas as pl
from jax.experimental.pallas import tpu as pltpu

F32 = jnp.float32
BF16 = jnp.bfloat16

D_MODEL = 1024
BATCH = 16
SEQ = 256
DEPTH = 2
DEC_BATCH = 8
DEC_SEQ = 1024
PAST_LEN = 512
GRID_W = 64
N_BRANCH = 4
BR_W = 256
MLA_HEADS = 4
MLA_NOPE = 64
MLA_ROPE = 32
MLA_QK = 96
MLA_V = 64
MLA_Q_LORA = 256
MLA_KV_LORA = 128
SC_K = 3
NA_HEADS = 4
NA_HD = 64
NA_WIN_R = 8
NA_WIN_C = 16
CF_K = 31
D_FF = 3584
N_EXPERTS = 8
ROPE_THETA = 10000.0
EPS = 1e-6

N_CTX_TOK = BATCH * SEQ
N_LAT_TOK = DEC_BATCH * DEC_SEQ
N_TOK = N_CTX_TOK + N_LAT_TOK
MOD_ROWS = 16

HEAD_PAD = 128
MLA_PAD_W = MLA_HEADS * HEAD_PAD

P_QA = 0
P_KVA = 256
P_KPE4 = 512
P_HEAD = 1024
P_SC = 1024
P_NA = 1792
P_CF = 2560
P_GATE = 3072
P_COLS = 7168

VMEM_LIMIT = 56 * 1024 * 1024

NA_TQ = 256
NA_WIN_ROWS = 12
NA_WIN_KEYS = NA_WIN_ROWS * GRID_W


def _cparams(sem):
    return pltpu.CompilerParams(dimension_semantics=sem, vmem_limit_bytes=VMEM_LIMIT)


def _const_spec(shape):
    nd = len(shape)
    return pl.BlockSpec(shape, lambda *_: (0,) * nd)


def _mod_row(tok_start):
    return jnp.where(tok_start < N_CTX_TOK, 0, 1 + (tok_start - N_CTX_TOK) // DEC_SEQ)


def _sigmoid(x):
    return 0.5 * jnp.tanh(0.5 * x) + 0.5


def _silu(x):
    return x * _sigmoid(x)


def _ctx_or_lat(is_ctx, c_ref, l_ref):
    return jnp.where(is_ctx, c_ref[...], l_ref[...])


def _dot(a, b):
    return jnp.dot(a.astype(BF16), b.astype(BF16), preferred_element_type=F32)


def _dot_nt(a, b):
    return lax.dot_general(a.astype(BF16), b.astype(BF16), (((1,), (1,)), ((), ())),
                           preferred_element_type=F32)


def _mod_kernel(c_ref, w_ref, b_ref, o_ref):
    o_ref[...] = _dot(_silu(c_ref[...]), w_ref[...]) + b_ref[...]


def _modulation(cvec, w_ada, b_ada, l):
    tn = 1024
    return pl.pallas_call(
        _mod_kernel,
        out_shape=jax.ShapeDtypeStruct((MOD_ROWS, 6 * D_MODEL), F32),
        grid=(6 * D_MODEL // tn,),
        in_specs=[
            _const_spec((MOD_ROWS, D_MODEL)),
            pl.BlockSpec((None, D_MODEL, tn), lambda j: (l, 0, j)),
            pl.BlockSpec((None, 1, tn), lambda j: (l, 0, j)),
        ],
        out_specs=pl.BlockSpec((MOD_ROWS, tn), lambda j: (0, j)),
        compiler_params=_cparams(("arbitrary",)),
        name="modulation",
    )(cvec, w_ada, b_ada.reshape(DEPTH, 1, 6 * D_MODEL))


def _pair_specs(tm, width, buffers=None):
    nc = N_CTX_TOK // tm
    mode = {} if buffers is None else dict(pipeline_mode=pl.Buffered(buffers))
    return [pl.BlockSpec((tm, width), lambda i, *_: (jnp.minimum(i, nc - 1), 0), **mode),
            pl.BlockSpec((tm, width), lambda i, *_: (jnp.maximum(i - nc, 0), 0), **mode)]


def _pair_shapes(width, dtype):
    return [jax.ShapeDtypeStruct((N_CTX_TOK, width), dtype), jax.ShapeDtypeStruct((N_LAT_TOK, width), dtype)]


def _store_pair(is_ctx, c_ref, l_ref, val):
    @pl.when(is_ctx)
    def _():
        c_ref[...] = val

    @pl.when(jnp.logical_not(is_ctx))
    def _():
        l_ref[...] = val


def _x_specs(x, tm, buffers=None):
    if len(x) == 2:
        return _pair_specs(tm, D_MODEL, buffers)
    mode = {} if buffers is None else dict(pipeline_mode=pl.Buffered(buffers))
    return [pl.BlockSpec((tm, D_MODEL), lambda i, *_: (i, 0), **mode)]


def _x_shapes(split):
    return _pair_shapes(D_MODEL, F32) if split else [jax.ShapeDtypeStruct((N_TOK, D_MODEL), F32)]


def _x_load(x_refs):
    if len(x_refs) == 1:
        return x_refs[0][...]
    tm = x_refs[0].shape[0]
    return _ctx_or_lat(pl.program_id(0) < N_CTX_TOK // tm, *x_refs)


def _x_store(y_refs, val):
    if len(y_refs) == 1:
        y_refs[0][...] = val
    else:
        tm = y_refs[0].shape[0]
        _store_pair(pl.program_id(0) < N_CTX_TOK // tm, *y_refs, val)


def _inproj_kernel(nx, *refs):
    x_refs, (mod_ref, g_ref, wh_ref, wt_ref, o_ref, h_scr) = refs[:nx], refs[nx:]
    j = pl.program_id(1)

    @pl.when(j == 0)
    def _():
        x = _x_load(x_refs)
        y = x * lax.rsqrt(jnp.mean(x * x, axis=-1, keepdims=True) + EPS) * g_ref[...]
        h_scr[...] = (y * (1.0 + mod_ref[1:2, :]) + mod_ref[0:1, :]).astype(BF16)
        o_ref[...] = jnp.dot(h_scr[...], wh_ref[...], preferred_element_type=F32).astype(o_ref.dtype)

    @pl.when(j > 0)
    def _():
        o_ref[...] = jnp.dot(h_scr[...], wt_ref[...], preferred_element_type=F32).astype(o_ref.dtype)


def _in_projection(x, mod3, g_norm1, w_head, w_tail, l):
    tm, tn = 1024, 1024
    return pl.pallas_call(
        functools.partial(_inproj_kernel, len(x)),
        out_shape=jax.ShapeDtypeStruct((N_TOK, P_COLS), BF16),
        grid=(N_TOK // tm, P_COLS // tn),
        in_specs=_x_specs(x, tm) + [
            pl.BlockSpec((None, 6, D_MODEL), lambda i, j: (_mod_row(i * tm), 0, 0)),
            pl.BlockSpec((None, 1, D_MODEL), lambda i, j: (l, 0, 0)),
            _const_spec((D_MODEL, P_HEAD)),
            pl.BlockSpec((D_MODEL, tn), lambda i, j: (0, jnp.maximum(j - 1, 0))),
        ],
        out_specs=pl.BlockSpec((tm, tn), lambda i, j: (i, j)),
        scratch_shapes=[pltpu.VMEM((tm, D_MODEL), BF16)],
        compiler_params=_cparams(("arbitrary", "arbitrary")),
        name="in_projection",
    )(*x, mod3, g_norm1.reshape(DEPTH, 1, D_MODEL), w_head, w_tail)


def _rope_tables():
    t = np.arange(DEC_SEQ)
    nf = MLA_ROPE // 4
    inv = (np.float32(ROPE_THETA) ** (-np.arange(nf, dtype=np.float32) / np.float32(nf))).astype(np.float32)
    ang_r = (t // GRID_W).astype(np.float32)[:, None] * inv[None, :]
    ang_c = (t % GRID_W).astype(np.float32)[:, None] * inv[None, :]
    c = np.zeros((DEC_SEQ, HEAD_PAD), np.float32)
    s1 = np.zeros((DEC_SEQ, HEAD_PAD), np.float32)
    s2 = np.zeros((DEC_SEQ, HEAD_PAD), np.float32)
    c[:, :MLA_NOPE] = 1.0
    for base, ang in ((MLA_NOPE, ang_r), (MLA_NOPE + 2 * nf, ang_c)):
        c[:, base:base + nf] = np.cos(ang)
        c[:, base + nf:base + 2 * nf] = np.cos(ang)
        s1[:, base:base + nf] = -np.sin(ang)
        s2[:, base + nf:base + 2 * nf] = np.sin(ang)
    return c, s1, s2


def _rope(x, c, s1, s2):
    nf = MLA_ROPE // 4
    return x * c + pltpu.roll(x, HEAD_PAD - nf, 1) * s1 + pltpu.roll(x, nf, 1) * s2


def _head_norm(xh, g):
    ms = jnp.sum(xh * xh, axis=-1, keepdims=True) * (1.0 / MLA_QK)
    return xh * lax.rsqrt(ms + EPS) * g


def _mla_kernel(latent, tq, *refs):
    if latent:
        (pqa_ref, pkva_ref, pkpe_ref, cckv_ref, ckpe_ref, rc_ref, rs1_ref, rs2_ref,
         gqa_ref, wuq_ref, gkva_ref, wk_ref, wv_ref, gq_ref, gk_ref,
         o_ref, k_scr, v_scr) = refs
    else:
        (pqa_ref, pkva_ref, pkpe_ref,
         gqa_ref, wuq_ref, gkva_ref, wk_ref, wv_ref, gq_ref, gk_ref,
         o_ref, ckv_ref, kpe_ref, k_scr, v_scr) = refs
    qi = pl.program_id(1)
    n_past = PAST_LEN if latent else 0

    vlane = lax.broadcasted_iota(jnp.int32, (1, MLA_PAD_W), 1)
    v_ones = jnp.where(vlane % HEAD_PAD == MLA_V, 1.0, 0.0)

    def put_kv(ckvn, kpe4, row0, rope):
        n = ckvn.shape[0]
        kk = _dot(ckvn, wk_ref[...]) + kpe4
        v_scr[row0:row0 + n, :] = (_dot(ckvn, wv_ref[...]) + v_ones).astype(BF16)
        for h in range(MLA_HEADS):
            sl = slice(h * HEAD_PAD, (h + 1) * HEAD_PAD)
            kh = _head_norm(kk[:, sl], gk_ref[...])
            if rope:
                kh = _rope(kh, rc_ref[...], rs1_ref[...], rs2_ref[...])
            k_scr[row0:row0 + n, sl] = kh.astype(BF16)

    @pl.when(qi == 0)
    def _():
        if latent:
            put_kv(cckv_ref[...], ckpe_ref[...], 0, False)
        kva = pkva_ref[...].astype(F32)
        ckv = kva[:, :MLA_KV_LORA]
        ckvn = ckv * lax.rsqrt(jnp.mean(ckv * ckv, axis=-1, keepdims=True) + EPS) * gkva_ref[...]
        if not latent:
            ckv_ref[...] = ckvn
            kpe_ref[...] = kva[:, MLA_KV_LORA:MLA_KV_LORA + MLA_ROPE]
        put_kv(ckvn, pkpe_ref[...].astype(F32), n_past, latent)

    qa = pqa_ref[...].astype(F32)
    qan = qa * lax.rsqrt(jnp.mean(qa * qa, axis=-1, keepdims=True) + EPS) * gqa_ref[...]
    q = _dot(qan, wuq_ref[...])
    scale = MLA_QK ** -0.5
    for h in range(MLA_HEADS):
        sl = slice(h * HEAD_PAD, (h + 1) * HEAD_PAD)
        qh = _head_norm(q[:, sl], gq_ref[...])
        if latent:
            rows = pl.ds(pl.multiple_of(qi * tq, tq), tq)
            qh = _rope(qh, rc_ref[rows, :], rs1_ref[rows, :], rs2_ref[rows, :])
        s = _dot_nt(qh * scale, k_scr[:, sl])
        m = jnp.max(s, axis=-1, keepdims=True)
        e = jnp.exp((s - m).astype(BF16))
        o = jnp.dot(e, v_scr[:, sl], preferred_element_type=F32)
        o_ref[:, sl] = (o / o[:, MLA_V:MLA_V + 1]).astype(o_ref.dtype)


def _mla(p, l, latent, cache_ckv, cache_kpe4, W):
    if latent:
        nb, s, tq, row_off = DEC_BATCH, DEC_SEQ, 256, N_CTX_TOK
    else:
        nb, s, tq, row_off = BATCH, SEQ, 256, 0
    nq = s // tq
    sk = s + (PAST_LEN if latent else 0)
    in_specs = [
        pl.BlockSpec((tq, MLA_Q_LORA), lambda b, qi: (row_off // tq + b * nq + qi, P_QA // MLA_Q_LORA)),
        pl.BlockSpec((s, 256), lambda b, qi: (row_off // s + b, P_KVA // 256)),
        pl.BlockSpec((s, MLA_PAD_W), lambda b, qi: (row_off // s + b, P_KPE4 // MLA_PAD_W)),
    ]
    args = [p, p, p]
    if latent:
        in_specs += [
            pl.BlockSpec((None, None, PAST_LEN, MLA_KV_LORA), lambda b, qi: (b, l, 0, 0)),
            pl.BlockSpec((None, PAST_LEN, MLA_PAD_W), lambda b, qi: (b, 0, 0)),
            _const_spec((DEC_SEQ, HEAD_PAD)), _const_spec((DEC_SEQ, HEAD_PAD)), _const_spec((DEC_SEQ, HEAD_PAD)),
        ]
        args += [cache_ckv, cache_kpe4] + [jnp.asarray(t) for t in _rope_tables()]
    in_specs += [
        _const_spec((1, MLA_Q_LORA)), _const_spec((MLA_Q_LORA, MLA_PAD_W)), _const_spec((1, MLA_KV_LORA)),
        _const_spec((MLA_KV_LORA, MLA_PAD_W)), _const_spec((MLA_KV_LORA, MLA_PAD_W)),
        _const_spec((1, HEAD_PAD)), _const_spec((1, HEAD_PAD)),
    ]
    args += [W['g_qa'], W['w_uq_p'], W['g_kva'], W['w_ukv_k'], W['w_ukv_v'], W['g_mla_q_p'], W['g_mla_k_p']]
    out_shape = [jax.ShapeDtypeStruct((nb * s, MLA_PAD_W), BF16)]
    out_specs = [pl.BlockSpec((tq, MLA_PAD_W), lambda b, qi: (b * nq + qi, 0))]
    if not latent:
        out_shape.append(jax.ShapeDtypeStruct((nb, s, MLA_KV_LORA), F32))
        out_specs.append(pl.BlockSpec((None, s, MLA_KV_LORA), lambda b, qi: (b, 0, 0)))
        out_shape.append(jax.ShapeDtypeStruct((nb, s, MLA_ROPE), F32))
        out_specs.append(pl.BlockSpec((None, s, MLA_ROPE), lambda b, qi: (b, 0, 0)))
    return pl.pallas_call(
        functools.partial(_mla_kernel, latent, tq),
        out_shape=out_shape,
        grid=(nb, nq),
        in_specs=in_specs,
        out_specs=out_specs,
        scratch_shapes=[pltpu.VMEM((sk, MLA_PAD_W), BF16), pltpu.VMEM((sk, MLA_PAD_W), BF16)],
        compiler_params=_cparams(("arbitrary", "arbitrary")),
        name="mla_latent" if latent else "mla_context",
    )(*args)


def _head_masks(width):
    lane = lax.broadcasted_iota(jnp.int32, (1, width), 1)
    return [(lane >= h * NA_HD) & (lane < (h + 1) * NA_HD) for h in range(NA_HEADS)]


def _group_norm64(x, g, masks):
    x2 = x * x
    inv = jnp.zeros_like(x)
    for m in masks:
        ms = jnp.sum(jnp.where(m, x2, 0.0), axis=-1, keepdims=True) * (1.0 / NA_HD)
        inv = jnp.where(m, lax.rsqrt(ms + EPS), inv)
    return x * inv * g


def _p_blocks(col0, n, rows, row_fn):
    def spec(k):
        return pl.BlockSpec((rows, BR_W), lambda *g: (row_fn(*g), col0 // BR_W + k))
    return [spec(k) for k in range(n)]


def _na_ctx_kernel(pq_ref, pk_ref, pv_ref, gq_ref, gk_ref, o_ref, k_ref, v_ref):
    masks = _head_masks(BR_W)
    qn = _group_norm64(pq_ref[...].astype(F32), gq_ref[...], masks)
    kn = _group_norm64(pk_ref[...].astype(F32), gk_ref[...], masks)
    v = pv_ref[...].astype(F32)
    k_ref[...] = kn
    v_ref[...] = v
    scale = NA_HD ** -0.5
    acc = jnp.zeros((pq_ref.shape[0], BR_W), F32)
    for m in masks:
        s = _dot_nt(jnp.where(m, qn, 0.0), kn) * scale
        mx = jnp.max(s, axis=-1, keepdims=True)
        e = jnp.exp(s - mx)
        den = jnp.sum(e, axis=-1, keepdims=True)
        acc = acc + jnp.where(m, _dot(e, v) / den, 0.0)
    o_ref[...] = acc.astype(o_ref.dtype)


def _na_context(p, W):
    s = SEQ
    return pl.pallas_call(
        _na_ctx_kernel,
        out_shape=[jax.ShapeDtypeStruct((N_CTX_TOK, BR_W), BF16),
                   jax.ShapeDtypeStruct((N_CTX_TOK, BR_W), F32),
                   jax.ShapeDtypeStruct((N_CTX_TOK, BR_W), F32)],
        grid=(BATCH,),
        in_specs=_p_blocks(P_NA, 3, s, lambda b: b) + [_const_spec((1, BR_W)), _const_spec((1, BR_W))],
        out_specs=[pl.BlockSpec((s, BR_W), lambda b: (b, 0))] * 3,
        compiler_params=_cparams(("arbitrary",)),
        name="na_context",
    )(p, p, p, W['g_na_q_t'], W['g_na_k_t'])


def _na_lat_kernel(pq_ref, pk_ref, pv_ref, ck_ref, cv_ref, bias_ref, gq_ref, gk_ref, o_ref,
                   q_scr, k_scr, v_scr, kc_scr, vc_scr):
    j = pl.program_id(1)
    masks = _head_masks(BR_W)

    zero = jnp.zeros((), BF16)
    one = jnp.ones((), BF16)
    lane = lax.broadcasted_iota(jnp.int32, (1, BR_W), 1)
    den_lanes = [((h + 1) % NA_HEADS) * NA_HD for h in range(NA_HEADS)]

    @pl.when(j == 0)
    def _():
        scale = NA_HD ** -0.5
        q_scr[...] = (_group_norm64(pq_ref[...].astype(F32), gq_ref[...], masks) * scale).astype(BF16)
        k_scr[...] = _group_norm64(pk_ref[...].astype(F32), gk_ref[...], masks).astype(BF16)
        kc_scr[...] = ck_ref[...].astype(BF16)
        v = pv_ref[...]
        vc = cv_ref[...].astype(BF16)
        for h, m in enumerate(masks):
            v_scr[h] = jnp.where(lane == den_lanes[h], one, jnp.where(m, v, zero))
            vc_scr[h] = jnp.where(lane == den_lanes[h], one, jnp.where(m, vc, zero))

    win0 = pl.multiple_of(jnp.where(j < 2, 0, DEC_SEQ - NA_WIN_KEYS), 256)
    q = q_scr[pl.ds(pl.multiple_of(j * NA_TQ, NA_TQ), NA_TQ), :]
    kw = k_scr[pl.ds(win0, NA_WIN_KEYS), :]
    kc = kc_scr[...]
    acc = jnp.zeros((NA_TQ, BR_W), F32)
    for h, m in enumerate(masks):
        qm = jnp.where(m, q, zero)
        s_loc = _dot_nt(qm, kw) + bias_ref[h]
        s_ctx = _dot_nt(qm, kc)
        mx = jnp.maximum(jnp.max(s_loc, axis=-1, keepdims=True), jnp.max(s_ctx, axis=-1, keepdims=True))
        e_loc = jnp.exp((s_loc - mx).astype(BF16))
        e_ctx = jnp.exp((s_ctx - mx).astype(BF16))
        o = (jnp.dot(e_loc, v_scr[h, pl.ds(win0, NA_WIN_KEYS), :], preferred_element_type=F32)
             + jnp.dot(e_ctx, vc_scr[h], preferred_element_type=F32))
        acc = acc + jnp.where(m, o / o[:, den_lanes[h]:den_lanes[h] + 1], 0.0)
    o_ref[...] = acc.astype(o_ref.dtype)


def _na_bias_table(rpb):
    n_dr, n_dc = 2 * NA_WIN_R - 1, 2 * NA_WIN_C - 1
    cols = np.arange(GRID_W)
    dc = np.clip(cols[None, :] - cols[:, None], -(NA_WIN_C - 1), NA_WIN_C - 1) + (NA_WIN_C - 1)
    cstart = np.clip(cols - NA_WIN_C // 2, 0, GRID_W - NA_WIN_C)
    col_ok = (cols[None, :] >= cstart[:, None]) & (cols[None, :] < cstart[:, None] + NA_WIN_C)
    place = (dc.reshape(-1)[None, :] == np.arange(n_dc)[:, None]).astype(np.float32)
    blocks = jnp.dot(rpb.reshape(NA_HEADS * n_dr, n_dc), place, precision=lax.Precision.HIGHEST)
    blocks = jnp.where(col_ok.reshape(-1)[None, :], blocks, -jnp.inf).reshape(NA_HEADS, n_dr, GRID_W, GRID_W)
    neg = jnp.full((NA_HEADS, 1, GRID_W, GRID_W), -jnp.inf, F32)
    blocks = jnp.concatenate([blocks, neg], axis=1)

    nt = DEC_SEQ // NA_TQ
    n_rows = DEC_SEQ // GRID_W
    rows_per_tile = NA_TQ // GRID_W
    sel = np.full((nt, rows_per_tile, NA_WIN_ROWS), n_dr, np.int32)
    for j in range(nt):
        win_row0 = 0 if j < nt // 2 else n_rows - NA_WIN_ROWS
        for rq in range(rows_per_tile):
            r = j * rows_per_tile + rq
            start = min(max(r - NA_WIN_R // 2, 0), n_rows - NA_WIN_R)
            for kr in range(NA_WIN_ROWS):
                if start <= win_row0 + kr < start + NA_WIN_R:
                    sel[j, rq, kr] = win_row0 + kr - r + (NA_WIN_R - 1)
    tiles = []
    for j in range(nt):
        rows = [jnp.concatenate([blocks[:, sel[j, rq, kr]] for kr in range(NA_WIN_ROWS)], axis=-1)
                for rq in range(rows_per_tile)]
        tiles.append(jnp.concatenate(rows, axis=1))
    return jnp.stack(tiles, axis=1)


def _na_latent(p, l, cache_k, cache_v, bias, W):
    s = DEC_SEQ
    nt = s // NA_TQ
    row_off = N_CTX_TOK
    return pl.pallas_call(
        _na_lat_kernel,
        out_shape=jax.ShapeDtypeStruct((N_LAT_TOK, BR_W), BF16),
        grid=(DEC_BATCH, nt),
        in_specs=_p_blocks(P_NA, 3, s, lambda b, j: row_off // s + b) + [
                  pl.BlockSpec((None, None, PAST_LEN, BR_W), lambda b, j: (b, l, 0, 0)),
                  pl.BlockSpec((None, None, PAST_LEN, BR_W), lambda b, j: (b, l, 0, 0)),
                  pl.BlockSpec((NA_HEADS, None, NA_TQ, NA_WIN_KEYS), lambda b, j: (0, j, 0, 0)),
                  _const_spec((1, BR_W)), _const_spec((1, BR_W))],
        out_specs=pl.BlockSpec((NA_TQ, BR_W), lambda b, j: (b * nt + j, 0)),
        scratch_shapes=[pltpu.VMEM((s, BR_W), BF16), pltpu.VMEM((s, BR_W), BF16),
                        pltpu.VMEM((NA_HEADS, s, BR_W), BF16),
                        pltpu.VMEM((PAST_LEN, BR_W), BF16), pltpu.VMEM((NA_HEADS, PAST_LEN, BR_W), BF16)],
        compiler_params=_cparams(("arbitrary", "arbitrary")),
        name="na_latent",
    )(p, p, p, cache_k, cache_v, bias, W['g_na_q_t'], W['g_na_k_t'])


CONV_HALO = 16
CONV_CHUNK = 128


SUBLANES = 8


def _dwconv_from_pad(pad_ref, w_ref, ksize, s, emit, shift_ref=None):
    half = ksize // 2
    if shift_ref is not None:
        n_rows = s + 2 * CONV_HALO - SUBLANES
        for p in range(1, SUBLANES):
            for c0 in range(0, n_rows, CONV_CHUNK):
                n = min(CONV_CHUNK, n_rows - c0)
                shift_ref[p - 1, c0:c0 + n, :] = pad_ref[c0 + p:c0 + p + n, :]
    for c0 in range(0, s, CONV_CHUNK):
        acc = jnp.zeros((CONV_CHUNK, BR_W), F32)
        for k in range(ksize):
            r0 = CONV_HALO + c0 + k - half
            p = r0 % SUBLANES
            if shift_ref is None or p == 0:
                win = pad_ref[r0:r0 + CONV_CHUNK, :]
            else:
                win = shift_ref[p - 1, r0 - p:r0 - p + CONV_CHUNK, :]
            acc = acc + win * w_ref[k:k + 1, :]
        emit(c0, acc)


def _conv_kernel(s, scb_ref, scc_ref, scx_ref, cfa_ref, cfb2_ref, scw_ref, cfw_ref, cfb_ref, lng_ref, lnb_ref,
                 ob_ref, od_ref, pad_ref, shift_ref):
    zeros = jnp.zeros((CONV_HALO, BR_W), F32)
    pad_ref[0:CONV_HALO, :] = zeros
    pad_ref[CONV_HALO + s:2 * CONV_HALO + s, :] = zeros

    pad_ref[CONV_HALO:CONV_HALO + s, :] = scc_ref[...].astype(F32) * scx_ref[...].astype(F32)

    def emit_b(c0, acc):
        ob_ref[c0:c0 + CONV_CHUNK, :] = (scb_ref[c0:c0 + CONV_CHUNK, :].astype(F32) * acc).astype(ob_ref.dtype)

    _dwconv_from_pad(pad_ref, scw_ref, SC_K, s, emit_b)

    pad_ref[CONV_HALO:CONV_HALO + s, :] = cfa_ref[...].astype(F32) * _sigmoid(cfb2_ref[...].astype(F32))

    def emit_d(c0, acc):
        u = acc + cfb_ref[...]
        mu = jnp.mean(u, axis=-1, keepdims=True)
        d = u - mu
        var = jnp.mean(d * d, axis=-1, keepdims=True)
        y = d * lax.rsqrt(var + EPS) * lng_ref[...] + lnb_ref[...]
        od_ref[c0:c0 + CONV_CHUNK, :] = _silu(y).astype(od_ref.dtype)

    _dwconv_from_pad(pad_ref, cfw_ref, CF_K, s, emit_d, shift_ref)


def _convs(p, latent, W):
    if latent:
        nb, s, row_off = DEC_BATCH, DEC_SEQ, N_CTX_TOK
    else:
        nb, s, row_off = BATCH, SEQ, 0
    return pl.pallas_call(
        functools.partial(_conv_kernel, s),
        out_shape=[jax.ShapeDtypeStruct((nb * s, BR_W), BF16)] * 2,
        grid=(nb,),
        in_specs=(_p_blocks(P_SC, 3, s, lambda b: row_off // s + b)
                  + _p_blocks(P_CF, 2, s, lambda b: row_off // s + b)
                  + [_const_spec((SC_K, BR_W)), _const_spec((CF_K, BR_W)),
                     _const_spec((1, BR_W)), _const_spec((1, BR_W)), _const_spec((1, BR_W))]),
        out_specs=[pl.BlockSpec((s, BR_W), lambda b: (b, 0))] * 2,
        scratch_shapes=[pltpu.VMEM((s + 2 * CONV_HALO, BR_W), F32),
                        pltpu.VMEM((SUBLANES - 1, s + 2 * CONV_HALO, BR_W), F32)],
        compiler_params=_cparams(("arbitrary",)),
        name="convs_latent" if latent else "convs_context",
    )(p, p, p, p, p, W['sc_w'], W['cf_w'], W['cf_b'], W['cf_ln_g'], W['cf_ln_b'])


def _merge_kernel(nc, nx, *refs):
    x_refs = refs[:nx]
    (mod_ref, oac_ref, oal_ref, obc_ref, obl_ref, occ_ref, ocl_ref, odc_ref, odl_ref,
     g0_ref, g1_ref, g2_ref, g3_ref, wa_ref, wb_ref, wo_ref, y_ref) = refs[nx:]
    is_ctx = pl.program_id(0) < nc
    merged = (_sigmoid(g0_ref[...]).astype(F32)
              * jnp.dot(_ctx_or_lat(is_ctx, oac_ref, oal_ref), wa_ref[...], preferred_element_type=F32))
    branches = ((obc_ref, obl_ref, g1_ref), (occ_ref, ocl_ref, g2_ref), (odc_ref, odl_ref, g3_ref))
    for n, (oc_ref, ol_ref, g_ref) in enumerate(branches):
        merged = merged + (_sigmoid(g_ref[...]).astype(F32)
                           * jnp.dot(_ctx_or_lat(is_ctx, oc_ref, ol_ref), wb_ref[n], preferred_element_type=F32))
    y_ref[...] = _x_load(x_refs) + mod_ref[2:3, :] * _dot(merged, wo_ref[...])


def _merge(x, mod3, p, o_a, o_b, o_c, o_d, W):
    tm = 512
    gate_spec = lambda n: pl.BlockSpec((tm, D_MODEL), lambda i: (i, P_GATE // D_MODEL + n))
    return pl.pallas_call(
        functools.partial(_merge_kernel, N_CTX_TOK // tm, len(x)),
        out_shape=jax.ShapeDtypeStruct((N_TOK, D_MODEL), F32),
        grid=(N_TOK // tm,),
        in_specs=(_x_specs(x, tm)
                  + [pl.BlockSpec((None, 6, D_MODEL), lambda i: (_mod_row(i * tm), 0, 0))]
                  + _pair_specs(tm, MLA_PAD_W) + _pair_specs(tm, BR_W) + _pair_specs(tm, BR_W) + _pair_specs(tm, BR_W)
                  + [gate_spec(0), gate_spec(1), gate_spec(2), gate_spec(3),
                     _const_spec((MLA_PAD_W, D_MODEL)), _const_spec((3, BR_W, D_MODEL)),
                     _const_spec((D_MODEL, D_MODEL))]),
        out_specs=pl.BlockSpec((tm, D_MODEL), lambda i: (i, 0)),
        compiler_params=_cparams(("arbitrary",)),
        name="merge",
    )(*x, mod3, *o_a, *o_b, *o_c, *o_d, p, p, p, p, W['w_br_a'], W['w_br_bcd'], W['w_o'])


def _norm2(x, g, mod_ref):
    y = x * lax.rsqrt(jnp.mean(x * x, axis=-1, keepdims=True) + EPS) * g
    return y * (1.0 + mod_ref[4:5, :]) + mod_ref[3:4, :]


def _ffn_kernel(x_ref, mod_ref, g_ref, wg_ref, wu_ref, wd_ref, y_ref, h_scr, acc_scr):
    f = pl.program_id(1)

    @pl.when(f == 0)
    def _():
        h_scr[...] = _norm2(x_ref[...], g_ref[...], mod_ref).astype(BF16)
        acc_scr[...] = jnp.zeros_like(acc_scr)

    h = h_scr[...]
    a = _silu(_dot(h, wg_ref[...])) * _dot(h, wu_ref[...])
    acc_scr[...] += _dot(a, wd_ref[...])

    @pl.when(f == pl.num_programs(1) - 1)
    def _():
        y_ref[...] = x_ref[...] + mod_ref[5:6, :] * acc_scr[...]


def _ffn(x_all, mod3, g_norm2, w_g, w_u, w_d, l, j):
    tm, tf = 1024, 512
    return pl.pallas_call(
        _ffn_kernel,
        out_shape=jax.ShapeDtypeStruct((N_TOK, D_MODEL), F32),
        grid=(N_TOK // tm, D_FF // tf),
        in_specs=[pl.BlockSpec((tm, D_MODEL), lambda i, f: (i, 0)),
                  pl.BlockSpec((None, 6, D_MODEL), lambda i, f: (_mod_row(i * tm), 0, 0)),
                  pl.BlockSpec((None, 1, D_MODEL), lambda i, f: (l, 0, 0)),
                  pl.BlockSpec((None, D_MODEL, tf), lambda i, f: (j, 0, f)),
                  pl.BlockSpec((None, D_MODEL, tf), lambda i, f: (j, 0, f)),
                  pl.BlockSpec((None, tf, D_MODEL), lambda i, f: (j, f, 0))],
        out_specs=pl.BlockSpec((tm, D_MODEL), lambda i, f: (i, 0)),
        scratch_shapes=[pltpu.VMEM((tm, D_MODEL), BF16), pltpu.VMEM((tm, D_MODEL), F32)],
        compiler_params=_cparams(("arbitrary", "arbitrary")),
        name="ffn_dense",
    )(x_all, mod3, g_norm2.reshape(DEPTH, 1, D_MODEL), w_g, w_u, w_d)


ROUTER_LANES = 128
ROUTE_TM = 512
MOE_TM = 1024
MOE_SUB = 256
MOE_TF = 512
RUN_ALIGN = 8
MOE_TILES = -(-(2 * N_TOK + (N_TOK // ROUTE_TM) * N_EXPERTS * RUN_ALIGN + N_EXPERTS * MOE_TM) // MOE_TM)
MOE_ROWS = MOE_TILES * MOE_TM
ROW_DMA_TM = 256
R_I1, R_I2, R_W1, R_W2, R_RANK1, R_RANK2 = range(6)


def _route_top2(h, wr):
    logits = jnp.dot(h, wr, preferred_element_type=F32, precision=lax.Precision.HIGHEST)
    lane = lax.broadcasted_iota(jnp.int32, logits.shape, 1).astype(F32)
    neg = jnp.float32(-jnp.inf)
    logits = jnp.where(lane < N_EXPERTS, logits, neg)
    m1 = jnp.max(logits, axis=-1, keepdims=True)
    i1 = jnp.min(jnp.where(logits == m1, lane, float(ROUTER_LANES)), axis=-1, keepdims=True)
    rest = jnp.where(lane == i1, neg, logits)
    m2 = jnp.max(rest, axis=-1, keepdims=True)
    i2 = jnp.min(jnp.where(rest == m2, lane, float(ROUTER_LANES)), axis=-1, keepdims=True)
    e2 = jnp.exp(m2 - m1)
    return lane, i1, i2, 1.0 / (1.0 + e2), e2 / (1.0 + e2)


def _route_kernel(x_ref, mod_ref, g_ref, wr_ref, route_ref, cnt_ref, tcarry_ref, carry_scr):
    @pl.when(pl.program_id(0) == 0)
    def _():
        carry_scr[...] = jnp.zeros_like(carry_scr)

    tcarry_ref[...] = carry_scr[...]

    h = _norm2(x_ref[...], g_ref[...], mod_ref)
    lane, i1, i2, w1, w2 = _route_top2(h, wr_ref[...])
    tm = h.shape[0]
    oh1 = lane == i1
    oh2 = lane == i2
    oh = jnp.where(oh1, 1.0, 0.0) + jnp.where(oh2, 1.0, 0.0)
    r = lax.broadcasted_iota(jnp.int32, (tm, tm), 0)
    c = lax.broadcasted_iota(jnp.int32, (tm, tm), 1)
    lower = jnp.where(r > c, 1.0, 0.0).astype(BF16)
    before = jnp.dot(lower, oh.astype(BF16), preferred_element_type=F32)
    rank1 = jnp.sum(jnp.where(oh1, before, 0.0), axis=-1, keepdims=True)
    rank2 = jnp.sum(jnp.where(oh2, before, 0.0), axis=-1, keepdims=True)
    carry_scr[...] += jnp.sum(oh, axis=0, keepdims=True)
    rec = jnp.zeros_like(lane)
    for k, v in ((R_I1, i1), (R_I2, i2), (R_W1, w1), (R_W2, w2), (R_RANK1, rank1), (R_RANK2, rank2)):
        rec = jnp.where(lane == float(k), v, rec)
    route_ref[...] = rec
    cnt_ref[...] = carry_scr[...]


def _moe_route(x_all, mod3, g_norm2, w_router_p, l, j):
    tm = ROUTE_TM
    return pl.pallas_call(
        _route_kernel,
        out_shape=[jax.ShapeDtypeStruct((N_TOK, ROUTER_LANES), F32),
                   jax.ShapeDtypeStruct((1, ROUTER_LANES), F32),
                   jax.ShapeDtypeStruct((N_TOK // tm, 1, ROUTER_LANES), F32)],
        grid=(N_TOK // tm,),
        in_specs=[pl.BlockSpec((tm, D_MODEL), lambda i: (i, 0)),
                  pl.BlockSpec((None, 6, D_MODEL), lambda i: (_mod_row(i * tm), 0, 0)),
                  pl.BlockSpec((None, 1, D_MODEL), lambda i: (l, 0, 0)),
                  pl.BlockSpec((None, D_MODEL, ROUTER_LANES), lambda i: (j, 0, 0))],
        out_specs=[pl.BlockSpec((tm, ROUTER_LANES), lambda i: (i, 0)),
                   pl.BlockSpec((1, ROUTER_LANES), lambda i: (0, 0)),
                   pl.BlockSpec((None, 1, ROUTER_LANES), lambda i: (i, 0, 0))],
        scratch_shapes=[pltpu.VMEM((1, ROUTER_LANES), F32)],
        compiler_params=_cparams(("arbitrary",)),
        name="moe_route",
    )(x_all, mod3, g_norm2.reshape(DEPTH, 1, D_MODEL), w_router_p)


def _row_copies(n, pos_refs, base, make):
    def body(r, carry):
        for pos_ref in pos_refs:
            make(r, pos_ref[base + r]).start()
        return carry
    lax.fori_loop(0, n, body, 0, unroll=8)


ZERO_ROWS = 512
N_GAPS = N_EXPERTS + 1


def _bit_chunks(n_units, max_units, make, wait=False):
    for b in range(max_units.bit_length()):
        units = 1 << b

        @pl.when((n_units & units) != 0)
        def _():
            cp = make(pl.multiple_of((n_units & (units - 1)) * RUN_ALIGN, RUN_ALIGN), units * RUN_ALIGN)
            if wait:
                cp.wait()
            else:
                cp.start()


def _dispatch_kernel(n_ref, lo_ref, dst_ref, gap0_ref, gapn_ref, x_ref, mod_ref, g_ref, route_ref, lovec_ref,
                     xs_ref, z_scr, zero_scr, sems, zsem):
    i = pl.program_id(0)
    tm = x_ref.shape[0]
    h = _norm2(x_ref[...], g_ref[...], mod_ref).astype(BF16)
    rec = route_ref[...]
    lane = lax.broadcasted_iota(jnp.int32, rec.shape, 1).astype(F32)
    lo_row = lovec_ref[...]

    def local_pos(i_lane, r_lane):
        lp = (jnp.sum(jnp.where(lane == rec[:, i_lane:i_lane + 1], lo_row, 0.0), axis=-1, keepdims=True)
              + rec[:, r_lane:r_lane + 1])
        return jnp.transpose(jnp.broadcast_to(lp, (tm, ROUTER_LANES)))[0:1, :]

    slot = lax.broadcasted_iota(jnp.int32, (z_scr.shape[1], tm), 0).astype(F32)
    perm = jnp.where((slot == local_pos(R_I1, R_RANK1)) | (slot == local_pos(R_I2, R_RANK2)), 1.0, 0.0)
    buf = i % 2
    z_scr[buf] = jnp.dot(perm.astype(BF16), h, preferred_element_type=F32)

    def run_copies(tile, b, wait):
        for e in range(N_EXPERTS):
            k = tile * N_EXPERTS + e
            src0 = lo_ref[k]
            dst0 = dst_ref[k]

            def make(off, rows, src0=src0, dst0=dst0):
                return pltpu.make_async_copy(
                    z_scr.at[b, pl.ds(pl.multiple_of(src0 + off, RUN_ALIGN), rows), :],
                    xs_ref.at[pl.ds(pl.multiple_of(dst0 + off, RUN_ALIGN), rows), :], sems.at[b])

            _bit_chunks(n_ref[k], tm // RUN_ALIGN, make, wait)

    run_copies(i, buf, False)

    @pl.when(i > 0)
    def _():
        run_copies(i - 1, 1 - buf, True)

    @pl.when(i == pl.num_programs(0) - 1)
    def _():
        run_copies(i, buf, True)

    @pl.when(i == pl.num_programs(0) - 1)
    def _():
        zero_scr[...] = jnp.zeros_like(zero_scr)
        for wait in (False, True):
            for g in range(N_GAPS):
                start = gap0_ref[g]
                n = gapn_ref[g]
                max_rows = MOE_TM if g < N_EXPERTS else MOE_ROWS - 2 * N_TOK
                for c in range(max_rows // ZERO_ROWS):
                    @pl.when(n >= (c + 1) * ZERO_ROWS)
                    def _():
                        cp = pltpu.make_async_copy(
                            zero_scr,
                            xs_ref.at[pl.ds(pl.multiple_of(start + c * ZERO_ROWS, RUN_ALIGN), ZERO_ROWS), :], zsem)
                        if wait:
                            cp.wait()
                        else:
                            cp.start()

                def make(off, rows, start=start, n=n):
                    tail0 = start + (n // ZERO_ROWS) * ZERO_ROWS
                    return pltpu.make_async_copy(
                        zero_scr.at[pl.ds(0, rows), :],
                        xs_ref.at[pl.ds(pl.multiple_of(tail0 + off, RUN_ALIGN), rows), :], zsem)

                _bit_chunks((n % ZERO_ROWS) // RUN_ALIGN, ZERO_ROWS // RUN_ALIGN - 1, make, wait)


def _moe_dispatch(n_tile, lo_tile, dst_tile, gap0, gapn, x_all, mod3, g_norm2, route, lo_vec, l):
    tm = ROUTE_TM
    spec = lambda shape, fn: pl.BlockSpec(shape, lambda i, *_: fn(i))
    return pl.pallas_call(
        _dispatch_kernel,
        out_shape=jax.ShapeDtypeStruct((MOE_ROWS, D_MODEL), F32),
        grid_spec=pltpu.PrefetchScalarGridSpec(
            num_scalar_prefetch=5,
            grid=(N_TOK // tm,),
            in_specs=[spec((tm, D_MODEL), lambda i: (i, 0)),
                      spec((None, 6, D_MODEL), lambda i: (_mod_row(i * tm), 0, 0)),
                      spec((None, 1, D_MODEL), lambda i: (l, 0, 0)),
                      spec((tm, ROUTER_LANES), lambda i: (i, 0)),
                      spec((None, 1, ROUTER_LANES), lambda i: (i, 0, 0))],
            out_specs=pl.BlockSpec(memory_space=pl.ANY),
            scratch_shapes=[pltpu.VMEM((2, 2 * tm + N_EXPERTS * RUN_ALIGN, D_MODEL), F32),
                            pltpu.VMEM((ZERO_ROWS, D_MODEL), F32),
                            pltpu.SemaphoreType.DMA((2,)), pltpu.SemaphoreType.DMA],
        ),
        compiler_params=_cparams(("arbitrary",)),
        name="moe_dispatch",
    )(n_tile, lo_tile, dst_tile, gap0, gapn, x_all, mod3, g_norm2.reshape(DEPTH, 1, D_MODEL), route, lo_vec)


def _gmm_kernel(te_ref, tr_ref, xs_ref, wg_ref, wu_ref, wd_ref, y_ref, wg_scr, wu_scr, wd_scr):
    del te_ref
    g = pl.program_id(0)
    f = pl.program_id(1)
    rows = tr_ref[g]

    @pl.when(f == 0)
    def _():
        y_ref[...] = jnp.zeros_like(y_ref)

    def sub_tile(s, wg, wu, wd):
        sl = slice(s * MOE_SUB, (s + 1) * MOE_SUB)
        x = xs_ref[sl, :].astype(BF16)
        a = (_silu(jnp.dot(x, wg, preferred_element_type=F32)) * jnp.dot(x, wu, preferred_element_type=F32))
        y_ref[sl, :] += jnp.dot(a.astype(BF16), wd, preferred_element_type=F32)

    @pl.when(rows == MOE_TM)
    def _():
        x = xs_ref[...].astype(BF16)
        a = _silu(_dot(x, wg_ref[...])) * _dot(x, wu_ref[...])
        y_ref[...] += _dot(a, wd_ref[...])

    @pl.when((rows > 0) & (rows < MOE_TM))
    def _():
        wg = wg_ref[...].astype(BF16)
        wu = wu_ref[...].astype(BF16)
        wd = wd_ref[...].astype(BF16)
        wg_scr[...] = wg
        wu_scr[...] = wu
        wd_scr[...] = wd
        sub_tile(0, wg, wu, wd)

    for s in range(1, MOE_TM // MOE_SUB):
        @pl.when((rows > s * MOE_SUB) & (rows < MOE_TM))
        def _():
            sub_tile(s, wg_scr[...], wu_scr[...], wd_scr[...])


def _moe_gmm(tile_expert, tile_rows, xs, w_g, w_u, w_d, j):
    tm, tf = MOE_TM, MOE_TF
    nf = D_FF // tf

    def f_eff(g, f, tr):
        return jnp.where(tr[g] > 0, f, nf - 1)

    return pl.pallas_call(
        _gmm_kernel,
        out_shape=jax.ShapeDtypeStruct((MOE_ROWS, D_MODEL), F32),
        grid_spec=pltpu.PrefetchScalarGridSpec(
            num_scalar_prefetch=2,
            grid=(MOE_TILES, nf),
            in_specs=[pl.BlockSpec((tm, D_MODEL), lambda g, f, te, tr: (g, 0)),
                      pl.BlockSpec((None, None, D_MODEL, tf), lambda g, f, te, tr: (j, te[g], 0, f_eff(g, f, tr))),
                      pl.BlockSpec((None, None, D_MODEL, tf), lambda g, f, te, tr: (j, te[g], 0, f_eff(g, f, tr))),
                      pl.BlockSpec((None, None, tf, D_MODEL), lambda g, f, te, tr: (j, te[g], f_eff(g, f, tr), 0))],
            out_specs=pl.BlockSpec((tm, D_MODEL), lambda g, f, te, tr: (g, 0)),
            scratch_shapes=[pltpu.VMEM((D_MODEL, tf), BF16), pltpu.VMEM((D_MODEL, tf), BF16),
                            pltpu.VMEM((tf, D_MODEL), BF16)],
        ),
        compiler_params=_cparams(("arbitrary", "arbitrary")),
        name="moe_experts",
    )(tile_expert, tile_rows, xs, w_g, w_u, w_d)


def _combine_kernel(n_out, pos1_ref, pos2_ref, x_ref, mod_ref, route_ref, y_ref, *refs):
    o_refs, (g1_scr, g2_scr, sems) = refs[:n_out], refs[n_out:]
    tm = g1_scr.shape[1]
    i = pl.program_id(0)
    n = pl.num_programs(0)

    def start_gathers(tile, slot):
        def make1(r, pos):
            return pltpu.make_async_copy(y_ref.at[pl.ds(pos, 1), :], g1_scr.at[slot, pl.ds(r, 1), :], sems.at[slot])

        def make2(r, pos):
            return pltpu.make_async_copy(y_ref.at[pl.ds(pos, 1), :], g2_scr.at[slot, pl.ds(r, 1), :], sems.at[slot])

        _row_copies(tm, (pos1_ref,), tile * tm, make1)
        _row_copies(tm, (pos2_ref,), tile * tm, make2)

    @pl.when(i == 0)
    def _():
        start_gathers(0, 0)

    slot = i % 2

    @pl.when(i + 1 < n)
    def _():
        start_gathers(i + 1, 1 - slot)

    pltpu.make_async_copy(y_ref.at[pl.ds(0, tm), :], g1_scr.at[slot], sems.at[slot]).wait()
    pltpu.make_async_copy(y_ref.at[pl.ds(0, tm), :], g2_scr.at[slot], sems.at[slot]).wait()
    rec = route_ref[...]
    w1 = rec[:, R_W1:R_W1 + 1]
    w2 = rec[:, R_W2:R_W2 + 1]
    _x_store(o_refs, x_ref[...] + mod_ref[5:6, :] * (w1 * g1_scr[slot] + w2 * g2_scr[slot]))


def _moe_combine(pos1, pos2, x_all, mod3, route, y, split_out):
    tm = ROW_DMA_TM
    return pl.pallas_call(
        functools.partial(_combine_kernel, 2 if split_out else 1),
        out_shape=_x_shapes(split_out),
        grid_spec=pltpu.PrefetchScalarGridSpec(
            num_scalar_prefetch=2,
            grid=(N_TOK // tm,),
            in_specs=[pl.BlockSpec((tm, D_MODEL), lambda i, p1, p2: (i, 0)),
                      pl.BlockSpec((None, 6, D_MODEL), lambda i, p1, p2: (_mod_row(i * tm), 0, 0)),
                      pl.BlockSpec((tm, ROUTER_LANES), lambda i, p1, p2: (i, 0)),
                      pl.BlockSpec(memory_space=pl.ANY)],
            out_specs=_x_specs((None,) * (2 if split_out else 1), tm),
            scratch_shapes=[pltpu.VMEM((2, tm, D_MODEL), F32), pltpu.VMEM((2, tm, D_MODEL), F32),
                            pltpu.SemaphoreType.DMA((2,))],
        ),
        compiler_params=_cparams(("arbitrary",)),
        name="moe_combine",
    )(pos1, pos2, x_all, mod3, route, y)


def _moe(x_all, mod3, g_norm2, w_router_p, w_g, w_u, w_d, l, j, split_out):
    route, cnt, tcarry = _moe_route(x_all, mod3, g_norm2, w_router_p, l, j)
    cnt = cnt[0, :N_EXPERTS].astype(jnp.int32)
    carry = tcarry[:, 0, :N_EXPERTS].astype(jnp.int32)
    n_tile = jnp.concatenate([carry[1:], cnt[None, :]], axis=0) - carry
    n_tile = (n_tile + RUN_ALIGN - 1) // RUN_ALIGN * RUN_ALIGN
    lo_tile = jnp.cumsum(n_tile, axis=1) - n_tile
    carry = jnp.cumsum(n_tile, axis=0) - n_tile
    cnt = jnp.sum(n_tile, axis=0)
    padded = (cnt + MOE_TM - 1) // MOE_TM * MOE_TM
    ends = jnp.cumsum(padded)
    offs = ends - padded
    dst_tile = offs[None, :] + carry
    experts = jnp.arange(N_EXPERTS, dtype=jnp.int32)
    gap0 = jnp.concatenate([offs + cnt, ends[-1:]])
    gapn = jnp.concatenate([padded - cnt, MOE_ROWS - ends[-1:]])
    lo_vec = jnp.pad(lo_tile.astype(F32), ((0, 0), (0, ROUTER_LANES - N_EXPERTS)))[:, None, :]

    def positions(i_lane, rank_lane):
        e = route[:, i_lane].astype(jnp.int32)
        base = jnp.repeat(dst_tile, ROUTE_TM, axis=0)
        off = jnp.sum(jnp.where(e[:, None] == experts[None, :], base, 0), axis=1)
        return off + route[:, rank_lane].astype(jnp.int32)

    pos1 = positions(R_I1, R_RANK1)
    pos2 = positions(R_I2, R_RANK2)
    tile_start = jnp.arange(MOE_TILES, dtype=jnp.int32) * MOE_TM
    last_tile = jnp.maximum(ends[-1] - MOE_TM, 0)
    owner_start = jnp.minimum(tile_start, last_tile)
    tile_expert = jnp.minimum(jnp.sum(owner_start[:, None] >= ends[None, :], axis=1), N_EXPERTS - 1).astype(jnp.int32)
    group_end = jnp.sum(jnp.where(tile_expert[:, None] == experts[None, :], (offs + cnt)[None, :], 0), axis=1)
    tile_rows = jnp.where(tile_start < ends[-1], jnp.clip(group_end - tile_start, 0, MOE_TM), 0).astype(jnp.int32)

    i32 = lambda a: a.reshape(-1).astype(jnp.int32)
    xs = _moe_dispatch(i32(n_tile // RUN_ALIGN), i32(lo_tile), i32(dst_tile), i32(gap0), i32(gapn),
                       x_all, mod3, g_norm2, route, lo_vec, l)
    y = _moe_gmm(tile_expert, tile_rows, xs, w_g, w_u, w_d, j)
    return tuple(_moe_combine(pos1, pos2, x_all, mod3, route, y, split_out))


def _pad_heads(w, per_head, lo):
    lead = w.shape[:-1]
    w = w.reshape(lead + (MLA_HEADS, per_head))
    w = jnp.pad(w, [(0, 0)] * len(lead) + [(0, 0), (lo, HEAD_PAD - lo - per_head)])
    return w.reshape(lead + (MLA_PAD_W,))


def _kpe_to_heads(kpe):
    z = jnp.pad(kpe, [(0, 0)] * (kpe.ndim - 1) + [(MLA_NOPE, HEAD_PAD - MLA_QK)])
    return jnp.tile(z, (1,) * (kpe.ndim - 1) + (MLA_HEADS,))


def _layer_weights(l, w_in, g_qa, w_uq, g_kva, w_ukv, g_mla_q, g_mla_k, sc_w, g_na_q, g_na_k,
                   cf_w, cf_b, cf_ln_g, cf_ln_b, w_br, w_o):
    s2 = MLA_Q_LORA + MLA_KV_LORA
    s3 = s2 + MLA_ROPE
    w_lo = w_in[l, :, :s3]
    w_head = jnp.concatenate([
        w_lo, jnp.zeros((D_MODEL, 256 - MLA_KV_LORA - MLA_ROPE), F32), _kpe_to_heads(w_lo[:, s2:s3])],
        axis=1).astype(BF16)
    ukv = w_ukv[l].reshape(MLA_KV_LORA, MLA_HEADS, MLA_NOPE + MLA_V)
    W = dict(
        w_head=w_head,
        w_tail=w_in[l, :, s3:].astype(BF16),
        g_qa=g_qa[l][None], g_kva=g_kva[l][None],
        w_uq_p=_pad_heads(w_uq[l], MLA_QK, 0).astype(BF16),
        w_ukv_k=_pad_heads(ukv[:, :, :MLA_NOPE].reshape(MLA_KV_LORA, -1), MLA_NOPE, 0).astype(BF16),
        w_ukv_v=_pad_heads(ukv[:, :, MLA_NOPE:].reshape(MLA_KV_LORA, -1), MLA_V, 0).astype(BF16),
        g_mla_q_p=jnp.pad(g_mla_q[l], (0, HEAD_PAD - MLA_QK))[None],
        g_mla_k_p=jnp.pad(g_mla_k[l], (0, HEAD_PAD - MLA_QK))[None],
        g_na_q_t=jnp.tile(g_na_q[l], NA_HEADS)[None], g_na_k_t=jnp.tile(g_na_k[l], NA_HEADS)[None],
        sc_w=sc_w[l], cf_w=cf_w[l], cf_b=cf_b[l][None], cf_ln_g=cf_ln_g[l][None], cf_ln_b=cf_ln_b[l][None],
        w_br_a=jnp.pad(w_br[l, 0].reshape(MLA_HEADS, MLA_V, D_MODEL),
                       ((0, 0), (0, HEAD_PAD - MLA_V), (0, 0))).reshape(MLA_PAD_W, D_MODEL).astype(BF16),
        w_br_bcd=w_br[l, 1:].astype(BF16),
        w_o=w_o[l].astype(BF16),
    )
    return W


def kernel(x_prompt, x_sample, cache_mla_ckv, cache_mla_kpe, cache_na_k, cache_na_v, c, c_ctx, w_ada, b_ada, g_norm1, w_in, g_qa, w_uq, g_kva, w_ukv, g_mla_q, g_mla_k, sc_w, g_na_q, g_na_k, na_rpb, cf_w, cf_b, cf_ln_g, cf_ln_b, w_br, w_o, g_norm2, w_ff_gate, w_ff_up, w_ff_down, w_router, w_e_gate, w_e_up, w_e_down):
    x = (x_prompt.reshape(N_CTX_TOK, D_MODEL), x_sample.reshape(N_LAT_TOK, D_MODEL))
    cvec = jnp.concatenate([c_ctx[None, :], c, jnp.zeros((MOD_ROWS - 1 - DEC_BATCH, D_MODEL), F32)], axis=0)
    cache_na_k2 = cache_na_k.reshape(DEC_BATCH, DEPTH, PAST_LEN, BR_W)
    cache_na_v2 = cache_na_v.reshape(DEC_BATCH, DEPTH, PAST_LEN, BR_W)
    w_router_p = jnp.pad(w_router, ((0, 0), (0, 0), (0, ROUTER_LANES - N_EXPERTS)))

    ckv_l, kpe_l, nak_l, nav_l = [], [], [], []
    for l in range(DEPTH):
        W = _layer_weights(l, w_in, g_qa, w_uq, g_kva, w_ukv, g_mla_q, g_mla_k, sc_w, g_na_q, g_na_k,
                           cf_w, cf_b, cf_ln_g, cf_ln_b, w_br, w_o)
        mod3 = _modulation(cvec, w_ada, b_ada, l).reshape(MOD_ROWS, 6, D_MODEL)
        p = _in_projection(x, mod3, g_norm1, W['w_head'], W['w_tail'], l)

        oa_c, ckv_new, kpe_new = _mla(p, l, False, None, None, W)
        (oa_l,) = _mla(p, l, True, cache_mla_ckv, _kpe_to_heads(cache_mla_kpe[:, l]), W)
        oc_c, nak_new, nav_new = _na_context(p, W)
        oc_l = _na_latent(p, l, cache_na_k2, cache_na_v2, _na_bias_table(na_rpb[l]), W)
        ob_c, od_c = _convs(p, False, W)
        ob_l, od_l = _convs(p, True, W)

        x_all = _merge(x, mod3, p, (oa_c, oa_l), (ob_c, ob_l), (oc_c, oc_l), (od_c, od_l), W)

        last = l == DEPTH - 1
        if l % 2 == 0:
            x = (_ffn(x_all, mod3, g_norm2, w_ff_gate, w_ff_up, w_ff_down, l, l // 2),)
        else:
            x = _moe(x_all, mod3, g_norm2, w_router_p, w_e_gate, w_e_up, w_e_down, l, l // 2, last)

        ckv_l.append(ckv_new)
        kpe_l.append(kpe_new)
        nak_l.append(nak_new.reshape(BATCH, SEQ, NA_HEADS, NA_HD))
        nav_l.append(nav_new.reshape(BATCH, SEQ, NA_HEADS, NA_HD))

    if len(x) == 1:
        x = (x[0][:N_CTX_TOK], x[0][N_CTX_TOK:])
    y_prompt = x[0].reshape(BATCH, SEQ, D_MODEL)
    y_sample = x[1].reshape(DEC_BATCH, DEC_SEQ, D_MODEL)
    return (y_prompt, y_sample, jnp.stack(ckv_l, axis=1), jnp.stack(kpe_l, axis=1),
            jnp.stack(nak_l, axis=1), jnp.stack(nav_l, axis=1))
```

```python
import functools

import numpy as np
import jax
import jax.numpy as jnp
from jax import lax
from jax.experimental import pallas as pl
from jax.experimental.pallas import tpu as pltpu

F32 = jnp.float32
BF16 = jnp.bfloat16

D_MODEL = 1024
BATCH = 16
SEQ = 256
DEPTH = 2
DEC_BATCH = 8
DEC_SEQ = 1024
PAST_LEN = 512
GRID_W = 64
N_BRANCH = 4
BR_W = 256
MLA_HEADS = 4
MLA_NOPE = 64
MLA_ROPE = 32
MLA_QK = 96
MLA_V = 64
MLA_Q_LORA = 256
MLA_KV_LORA = 128
SC_K = 3
NA_HEADS = 4
NA_HD = 64
NA_WIN_R = 8
NA_WIN_C = 16
CF_K = 31
D_FF = 3584
N_EXPERTS = 8
ROPE_THETA = 10000.0
EPS = 1e-6

N_CTX_TOK = BATCH * SEQ
N_LAT_TOK = DEC_BATCH * DEC_SEQ
N_TOK = N_CTX_TOK + N_LAT_TOK
MOD_ROWS = 16

HEAD_PAD = 128
MLA_PAD_W = MLA_HEADS * HEAD_PAD

P_GATE = 0
P_QA = 4096
P_KVA = 4352
P_SC = 4608
P_NA = 5376
P_CF = 6144
P_COLS = 6656

VMEM_LIMIT = 56 * 1024 * 1024

NA_TQ = 256
NA_WIN_ROWS = 12
NA_WIN_KEYS = NA_WIN_ROWS * GRID_W


def _cparams(sem):
    return pltpu.CompilerParams(dimension_semantics=sem, vmem_limit_bytes=VMEM_LIMIT)


def _const_spec(shape):
    nd = len(shape)
    return pl.BlockSpec(shape, lambda *_: (0,) * nd)


def _mod_row(tok_start):
    return jnp.where(tok_start < N_CTX_TOK, 0, 1 + (tok_start - N_CTX_TOK) // DEC_SEQ)


def _sigmoid(x):
    return 0.5 * jnp.tanh(0.5 * x) + 0.5


def _silu(x):
    return x * _sigmoid(x)


def _ctx_or_lat(is_ctx, c_ref, l_ref):
    return jnp.where(is_ctx, c_ref[...], l_ref[...])


def _dot(a, b):
    return jnp.dot(a.astype(BF16), b.astype(BF16), preferred_element_type=F32)


def _dot_nt(a, b):
    return lax.dot_general(a.astype(BF16), b.astype(BF16), (((1,), (1,)), ((), ())),
                           preferred_element_type=F32)


def _mod_kernel(c_ref, w_ref, b_ref, o_ref):
    o_ref[...] = _dot(_silu(c_ref[...]), w_ref[...]) + b_ref[...]


def _modulation(cvec, w_ada, b_ada, l):
    tn = 1024
    return pl.pallas_call(
        _mod_kernel,
        out_shape=jax.ShapeDtypeStruct((MOD_ROWS, 6 * D_MODEL), F32),
        grid=(6 * D_MODEL // tn,),
        in_specs=[
            _const_spec((MOD_ROWS, D_MODEL)),
            pl.BlockSpec((None, D_MODEL, tn), lambda j: (l, 0, j)),
            pl.BlockSpec((None, 1, tn), lambda j: (l, 0, j)),
        ],
        out_specs=pl.BlockSpec((MOD_ROWS, tn), lambda j: (0, j)),
        compiler_params=_cparams(("arbitrary",)),
        name="modulation",
    )(cvec, w_ada, b_ada.reshape(DEPTH, 1, 6 * D_MODEL))


def _pair_specs(tm, width, buffers=None):
    nc = N_CTX_TOK // tm
    mode = {} if buffers is None else dict(pipeline_mode=pl.Buffered(buffers))
    return [pl.BlockSpec((tm, width), lambda i, *_: (jnp.minimum(i, nc - 1), 0), **mode),
            pl.BlockSpec((tm, width), lambda i, *_: (jnp.maximum(i - nc, 0), 0), **mode)]


def _pair_shapes(width, dtype):
    return [jax.ShapeDtypeStruct((N_CTX_TOK, width), dtype), jax.ShapeDtypeStruct((N_LAT_TOK, width), dtype)]


def _store_pair(is_ctx, c_ref, l_ref, val):
    @pl.when(is_ctx)
    def _():
        c_ref[...] = val

    @pl.when(jnp.logical_not(is_ctx))
    def _():
        l_ref[...] = val


def _x_specs(x, tm, buffers=None):
    if len(x) == 2:
        return _pair_specs(tm, D_MODEL, buffers)
    mode = {} if buffers is None else dict(pipeline_mode=pl.Buffered(buffers))
    return [pl.BlockSpec((tm, D_MODEL), lambda i, *_: (i, 0), **mode)]


def _x_shapes(split):
    return _pair_shapes(D_MODEL, F32) if split else [jax.ShapeDtypeStruct((N_TOK, D_MODEL), F32)]


def _x_load(x_refs):
    if len(x_refs) == 1:
        return x_refs[0][...]
    tm = x_refs[0].shape[0]
    return _ctx_or_lat(pl.program_id(0) < N_CTX_TOK // tm, *x_refs)


def _x_store(y_refs, val):
    if len(y_refs) == 1:
        y_refs[0][...] = val
    else:
        tm = y_refs[0].shape[0]
        _store_pair(pl.program_id(0) < N_CTX_TOK // tm, *y_refs, val)


def _inproj_kernel(nx, *refs):
    x_refs, (mod_ref, g_ref, w_ref, o_ref, h_scr) = refs[:nx], refs[nx:]

    @pl.when(pl.program_id(1) == 0)
    def _():
        x = _x_load(x_refs)
        y = x * lax.rsqrt(jnp.mean(x * x, axis=-1, keepdims=True) + EPS) * g_ref[...]
        h_scr[...] = (y * (1.0 + mod_ref[1:2, :]) + mod_ref[0:1, :]).astype(BF16)

    o_ref[...] = jnp.dot(h_scr[...], w_ref[...], preferred_element_type=F32).astype(o_ref.dtype)


def _in_projection(x, mod3, g_norm1, w_in_p, l):
    tm, tn = 512, P_COLS // 2
    return pl.pallas_call(
        functools.partial(_inproj_kernel, len(x)),
        out_shape=jax.ShapeDtypeStruct((N_TOK, P_COLS), BF16),
        grid=(N_TOK // tm, P_COLS // tn),
        in_specs=_x_specs(x, tm) + [
            pl.BlockSpec((None, 6, D_MODEL), lambda i, j: (_mod_row(i * tm), 0, 0)),
            pl.BlockSpec((None, 1, D_MODEL), lambda i, j: (l, 0, 0)),
            pl.BlockSpec((D_MODEL, tn), lambda i, j: (0, j)),
        ],
        out_specs=pl.BlockSpec((tm, tn), lambda i, j: (i, j)),
        scratch_shapes=[pltpu.VMEM((tm, D_MODEL), BF16)],
        compiler_params=_cparams(("arbitrary", "arbitrary")),
        name="in_projection",
    )(*x, mod3, g_norm1.reshape(DEPTH, 1, D_MODEL), w_in_p)


def _rope_tables():
    t = np.arange(DEC_SEQ)
    nf = MLA_ROPE // 4
    inv = (np.float32(ROPE_THETA) ** (-np.arange(nf, dtype=np.float32) / np.float32(nf))).astype(np.float32)
    ang_r = (t // GRID_W).astype(np.float32)[:, None] * inv[None, :]
    ang_c = (t % GRID_W).astype(np.float32)[:, None] * inv[None, :]
    c = np.zeros((DEC_SEQ, HEAD_PAD), np.float32)
    s1 = np.zeros((DEC_SEQ, HEAD_PAD), np.float32)
    s2 = np.zeros((DEC_SEQ, HEAD_PAD), np.float32)
    c[:, :MLA_NOPE] = 1.0
    for base, ang in ((MLA_NOPE, ang_r), (MLA_NOPE + 2 * nf, ang_c)):
        c[:, base:base + nf] = np.cos(ang)
        c[:, base + nf:base + 2 * nf] = np.cos(ang)
        s1[:, base:base + nf] = -np.sin(ang)
        s2[:, base + nf:base + 2 * nf] = np.sin(ang)
    return c, s1, s2


def _rope(x, c, s1, s2):
    nf = MLA_ROPE // 4
    return x * c + pltpu.roll(x, HEAD_PAD - nf, 1) * s1 + pltpu.roll(x, nf, 1) * s2


def _head_norm(xh, g):
    ms = jnp.sum(xh * xh, axis=-1, keepdims=True) * (1.0 / MLA_QK)
    return xh * lax.rsqrt(ms + EPS) * g


def _mla_kernel(latent, tq, *refs):
    if latent:
        (pqa_ref, pkva_ref, place_ref, cckv_ref, ckpe_ref, rc_ref, rs1_ref, rs2_ref,
         gqa_ref, wuq_ref, gkva_ref, wk_ref, wv_ref, gq_ref, gk_ref,
         o_ref, k_scr, v_scr) = refs
    else:
        (pqa_ref, pkva_ref, place_ref,
         gqa_ref, wuq_ref, gkva_ref, wk_ref, wv_ref, gq_ref, gk_ref,
         o_ref, ckv_ref, kpe_ref, k_scr, v_scr) = refs
    qi = pl.program_id(1)
    n_past = PAST_LEN if latent else 0

    vlane = lax.broadcasted_iota(jnp.int32, (1, MLA_PAD_W), 1)
    v_ones = jnp.where(vlane % HEAD_PAD == MLA_V, 1.0, 0.0)

    def put_kv(ckvn, kpe, row0, rope, kpe_is_bf16):
        n = ckvn.shape[0]
        if kpe_is_bf16:
            kpe4 = _dot(kpe, place_ref[...])
        else:
            kpe4 = jnp.dot(kpe, place_ref[...], preferred_element_type=F32, precision=lax.Precision.HIGHEST)
        kk = _dot(ckvn, wk_ref[...]) + kpe4
        v_scr[row0:row0 + n, :] = (_dot(ckvn, wv_ref[...]) + v_ones).astype(BF16)
        for h in range(MLA_HEADS):
            sl = slice(h * HEAD_PAD, (h + 1) * HEAD_PAD)
            kh = _head_norm(kk[:, sl], gk_ref[...])
            if rope:
                kh = _rope(kh, rc_ref[...], rs1_ref[...], rs2_ref[...])
            k_scr[row0:row0 + n, sl] = kh.astype(BF16)

    @pl.when(qi == 0)
    def _():
        if latent:
            put_kv(cckv_ref[...], ckpe_ref[...], 0, False, False)
        kva = pkva_ref[...].astype(F32)
        ckv = kva[:, :MLA_KV_LORA]
        kpe = kva[:, MLA_KV_LORA:MLA_KV_LORA + MLA_ROPE]
        ckvn = ckv * lax.rsqrt(jnp.mean(ckv * ckv, axis=-1, keepdims=True) + EPS) * gkva_ref[...]
        if not latent:
            ckv_ref[...] = ckvn
            kpe_ref[...] = kpe
        put_kv(ckvn, kpe, n_past, latent, True)

    qa = pqa_ref[...].astype(F32)
    qan = qa * lax.rsqrt(jnp.mean(qa * qa, axis=-1, keepdims=True) + EPS) * gqa_ref[...]
    q = _dot(qan, wuq_ref[...])
    scale = MLA_QK ** -0.5
    for h in range(MLA_HEADS):
        sl = slice(h * HEAD_PAD, (h + 1) * HEAD_PAD)
        qh = _head_norm(q[:, sl], gq_ref[...])
        if latent:
            rows = pl.ds(pl.multiple_of(qi * tq, tq), tq)
            qh = _rope(qh, rc_ref[rows, :], rs1_ref[rows, :], rs2_ref[rows, :])
        s = _dot_nt(qh * scale, k_scr[:, sl])
        m = jnp.max(s, axis=-1, keepdims=True)
        e = jnp.exp((s - m).astype(BF16))
        o = jnp.dot(e, v_scr[:, sl], preferred_element_type=F32)
        o_ref[:, sl] = (o / o[:, MLA_V:MLA_V + 1]).astype(o_ref.dtype)


def _kpe_placement():
    place = np.zeros((MLA_ROPE, MLA_PAD_W), np.float32)
    for h in range(MLA_HEADS):
        place[np.arange(MLA_ROPE), h * HEAD_PAD + MLA_NOPE + np.arange(MLA_ROPE)] = 1.0
    return place


def _mla(p, l, latent, cache_ckv, cache_kpe, W):
    if latent:
        nb, s, tq, row_off = DEC_BATCH, DEC_SEQ, 256, N_CTX_TOK
    else:
        nb, s, tq, row_off = BATCH, SEQ, 256, 0
    nq = s // tq
    sk = s + (PAST_LEN if latent else 0)
    in_specs = [
        pl.BlockSpec((tq, MLA_Q_LORA), lambda b, qi: (row_off // tq + b * nq + qi, P_QA // MLA_Q_LORA)),
        pl.BlockSpec((s, 256), lambda b, qi: (row_off // s + b, P_KVA // 256)),
        _const_spec((MLA_ROPE, MLA_PAD_W)),
    ]
    args = [p, p, jnp.asarray(_kpe_placement())]
    if latent:
        in_specs += [
            pl.BlockSpec((None, None, PAST_LEN, MLA_KV_LORA), lambda b, qi: (b, l, 0, 0)),
            pl.BlockSpec((None, None, PAST_LEN, MLA_ROPE), lambda b, qi: (b, l, 0, 0)),
            _const_spec((DEC_SEQ, HEAD_PAD)), _const_spec((DEC_SEQ, HEAD_PAD)), _const_spec((DEC_SEQ, HEAD_PAD)),
        ]
        args += [cache_ckv, cache_kpe] + [jnp.asarray(t) for t in _rope_tables()]
    in_specs += [
        _const_spec((1, MLA_Q_LORA)), _const_spec((MLA_Q_LORA, MLA_PAD_W)), _const_spec((1, MLA_KV_LORA)),
        _const_spec((MLA_KV_LORA, MLA_PAD_W)), _const_spec((MLA_KV_LORA, MLA_PAD_W)),
        _const_spec((1, HEAD_PAD)), _const_spec((1, HEAD_PAD)),
    ]
    args += [W['g_qa'], W['w_uq_p'], W['g_kva'], W['w_ukv_k'], W['w_ukv_v'], W['g_mla_q_p'], W['g_mla_k_p']]
    out_shape = [jax.ShapeDtypeStruct((nb * s, MLA_PAD_W), BF16)]
    out_specs = [pl.BlockSpec((tq, MLA_PAD_W), lambda b, qi: (b * nq + qi, 0))]
    if not latent:
        out_shape.append(jax.ShapeDtypeStruct((nb, s, MLA_KV_LORA), F32))
        out_specs.append(pl.BlockSpec((None, s, MLA_KV_LORA), lambda b, qi: (b, 0, 0)))
        out_shape.append(jax.ShapeDtypeStruct((nb, s, MLA_ROPE), F32))
        out_specs.append(pl.BlockSpec((None, s, MLA_ROPE), lambda b, qi: (b, 0, 0)))
    return pl.pallas_call(
        functools.partial(_mla_kernel, latent, tq),
        out_shape=out_shape,
        grid=(nb, nq),
        in_specs=in_specs,
        out_specs=out_specs,
        scratch_shapes=[pltpu.VMEM((sk, MLA_PAD_W), BF16), pltpu.VMEM((sk, MLA_PAD_W), BF16)],
        compiler_params=_cparams(("arbitrary", "arbitrary")),
        name="mla_latent" if latent else "mla_context",
    )(*args)


def _head_masks(width):
    lane = lax.broadcasted_iota(jnp.int32, (1, width), 1)
    return [(lane >= h * NA_HD) & (lane < (h + 1) * NA_HD) for h in range(NA_HEADS)]


def _group_norm64(x, g, masks):
    x2 = x * x
    inv = jnp.zeros_like(x)
    for m in masks:
        ms = jnp.sum(jnp.where(m, x2, 0.0), axis=-1, keepdims=True) * (1.0 / NA_HD)
        inv = jnp.where(m, lax.rsqrt(ms + EPS), inv)
    return x * inv * g


def _p_blocks(col0, n, rows, row_fn):
    def spec(k):
        return pl.BlockSpec((rows, BR_W), lambda *g: (row_fn(*g), col0 // BR_W + k))
    return [spec(k) for k in range(n)]


def _na_ctx_kernel(pq_ref, pk_ref, pv_ref, gq_ref, gk_ref, o_ref, k_ref, v_ref):
    masks = _head_masks(BR_W)
    qn = _group_norm64(pq_ref[...].astype(F32), gq_ref[...], masks)
    kn = _group_norm64(pk_ref[...].astype(F32), gk_ref[...], masks)
    v = pv_ref[...].astype(F32)
    k_ref[...] = kn
    v_ref[...] = v
    scale = NA_HD ** -0.5
    acc = jnp.zeros((pq_ref.shape[0], BR_W), F32)
    for m in masks:
        s = _dot_nt(jnp.where(m, qn, 0.0), kn) * scale
        mx = jnp.max(s, axis=-1, keepdims=True)
        e = jnp.exp(s - mx)
        den = jnp.sum(e, axis=-1, keepdims=True)
        acc = acc + jnp.where(m, _dot(e, v) / den, 0.0)
    o_ref[...] = acc.astype(o_ref.dtype)


def _na_context(p, W):
    s = SEQ
    return pl.pallas_call(
        _na_ctx_kernel,
        out_shape=[jax.ShapeDtypeStruct((N_CTX_TOK, BR_W), BF16),
                   jax.ShapeDtypeStruct((N_CTX_TOK, BR_W), F32),
                   jax.ShapeDtypeStruct((N_CTX_TOK, BR_W), F32)],
        grid=(BATCH,),
        in_specs=_p_blocks(P_NA, 3, s, lambda b: b) + [_const_spec((1, BR_W)), _const_spec((1, BR_W))],
        out_specs=[pl.BlockSpec((s, BR_W), lambda b: (b, 0))] * 3,
        compiler_params=_cparams(("arbitrary",)),
        name="na_context",
    )(p, p, p, W['g_na_q_t'], W['g_na_k_t'])


def _na_lat_kernel(pq_ref, pk_ref, pv_ref, ck_ref, cv_ref, bias_ref, gq_ref, gk_ref, o_ref,
                   q_scr, k_scr, v_scr, kc_scr, vc_scr):
    j = pl.program_id(1)
    masks = _head_masks(BR_W)

    zero = jnp.zeros((), BF16)
    one = jnp.ones((), BF16)
    lane = lax.broadcasted_iota(jnp.int32, (1, BR_W), 1)
    den_lanes = [((h + 1) % NA_HEADS) * NA_HD for h in range(NA_HEADS)]

    @pl.when(j == 0)
    def _():
        scale = NA_HD ** -0.5
        q_scr[...] = (_group_norm64(pq_ref[...].astype(F32), gq_ref[...], masks) * scale).astype(BF16)
        k_scr[...] = _group_norm64(pk_ref[...].astype(F32), gk_ref[...], masks).astype(BF16)
        kc_scr[...] = ck_ref[...].astype(BF16)
        v = pv_ref[...]
        vc = cv_ref[...].astype(BF16)
        for h, m in enumerate(masks):
            v_scr[h] = jnp.where(lane == den_lanes[h], one, jnp.where(m, v, zero))
            vc_scr[h] = jnp.where(lane == den_lanes[h], one, jnp.where(m, vc, zero))

    win0 = pl.multiple_of(jnp.where(j < 2, 0, DEC_SEQ - NA_WIN_KEYS), 256)
    q = q_scr[pl.ds(pl.multiple_of(j * NA_TQ, NA_TQ), NA_TQ), :]
    kw = k_scr[pl.ds(win0, NA_WIN_KEYS), :]
    kc = kc_scr[...]
    acc = jnp.zeros((NA_TQ, BR_W), F32)
    for h, m in enumerate(masks):
        qm = jnp.where(m, q, zero)
        s_loc = _dot_nt(qm, kw) + bias_ref[h]
        s_ctx = _dot_nt(qm, kc)
        mx = jnp.maximum(jnp.max(s_loc, axis=-1, keepdims=True), jnp.max(s_ctx, axis=-1, keepdims=True))
        e_loc = jnp.exp((s_loc - mx).astype(BF16))
        e_ctx = jnp.exp((s_ctx - mx).astype(BF16))
        o = (jnp.dot(e_loc, v_scr[h, pl.ds(win0, NA_WIN_KEYS), :], preferred_element_type=F32)
             + jnp.dot(e_ctx, vc_scr[h], preferred_element_type=F32))
        acc = acc + jnp.where(m, o / o[:, den_lanes[h]:den_lanes[h] + 1], 0.0)
    o_ref[...] = acc.astype(o_ref.dtype)


def _na_bias_table(rpb):
    n_dr, n_dc = 2 * NA_WIN_R - 1, 2 * NA_WIN_C - 1
    cols = np.arange(GRID_W)
    dc = np.clip(cols[None, :] - cols[:, None], -(NA_WIN_C - 1), NA_WIN_C - 1) + (NA_WIN_C - 1)
    cstart = np.clip(cols - NA_WIN_C // 2, 0, GRID_W - NA_WIN_C)
    col_ok = (cols[None, :] >= cstart[:, None]) & (cols[None, :] < cstart[:, None] + NA_WIN_C)
    place = (dc.reshape(-1)[None, :] == np.arange(n_dc)[:, None]).astype(np.float32)
    blocks = jnp.dot(rpb.reshape(NA_HEADS * n_dr, n_dc), place, precision=lax.Precision.HIGHEST)
    blocks = jnp.where(col_ok.reshape(-1)[None, :], blocks, -jnp.inf).reshape(NA_HEADS, n_dr, GRID_W, GRID_W)
    neg = jnp.full((NA_HEADS, 1, GRID_W, GRID_W), -jnp.inf, F32)
    blocks = jnp.concatenate([blocks, neg], axis=1)

    nt = DEC_SEQ // NA_TQ
    n_rows = DEC_SEQ // GRID_W
    rows_per_tile = NA_TQ // GRID_W
    sel = np.full((nt, rows_per_tile, NA_WIN_ROWS), n_dr, np.int32)
    for j in range(nt):
        win_row0 = 0 if j < nt // 2 else n_rows - NA_WIN_ROWS
        for rq in range(rows_per_tile):
            r = j * rows_per_tile + rq
            start = min(max(r - NA_WIN_R // 2, 0), n_rows - NA_WIN_R)
            for kr in range(NA_WIN_ROWS):
                if start <= win_row0 + kr < start + NA_WIN_R:
                    sel[j, rq, kr] = win_row0 + kr - r + (NA_WIN_R - 1)
    tiles = []
    for j in range(nt):
        rows = [jnp.concatenate([blocks[:, sel[j, rq, kr]] for kr in range(NA_WIN_ROWS)], axis=-1)
                for rq in range(rows_per_tile)]
        tiles.append(jnp.concatenate(rows, axis=1))
    return jnp.stack(tiles, axis=1)


def _na_latent(p, l, cache_k, cache_v, bias, W):
    s = DEC_SEQ
    nt = s // NA_TQ
    row_off = N_CTX_TOK
    return pl.pallas_call(
        _na_lat_kernel,
        out_shape=jax.ShapeDtypeStruct((N_LAT_TOK, BR_W), BF16),
        grid=(DEC_BATCH, nt),
        in_specs=_p_blocks(P_NA, 3, s, lambda b, j: row_off // s + b) + [
                  pl.BlockSpec((None, None, PAST_LEN, BR_W), lambda b, j: (b, l, 0, 0)),
                  pl.BlockSpec((None, None, PAST_LEN, BR_W), lambda b, j: (b, l, 0, 0)),
                  pl.BlockSpec((NA_HEADS, None, NA_TQ, NA_WIN_KEYS), lambda b, j: (0, j, 0, 0)),
                  _const_spec((1, BR_W)), _const_spec((1, BR_W))],
        out_specs=pl.BlockSpec((NA_TQ, BR_W), lambda b, j: (b * nt + j, 0)),
        scratch_shapes=[pltpu.VMEM((s, BR_W), BF16), pltpu.VMEM((s, BR_W), BF16),
                        pltpu.VMEM((NA_HEADS, s, BR_W), BF16),
                        pltpu.VMEM((PAST_LEN, BR_W), BF16), pltpu.VMEM((NA_HEADS, PAST_LEN, BR_W), BF16)],
        compiler_params=_cparams(("arbitrary", "arbitrary")),
        name="na_latent",
    )(p, p, p, cache_k, cache_v, bias, W['g_na_q_t'], W['g_na_k_t'])


CONV_HALO = 16
CONV_CHUNK = 128


SUBLANES = 8


def _dwconv_from_pad(pad_ref, w_ref, ksize, s, emit, shift_ref=None):
    half = ksize // 2
    if shift_ref is not None:
        n_rows = s + 2 * CONV_HALO - SUBLANES
        for p in range(1, SUBLANES):
            for c0 in range(0, n_rows, CONV_CHUNK):
                n = min(CONV_CHUNK, n_rows - c0)
                shift_ref[p - 1, c0:c0 + n, :] = pad_ref[c0 + p:c0 + p + n, :]
    for c0 in range(0, s, CONV_CHUNK):
        acc = jnp.zeros((CONV_CHUNK, BR_W), F32)
        for k in range(ksize):
            r0 = CONV_HALO + c0 + k - half
            p = r0 % SUBLANES
            if shift_ref is None or p == 0:
                win = pad_ref[r0:r0 + CONV_CHUNK, :]
            else:
                win = shift_ref[p - 1, r0 - p:r0 - p + CONV_CHUNK, :]
            acc = acc + win * w_ref[k:k + 1, :]
        emit(c0, acc)


def _conv_kernel(s, scb_ref, scc_ref, scx_ref, cfa_ref, cfb2_ref, scw_ref, cfw_ref, cfb_ref, lng_ref, lnb_ref,
                 ob_ref, od_ref, pad_ref, shift_ref):
    zeros = jnp.zeros((CONV_HALO, BR_W), F32)
    pad_ref[0:CONV_HALO, :] = zeros
    pad_ref[CONV_HALO + s:2 * CONV_HALO + s, :] = zeros

    pad_ref[CONV_HALO:CONV_HALO + s, :] = scc_ref[...].astype(F32) * scx_ref[...].astype(F32)

    def emit_b(c0, acc):
        ob_ref[c0:c0 + CONV_CHUNK, :] = (scb_ref[c0:c0 + CONV_CHUNK, :].astype(F32) * acc).astype(ob_ref.dtype)

    _dwconv_from_pad(pad_ref, scw_ref, SC_K, s, emit_b)

    pad_ref[CONV_HALO:CONV_HALO + s, :] = cfa_ref[...].astype(F32) * _sigmoid(cfb2_ref[...].astype(F32))

    def emit_d(c0, acc):
        u = acc + cfb_ref[...]
        mu = jnp.mean(u, axis=-1, keepdims=True)
        d = u - mu
        var = jnp.mean(d * d, axis=-1, keepdims=True)
        y = d * lax.rsqrt(var + EPS) * lng_ref[...] + lnb_ref[...]
        od_ref[c0:c0 + CONV_CHUNK, :] = _silu(y).astype(od_ref.dtype)

    _dwconv_from_pad(pad_ref, cfw_ref, CF_K, s, emit_d, shift_ref)


def _convs(p, latent, W):
    if latent:
        nb, s, row_off = DEC_BATCH, DEC_SEQ, N_CTX_TOK
    else:
        nb, s, row_off = BATCH, SEQ, 0
    return pl.pallas_call(
        functools.partial(_conv_kernel, s),
        out_shape=[jax.ShapeDtypeStruct((nb * s, BR_W), BF16)] * 2,
        grid=(nb,),
        in_specs=(_p_blocks(P_SC, 3, s, lambda b: row_off // s + b)
                  + _p_blocks(P_CF, 2, s, lambda b: row_off // s + b)
                  + [_const_spec((SC_K, BR_W)), _const_spec((CF_K, BR_W)),
                     _const_spec((1, BR_W)), _const_spec((1, BR_W)), _const_spec((1, BR_W))]),
        out_specs=[pl.BlockSpec((s, BR_W), lambda b: (b, 0))] * 2,
        scratch_shapes=[pltpu.VMEM((s + 2 * CONV_HALO, BR_W), F32),
                        pltpu.VMEM((SUBLANES - 1, s + 2 * CONV_HALO, BR_W), F32)],
        compiler_params=_cparams(("arbitrary",)),
        name="convs_latent" if latent else "convs_context",
    )(p, p, p, p, p, W['sc_w'], W['cf_w'], W['cf_b'], W['cf_ln_g'], W['cf_ln_b'])


def _merge_kernel(nc, nx, *refs):
    x_refs = refs[:nx]
    (mod_ref, oac_ref, oal_ref, obc_ref, obl_ref, occ_ref, ocl_ref, odc_ref, odl_ref,
     g0_ref, g1_ref, g2_ref, g3_ref, wa_ref, wb_ref, wo_ref, y_ref) = refs[nx:]
    is_ctx = pl.program_id(0) < nc
    merged = (_sigmoid(g0_ref[...]).astype(F32)
              * jnp.dot(_ctx_or_lat(is_ctx, oac_ref, oal_ref), wa_ref[...], preferred_element_type=F32))
    branches = ((obc_ref, obl_ref, g1_ref), (occ_ref, ocl_ref, g2_ref), (odc_ref, odl_ref, g3_ref))
    for n, (oc_ref, ol_ref, g_ref) in enumerate(branches):
        merged = merged + (_sigmoid(g_ref[...]).astype(F32)
                           * jnp.dot(_ctx_or_lat(is_ctx, oc_ref, ol_ref), wb_ref[n], preferred_element_type=F32))
    y_ref[...] = _x_load(x_refs) + mod_ref[2:3, :] * _dot(merged, wo_ref[...])


def _merge(x, mod3, p, o_a, o_b, o_c, o_d, W):
    tm = 512
    gate_spec = lambda n: pl.BlockSpec((tm, D_MODEL), lambda i: (i, P_GATE // D_MODEL + n))
    return pl.pallas_call(
        functools.partial(_merge_kernel, N_CTX_TOK // tm, len(x)),
        out_shape=jax.ShapeDtypeStruct((N_TOK, D_MODEL), F32),
        grid=(N_TOK // tm,),
        in_specs=(_x_specs(x, tm)
                  + [pl.BlockSpec((None, 6, D_MODEL), lambda i: (_mod_row(i * tm), 0, 0))]
                  + _pair_specs(tm, MLA_PAD_W) + _pair_specs(tm, BR_W) + _pair_specs(tm, BR_W) + _pair_specs(tm, BR_W)
                  + [gate_spec(0), gate_spec(1), gate_spec(2), gate_spec(3),
                     _const_spec((MLA_PAD_W, D_MODEL)), _const_spec((3, BR_W, D_MODEL)),
                     _const_spec((D_MODEL, D_MODEL))]),
        out_specs=pl.BlockSpec((tm, D_MODEL), lambda i: (i, 0)),
        compiler_params=_cparams(("arbitrary",)),
        name="merge",
    )(*x, mod3, *o_a, *o_b, *o_c, *o_d, p, p, p, p, W['w_br_a'], W['w_br_bcd'], W['w_o'])


def _norm2(x, g, mod_ref):
    y = x * lax.rsqrt(jnp.mean(x * x, axis=-1, keepdims=True) + EPS) * g
    return y * (1.0 + mod_ref[4:5, :]) + mod_ref[3:4, :]


def _ffn_kernel(x_ref, mod_ref, g_ref, wg_ref, wu_ref, wd_ref, y_ref, h_scr, acc_scr):
    f = pl.program_id(1)

    @pl.when(f == 0)
    def _():
        h_scr[...] = _norm2(x_ref[...], g_ref[...], mod_ref).astype(BF16)
        acc_scr[...] = jnp.zeros_like(acc_scr)

    h = h_scr[...]
    a = _silu(_dot(h, wg_ref[...])) * _dot(h, wu_ref[...])
    acc_scr[...] += _dot(a, wd_ref[...])

    @pl.when(f == pl.num_programs(1) - 1)
    def _():
        y_ref[...] = x_ref[...] + mod_ref[5:6, :] * acc_scr[...]


def _ffn(x_all, mod3, g_norm2, w_g, w_u, w_d, l, j):
    tm, tf = 1024, 512
    return pl.pallas_call(
        _ffn_kernel,
        out_shape=jax.ShapeDtypeStruct((N_TOK, D_MODEL), F32),
        grid=(N_TOK // tm, D_FF // tf),
        in_specs=[pl.BlockSpec((tm, D_MODEL), lambda i, f: (i, 0)),
                  pl.BlockSpec((None, 6, D_MODEL), lambda i, f: (_mod_row(i * tm), 0, 0)),
                  pl.BlockSpec((None, 1, D_MODEL), lambda i, f: (l, 0, 0)),
                  pl.BlockSpec((None, D_MODEL, tf), lambda i, f: (j, 0, f)),
                  pl.BlockSpec((None, D_MODEL, tf), lambda i, f: (j, 0, f)),
                  pl.BlockSpec((None, tf, D_MODEL), lambda i, f: (j, f, 0))],
        out_specs=pl.BlockSpec((tm, D_MODEL), lambda i, f: (i, 0)),
        scratch_shapes=[pltpu.VMEM((tm, D_MODEL), BF16), pltpu.VMEM((tm, D_MODEL), F32)],
        compiler_params=_cparams(("arbitrary", "arbitrary")),
        name="ffn_dense",
    )(x_all, mod3, g_norm2.reshape(DEPTH, 1, D_MODEL), w_g, w_u, w_d)


ROUTER_LANES = 128
ROUTE_TM = 512
MOE_TM = 1024
MOE_SUB = 256
MOE_TF = 512
RUN_ALIGN = 8
MOE_TILES = -(-(2 * N_TOK + (N_TOK // ROUTE_TM) * N_EXPERTS * RUN_ALIGN + N_EXPERTS * MOE_TM) // MOE_TM)
MOE_ROWS = MOE_TILES * MOE_TM
ROW_DMA_TM = 256
R_I1, R_I2, R_W1, R_W2, R_RANK1, R_RANK2 = range(6)


def _route_top2(h, wr):
    h_hi = h.astype(BF16)
    h_lo = (h - h_hi.astype(F32)).astype(BF16)
    w_hi = wr.astype(BF16)
    w_lo = (wr - w_hi.astype(F32)).astype(BF16)
    logits = (jnp.dot(h_hi, w_hi, preferred_element_type=F32) + jnp.dot(h_lo, w_hi, preferred_element_type=F32)
              + jnp.dot(h_hi, w_lo, preferred_element_type=F32))
    lane = lax.broadcasted_iota(jnp.int32, logits.shape, 1).astype(F32)
    neg = jnp.float32(-jnp.inf)
    logits = jnp.where(lane < N_EXPERTS, logits, neg)
    m1 = jnp.max(logits, axis=-1, keepdims=True)
    i1 = jnp.min(jnp.where(logits == m1, lane, float(ROUTER_LANES)), axis=-1, keepdims=True)
    rest = jnp.where(lane == i1, neg, logits)
    m2 = jnp.max(rest, axis=-1, keepdims=True)
    i2 = jnp.min(jnp.where(rest == m2, lane, float(ROUTER_LANES)), axis=-1, keepdims=True)
    e2 = jnp.exp(m2 - m1)
    return lane, i1, i2, 1.0 / (1.0 + e2), e2 / (1.0 + e2)


def _route_kernel(x_ref, mod_ref, g_ref, wr_ref, route_ref, cnt_ref, tcarry_ref, carry_scr):
    @pl.when(pl.program_id(0) == 0)
    def _():
        carry_scr[...] = jnp.zeros_like(carry_scr)

    tcarry_ref[...] = carry_scr[...]

    h = _norm2(x_ref[...], g_ref[...], mod_ref)
    lane, i1, i2, w1, w2 = _route_top2(h, wr_ref[...])
    tm = h.shape[0]
    oh1 = lane == i1
    oh2 = lane == i2
    oh = jnp.where(oh1, 1.0, 0.0) + jnp.where(oh2, 1.0, 0.0)
    r = lax.broadcasted_iota(jnp.int32, (tm, tm), 0)
    c = lax.broadcasted_iota(jnp.int32, (tm, tm), 1)
    lower = jnp.where(r > c, 1.0, 0.0).astype(BF16)
    before = jnp.dot(lower, oh.astype(BF16), preferred_element_type=F32)
    rank1 = jnp.sum(jnp.where(oh1, before, 0.0), axis=-1, keepdims=True)
    rank2 = jnp.sum(jnp.where(oh2, before, 0.0), axis=-1, keepdims=True)
    carry_scr[...] += jnp.sum(oh, axis=0, keepdims=True)
    rec = jnp.zeros_like(lane)
    for k, v in ((R_I1, i1), (R_I2, i2), (R_W1, w1), (R_W2, w2), (R_RANK1, rank1), (R_RANK2, rank2)):
        rec = jnp.where(lane == float(k), v, rec)
    route_ref[...] = rec
    cnt_ref[...] = carry_scr[...]


def _moe_route(x_all, mod3, g_norm2, w_router_p, l, j):
    tm = ROUTE_TM
    return pl.pallas_call(
        _route_kernel,
        out_shape=[jax.ShapeDtypeStruct((N_TOK, ROUTER_LANES), F32),
                   jax.ShapeDtypeStruct((1, ROUTER_LANES), F32),
                   jax.ShapeDtypeStruct((N_TOK // tm, 1, ROUTER_LANES), F32)],
        grid=(N_TOK // tm,),
        in_specs=[pl.BlockSpec((tm, D_MODEL), lambda i: (i, 0)),
                  pl.BlockSpec((None, 6, D_MODEL), lambda i: (_mod_row(i * tm), 0, 0)),
                  pl.BlockSpec((None, 1, D_MODEL), lambda i: (l, 0, 0)),
                  pl.BlockSpec((None, D_MODEL, ROUTER_LANES), lambda i: (j, 0, 0))],
        out_specs=[pl.BlockSpec((tm, ROUTER_LANES), lambda i: (i, 0)),
                   pl.BlockSpec((1, ROUTER_LANES), lambda i: (0, 0)),
                   pl.BlockSpec((None, 1, ROUTER_LANES), lambda i: (i, 0, 0))],
        scratch_shapes=[pltpu.VMEM((1, ROUTER_LANES), F32)],
        compiler_params=_cparams(("arbitrary",)),
        name="moe_route",
    )(x_all, mod3, g_norm2.reshape(DEPTH, 1, D_MODEL), w_router_p)


def _row_copies(n, pos_refs, base, make):
    def body(r, carry):
        for pos_ref in pos_refs:
            make(r, pos_ref[base + r]).start()
        return carry
    lax.fori_loop(0, n, body, 0, unroll=8)


ZERO_ROWS = 512
N_GAPS = N_EXPERTS + 1


def _bit_chunks(n_units, max_units, make, wait=False):
    for b in range(max_units.bit_length()):
        units = 1 << b

        @pl.when((n_units & units) != 0)
        def _():
            cp = make(pl.multiple_of((n_units & (units - 1)) * RUN_ALIGN, RUN_ALIGN), units * RUN_ALIGN)
            if wait:
                cp.wait()
            else:
                cp.start()


def _dispatch_kernel(n_ref, lo_ref, dst_ref, gap0_ref, gapn_ref, x_ref, mod_ref, g_ref, route_ref, lovec_ref,
                     xs_ref, z_scr, zero_scr, sems, zsem):
    i = pl.program_id(0)
    tm = x_ref.shape[0]
    h = _norm2(x_ref[...], g_ref[...], mod_ref).astype(BF16)
    rec = route_ref[...]
    lane = lax.broadcasted_iota(jnp.int32, rec.shape, 1).astype(F32)
    lo_row = lovec_ref[...]

    def local_pos(i_lane, r_lane):
        lp = (jnp.sum(jnp.where(lane == rec[:, i_lane:i_lane + 1], lo_row, 0.0), axis=-1, keepdims=True)
              + rec[:, r_lane:r_lane + 1])
        return jnp.transpose(jnp.broadcast_to(lp, (tm, ROUTER_LANES)))[0:1, :]

    slot = lax.broadcasted_iota(jnp.int32, (z_scr.shape[1], tm), 0).astype(F32)
    perm = jnp.where((slot == local_pos(R_I1, R_RANK1)) | (slot == local_pos(R_I2, R_RANK2)), 1.0, 0.0)
    buf = i % 2
    z_scr[buf] = jnp.dot(perm.astype(BF16), h, preferred_element_type=F32)

    def run_copies(tile, b, wait):
        for e in range(N_EXPERTS):
            k = tile * N_EXPERTS + e
            src0 = lo_ref[k]
            dst0 = dst_ref[k]

            def make(off, rows, src0=src0, dst0=dst0):
                return pltpu.make_async_copy(
                    z_scr.at[b, pl.ds(pl.multiple_of(src0 + off, RUN_ALIGN), rows), :],
                    xs_ref.at[pl.ds(pl.multiple_of(dst0 + off, RUN_ALIGN), rows), :], sems.at[b])

            _bit_chunks(n_ref[k], tm // RUN_ALIGN, make, wait)

    run_copies(i, buf, False)

    @pl.when(i > 0)
    def _():
        run_copies(i - 1, 1 - buf, True)

    @pl.when(i == pl.num_programs(0) - 1)
    def _():
        run_copies(i, buf, True)

    @pl.when(i == pl.num_programs(0) - 1)
    def _():
        zero_scr[...] = jnp.zeros_like(zero_scr)
        for wait in (False, True):
            for g in range(N_GAPS):
                start = gap0_ref[g]
                n = gapn_ref[g]
                max_rows = MOE_TM if g < N_EXPERTS else MOE_ROWS - 2 * N_TOK
                for c in range(max_rows // ZERO_ROWS):
                    @pl.when(n >= (c + 1) * ZERO_ROWS)
                    def _():
                        cp = pltpu.make_async_copy(
                            zero_scr,
                            xs_ref.at[pl.ds(pl.multiple_of(start + c * ZERO_ROWS, RUN_ALIGN), ZERO_ROWS), :], zsem)
                        if wait:
                            cp.wait()
                        else:
                            cp.start()

                def make(off, rows, start=start, n=n):
                    tail0 = start + (n // ZERO_ROWS) * ZERO_ROWS
                    return pltpu.make_async_copy(
                        zero_scr.at[pl.ds(0, rows), :],
                        xs_ref.at[pl.ds(pl.multiple_of(tail0 + off, RUN_ALIGN), rows), :], zsem)

                _bit_chunks((n % ZERO_ROWS) // RUN_ALIGN, ZERO_ROWS // RUN_ALIGN - 1, make, wait)


def _moe_dispatch(n_tile, lo_tile, dst_tile, gap0, gapn, x_all, mod3, g_norm2, route, lo_vec, l):
    tm = ROUTE_TM
    spec = lambda shape, fn: pl.BlockSpec(shape, lambda i, *_: fn(i))
    return pl.pallas_call(
        _dispatch_kernel,
        out_shape=jax.ShapeDtypeStruct((MOE_ROWS, D_MODEL), F32),
        grid_spec=pltpu.PrefetchScalarGridSpec(
            num_scalar_prefetch=5,
            grid=(N_TOK // tm,),
            in_specs=[spec((tm, D_MODEL), lambda i: (i, 0)),
                      spec((None, 6, D_MODEL), lambda i: (_mod_row(i * tm), 0, 0)),
                      spec((None, 1, D_MODEL), lambda i: (l, 0, 0)),
                      spec((tm, ROUTER_LANES), lambda i: (i, 0)),
                      spec((None, 1, ROUTER_LANES), lambda i: (i, 0, 0))],
            out_specs=pl.BlockSpec(memory_space=pl.ANY),
            scratch_shapes=[pltpu.VMEM((2, 2 * tm + N_EXPERTS * RUN_ALIGN, D_MODEL), F32),
                            pltpu.VMEM((ZERO_ROWS, D_MODEL), F32),
                            pltpu.SemaphoreType.DMA((2,)), pltpu.SemaphoreType.DMA],
        ),
        compiler_params=_cparams(("arbitrary",)),
        name="moe_dispatch",
    )(n_tile, lo_tile, dst_tile, gap0, gapn, x_all, mod3, g_norm2.reshape(DEPTH, 1, D_MODEL), route, lo_vec)


def _gmm_kernel(te_ref, tr_ref, xs_ref, wg_ref, wu_ref, wd_ref, y_ref, wg_scr, wu_scr, wd_scr):
    del te_ref
    g = pl.program_id(0)
    f = pl.program_id(1)
    rows = tr_ref[g]

    @pl.when(f == 0)
    def _():
        y_ref[...] = jnp.zeros_like(y_ref)

    def sub_tile(s, wg, wu, wd):
        sl = slice(s * MOE_SUB, (s + 1) * MOE_SUB)
        x = xs_ref[sl, :].astype(BF16)
        a = (_silu(jnp.dot(x, wg, preferred_element_type=F32)) * jnp.dot(x, wu, preferred_element_type=F32))
        y_ref[sl, :] += jnp.dot(a.astype(BF16), wd, preferred_element_type=F32)

    @pl.when(rows == MOE_TM)
    def _():
        x = xs_ref[...].astype(BF16)
        a = _silu(_dot(x, wg_ref[...])) * _dot(x, wu_ref[...])
        y_ref[...] += _dot(a, wd_ref[...])

    @pl.when((rows > 0) & (rows < MOE_TM))
    def _():
        wg = wg_ref[...].astype(BF16)
        wu = wu_ref[...].astype(BF16)
        wd = wd_ref[...].astype(BF16)
        wg_scr[...] = wg
        wu_scr[...] = wu
        wd_scr[...] = wd
        sub_tile(0, wg, wu, wd)

    for s in range(1, MOE_TM // MOE_SUB):
        @pl.when((rows > s * MOE_SUB) & (rows < MOE_TM))
        def _():
            sub_tile(s, wg_scr[...], wu_scr[...], wd_scr[...])


def _moe_gmm(tile_expert, tile_rows, xs, w_g, w_u, w_d, j):
    tm, tf = MOE_TM, MOE_TF
    nf = D_FF // tf

    def f_eff(g, f, tr):
        return jnp.where(tr[g] > 0, f, nf - 1)

    return pl.pallas_call(
        _gmm_kernel,
        out_shape=jax.ShapeDtypeStruct((MOE_ROWS, D_MODEL), F32),
        grid_spec=pltpu.PrefetchScalarGridSpec(
            num_scalar_prefetch=2,
            grid=(MOE_TILES, nf),
            in_specs=[pl.BlockSpec((tm, D_MODEL), lambda g, f, te, tr: (g, 0)),
                      pl.BlockSpec((None, None, D_MODEL, tf), lambda g, f, te, tr: (j, te[g], 0, f_eff(g, f, tr))),
                      pl.BlockSpec((None, None, D_MODEL, tf), lambda g, f, te, tr: (j, te[g], 0, f_eff(g, f, tr))),
                      pl.BlockSpec((None, None, tf, D_MODEL), lambda g, f, te, tr: (j, te[g], f_eff(g, f, tr), 0))],
            out_specs=pl.BlockSpec((tm, D_MODEL), lambda g, f, te, tr: (g, 0)),
            scratch_shapes=[pltpu.VMEM((D_MODEL, tf), BF16), pltpu.VMEM((D_MODEL, tf), BF16),
                            pltpu.VMEM((tf, D_MODEL), BF16)],
        ),
        compiler_params=_cparams(("arbitrary", "arbitrary")),
        name="moe_experts",
    )(tile_expert, tile_rows, xs, w_g, w_u, w_d)


def _combine_kernel(n_out, pos1_ref, pos2_ref, x_ref, mod_ref, route_ref, y_ref, *refs):
    o_refs, (g1_scr, g2_scr, sems) = refs[:n_out], refs[n_out:]
    tm = g1_scr.shape[1]
    i = pl.program_id(0)
    n = pl.num_programs(0)

    def start_gathers(tile, slot):
        def make1(r, pos):
            return pltpu.make_async_copy(y_ref.at[pl.ds(pos, 1), :], g1_scr.at[slot, pl.ds(r, 1), :], sems.at[slot])

        def make2(r, pos):
            return pltpu.make_async_copy(y_ref.at[pl.ds(pos, 1), :], g2_scr.at[slot, pl.ds(r, 1), :], sems.at[slot])

        _row_copies(tm, (pos1_ref,), tile * tm, make1)
        _row_copies(tm, (pos2_ref,), tile * tm, make2)

    @pl.when(i == 0)
    def _():
        start_gathers(0, 0)

    slot = i % 2

    @pl.when(i + 1 < n)
    def _():
        start_gathers(i + 1, 1 - slot)

    pltpu.make_async_copy(y_ref.at[pl.ds(0, tm), :], g1_scr.at[slot], sems.at[slot]).wait()
    pltpu.make_async_copy(y_ref.at[pl.ds(0, tm), :], g2_scr.at[slot], sems.at[slot]).wait()
    rec = route_ref[...]
    w1 = rec[:, R_W1:R_W1 + 1]
    w2 = rec[:, R_W2:R_W2 + 1]
    _x_store(o_refs, x_ref[...] + mod_ref[5:6, :] * (w1 * g1_scr[slot] + w2 * g2_scr[slot]))


def _moe_combine(pos1, pos2, x_all, mod3, route, y, split_out):
    tm = ROW_DMA_TM
    return pl.pallas_call(
        functools.partial(_combine_kernel, 2 if split_out else 1),
        out_shape=_x_shapes(split_out),
        grid_spec=pltpu.PrefetchScalarGridSpec(
            num_scalar_prefetch=2,
            grid=(N_TOK // tm,),
            in_specs=[pl.BlockSpec((tm, D_MODEL), lambda i, p1, p2: (i, 0)),
                      pl.BlockSpec((None, 6, D_MODEL), lambda i, p1, p2: (_mod_row(i * tm), 0, 0)),
                      pl.BlockSpec((tm, ROUTER_LANES), lambda i, p1, p2: (i, 0)),
                      pl.BlockSpec(memory_space=pl.ANY)],
            out_specs=_x_specs((None,) * (2 if split_out else 1), tm),
            scratch_shapes=[pltpu.VMEM((2, tm, D_MODEL), F32), pltpu.VMEM((2, tm, D_MODEL), F32),
                            pltpu.SemaphoreType.DMA((2,))],
        ),
        compiler_params=_cparams(("arbitrary",)),
        name="moe_combine",
    )(pos1, pos2, x_all, mod3, route, y)


def _moe(x_all, mod3, g_norm2, w_router_p, w_g, w_u, w_d, l, j, split_out):
    route, cnt, tcarry = _moe_route(x_all, mod3, g_norm2, w_router_p, l, j)
    cnt = cnt[0, :N_EXPERTS].astype(jnp.int32)
    carry = tcarry[:, 0, :N_EXPERTS].astype(jnp.int32)
    n_tile = jnp.concatenate([carry[1:], cnt[None, :]], axis=0) - carry
    n_tile = (n_tile + RUN_ALIGN - 1) // RUN_ALIGN * RUN_ALIGN
    lo_tile = jnp.cumsum(n_tile, axis=1) - n_tile
    carry = jnp.cumsum(n_tile, axis=0) - n_tile
    cnt = jnp.sum(n_tile, axis=0)
    padded = (cnt + MOE_TM - 1) // MOE_TM * MOE_TM
    ends = jnp.cumsum(padded)
    offs = ends - padded
    dst_tile = offs[None, :] + carry
    experts = jnp.arange(N_EXPERTS, dtype=jnp.int32)
    gap0 = jnp.concatenate([offs + cnt, ends[-1:]])
    gapn = jnp.concatenate([padded - cnt, MOE_ROWS - ends[-1:]])
    lo_vec = jnp.pad(lo_tile.astype(F32), ((0, 0), (0, ROUTER_LANES - N_EXPERTS)))[:, None, :]

    def positions(i_lane, rank_lane):
        e = route[:, i_lane].astype(jnp.int32)
        base = jnp.repeat(dst_tile, ROUTE_TM, axis=0)
        off = jnp.sum(jnp.where(e[:, None] == experts[None, :], base, 0), axis=1)
        return off + route[:, rank_lane].astype(jnp.int32)

    pos1 = positions(R_I1, R_RANK1)
    pos2 = positions(R_I2, R_RANK2)
    tile_start = jnp.arange(MOE_TILES, dtype=jnp.int32) * MOE_TM
    last_tile = jnp.maximum(ends[-1] - MOE_TM, 0)
    owner_start = jnp.minimum(tile_start, last_tile)
    tile_expert = jnp.minimum(jnp.sum(owner_start[:, None] >= ends[None, :], axis=1), N_EXPERTS - 1).astype(jnp.int32)
    group_end = jnp.sum(jnp.where(tile_expert[:, None] == experts[None, :], (offs + cnt)[None, :], 0), axis=1)
    tile_rows = jnp.where(tile_start < ends[-1], jnp.clip(group_end - tile_start, 0, MOE_TM), 0).astype(jnp.int32)

    i32 = lambda a: a.reshape(-1).astype(jnp.int32)
    xs = _moe_dispatch(i32(n_tile // RUN_ALIGN), i32(lo_tile), i32(dst_tile), i32(gap0), i32(gapn),
                       x_all, mod3, g_norm2, route, lo_vec, l)
    y = _moe_gmm(tile_expert, tile_rows, xs, w_g, w_u, w_d, j)
    return tuple(_moe_combine(pos1, pos2, x_all, mod3, route, y, split_out))


def _pad_heads(w, per_head, lo):
    lead = w.shape[:-1]
    w = w.reshape(lead + (MLA_HEADS, per_head))
    w = jnp.pad(w, [(0, 0)] * len(lead) + [(0, 0), (lo, HEAD_PAD - lo - per_head)])
    return w.reshape(lead + (MLA_PAD_W,))


def _layer_weights(l, w_in, g_qa, w_uq, g_kva, w_ukv, g_mla_q, g_mla_k, sc_w, g_na_q, g_na_k,
                   cf_w, cf_b, cf_ln_g, cf_ln_b, w_br, w_o):
    s3 = MLA_Q_LORA + MLA_KV_LORA + MLA_ROPE
    s6 = s3 + 8 * BR_W
    w_in_p = jnp.concatenate([
        w_in[l, :, s6:], w_in[l, :, :s3], jnp.zeros((D_MODEL, 2 * BR_W - s3), F32),
        w_in[l, :, s3:s6]], axis=1).astype(BF16)
    ukv = w_ukv[l].reshape(MLA_KV_LORA, MLA_HEADS, MLA_NOPE + MLA_V)
    W = dict(
        w_in_p=w_in_p,
        g_qa=g_qa[l][None], g_kva=g_kva[l][None],
        w_uq_p=_pad_heads(w_uq[l], MLA_QK, 0).astype(BF16),
        w_ukv_k=_pad_heads(ukv[:, :, :MLA_NOPE].reshape(MLA_KV_LORA, -1), MLA_NOPE, 0).astype(BF16),
        w_ukv_v=_pad_heads(ukv[:, :, MLA_NOPE:].reshape(MLA_KV_LORA, -1), MLA_V, 0).astype(BF16),
        g_mla_q_p=jnp.pad(g_mla_q[l], (0, HEAD_PAD - MLA_QK))[None],
        g_mla_k_p=jnp.pad(g_mla_k[l], (0, HEAD_PAD - MLA_QK))[None],
        g_na_q_t=jnp.tile(g_na_q[l], NA_HEADS)[None], g_na_k_t=jnp.tile(g_na_k[l], NA_HEADS)[None],
        sc_w=sc_w[l], cf_w=cf_w[l], cf_b=cf_b[l][None], cf_ln_g=cf_ln_g[l][None], cf_ln_b=cf_ln_b[l][None],
        w_br_a=jnp.pad(w_br[l, 0].reshape(MLA_HEADS, MLA_V, D_MODEL),
                       ((0, 0), (0, HEAD_PAD - MLA_V), (0, 0))).reshape(MLA_PAD_W, D_MODEL).astype(BF16),
        w_br_bcd=w_br[l, 1:].astype(BF16),
        w_o=w_o[l].astype(BF16),
    )
    return W


def kernel(x_prompt, x_sample, cache_mla_ckv, cache_mla_kpe, cache_na_k, cache_na_v, c, c_ctx, w_ada, b_ada, g_norm1, w_in, g_qa, w_uq, g_kva, w_ukv, g_mla_q, g_mla_k, sc_w, g_na_q, g_na_k, na_rpb, cf_w, cf_b, cf_ln_g, cf_ln_b, w_br, w_o, g_norm2, w_ff_gate, w_ff_up, w_ff_down, w_router, w_e_gate, w_e_up, w_e_down):
    x = (x_prompt.reshape(N_CTX_TOK, D_MODEL), x_sample.reshape(N_LAT_TOK, D_MODEL))
    cvec = jnp.concatenate([c_ctx[None, :], c, jnp.zeros((MOD_ROWS - 1 - DEC_BATCH, D_MODEL), F32)], axis=0)
    cache_na_k2 = cache_na_k.reshape(DEC_BATCH, DEPTH, PAST_LEN, BR_W)
    cache_na_v2 = cache_na_v.reshape(DEC_BATCH, DEPTH, PAST_LEN, BR_W)
    w_router_p = jnp.pad(w_router, ((0, 0), (0, 0), (0, ROUTER_LANES - N_EXPERTS)))

    ckv_l, kpe_l, nak_l, nav_l = [], [], [], []
    for l in range(DEPTH):
        W = _layer_weights(l, w_in, g_qa, w_uq, g_kva, w_ukv, g_mla_q, g_mla_k, sc_w, g_na_q, g_na_k,
                           cf_w, cf_b, cf_ln_g, cf_ln_b, w_br, w_o)
        mod3 = _modulation(cvec, w_ada, b_ada, l).reshape(MOD_ROWS, 6, D_MODEL)
        p = _in_projection(x, mod3, g_norm1, W['w_in_p'], l)

        oa_c, ckv_new, kpe_new = _mla(p, l, False, None, None, W)
        (oa_l,) = _mla(p, l, True, cache_mla_ckv, cache_mla_kpe, W)
        oc_c, nak_new, nav_new = _na_context(p, W)
        oc_l = _na_latent(p, l, cache_na_k2, cache_na_v2, _na_bias_table(na_rpb[l]), W)
        ob_c, od_c = _convs(p, False, W)
        ob_l, od_l = _convs(p, True, W)

        x_all = _merge(x, mod3, p, (oa_c, oa_l), (ob_c, ob_l), (oc_c, oc_l), (od_c, od_l), W)

        last = l == DEPTH - 1
        if l % 2 == 0:
            x = (_ffn(x_all, mod3, g_norm2, w_ff_gate, w_ff_up, w_ff_down, l, l // 2),)
        else:
            x = _moe(x_all, mod3, g_norm2, w_router_p, w_e_gate, w_e_up, w_e_down, l, l // 2, last)

        ckv_l.append(ckv_new)
        kpe_l.append(kpe_new)
        nak_l.append(nak_new.reshape(BATCH, SEQ, NA_HEADS, NA_HD))
        nav_l.append(nav_new.reshape(BATCH, SEQ, NA_HEADS, NA_HD))

    if len(x) == 1:
        x = (x[0][:N_CTX_TOK], x[0][N_CTX_TOK:])
    y_prompt = x[0].reshape(BATCH, SEQ, D_MODEL)
    y_sample = x[1].reshape(DEC_BATCH, DEC_SEQ, D_MODEL)
    return (y_prompt, y_sample, jnp.stack(ckv_l, axis=1), jnp.stack(kpe_l, axis=1),
            jnp.stack(nak_l, axis=1), jnp.stack(nav_l, axis=1))
```

```python
import functools

import numpy as np
import jax
import jax.numpy as jnp
from jax import lax
from jax.experimental import pallas as pl
from jax.experimental.pallas import tpu as pltpu

F32 = jnp.float32
BF16 = jnp.bfloat16

D_MODEL = 1024
BATCH = 16
SEQ = 256
DEPTH = 2
DEC_BATCH = 8
DEC_SEQ = 1024
PAST_LEN = 512
GRID_W = 64
N_BRANCH = 4
BR_W = 256
MLA_HEADS = 4
MLA_NOPE = 64
MLA_ROPE = 32
MLA_QK = 96
MLA_V = 64
MLA_Q_LORA = 256
MLA_KV_LORA = 128
SC_K = 3
NA_HEADS = 4
NA_HD = 64
NA_WIN_R = 8
NA_WIN_C = 16
CF_K = 31
D_FF = 3584
N_EXPERTS = 8
ROPE_THETA = 10000.0
EPS = 1e-6

N_CTX_TOK = BATCH * SEQ
N_LAT_TOK = DEC_BATCH * DEC_SEQ
N_TOK = N_CTX_TOK + N_LAT_TOK
MOD_ROWS = 16

HEAD_PAD = 128
MLA_PAD_W = MLA_HEADS * HEAD_PAD

P_QA = 0
P_KVA = 256
P_SC = 512
P_NA = 1280
P_CF = 2048
P_GATE = 2560
GATE_BLK = 512
P_COLS = 6656

VMEM_LIMIT = 56 * 1024 * 1024

NA_TQ = 256
NA_WIN_ROWS = 12
NA_WIN_KEYS = NA_WIN_ROWS * GRID_W


def _cparams(sem):
    return pltpu.CompilerParams(dimension_semantics=sem, vmem_limit_bytes=VMEM_LIMIT)


def _const_spec(shape):
    nd = len(shape)
    return pl.BlockSpec(shape, lambda *_: (0,) * nd)


def _mod_row(tok_start):
    return jnp.where(tok_start < N_CTX_TOK, 0, 1 + (tok_start - N_CTX_TOK) // DEC_SEQ)


def _sigmoid(x):
    return 0.5 * jnp.tanh(0.5 * x) + 0.5


def _silu(x):
    return x * _sigmoid(x)


def _ctx_or_lat(is_ctx, c_ref, l_ref):
    return jnp.where(is_ctx, c_ref[...], l_ref[...])


def _dot(a, b):
    return jnp.dot(a.astype(BF16), b.astype(BF16), preferred_element_type=F32)


def _dot_nt(a, b):
    return lax.dot_general(a.astype(BF16), b.astype(BF16), (((1,), (1,)), ((), ())),
                           preferred_element_type=F32)


def _mod_kernel(c_ref, w_ref, b_ref, o_ref):
    o_ref[...] = _dot(_silu(c_ref[...]), w_ref[...]) + b_ref[...]


def _modulation(cvec, w_ada, b_ada, l):
    tn = 1024
    return pl.pallas_call(
        _mod_kernel,
        out_shape=jax.ShapeDtypeStruct((MOD_ROWS, 6 * D_MODEL), F32),
        grid=(6 * D_MODEL // tn,),
        in_specs=[
            _const_spec((MOD_ROWS, D_MODEL)),
            pl.BlockSpec((None, D_MODEL, tn), lambda j: (l, 0, j)),
            pl.BlockSpec((None, 1, tn), lambda j: (l, 0, j)),
        ],
        out_specs=pl.BlockSpec((MOD_ROWS, tn), lambda j: (0, j)),
        compiler_params=_cparams(("arbitrary",)),
        name="modulation",
    )(cvec, w_ada, b_ada.reshape(DEPTH, 1, 6 * D_MODEL))


def _pair_specs(tm, width, buffers=None):
    nc = N_CTX_TOK // tm
    mode = {} if buffers is None else dict(pipeline_mode=pl.Buffered(buffers))
    return [pl.BlockSpec((tm, width), lambda i, *_: (jnp.minimum(i, nc - 1), 0), **mode),
            pl.BlockSpec((tm, width), lambda i, *_: (jnp.maximum(i - nc, 0), 0), **mode)]


def _pair_shapes(width, dtype):
    return [jax.ShapeDtypeStruct((N_CTX_TOK, width), dtype), jax.ShapeDtypeStruct((N_LAT_TOK, width), dtype)]


def _store_pair(is_ctx, c_ref, l_ref, val):
    @pl.when(is_ctx)
    def _():
        c_ref[...] = val

    @pl.when(jnp.logical_not(is_ctx))
    def _():
        l_ref[...] = val


def _x_specs(x, tm, buffers=None):
    if len(x) == 2:
        return _pair_specs(tm, D_MODEL, buffers)
    mode = {} if buffers is None else dict(pipeline_mode=pl.Buffered(buffers))
    return [pl.BlockSpec((tm, D_MODEL), lambda i, *_: (i, 0), **mode)]


def _x_shapes(split):
    return _pair_shapes(D_MODEL, F32) if split else [jax.ShapeDtypeStruct((N_TOK, D_MODEL), F32)]


def _x_load(x_refs):
    if len(x_refs) == 1:
        return x_refs[0][...]
    tm = x_refs[0].shape[0]
    return _ctx_or_lat(pl.program_id(0) < N_CTX_TOK // tm, *x_refs)


def _x_store(y_refs, val):
    if len(y_refs) == 1:
        y_refs[0][...] = val
    else:
        tm = y_refs[0].shape[0]
        _store_pair(pl.program_id(0) < N_CTX_TOK // tm, *y_refs, val)


def _inproj_kernel(nx, *refs):
    x_refs, (mod_ref, g_ref, w_ref, o_ref, h_scr) = refs[:nx], refs[nx:]

    @pl.when(pl.program_id(1) == 0)
    def _():
        x = _x_load(x_refs)
        y = x * lax.rsqrt(jnp.mean(x * x, axis=-1, keepdims=True) + EPS) * g_ref[...]
        h_scr[...] = (y * (1.0 + mod_ref[1:2, :]) + mod_ref[0:1, :]).astype(BF16)

    o_ref[...] = jnp.dot(h_scr[...], w_ref[...], preferred_element_type=F32).astype(o_ref.dtype)


def _in_projection(x, mod3, g_norm1, w_in_p, l):
    tm, tn = 512, P_COLS // 2
    return pl.pallas_call(
        functools.partial(_inproj_kernel, len(x)),
        out_shape=jax.ShapeDtypeStruct((N_TOK, P_COLS), BF16),
        grid=(N_TOK // tm, P_COLS // tn),
        in_specs=_x_specs(x, tm) + [
            pl.BlockSpec((None, 6, D_MODEL), lambda i, j: (_mod_row(i * tm), 0, 0)),
            pl.BlockSpec((None, 1, D_MODEL), lambda i, j: (l, 0, 0)),
            pl.BlockSpec((D_MODEL, tn), lambda i, j: (0, j)),
        ],
        out_specs=pl.BlockSpec((tm, tn), lambda i, j: (i, j)),
        scratch_shapes=[pltpu.VMEM((tm, D_MODEL), BF16)],
        compiler_params=_cparams(("arbitrary", "arbitrary")),
        name="in_projection",
    )(*x, mod3, g_norm1.reshape(DEPTH, 1, D_MODEL), w_in_p)


def _rope_tables():
    t = np.arange(DEC_SEQ)
    nf = MLA_ROPE // 4
    inv = (np.float32(ROPE_THETA) ** (-np.arange(nf, dtype=np.float32) / np.float32(nf))).astype(np.float32)
    ang_r = (t // GRID_W).astype(np.float32)[:, None] * inv[None, :]
    ang_c = (t % GRID_W).astype(np.float32)[:, None] * inv[None, :]
    c = np.zeros((DEC_SEQ, HEAD_PAD), np.float32)
    s1 = np.zeros((DEC_SEQ, HEAD_PAD), np.float32)
    s2 = np.zeros((DEC_SEQ, HEAD_PAD), np.float32)
    c[:, :MLA_NOPE] = 1.0
    for base, ang in ((MLA_NOPE, ang_r), (MLA_NOPE + 2 * nf, ang_c)):
        c[:, base:base + nf] = np.cos(ang)
        c[:, base + nf:base + 2 * nf] = np.cos(ang)
        s1[:, base:base + nf] = -np.sin(ang)
        s2[:, base + nf:base + 2 * nf] = np.sin(ang)
    return c, s1, s2


def _rope(x, c, s1, s2):
    nf = MLA_ROPE // 4
    return x * c + pltpu.roll(x, HEAD_PAD - nf, 1) * s1 + pltpu.roll(x, nf, 1) * s2


def _head_norm(xh, g):
    ms = jnp.sum(xh * xh, axis=-1, keepdims=True) * (1.0 / MLA_QK)
    return xh * lax.rsqrt(ms + EPS) * g


def _mla_kernel(latent, tq, *refs):
    if latent:
        (pqa_ref, pkva_ref, cckv_ref, ckpe_ref, rc_ref, rs1_ref, rs2_ref,
         gqa_ref, wuq_ref, gkva_ref, wk_ref, wv_ref, gq_ref, gk_ref,
         o_ref, k_scr, v_scr) = refs
    else:
        (pqa_ref, pkva_ref,
         gqa_ref, wuq_ref, gkva_ref, wk_ref, wv_ref, gq_ref, gk_ref,
         o_ref, ckv_ref, kpe_ref, k_scr, v_scr) = refs
    qi = pl.program_id(1)
    n_past = PAST_LEN if latent else 0

    vlane = lax.broadcasted_iota(jnp.int32, (1, MLA_PAD_W), 1)
    v_ones = jnp.where(vlane % HEAD_PAD == MLA_V, 1.0, 0.0)

    def put_kv(ckvn, kpe_g, row0, rope):
        n = ckvn.shape[0]
        kk = _dot(ckvn, wk_ref[...])
        v_scr[row0:row0 + n, :] = (_dot(ckvn, wv_ref[...]) + v_ones).astype(BF16)
        for h in range(MLA_HEADS):
            sl = slice(h * HEAD_PAD, (h + 1) * HEAD_PAD)
            kh = _head_norm(kk[:, sl] + kpe_g, gk_ref[...])
            if rope:
                kh = _rope(kh, rc_ref[...], rs1_ref[...], rs2_ref[...])
            k_scr[row0:row0 + n, sl] = kh.astype(BF16)

    @pl.when(qi == 0)
    def _():
        if latent:
            put_kv(cckv_ref[...], ckpe_ref[...], 0, False)
        kva = pkva_ref[...].astype(F32)
        ckv = kva[:, :MLA_KV_LORA]
        ckvn = ckv * lax.rsqrt(jnp.mean(ckv * ckv, axis=-1, keepdims=True) + EPS) * gkva_ref[...]
        if not latent:
            ckv_ref[...] = ckvn
            kpe_ref[...] = kva[:, MLA_KV_LORA:MLA_KV_LORA + MLA_ROPE]
        put_kv(ckvn, pltpu.roll(kva[:, MLA_KV_LORA:], MLA_NOPE, 1), n_past, latent)

    qa = pqa_ref[...].astype(F32)
    qan = qa * lax.rsqrt(jnp.mean(qa * qa, axis=-1, keepdims=True) + EPS) * gqa_ref[...]
    q = _dot(qan, wuq_ref[...])
    scale = MLA_QK ** -0.5
    for h in range(MLA_HEADS):
        sl = slice(h * HEAD_PAD, (h + 1) * HEAD_PAD)
        qh = _head_norm(q[:, sl], gq_ref[...])
        if latent:
            rows = pl.ds(pl.multiple_of(qi * tq, tq), tq)
            qh = _rope(qh, rc_ref[rows, :], rs1_ref[rows, :], rs2_ref[rows, :])
        s = _dot_nt(qh * scale, k_scr[:, sl])
        m = jnp.max(s, axis=-1, keepdims=True)
        e = jnp.exp((s - m).astype(BF16))
        o = jnp.dot(e, v_scr[:, sl], preferred_element_type=F32)
        o_ref[:, sl] = (o / o[:, MLA_V:MLA_V + 1]).astype(o_ref.dtype)


def _mla(p, l, latent, cache_ckv, cache_kpe_g, W):
    if latent:
        nb, s, tq, row_off = DEC_BATCH, DEC_SEQ, 256, N_CTX_TOK
    else:
        nb, s, tq, row_off = BATCH, SEQ, 256, 0
    nq = s // tq
    sk = s + (PAST_LEN if latent else 0)
    in_specs = [
        pl.BlockSpec((tq, MLA_Q_LORA), lambda b, qi: (row_off // tq + b * nq + qi, P_QA // MLA_Q_LORA)),
        pl.BlockSpec((s, 256), lambda b, qi: (row_off // s + b, P_KVA // 256)),
    ]
    args = [p, p]
    if latent:
        in_specs += [
            pl.BlockSpec((None, None, PAST_LEN, MLA_KV_LORA), lambda b, qi: (b, l, 0, 0)),
            pl.BlockSpec((None, PAST_LEN, HEAD_PAD), lambda b, qi: (b, 0, 0)),
            _const_spec((DEC_SEQ, HEAD_PAD)), _const_spec((DEC_SEQ, HEAD_PAD)), _const_spec((DEC_SEQ, HEAD_PAD)),
        ]
        args += [cache_ckv, cache_kpe_g] + [jnp.asarray(t) for t in _rope_tables()]
    in_specs += [
        _const_spec((1, MLA_Q_LORA)), _const_spec((MLA_Q_LORA, MLA_PAD_W)), _const_spec((1, MLA_KV_LORA)),
        _const_spec((MLA_KV_LORA, MLA_PAD_W)), _const_spec((MLA_KV_LORA, MLA_PAD_W)),
        _const_spec((1, HEAD_PAD)), _const_spec((1, HEAD_PAD)),
    ]
    args += [W['g_qa'], W['w_uq_p'], W['g_kva'], W['w_ukv_k'], W['w_ukv_v'], W['g_mla_q_p'], W['g_mla_k_p']]
    out_shape = [jax.ShapeDtypeStruct((nb * s, MLA_PAD_W), BF16)]
    out_specs = [pl.BlockSpec((tq, MLA_PAD_W), lambda b, qi: (b * nq + qi, 0))]
    if not latent:
        out_shape.append(jax.ShapeDtypeStruct((nb, s, MLA_KV_LORA), F32))
        out_specs.append(pl.BlockSpec((None, s, MLA_KV_LORA), lambda b, qi: (b, 0, 0)))
        out_shape.append(jax.ShapeDtypeStruct((nb, s, MLA_ROPE), F32))
        out_specs.append(pl.BlockSpec((None, s, MLA_ROPE), lambda b, qi: (b, 0, 0)))
    return pl.pallas_call(
        functools.partial(_mla_kernel, latent, tq),
        out_shape=out_shape,
        grid=(nb, nq),
        in_specs=in_specs,
        out_specs=out_specs,
        scratch_shapes=[pltpu.VMEM((sk, MLA_PAD_W), BF16), pltpu.VMEM((sk, MLA_PAD_W), BF16)],
        compiler_params=_cparams(("arbitrary", "arbitrary")),
        name="mla_latent" if latent else "mla_context",
    )(*args)


def _head_masks(width):
    lane = lax.broadcasted_iota(jnp.int32, (1, width), 1)
    return [(lane >= h * NA_HD) & (lane < (h + 1) * NA_HD) for h in range(NA_HEADS)]


def _group_norm64(x, g, masks):
    x2 = x * x
    inv = jnp.zeros_like(x)
    for m in masks:
        ms = jnp.sum(jnp.where(m, x2, 0.0), axis=-1, keepdims=True) * (1.0 / NA_HD)
        inv = jnp.where(m, lax.rsqrt(ms + EPS), inv)
    return x * inv * g


def _p_blocks(col0, n, rows, row_fn):
    def spec(k):
        return pl.BlockSpec((rows, BR_W), lambda *g: (row_fn(*g), col0 // BR_W + k))
    return [spec(k) for k in range(n)]


def _na_ctx_kernel(pq_ref, pk_ref, pv_ref, gq_ref, gk_ref, o_ref, k_ref, v_ref):
    masks = _head_masks(BR_W)
    qn = _group_norm64(pq_ref[...].astype(F32), gq_ref[...], masks)
    kn = _group_norm64(pk_ref[...].astype(F32), gk_ref[...], masks)
    v = pv_ref[...].astype(F32)
    k_ref[...] = kn
    v_ref[...] = v
    scale = NA_HD ** -0.5
    acc = jnp.zeros((pq_ref.shape[0], BR_W), F32)
    for m in masks:
        s = _dot_nt(jnp.where(m, qn, 0.0), kn) * scale
        mx = jnp.max(s, axis=-1, keepdims=True)
        e = jnp.exp(s - mx)
        den = jnp.sum(e, axis=-1, keepdims=True)
        acc = acc + jnp.where(m, _dot(e, v) / den, 0.0)
    o_ref[...] = acc.astype(o_ref.dtype)


def _na_context(p, W):
    s = SEQ
    return pl.pallas_call(
        _na_ctx_kernel,
        out_shape=[jax.ShapeDtypeStruct((N_CTX_TOK, BR_W), BF16),
                   jax.ShapeDtypeStruct((N_CTX_TOK, BR_W), F32),
                   jax.ShapeDtypeStruct((N_CTX_TOK, BR_W), F32)],
        grid=(BATCH,),
        in_specs=_p_blocks(P_NA, 3, s, lambda b: b) + [_const_spec((1, BR_W)), _const_spec((1, BR_W))],
        out_specs=[pl.BlockSpec((s, BR_W), lambda b: (b, 0))] * 3,
        compiler_params=_cparams(("arbitrary",)),
        name="na_context",
    )(p, p, p, W['g_na_q_t'], W['g_na_k_t'])


def _na_lat_kernel(pq_ref, pk_ref, pv_ref, ck_ref, cv_ref, bias_ref, gq_ref, gk_ref, o_ref,
                   q_scr, k_scr, v_scr, kc_scr, vc_scr):
    j = pl.program_id(1)
    masks = _head_masks(BR_W)

    zero = jnp.zeros((), BF16)
    one = jnp.ones((), BF16)
    lane = lax.broadcasted_iota(jnp.int32, (1, BR_W), 1)
    den_lanes = [((h + 1) % NA_HEADS) * NA_HD for h in range(NA_HEADS)]

    @pl.when(j == 0)
    def _():
        scale = NA_HD ** -0.5
        q_scr[...] = (_group_norm64(pq_ref[...].astype(F32), gq_ref[...], masks) * scale).astype(BF16)
        k_scr[...] = _group_norm64(pk_ref[...].astype(F32), gk_ref[...], masks).astype(BF16)
        kc_scr[...] = ck_ref[...].astype(BF16)
        v = pv_ref[...]
        vc = cv_ref[...].astype(BF16)
        for h, m in enumerate(masks):
            v_scr[h] = jnp.where(lane == den_lanes[h], one, jnp.where(m, v, zero))
            vc_scr[h] = jnp.where(lane == den_lanes[h], one, jnp.where(m, vc, zero))

    win0 = pl.multiple_of(jnp.where(j < 2, 0, DEC_SEQ - NA_WIN_KEYS), 256)
    q = q_scr[pl.ds(pl.multiple_of(j * NA_TQ, NA_TQ), NA_TQ), :]
    kw = k_scr[pl.ds(win0, NA_WIN_KEYS), :]
    kc = kc_scr[...]
    acc = jnp.zeros((NA_TQ, BR_W), F32)
    for h, m in enumerate(masks):
        qm = jnp.where(m, q, zero)
        s_loc = _dot_nt(qm, kw) + bias_ref[h]
        s_ctx = _dot_nt(qm, kc)
        mx = jnp.maximum(jnp.max(s_loc, axis=-1, keepdims=True), jnp.max(s_ctx, axis=-1, keepdims=True))
        e_loc = jnp.exp((s_loc - mx).astype(BF16))
        e_ctx = jnp.exp((s_ctx - mx).astype(BF16))
        o = (jnp.dot(e_loc, v_scr[h, pl.ds(win0, NA_WIN_KEYS), :], preferred_element_type=F32)
             + jnp.dot(e_ctx, vc_scr[h], preferred_element_type=F32))
        acc = acc + jnp.where(m, o / o[:, den_lanes[h]:den_lanes[h] + 1], 0.0)
    o_ref[...] = acc.astype(o_ref.dtype)


def _na_bias_table(rpb):
    n_dr, n_dc = 2 * NA_WIN_R - 1, 2 * NA_WIN_C - 1
    cols = np.arange(GRID_W)
    dc = np.clip(cols[None, :] - cols[:, None], -(NA_WIN_C - 1), NA_WIN_C - 1) + (NA_WIN_C - 1)
    cstart = np.clip(cols - NA_WIN_C // 2, 0, GRID_W - NA_WIN_C)
    col_ok = (cols[None, :] >= cstart[:, None]) & (cols[None, :] < cstart[:, None] + NA_WIN_C)
    place = (dc.reshape(-1)[None, :] == np.arange(n_dc)[:, None]).astype(np.float32)
    blocks = jnp.dot(rpb.reshape(NA_HEADS * n_dr, n_dc), place, precision=lax.Precision.HIGHEST)
    blocks = jnp.where(col_ok.reshape(-1)[None, :], blocks, -jnp.inf).reshape(NA_HEADS, n_dr, GRID_W, GRID_W)
    neg = jnp.full((NA_HEADS, 1, GRID_W, GRID_W), -jnp.inf, F32)
    blocks = jnp.concatenate([blocks, neg], axis=1)

    nt = DEC_SEQ // NA_TQ
    n_rows = DEC_SEQ // GRID_W
    rows_per_tile = NA_TQ // GRID_W
    sel = np.full((nt, rows_per_tile, NA_WIN_ROWS), n_dr, np.int32)
    for j in range(nt):
        win_row0 = 0 if j < nt // 2 else n_rows - NA_WIN_ROWS
        for rq in range(rows_per_tile):
            r = j * rows_per_tile + rq
            start = min(max(r - NA_WIN_R // 2, 0), n_rows - NA_WIN_R)
            for kr in range(NA_WIN_ROWS):
                if start <= win_row0 + kr < start + NA_WIN_R:
                    sel[j, rq, kr] = win_row0 + kr - r + (NA_WIN_R - 1)
    tiles = []
    for j in range(nt):
        rows = [jnp.concatenate([blocks[:, sel[j, rq, kr]] for kr in range(NA_WIN_ROWS)], axis=-1)
                for rq in range(rows_per_tile)]
        tiles.append(jnp.concatenate(rows, axis=1))
    return jnp.stack(tiles, axis=1)


def _na_latent(p, l, cache_k, cache_v, bias, W):
    s = DEC_SEQ
    nt = s // NA_TQ
    row_off = N_CTX_TOK
    return pl.pallas_call(
        _na_lat_kernel,
        out_shape=jax.ShapeDtypeStruct((N_LAT_TOK, BR_W), BF16),
        grid=(DEC_BATCH, nt),
        in_specs=_p_blocks(P_NA, 3, s, lambda b, j: row_off // s + b) + [
                  pl.BlockSpec((None, None, PAST_LEN, BR_W), lambda b, j: (b, l, 0, 0)),
                  pl.BlockSpec((None, None, PAST_LEN, BR_W), lambda b, j: (b, l, 0, 0)),
                  pl.BlockSpec((NA_HEADS, None, NA_TQ, NA_WIN_KEYS), lambda b, j: (0, j, 0, 0)),
                  _const_spec((1, BR_W)), _const_spec((1, BR_W))],
        out_specs=pl.BlockSpec((NA_TQ, BR_W), lambda b, j: (b * nt + j, 0)),
        scratch_shapes=[pltpu.VMEM((s, BR_W), BF16), pltpu.VMEM((s, BR_W), BF16),
                        pltpu.VMEM((NA_HEADS, s, BR_W), BF16),
                        pltpu.VMEM((PAST_LEN, BR_W), BF16), pltpu.VMEM((NA_HEADS, PAST_LEN, BR_W), BF16)],
        compiler_params=_cparams(("arbitrary", "arbitrary")),
        name="na_latent",
    )(p, p, p, cache_k, cache_v, bias, W['g_na_q_t'], W['g_na_k_t'])


CONV_HALO = 16
CONV_CHUNK = 128


SUBLANES = 8


def _dwconv_from_pad(pad_ref, w_ref, ksize, s, emit, shift_ref=None):
    half = ksize // 2
    if shift_ref is not None:
        n_rows = s + 2 * CONV_HALO - SUBLANES
        for p in range(1, SUBLANES):
            for c0 in range(0, n_rows, CONV_CHUNK):
                n = min(CONV_CHUNK, n_rows - c0)
                shift_ref[p - 1, c0:c0 + n, :] = pad_ref[c0 + p:c0 + p + n, :]
    for c0 in range(0, s, CONV_CHUNK):
        acc = jnp.zeros((CONV_CHUNK, BR_W), F32)
        for k in range(ksize):
            r0 = CONV_HALO + c0 + k - half
            p = r0 % SUBLANES
            if shift_ref is None or p == 0:
                win = pad_ref[r0:r0 + CONV_CHUNK, :]
            else:
                win = shift_ref[p - 1, r0 - p:r0 - p + CONV_CHUNK, :]
            acc = acc + win * w_ref[k:k + 1, :]
        emit(c0, acc)


def _conv_kernel(s, scb_ref, scc_ref, scx_ref, cfa_ref, cfb2_ref, scw_ref, cfw_ref, cfb_ref, lng_ref, lnb_ref,
                 ob_ref, od_ref, pad_ref, shift_ref):
    zeros = jnp.zeros((CONV_HALO, BR_W), F32)
    pad_ref[0:CONV_HALO, :] = zeros
    pad_ref[CONV_HALO + s:2 * CONV_HALO + s, :] = zeros

    pad_ref[CONV_HALO:CONV_HALO + s, :] = scc_ref[...].astype(F32) * scx_ref[...].astype(F32)

    def emit_b(c0, acc):
        ob_ref[c0:c0 + CONV_CHUNK, :] = (scb_ref[c0:c0 + CONV_CHUNK, :].astype(F32) * acc).astype(ob_ref.dtype)

    _dwconv_from_pad(pad_ref, scw_ref, SC_K, s, emit_b)

    pad_ref[CONV_HALO:CONV_HALO + s, :] = cfa_ref[...].astype(F32) * _sigmoid(cfb2_ref[...].astype(F32))

    def emit_d(c0, acc):
        u = acc + cfb_ref[...]
        mu = jnp.mean(u, axis=-1, keepdims=True)
        d = u - mu
        var = jnp.mean(d * d, axis=-1, keepdims=True)
        y = d * lax.rsqrt(var + EPS) * lng_ref[...] + lnb_ref[...]
        od_ref[c0:c0 + CONV_CHUNK, :] = _silu(y).astype(od_ref.dtype)

    _dwconv_from_pad(pad_ref, cfw_ref, CF_K, s, emit_d, shift_ref)


def _convs(p, latent, W):
    if latent:
        nb, s, row_off = DEC_BATCH, DEC_SEQ, N_CTX_TOK
    else:
        nb, s, row_off = BATCH, SEQ, 0
    return pl.pallas_call(
        functools.partial(_conv_kernel, s),
        out_shape=[jax.ShapeDtypeStruct((nb * s, BR_W), BF16)] * 2,
        grid=(nb,),
        in_specs=(_p_blocks(P_SC, 3, s, lambda b: row_off // s + b)
                  + _p_blocks(P_CF, 2, s, lambda b: row_off // s + b)
                  + [_const_spec((SC_K, BR_W)), _const_spec((CF_K, BR_W)),
                     _const_spec((1, BR_W)), _const_spec((1, BR_W)), _const_spec((1, BR_W))]),
        out_specs=[pl.BlockSpec((s, BR_W), lambda b: (b, 0))] * 2,
        scratch_shapes=[pltpu.VMEM((s + 2 * CONV_HALO, BR_W), F32),
                        pltpu.VMEM((SUBLANES - 1, s + 2 * CONV_HALO, BR_W), F32)],
        compiler_params=_cparams(("arbitrary",)),
        name="convs_latent" if latent else "convs_context",
    )(p, p, p, p, p, W['sc_w'], W['cf_w'], W['cf_b'], W['cf_ln_g'], W['cf_ln_b'])


def _merge_kernel(nc, nx, *refs):
    x_refs = refs[:nx]
    (mod_ref, oac_ref, oal_ref, obc_ref, obl_ref, occ_ref, ocl_ref, odc_ref, odl_ref) = refs[nx:nx + 9]
    gate_refs = refs[nx + 9:nx + 9 + 2 * N_BRANCH]
    wa_ref, wb_ref, wo_ref, y_ref = refs[nx + 9 + 2 * N_BRANCH:]
    is_ctx = pl.program_id(0) < nc

    def gate(n):
        return jnp.concatenate([_sigmoid(gate_refs[2 * n][...]), _sigmoid(gate_refs[2 * n + 1][...])],
                               axis=1).astype(F32)

    merged = gate(0) * jnp.dot(_ctx_or_lat(is_ctx, oac_ref, oal_ref), wa_ref[...], preferred_element_type=F32)
    for n, (oc_ref, ol_ref) in enumerate(((obc_ref, obl_ref), (occ_ref, ocl_ref), (odc_ref, odl_ref))):
        merged = merged + gate(n + 1) * jnp.dot(_ctx_or_lat(is_ctx, oc_ref, ol_ref), wb_ref[n],
                                                preferred_element_type=F32)
    y_ref[...] = _x_load(x_refs) + mod_ref[2:3, :] * _dot(merged, wo_ref[...])


def _merge(x, mod3, p, o_a, o_b, o_c, o_d, W):
    tm = 512
    gate_spec = lambda k: pl.BlockSpec((tm, GATE_BLK), lambda i: (i, P_GATE // GATE_BLK + k))
    n_gate_blk = N_BRANCH * D_MODEL // GATE_BLK
    return pl.pallas_call(
        functools.partial(_merge_kernel, N_CTX_TOK // tm, len(x)),
        out_shape=jax.ShapeDtypeStruct((N_TOK, D_MODEL), F32),
        grid=(N_TOK // tm,),
        in_specs=(_x_specs(x, tm)
                  + [pl.BlockSpec((None, 6, D_MODEL), lambda i: (_mod_row(i * tm), 0, 0))]
                  + _pair_specs(tm, MLA_PAD_W) + _pair_specs(tm, BR_W) + _pair_specs(tm, BR_W) + _pair_specs(tm, BR_W)
                  + [gate_spec(k) for k in range(n_gate_blk)]
                  + [_const_spec((MLA_PAD_W, D_MODEL)), _const_spec((3, BR_W, D_MODEL)),
                     _const_spec((D_MODEL, D_MODEL))]),
        out_specs=pl.BlockSpec((tm, D_MODEL), lambda i: (i, 0)),
        compiler_params=_cparams(("arbitrary",)),
        name="merge",
    )(*x, mod3, *o_a, *o_b, *o_c, *o_d, *([p] * n_gate_blk), W['w_br_a'], W['w_br_bcd'], W['w_o'])


def _norm2(x, g, mod_ref):
    y = x * lax.rsqrt(jnp.mean(x * x, axis=-1, keepdims=True) + EPS) * g
    return y * (1.0 + mod_ref[4:5, :]) + mod_ref[3:4, :]


def _ffn_kernel(x_ref, mod_ref, g_ref, wg_ref, wu_ref, wd_ref, y_ref, h_scr, acc_scr):
    f = pl.program_id(1)

    @pl.when(f == 0)
    def _():
        h_scr[...] = _norm2(x_ref[...], g_ref[...], mod_ref).astype(BF16)
        acc_scr[...] = jnp.zeros_like(acc_scr)

    h = h_scr[...]
    a = _silu(_dot(h, wg_ref[...])) * _dot(h, wu_ref[...])
    acc_scr[...] += _dot(a, wd_ref[...])

    @pl.when(f == pl.num_programs(1) - 1)
    def _():
        y_ref[...] = x_ref[...] + mod_ref[5:6, :] * acc_scr[...]


def _ffn(x_all, mod3, g_norm2, w_g, w_u, w_d, l, j):
    tm, tf = 1024, 512
    return pl.pallas_call(
        _ffn_kernel,
        out_shape=jax.ShapeDtypeStruct((N_TOK, D_MODEL), F32),
        grid=(N_TOK // tm, D_FF // tf),
        in_specs=[pl.BlockSpec((tm, D_MODEL), lambda i, f: (i, 0)),
                  pl.BlockSpec((None, 6, D_MODEL), lambda i, f: (_mod_row(i * tm), 0, 0)),
                  pl.BlockSpec((None, 1, D_MODEL), lambda i, f: (l, 0, 0)),
                  pl.BlockSpec((None, D_MODEL, tf), lambda i, f: (j, 0, f)),
                  pl.BlockSpec((None, D_MODEL, tf), lambda i, f: (j, 0, f)),
                  pl.BlockSpec((None, tf, D_MODEL), lambda i, f: (j, f, 0))],
        out_specs=pl.BlockSpec((tm, D_MODEL), lambda i, f: (i, 0)),
        scratch_shapes=[pltpu.VMEM((tm, D_MODEL), BF16), pltpu.VMEM((tm, D_MODEL), F32)],
        compiler_params=_cparams(("arbitrary", "arbitrary")),
        name="ffn_dense",
    )(x_all, mod3, g_norm2.reshape(DEPTH, 1, D_MODEL), w_g, w_u, w_d)


ROUTER_LANES = 128
ROUTE_TM = 512
MOE_TM = 1024
MOE_SUB = 256
MOE_TF = 512
RUN_ALIGN = 8
MOE_TILES = -(-(2 * N_TOK + (N_TOK // ROUTE_TM) * N_EXPERTS * RUN_ALIGN + N_EXPERTS * MOE_TM) // MOE_TM)
MOE_ROWS = MOE_TILES * MOE_TM
ROW_DMA_TM = 256
R_I1, R_I2, R_W1, R_W2, R_RANK1, R_RANK2 = range(6)


def _route_top2(h, wr):
    h_hi = h.astype(BF16)
    h_lo = (h - h_hi.astype(F32)).astype(BF16)
    w_hi = wr.astype(BF16)
    w_lo = (wr - w_hi.astype(F32)).astype(BF16)
    logits = (jnp.dot(h_hi, w_hi, preferred_element_type=F32) + jnp.dot(h_lo, w_hi, preferred_element_type=F32)
              + jnp.dot(h_hi, w_lo, preferred_element_type=F32))
    lane = lax.broadcasted_iota(jnp.int32, logits.shape, 1).astype(F32)
    neg = jnp.float32(-jnp.inf)
    logits = jnp.where(lane < N_EXPERTS, logits, neg)
    m1 = jnp.max(logits, axis=-1, keepdims=True)
    i1 = jnp.min(jnp.where(logits == m1, lane, float(ROUTER_LANES)), axis=-1, keepdims=True)
    rest = jnp.where(lane == i1, neg, logits)
    m2 = jnp.max(rest, axis=-1, keepdims=True)
    i2 = jnp.min(jnp.where(rest == m2, lane, float(ROUTER_LANES)), axis=-1, keepdims=True)
    e2 = jnp.exp(m2 - m1)
    return lane, i1, i2, 1.0 / (1.0 + e2), e2 / (1.0 + e2)


def _route_kernel(x_ref, mod_ref, g_ref, wr_ref, route_ref, cnt_ref, tcarry_ref, carry_scr):
    @pl.when(pl.program_id(0) == 0)
    def _():
        carry_scr[...] = jnp.zeros_like(carry_scr)

    tcarry_ref[...] = carry_scr[...]

    h = _norm2(x_ref[...], g_ref[...], mod_ref)
    lane, i1, i2, w1, w2 = _route_top2(h, wr_ref[...])
    tm = h.shape[0]
    oh1 = lane == i1
    oh2 = lane == i2
    oh = jnp.where(oh1, 1.0, 0.0) + jnp.where(oh2, 1.0, 0.0)
    r = lax.broadcasted_iota(jnp.int32, (tm, tm), 0)
    c = lax.broadcasted_iota(jnp.int32, (tm, tm), 1)
    lower = jnp.where(r > c, 1.0, 0.0).astype(BF16)
    before = jnp.dot(lower, oh.astype(BF16), preferred_element_type=F32)
    rank1 = jnp.sum(jnp.where(oh1, before, 0.0), axis=-1, keepdims=True)
    rank2 = jnp.sum(jnp.where(oh2, before, 0.0), axis=-1, keepdims=True)
    carry_scr[...] += jnp.sum(oh, axis=0, keepdims=True)
    rec = jnp.zeros_like(lane)
    for k, v in ((R_I1, i1), (R_I2, i2), (R_W1, w1), (R_W2, w2), (R_RANK1, rank1), (R_RANK2, rank2)):
        rec = jnp.where(lane == float(k), v, rec)
    route_ref[...] = rec
    cnt_ref[...] = carry_scr[...]


def _moe_route(x_all, mod3, g_norm2, w_router_p, l, j):
    tm = ROUTE_TM
    return pl.pallas_call(
        _route_kernel,
        out_shape=[jax.ShapeDtypeStruct((N_TOK, ROUTER_LANES), F32),
                   jax.ShapeDtypeStruct((1, ROUTER_LANES), F32),
                   jax.ShapeDtypeStruct((N_TOK // tm, 1, ROUTER_LANES), F32)],
        grid=(N_TOK // tm,),
        in_specs=[pl.BlockSpec((tm, D_MODEL), lambda i: (i, 0)),
                  pl.BlockSpec((None, 6, D_MODEL), lambda i: (_mod_row(i * tm), 0, 0)),
                  pl.BlockSpec((None, 1, D_MODEL), lambda i: (l, 0, 0)),
                  pl.BlockSpec((None, D_MODEL, ROUTER_LANES), lambda i: (j, 0, 0))],
        out_specs=[pl.BlockSpec((tm, ROUTER_LANES), lambda i: (i, 0)),
                   pl.BlockSpec((1, ROUTER_LANES), lambda i: (0, 0)),
                   pl.BlockSpec((None, 1, ROUTER_LANES), lambda i: (i, 0, 0))],
        scratch_shapes=[pltpu.VMEM((1, ROUTER_LANES), F32)],
        compiler_params=_cparams(("arbitrary",)),
        name="moe_route",
    )(x_all, mod3, g_norm2.reshape(DEPTH, 1, D_MODEL), w_router_p)


def _row_copies(n, pos_refs, base, make):
    def body(r, carry):
        for pos_ref in pos_refs:
            make(r, pos_ref[base + r]).start()
        return carry
    lax.fori_loop(0, n, body, 0, unroll=8)


ZERO_ROWS = 512
N_GAPS = N_EXPERTS + 1


def _bit_chunks(n_units, max_units, make, wait=False):
    for b in range(max_units.bit_length()):
        units = 1 << b

        @pl.when((n_units & units) != 0)
        def _():
            cp = make(pl.multiple_of((n_units & (units - 1)) * RUN_ALIGN, RUN_ALIGN), units * RUN_ALIGN)
            if wait:
                cp.wait()
            else:
                cp.start()


def _dispatch_kernel(n_ref, lo_ref, dst_ref, gap0_ref, gapn_ref, x_ref, mod_ref, g_ref, route_ref, lovec_ref,
                     xs_ref, z_scr, zero_scr, sems, zsem):
    i = pl.program_id(0)
    tm = x_ref.shape[0]
    h = _norm2(x_ref[...], g_ref[...], mod_ref).astype(BF16)
    rec = route_ref[...]
    lane = lax.broadcasted_iota(jnp.int32, rec.shape, 1).astype(F32)
    lo_row = lovec_ref[...]

    def local_pos(i_lane, r_lane):
        lp = (jnp.sum(jnp.where(lane == rec[:, i_lane:i_lane + 1], lo_row, 0.0), axis=-1, keepdims=True)
              + rec[:, r_lane:r_lane + 1])
        return jnp.transpose(jnp.broadcast_to(lp, (tm, ROUTER_LANES)))[0:1, :]

    slot = lax.broadcasted_iota(jnp.int32, (z_scr.shape[1], tm), 0).astype(F32)
    perm = jnp.where((slot == local_pos(R_I1, R_RANK1)) | (slot == local_pos(R_I2, R_RANK2)), 1.0, 0.0)
    buf = i % 2
    z_scr[buf] = jnp.dot(perm.astype(BF16), h, preferred_element_type=F32)

    def run_copies(tile, b, wait):
        for e in range(N_EXPERTS):
            k = tile * N_EXPERTS + e
            src0 = lo_ref[k]
            dst0 = dst_ref[k]

            def make(off, rows, src0=src0, dst0=dst0):
                return pltpu.make_async_copy(
                    z_scr.at[b, pl.ds(pl.multiple_of(src0 + off, RUN_ALIGN), rows), :],
                    xs_ref.at[pl.ds(pl.multiple_of(dst0 + off, RUN_ALIGN), rows), :], sems.at[b])

            _bit_chunks(n_ref[k], tm // RUN_ALIGN, make, wait)

    run_copies(i, buf, False)

    @pl.when(i > 0)
    def _():
        run_copies(i - 1, 1 - buf, True)

    @pl.when(i == pl.num_programs(0) - 1)
    def _():
        run_copies(i, buf, True)

    @pl.when(i == pl.num_programs(0) - 1)
    def _():
        zero_scr[...] = jnp.zeros_like(zero_scr)
        for wait in (False, True):
            for g in range(N_GAPS):
                start = gap0_ref[g]
                n = gapn_ref[g]
                max_rows = MOE_TM if g < N_EXPERTS else MOE_ROWS - 2 * N_TOK
                for c in range(max_rows // ZERO_ROWS):
                    @pl.when(n >= (c + 1) * ZERO_ROWS)
                    def _():
                        cp = pltpu.make_async_copy(
                            zero_scr,
                            xs_ref.at[pl.ds(pl.multiple_of(start + c * ZERO_ROWS, RUN_ALIGN), ZERO_ROWS), :], zsem)
                        if wait:
                            cp.wait()
                        else:
                            cp.start()

                def make(off, rows, start=start, n=n):
                    tail0 = start + (n // ZERO_ROWS) * ZERO_ROWS
                    return pltpu.make_async_copy(
                        zero_scr.at[pl.ds(0, rows), :],
                        xs_ref.at[pl.ds(pl.multiple_of(tail0 + off, RUN_ALIGN), rows), :], zsem)

                _bit_chunks((n % ZERO_ROWS) // RUN_ALIGN, ZERO_ROWS // RUN_ALIGN - 1, make, wait)


def _moe_dispatch(n_tile, lo_tile, dst_tile, gap0, gapn, x_all, mod3, g_norm2, route, lo_vec, l):
    tm = ROUTE_TM
    spec = lambda shape, fn: pl.BlockSpec(shape, lambda i, *_: fn(i))
    return pl.pallas_call(
        _dispatch_kernel,
        out_shape=jax.ShapeDtypeStruct((MOE_ROWS, D_MODEL), F32),
        grid_spec=pltpu.PrefetchScalarGridSpec(
            num_scalar_prefetch=5,
            grid=(N_TOK // tm,),
            in_specs=[spec((tm, D_MODEL), lambda i: (i, 0)),
                      spec((None, 6, D_MODEL), lambda i: (_mod_row(i * tm), 0, 0)),
                      spec((None, 1, D_MODEL), lambda i: (l, 0, 0)),
                      spec((tm, ROUTER_LANES), lambda i: (i, 0)),
                      spec((None, 1, ROUTER_LANES), lambda i: (i, 0, 0))],
            out_specs=pl.BlockSpec(memory_space=pl.ANY),
            scratch_shapes=[pltpu.VMEM((2, 2 * tm + N_EXPERTS * RUN_ALIGN, D_MODEL), F32),
                            pltpu.VMEM((ZERO_ROWS, D_MODEL), F32),
                            pltpu.SemaphoreType.DMA((2,)), pltpu.SemaphoreType.DMA],
        ),
        compiler_params=_cparams(("arbitrary",)),
        name="moe_dispatch",
    )(n_tile, lo_tile, dst_tile, gap0, gapn, x_all, mod3, g_norm2.reshape(DEPTH, 1, D_MODEL), route, lo_vec)


def _gmm_kernel(te_ref, tr_ref, xs_ref, wg_ref, wu_ref, wd_ref, y_ref, wg_scr, wu_scr, wd_scr):
    del te_ref
    g = pl.program_id(0)
    f = pl.program_id(1)
    rows = tr_ref[g]

    @pl.when(f == 0)
    def _():
        y_ref[...] = jnp.zeros_like(y_ref)

    def sub_tile(s, wg, wu, wd):
        sl = slice(s * MOE_SUB, (s + 1) * MOE_SUB)
        x = xs_ref[sl, :].astype(BF16)
        a = (_silu(jnp.dot(x, wg, preferred_element_type=F32)) * jnp.dot(x, wu, preferred_element_type=F32))
        y_ref[sl, :] += jnp.dot(a.astype(BF16), wd, preferred_element_type=F32)

    @pl.when(rows == MOE_TM)
    def _():
        x = xs_ref[...].astype(BF16)
        a = _silu(_dot(x, wg_ref[...])) * _dot(x, wu_ref[...])
        y_ref[...] += _dot(a, wd_ref[...])

    @pl.when((rows > 0) & (rows < MOE_TM))
    def _():
        wg = wg_ref[...].astype(BF16)
        wu = wu_ref[...].astype(BF16)
        wd = wd_ref[...].astype(BF16)
        wg_scr[...] = wg
        wu_scr[...] = wu
        wd_scr[...] = wd
        sub_tile(0, wg, wu, wd)

    for s in range(1, MOE_TM // MOE_SUB):
        @pl.when((rows > s * MOE_SUB) & (rows < MOE_TM))
        def _():
            sub_tile(s, wg_scr[...], wu_scr[...], wd_scr[...])


def _moe_gmm(tile_expert, tile_rows, xs, w_g, w_u, w_d, j):
    tm, tf = MOE_TM, MOE_TF
    nf = D_FF // tf

    def f_eff(g, f, tr):
        return jnp.where(tr[g] > 0, f, nf - 1)

    return pl.pallas_call(
        _gmm_kernel,
        out_shape=jax.ShapeDtypeStruct((MOE_ROWS, D_MODEL), F32),
        grid_spec=pltpu.PrefetchScalarGridSpec(
            num_scalar_prefetch=2,
            grid=(MOE_TILES, nf),
            in_specs=[pl.BlockSpec((tm, D_MODEL), lambda g, f, te, tr: (g, 0)),
                      pl.BlockSpec((None, None, D_MODEL, tf), lambda g, f, te, tr: (j, te[g], 0, f_eff(g, f, tr))),
                      pl.BlockSpec((None, None, D_MODEL, tf), lambda g, f, te, tr: (j, te[g], 0, f_eff(g, f, tr))),
                      pl.BlockSpec((None, None, tf, D_MODEL), lambda g, f, te, tr: (j, te[g], f_eff(g, f, tr), 0))],
            out_specs=pl.BlockSpec((tm, D_MODEL), lambda g, f, te, tr: (g, 0)),
            scratch_shapes=[pltpu.VMEM((D_MODEL, tf), BF16), pltpu.VMEM((D_MODEL, tf), BF16),
                            pltpu.VMEM((tf, D_MODEL), BF16)],
        ),
        compiler_params=_cparams(("arbitrary", "arbitrary")),
        name="moe_experts",
    )(tile_expert, tile_rows, xs, w_g, w_u, w_d)


def _combine_kernel(n_out, pos1_ref, pos2_ref, x_ref, mod_ref, route_ref, y_ref, *refs):
    o_refs, (g1_scr, g2_scr, sems) = refs[:n_out], refs[n_out:]
    tm = g1_scr.shape[1]
    i = pl.program_id(0)
    n = pl.num_programs(0)

    def start_gathers(tile, slot):
        def make1(r, pos):
            return pltpu.make_async_copy(y_ref.at[pl.ds(pos, 1), :], g1_scr.at[slot, pl.ds(r, 1), :], sems.at[slot])

        def make2(r, pos):
            return pltpu.make_async_copy(y_ref.at[pl.ds(pos, 1), :], g2_scr.at[slot, pl.ds(r, 1), :], sems.at[slot])

        _row_copies(tm, (pos1_ref,), tile * tm, make1)
        _row_copies(tm, (pos2_ref,), tile * tm, make2)

    @pl.when(i == 0)
    def _():
        start_gathers(0, 0)

    slot = i % 2

    @pl.when(i + 1 < n)
    def _():
        start_gathers(i + 1, 1 - slot)

    pltpu.make_async_copy(y_ref.at[pl.ds(0, tm), :], g1_scr.at[slot], sems.at[slot]).wait()
    pltpu.make_async_copy(y_ref.at[pl.ds(0, tm), :], g2_scr.at[slot], sems.at[slot]).wait()
    rec = route_ref[...]
    w1 = rec[:, R_W1:R_W1 + 1]
    w2 = rec[:, R_W2:R_W2 + 1]
    _x_store(o_refs, x_ref[...] + mod_ref[5:6, :] * (w1 * g1_scr[slot] + w2 * g2_scr[slot]))


def _moe_combine(pos1, pos2, x_all, mod3, route, y, split_out):
    tm = ROW_DMA_TM
    return pl.pallas_call(
        functools.partial(_combine_kernel, 2 if split_out else 1),
        out_shape=_x_shapes(split_out),
        grid_spec=pltpu.PrefetchScalarGridSpec(
            num_scalar_prefetch=2,
            grid=(N_TOK // tm,),
            in_specs=[pl.BlockSpec((tm, D_MODEL), lambda i, p1, p2: (i, 0)),
                      pl.BlockSpec((None, 6, D_MODEL), lambda i, p1, p2: (_mod_row(i * tm), 0, 0)),
                      pl.BlockSpec((tm, ROUTER_LANES), lambda i, p1, p2: (i, 0)),
                      pl.BlockSpec(memory_space=pl.ANY)],
            out_specs=_x_specs((None,) * (2 if split_out else 1), tm),
            scratch_shapes=[pltpu.VMEM((2, tm, D_MODEL), F32), pltpu.VMEM((2, tm, D_MODEL), F32),
                            pltpu.SemaphoreType.DMA((2,))],
        ),
        compiler_params=_cparams(("arbitrary",)),
        name="moe_combine",
    )(pos1, pos2, x_all, mod3, route, y)


def _moe(x_all, mod3, g_norm2, w_router_p, w_g, w_u, w_d, l, j, split_out):
    route, cnt, tcarry = _moe_route(x_all, mod3, g_norm2, w_router_p, l, j)
    cnt = cnt[0, :N_EXPERTS].astype(jnp.int32)
    carry = tcarry[:, 0, :N_EXPERTS].astype(jnp.int32)
    n_tile = jnp.concatenate([carry[1:], cnt[None, :]], axis=0) - carry
    n_tile = (n_tile + RUN_ALIGN - 1) // RUN_ALIGN * RUN_ALIGN
    lo_tile = jnp.cumsum(n_tile, axis=1) - n_tile
    carry = jnp.cumsum(n_tile, axis=0) - n_tile
    cnt = jnp.sum(n_tile, axis=0)
    padded = (cnt + MOE_TM - 1) // MOE_TM * MOE_TM
    ends = jnp.cumsum(padded)
    offs = ends - padded
    dst_tile = offs[None, :] + carry
    experts = jnp.arange(N_EXPERTS, dtype=jnp.int32)
    gap0 = jnp.concatenate([offs + cnt, ends[-1:]])
    gapn = jnp.concatenate([padded - cnt, MOE_ROWS - ends[-1:]])
    lo_vec = jnp.pad(lo_tile.astype(F32), ((0, 0), (0, ROUTER_LANES - N_EXPERTS)))[:, None, :]

    def positions(i_lane, rank_lane):
        e = route[:, i_lane].astype(jnp.int32)
        base = jnp.repeat(dst_tile, ROUTE_TM, axis=0)
        off = jnp.sum(jnp.where(e[:, None] == experts[None, :], base, 0), axis=1)
        return off + route[:, rank_lane].astype(jnp.int32)

    pos1 = positions(R_I1, R_RANK1)
    pos2 = positions(R_I2, R_RANK2)
    tile_start = jnp.arange(MOE_TILES, dtype=jnp.int32) * MOE_TM
    last_tile = jnp.maximum(ends[-1] - MOE_TM, 0)
    owner_start = jnp.minimum(tile_start, last_tile)
    tile_expert = jnp.minimum(jnp.sum(owner_start[:, None] >= ends[None, :], axis=1), N_EXPERTS - 1).astype(jnp.int32)
    group_end = jnp.sum(jnp.where(tile_expert[:, None] == experts[None, :], (offs + cnt)[None, :], 0), axis=1)
    tile_rows = jnp.where(tile_start < ends[-1], jnp.clip(group_end - tile_start, 0, MOE_TM), 0).astype(jnp.int32)

    i32 = lambda a: a.reshape(-1).astype(jnp.int32)
    xs = _moe_dispatch(i32(n_tile // RUN_ALIGN), i32(lo_tile), i32(dst_tile), i32(gap0), i32(gapn),
                       x_all, mod3, g_norm2, route, lo_vec, l)
    y = _moe_gmm(tile_expert, tile_rows, xs, w_g, w_u, w_d, j)
    return tuple(_moe_combine(pos1, pos2, x_all, mod3, route, y, split_out))


def _pad_heads(w, per_head, lo):
    lead = w.shape[:-1]
    w = w.reshape(lead + (MLA_HEADS, per_head))
    w = jnp.pad(w, [(0, 0)] * len(lead) + [(0, 0), (lo, HEAD_PAD - lo - per_head)])
    return w.reshape(lead + (MLA_PAD_W,))


def _layer_weights(l, w_in, g_qa, w_uq, g_kva, w_ukv, g_mla_q, g_mla_k, sc_w, g_na_q, g_na_k,
                   cf_w, cf_b, cf_ln_g, cf_ln_b, w_br, w_o):
    s3 = MLA_Q_LORA + MLA_KV_LORA + MLA_ROPE
    w_in_p = jnp.concatenate([
        w_in[l, :, :s3], jnp.zeros((D_MODEL, P_SC - s3), F32), w_in[l, :, s3:]], axis=1).astype(BF16)
    ukv = w_ukv[l].reshape(MLA_KV_LORA, MLA_HEADS, MLA_NOPE + MLA_V)
    W = dict(
        w_in_p=w_in_p,
        g_qa=g_qa[l][None], g_kva=g_kva[l][None],
        w_uq_p=_pad_heads(w_uq[l], MLA_QK, 0).astype(BF16),
        w_ukv_k=_pad_heads(ukv[:, :, :MLA_NOPE].reshape(MLA_KV_LORA, -1), MLA_NOPE, 0).astype(BF16),
        w_ukv_v=_pad_heads(ukv[:, :, MLA_NOPE:].reshape(MLA_KV_LORA, -1), MLA_V, 0).astype(BF16),
        g_mla_q_p=jnp.pad(g_mla_q[l], (0, HEAD_PAD - MLA_QK))[None],
        g_mla_k_p=jnp.pad(g_mla_k[l], (0, HEAD_PAD - MLA_QK))[None],
        g_na_q_t=jnp.tile(g_na_q[l], NA_HEADS)[None], g_na_k_t=jnp.tile(g_na_k[l], NA_HEADS)[None],
        sc_w=sc_w[l], cf_w=cf_w[l], cf_b=cf_b[l][None], cf_ln_g=cf_ln_g[l][None], cf_ln_b=cf_ln_b[l][None],
        w_br_a=jnp.pad(w_br[l, 0].reshape(MLA_HEADS, MLA_V, D_MODEL),
                       ((0, 0), (0, HEAD_PAD - MLA_V), (0, 0))).reshape(MLA_PAD_W, D_MODEL).astype(BF16),
        w_br_bcd=w_br[l, 1:].astype(BF16),
        w_o=w_o[l].astype(BF16),
    )
    return W


def kernel(x_prompt, x_sample, cache_mla_ckv, cache_mla_kpe, cache_na_k, cache_na_v, c, c_ctx, w_ada, b_ada, g_norm1, w_in, g_qa, w_uq, g_kva, w_ukv, g_mla_q, g_mla_k, sc_w, g_na_q, g_na_k, na_rpb, cf_w, cf_b, cf_ln_g, cf_ln_b, w_br, w_o, g_norm2, w_ff_gate, w_ff_up, w_ff_down, w_router, w_e_gate, w_e_up, w_e_down):
    x = (x_prompt.reshape(N_CTX_TOK, D_MODEL), x_sample.reshape(N_LAT_TOK, D_MODEL))
    cvec = jnp.concatenate([c_ctx[None, :], c, jnp.zeros((MOD_ROWS - 1 - DEC_BATCH, D_MODEL), F32)], axis=0)
    cache_na_k2 = cache_na_k.reshape(DEC_BATCH, DEPTH, PAST_LEN, BR_W)
    cache_na_v2 = cache_na_v.reshape(DEC_BATCH, DEPTH, PAST_LEN, BR_W)
    w_router_p = jnp.pad(w_router, ((0, 0), (0, 0), (0, ROUTER_LANES - N_EXPERTS)))

    ckv_l, kpe_l, nak_l, nav_l = [], [], [], []
    for l in range(DEPTH):
        W = _layer_weights(l, w_in, g_qa, w_uq, g_kva, w_ukv, g_mla_q, g_mla_k, sc_w, g_na_q, g_na_k,
                           cf_w, cf_b, cf_ln_g, cf_ln_b, w_br, w_o)
        mod3 = _modulation(cvec, w_ada, b_ada, l).reshape(MOD_ROWS, 6, D_MODEL)
        p = _in_projection(x, mod3, g_norm1, W['w_in_p'], l)

        oa_c, ckv_new, kpe_new = _mla(p, l, False, None, None, W)
        kpe_g = jnp.pad(cache_mla_kpe[:, l], ((0, 0), (0, 0), (MLA_NOPE, HEAD_PAD - MLA_QK)))
        (oa_l,) = _mla(p, l, True, cache_mla_ckv, kpe_g, W)
        oc_c, nak_new, nav_new = _na_context(p, W)
        oc_l = _na_latent(p, l, cache_na_k2, cache_na_v2, _na_bias_table(na_rpb[l]), W)
        ob_c, od_c = _convs(p, False, W)
        ob_l, od_l = _convs(p, True, W)

        x_all = _merge(x, mod3, p, (oa_c, oa_l), (ob_c, ob_l), (oc_c, oc_l), (od_c, od_l), W)

        last = l == DEPTH - 1
        if l % 2 == 0:
            x = (_ffn(x_all, mod3, g_norm2, w_ff_gate, w_ff_up, w_ff_down, l, l // 2),)
        else:
            x = _moe(x_all, mod3, g_norm2, w_router_p, w_e_gate, w_e_up, w_e_down, l, l // 2, last)

        ckv_l.append(ckv_new)
        kpe_l.append(kpe_new)
        nak_l.append(nak_new.reshape(BATCH, SEQ, NA_HEADS, NA_HD))
        nav_l.append(nav_new.reshape(BATCH, SEQ, NA_HEADS, NA_HD))

    if len(x) == 1:
        x = (x[0][:N_CTX_TOK], x[0][N_CTX_TOK:])
    y_prompt = x[0].reshape(BATCH, SEQ, D_MODEL)
    y_sample = x[1].reshape(DEC_BATCH, DEC_SEQ, D_MODEL)
    return (y_prompt, y_sample, jnp.stack(ckv_l, axis=1), jnp.stack(kpe_l, axis=1),
            jnp.stack(nak_l, axis=1), jnp.stack(nav_l, axis=1))
```

```python
import functools

import numpy as np
import jax
import jax.numpy as jnp
from jax import lax
from jax.experimental import pallas as pl
from jax.experimental.pallas import tpu as pltpu

F32 = jnp.float32
BF16 = jnp.bfloat16

D_MODEL = 1024
BATCH = 16
SEQ = 256
DEPTH = 2
DEC_BATCH = 8
DEC_SEQ = 1024
PAST_LEN = 512
GRID_W = 64
N_BRANCH = 4
BR_W = 256
MLA_HEADS = 4
MLA_NOPE = 64
MLA_ROPE = 32
MLA_QK = 96
MLA_V = 64
MLA_Q_LORA = 256
MLA_KV_LORA = 128
SC_K = 3
NA_HEADS = 4
NA_HD = 64
NA_WIN_R = 8
NA_WIN_C = 16
CF_K = 31
D_FF = 3584
N_EXPERTS = 8
ROPE_THETA = 10000.0
EPS = 1e-6

N_CTX_TOK = BATCH * SEQ
N_LAT_TOK = DEC_BATCH * DEC_SEQ
N_TOK = N_CTX_TOK + N_LAT_TOK
MOD_ROWS = 16

HEAD_PAD = 128
MLA_PAD_W = MLA_HEADS * HEAD_PAD

P_QA = 0
P_KVA = 256
P_SC = 512
P_NA = 1280
P_CF = 2048
P_GATE = 2560
GATE_BLK = 512
P_COLS = 6656

VMEM_LIMIT = 56 * 1024 * 1024

NA_TQ = 256
NA_WIN_ROWS = 12
NA_WIN_KEYS = NA_WIN_ROWS * GRID_W


def _cparams(sem):
    return pltpu.CompilerParams(dimension_semantics=sem, vmem_limit_bytes=VMEM_LIMIT)


def _const_spec(shape):
    nd = len(shape)
    return pl.BlockSpec(shape, lambda *_: (0,) * nd)


def _mod_row(tok_start):
    return jnp.where(tok_start < N_CTX_TOK, 0, 1 + (tok_start - N_CTX_TOK) // DEC_SEQ)


def _sigmoid(x):
    return 0.5 * jnp.tanh(0.5 * x) + 0.5


def _silu(x):
    return x * _sigmoid(x)


def _ctx_or_lat(is_ctx, c_ref, l_ref):
    return jnp.where(is_ctx, c_ref[...], l_ref[...])


def _dot(a, b):
    return jnp.dot(a.astype(BF16), b.astype(BF16), preferred_element_type=F32)


def _dot_nt(a, b):
    return lax.dot_general(a.astype(BF16), b.astype(BF16), (((1,), (1,)), ((), ())),
                           preferred_element_type=F32)


def _mod_kernel(c_ref, w_ref, b_ref, o_ref):
    o_ref[...] = _dot(_silu(c_ref[...]), w_ref[...]) + b_ref[...]


def _modulation(cvec, w_ada, b_ada, l):
    tn = 1024
    return pl.pallas_call(
        _mod_kernel,
        out_shape=jax.ShapeDtypeStruct((MOD_ROWS, 6 * D_MODEL), F32),
        grid=(6 * D_MODEL // tn,),
        in_specs=[
            _const_spec((MOD_ROWS, D_MODEL)),
            pl.BlockSpec((None, D_MODEL, tn), lambda j: (l, 0, j)),
            pl.BlockSpec((None, 1, tn), lambda j: (l, 0, j)),
        ],
        out_specs=pl.BlockSpec((MOD_ROWS, tn), lambda j: (0, j)),
        compiler_params=_cparams(("arbitrary",)),
        name="modulation",
    )(cvec, w_ada, b_ada.reshape(DEPTH, 1, 6 * D_MODEL))


def _pair_specs(tm, width, buffers=None):
    nc = N_CTX_TOK // tm
    mode = {} if buffers is None else dict(pipeline_mode=pl.Buffered(buffers))
    return [pl.BlockSpec((tm, width), lambda i, *_: (jnp.minimum(i, nc - 1), 0), **mode),
            pl.BlockSpec((tm, width), lambda i, *_: (jnp.maximum(i - nc, 0), 0), **mode)]


def _pair_shapes(width, dtype):
    return [jax.ShapeDtypeStruct((N_CTX_TOK, width), dtype), jax.ShapeDtypeStruct((N_LAT_TOK, width), dtype)]


def _store_pair(is_ctx, c_ref, l_ref, val):
    @pl.when(is_ctx)
    def _():
        c_ref[...] = val

    @pl.when(jnp.logical_not(is_ctx))
    def _():
        l_ref[...] = val


def _x_specs(x, tm, buffers=None):
    if len(x) == 2:
        return _pair_specs(tm, D_MODEL, buffers)
    mode = {} if buffers is None else dict(pipeline_mode=pl.Buffered(buffers))
    return [pl.BlockSpec((tm, D_MODEL), lambda i, *_: (i, 0), **mode)]


def _x_shapes(split):
    return _pair_shapes(D_MODEL, F32) if split else [jax.ShapeDtypeStruct((N_TOK, D_MODEL), F32)]


def _x_load(x_refs):
    if len(x_refs) == 1:
        return x_refs[0][...]
    tm = x_refs[0].shape[0]
    return _ctx_or_lat(pl.program_id(0) < N_CTX_TOK // tm, *x_refs)


def _x_store(y_refs, val):
    if len(y_refs) == 1:
        y_refs[0][...] = val
    else:
        tm = y_refs[0].shape[0]
        _store_pair(pl.program_id(0) < N_CTX_TOK // tm, *y_refs, val)


def _inproj_kernel(nx, *refs):
    x_refs, (mod_ref, g_ref, w_ref, o_ref, h_scr) = refs[:nx], refs[nx:]

    @pl.when(pl.program_id(1) == 0)
    def _():
        x = _x_load(x_refs)
        y = x * lax.rsqrt(jnp.mean(x * x, axis=-1, keepdims=True) + EPS) * g_ref[...]
        h_scr[...] = (y * (1.0 + mod_ref[1:2, :]) + mod_ref[0:1, :]).astype(BF16)

    o_ref[...] = jnp.dot(h_scr[...], w_ref[...], preferred_element_type=F32).astype(o_ref.dtype)


def _in_projection(x, mod3, g_norm1, w_in_p, l):
    tm, tn = 512, P_COLS // 2
    return pl.pallas_call(
        functools.partial(_inproj_kernel, len(x)),
        out_shape=jax.ShapeDtypeStruct((N_TOK, P_COLS), BF16),
        grid=(N_TOK // tm, P_COLS // tn),
        in_specs=_x_specs(x, tm) + [
            pl.BlockSpec((None, 6, D_MODEL), lambda i, j: (_mod_row(i * tm), 0, 0)),
            pl.BlockSpec((None, 1, D_MODEL), lambda i, j: (l, 0, 0)),
            pl.BlockSpec((D_MODEL, tn), lambda i, j: (0, j)),
        ],
        out_specs=pl.BlockSpec((tm, tn), lambda i, j: (i, j)),
        scratch_shapes=[pltpu.VMEM((tm, D_MODEL), BF16)],
        compiler_params=_cparams(("arbitrary", "arbitrary")),
        name="in_projection",
    )(*x, mod3, g_norm1.reshape(DEPTH, 1, D_MODEL), w_in_p)


def _rope_tables():
    t = np.arange(DEC_SEQ)
    nf = MLA_ROPE // 4
    inv = (np.float32(ROPE_THETA) ** (-np.arange(nf, dtype=np.float32) / np.float32(nf))).astype(np.float32)
    ang_r = (t // GRID_W).astype(np.float32)[:, None] * inv[None, :]
    ang_c = (t % GRID_W).astype(np.float32)[:, None] * inv[None, :]
    c = np.zeros((DEC_SEQ, HEAD_PAD), np.float32)
    s1 = np.zeros((DEC_SEQ, HEAD_PAD), np.float32)
    s2 = np.zeros((DEC_SEQ, HEAD_PAD), np.float32)
    c[:, :MLA_NOPE] = 1.0
    for base, ang in ((MLA_NOPE, ang_r), (MLA_NOPE + 2 * nf, ang_c)):
        c[:, base:base + nf] = np.cos(ang)
        c[:, base + nf:base + 2 * nf] = np.cos(ang)
        s1[:, base:base + nf] = -np.sin(ang)
        s2[:, base + nf:base + 2 * nf] = np.sin(ang)
    return c, s1, s2


def _rope(x, c, s1, s2):
    nf = MLA_ROPE // 4
    return x * c + pltpu.roll(x, HEAD_PAD - nf, 1) * s1 + pltpu.roll(x, nf, 1) * s2


def _head_norm(xh, g):
    ms = jnp.sum(xh * xh, axis=-1, keepdims=True) * (1.0 / MLA_QK)
    return xh * lax.rsqrt(ms + EPS) * g


def _mla_kernel(latent, tq, *refs):
    if latent:
        (pqa_ref, pkva_ref, cckv_ref, ckpe_ref, rc_ref, rs1_ref, rs2_ref,
         gqa_ref, wuq_ref, gkva_ref, wk_ref, wv_ref, gq_ref, gk_ref,
         o_ref, k_scr, v_scr) = refs
    else:
        (pqa_ref, pkva_ref,
         gqa_ref, wuq_ref, gkva_ref, wk_ref, wv_ref, gq_ref, gk_ref,
         o_ref, ckv_ref, kpe_ref, k_scr, v_scr) = refs
    qi = pl.program_id(1)
    n_past = PAST_LEN if latent else 0

    vlane = lax.broadcasted_iota(jnp.int32, (1, MLA_PAD_W), 1)
    v_ones = jnp.where(vlane % HEAD_PAD == MLA_V, 1.0, 0.0)

    def put_kv(ckvn, kpe_g, row0, rope):
        n = ckvn.shape[0]
        kk = _dot(ckvn, wk_ref[...])
        v_scr[row0:row0 + n, :] = (_dot(ckvn, wv_ref[...]) + v_ones).astype(BF16)
        for h in range(MLA_HEADS):
            sl = slice(h * HEAD_PAD, (h + 1) * HEAD_PAD)
            kh = _head_norm(kk[:, sl] + kpe_g, gk_ref[...])
            if rope:
                kh = _rope(kh, rc_ref[...], rs1_ref[...], rs2_ref[...])
            k_scr[row0:row0 + n, sl] = kh.astype(BF16)

    @pl.when(qi == 0)
    def _():
        if latent:
            put_kv(cckv_ref[...], ckpe_ref[...], 0, False)
        kva = pkva_ref[...].astype(F32)
        ckv = kva[:, :MLA_KV_LORA]
        ckvn = ckv * lax.rsqrt(jnp.mean(ckv * ckv, axis=-1, keepdims=True) + EPS) * gkva_ref[...]
        if not latent:
            ckv_ref[...] = ckvn
            kpe_ref[...] = kva[:, MLA_KV_LORA:MLA_KV_LORA + MLA_ROPE]
        put_kv(ckvn, pltpu.roll(kva[:, MLA_KV_LORA:], MLA_NOPE, 1), n_past, latent)

    qa = pqa_ref[...].astype(F32)
    qan = qa * lax.rsqrt(jnp.mean(qa * qa, axis=-1, keepdims=True) + EPS) * gqa_ref[...]
    q = _dot(qan, wuq_ref[...])
    scale = MLA_QK ** -0.5
    for h in range(MLA_HEADS):
        sl = slice(h * HEAD_PAD, (h + 1) * HEAD_PAD)
        qh = _head_norm(q[:, sl], gq_ref[...])
        if latent:
            rows = pl.ds(pl.multiple_of(qi * tq, tq), tq)
            qh = _rope(qh, rc_ref[rows, :], rs1_ref[rows, :], rs2_ref[rows, :])
        s = _dot_nt(qh * scale, k_scr[:, sl])
        m = jnp.max(s, axis=-1, keepdims=True)
        e = jnp.exp((s - m).astype(BF16))
        o = jnp.dot(e, v_scr[:, sl], preferred_element_type=F32)
        o_ref[:, sl] = (o / o[:, MLA_V:MLA_V + 1]).astype(o_ref.dtype)


def _mla(p, l, latent, cache_ckv, cache_kpe_g, W):
    if latent:
        nb, s, tq, row_off = DEC_BATCH, DEC_SEQ, 256, N_CTX_TOK
    else:
        nb, s, tq, row_off = BATCH, SEQ, 256, 0
    nq = s // tq
    sk = s + (PAST_LEN if latent else 0)
    in_specs = [
        pl.BlockSpec((tq, MLA_Q_LORA), lambda b, qi: (row_off // tq + b * nq + qi, P_QA // MLA_Q_LORA)),
        pl.BlockSpec((s, 256), lambda b, qi: (row_off // s + b, P_KVA // 256)),
    ]
    args = [p, p]
    if latent:
        in_specs += [
            pl.BlockSpec((None, None, PAST_LEN, MLA_KV_LORA), lambda b, qi: (b, l, 0, 0)),
            pl.BlockSpec((None, PAST_LEN, HEAD_PAD), lambda b, qi: (b, 0, 0)),
            _const_spec((DEC_SEQ, HEAD_PAD)), _const_spec((DEC_SEQ, HEAD_PAD)), _const_spec((DEC_SEQ, HEAD_PAD)),
        ]
        args += [cache_ckv, cache_kpe_g] + [jnp.asarray(t) for t in _rope_tables()]
    in_specs += [
        _const_spec((1, MLA_Q_LORA)), _const_spec((MLA_Q_LORA, MLA_PAD_W)), _const_spec((1, MLA_KV_LORA)),
        _const_spec((MLA_KV_LORA, MLA_PAD_W)), _const_spec((MLA_KV_LORA, MLA_PAD_W)),
        _const_spec((1, HEAD_PAD)), _const_spec((1, HEAD_PAD)),
    ]
    args += [W['g_qa'], W['w_uq_p'], W['g_kva'], W['w_ukv_k'], W['w_ukv_v'], W['g_mla_q_p'], W['g_mla_k_p']]
    out_shape = [jax.ShapeDtypeStruct((nb * s, MLA_PAD_W), BF16)]
    out_specs = [pl.BlockSpec((tq, MLA_PAD_W), lambda b, qi: (b * nq + qi, 0))]
    if not latent:
        out_shape.append(jax.ShapeDtypeStruct((nb, s, MLA_KV_LORA), F32))
        out_specs.append(pl.BlockSpec((None, s, MLA_KV_LORA), lambda b, qi: (b, 0, 0)))
        out_shape.append(jax.ShapeDtypeStruct((nb, s, MLA_ROPE), F32))
        out_specs.append(pl.BlockSpec((None, s, MLA_ROPE), lambda b, qi: (b, 0, 0)))
    return pl.pallas_call(
        functools.partial(_mla_kernel, latent, tq),
        out_shape=out_shape,
        grid=(nb, nq),
        in_specs=in_specs,
        out_specs=out_specs,
        scratch_shapes=[pltpu.VMEM((sk, MLA_PAD_W), BF16), pltpu.VMEM((sk, MLA_PAD_W), BF16)],
        compiler_params=_cparams(("arbitrary", "arbitrary")),
        name="mla_latent" if latent else "mla_context",
    )(*args)


def _head_masks(width):
    lane = lax.broadcasted_iota(jnp.int32, (1, width), 1)
    return [(lane >= h * NA_HD) & (lane < (h + 1) * NA_HD) for h in range(NA_HEADS)]


def _group_norm64(x, g, masks):
    x2 = x * x
    inv = jnp.zeros_like(x)
    for m in masks:
        ms = jnp.sum(jnp.where(m, x2, 0.0), axis=-1, keepdims=True) * (1.0 / NA_HD)
        inv = jnp.where(m, lax.rsqrt(ms + EPS), inv)
    return x * inv * g


def _p_blocks(col0, n, rows, row_fn):
    def spec(k):
        return pl.BlockSpec((rows, BR_W), lambda *g: (row_fn(*g), col0 // BR_W + k))
    return [spec(k) for k in range(n)]


def _na_ctx_kernel(pq_ref, pk_ref, pv_ref, gq_ref, gk_ref, o_ref, k_ref, v_ref):
    masks = _head_masks(BR_W)
    qn = _group_norm64(pq_ref[...].astype(F32), gq_ref[...], masks)
    kn = _group_norm64(pk_ref[...].astype(F32), gk_ref[...], masks)
    v = pv_ref[...].astype(F32)
    k_ref[...] = kn
    v_ref[...] = v
    scale = NA_HD ** -0.5
    acc = jnp.zeros((pq_ref.shape[0], BR_W), F32)
    for m in masks:
        s = _dot_nt(jnp.where(m, qn, 0.0), kn) * scale
        mx = jnp.max(s, axis=-1, keepdims=True)
        e = jnp.exp(s - mx)
        den = jnp.sum(e, axis=-1, keepdims=True)
        acc = acc + jnp.where(m, _dot(e, v) / den, 0.0)
    o_ref[...] = acc.astype(o_ref.dtype)


def _na_context(p, W):
    s = SEQ
    return pl.pallas_call(
        _na_ctx_kernel,
        out_shape=[jax.ShapeDtypeStruct((N_CTX_TOK, BR_W), BF16),
                   jax.ShapeDtypeStruct((N_CTX_TOK, BR_W), F32),
                   jax.ShapeDtypeStruct((N_CTX_TOK, BR_W), F32)],
        grid=(BATCH,),
        in_specs=_p_blocks(P_NA, 3, s, lambda b: b) + [_const_spec((1, BR_W)), _const_spec((1, BR_W))],
        out_specs=[pl.BlockSpec((s, BR_W), lambda b: (b, 0))] * 3,
        compiler_params=_cparams(("arbitrary",)),
        name="na_context",
    )(p, p, p, W['g_na_q_t'], W['g_na_k_t'])


def _na_lat_kernel(pq_ref, pk_ref, pv_ref, ck_ref, cv_ref, bias_ref, gq_ref, gk_ref, o_ref,
                   q_scr, k_scr, v_scr, kc_scr, vc_scr):
    j = pl.program_id(1)
    masks = _head_masks(BR_W)

    zero = jnp.zeros((), BF16)
    one = jnp.ones((), BF16)
    lane = lax.broadcasted_iota(jnp.int32, (1, BR_W), 1)
    den_lanes = [((h + 1) % NA_HEADS) * NA_HD for h in range(NA_HEADS)]

    @pl.when(j == 0)
    def _():
        scale = NA_HD ** -0.5
        q_scr[...] = (_group_norm64(pq_ref[...].astype(F32), gq_ref[...], masks) * scale).astype(BF16)
        k_scr[...] = _group_norm64(pk_ref[...].astype(F32), gk_ref[...], masks).astype(BF16)
        kc_scr[...] = ck_ref[...].astype(BF16)
        v = pv_ref[...]
        vc = cv_ref[...].astype(BF16)
        for h, m in enumerate(masks):
            v_scr[h] = jnp.where(lane == den_lanes[h], one, jnp.where(m, v, zero))
            vc_scr[h] = jnp.where(lane == den_lanes[h], one, jnp.where(m, vc, zero))

    win0 = pl.multiple_of(jnp.where(j < 2, 0, DEC_SEQ - NA_WIN_KEYS), 256)
    q = q_scr[pl.ds(pl.multiple_of(j * NA_TQ, NA_TQ), NA_TQ), :]
    kw = k_scr[pl.ds(win0, NA_WIN_KEYS), :]
    kc = kc_scr[...]
    acc = jnp.zeros((NA_TQ, BR_W), F32)
    for h, m in enumerate(masks):
        qm = jnp.where(m, q, zero)
        s_loc = _dot_nt(qm, kw) + bias_ref[h]
        s_ctx = _dot_nt(qm, kc)
        mx = jnp.maximum(jnp.max(s_loc, axis=-1, keepdims=True), jnp.max(s_ctx, axis=-1, keepdims=True))
        e_loc = jnp.exp((s_loc - mx).astype(BF16))
        e_ctx = jnp.exp((s_ctx - mx).astype(BF16))
        o = (jnp.dot(e_loc, v_scr[h, pl.ds(win0, NA_WIN_KEYS), :], preferred_element_type=F32)
             + jnp.dot(e_ctx, vc_scr[h], preferred_element_type=F32))
        acc = acc + jnp.where(m, o / o[:, den_lanes[h]:den_lanes[h] + 1], 0.0)
    o_ref[...] = acc.astype(o_ref.dtype)


def _na_bias_table(rpb):
    n_dr, n_dc = 2 * NA_WIN_R - 1, 2 * NA_WIN_C - 1
    cols = np.arange(GRID_W)
    dc = np.clip(cols[None, :] - cols[:, None], -(NA_WIN_C - 1), NA_WIN_C - 1) + (NA_WIN_C - 1)
    cstart = np.clip(cols - NA_WIN_C // 2, 0, GRID_W - NA_WIN_C)
    col_ok = (cols[None, :] >= cstart[:, None]) & (cols[None, :] < cstart[:, None] + NA_WIN_C)
    place = (dc.reshape(-1)[None, :] == np.arange(n_dc)[:, None]).astype(np.float32)
    blocks = jnp.dot(rpb.reshape(NA_HEADS * n_dr, n_dc), place, precision=lax.Precision.HIGHEST)
    blocks = jnp.where(col_ok.reshape(-1)[None, :], blocks, -jnp.inf).reshape(NA_HEADS, n_dr, GRID_W, GRID_W)
    neg = jnp.full((NA_HEADS, 1, GRID_W, GRID_W), -jnp.inf, F32)
    blocks = jnp.concatenate([blocks, neg], axis=1)

    nt = DEC_SEQ // NA_TQ
    n_rows = DEC_SEQ // GRID_W
    rows_per_tile = NA_TQ // GRID_W
    sel = np.full((nt, rows_per_tile, NA_WIN_ROWS), n_dr, np.int32)
    for j in range(nt):
        win_row0 = 0 if j < nt // 2 else n_rows - NA_WIN_ROWS
        for rq in range(rows_per_tile):
            r = j * rows_per_tile + rq
            start = min(max(r - NA_WIN_R // 2, 0), n_rows - NA_WIN_R)
            for kr in range(NA_WIN_ROWS):
                if start <= win_row0 + kr < start + NA_WIN_R:
                    sel[j, rq, kr] = win_row0 + kr - r + (NA_WIN_R - 1)
    tiles = []
    for j in range(nt):
        rows = [jnp.concatenate([blocks[:, sel[j, rq, kr]] for kr in range(NA_WIN_ROWS)], axis=-1)
                for rq in range(rows_per_tile)]
        tiles.append(jnp.concatenate(rows, axis=1))
    return jnp.stack(tiles, axis=1)


def _na_latent(p, l, cache_k, cache_v, bias, W):
    s = DEC_SEQ
    nt = s // NA_TQ
    row_off = N_CTX_TOK
    return pl.pallas_call(
        _na_lat_kernel,
        out_shape=jax.ShapeDtypeStruct((N_LAT_TOK, BR_W), BF16),
        grid=(DEC_BATCH, nt),
        in_specs=_p_blocks(P_NA, 3, s, lambda b, j: row_off // s + b) + [
                  pl.BlockSpec((None, None, PAST_LEN, BR_W), lambda b, j: (b, l, 0, 0)),
                  pl.BlockSpec((None, None, PAST_LEN, BR_W), lambda b, j: (b, l, 0, 0)),
                  pl.BlockSpec((NA_HEADS, None, NA_TQ, NA_WIN_KEYS), lambda b, j: (0, j, 0, 0)),
                  _const_spec((1, BR_W)), _const_spec((1, BR_W))],
        out_specs=pl.BlockSpec((NA_TQ, BR_W), lambda b, j: (b * nt + j, 0)),
        scratch_shapes=[pltpu.VMEM((s, BR_W), BF16), pltpu.VMEM((s, BR_W), BF16),
                        pltpu.VMEM((NA_HEADS, s, BR_W), BF16),
                        pltpu.VMEM((PAST_LEN, BR_W), BF16), pltpu.VMEM((NA_HEADS, PAST_LEN, BR_W), BF16)],
        compiler_params=_cparams(("arbitrary", "arbitrary")),
        name="na_latent",
    )(p, p, p, cache_k, cache_v, bias, W['g_na_q_t'], W['g_na_k_t'])


CONV_HALO = 16
CONV_CHUNK = 128


SUBLANES = 8


def _dwconv_from_pad(pad_ref, w_ref, ksize, s, emit, shift_ref=None):
    half = ksize // 2
    if shift_ref is not None:
        n_rows = s + 2 * CONV_HALO - SUBLANES
        for p in range(1, SUBLANES):
            for c0 in range(0, n_rows, CONV_CHUNK):
                n = min(CONV_CHUNK, n_rows - c0)
                shift_ref[p - 1, c0:c0 + n, :] = pad_ref[c0 + p:c0 + p + n, :]
    for c0 in range(0, s, CONV_CHUNK):
        acc = jnp.zeros((CONV_CHUNK, BR_W), F32)
        for k in range(ksize):
            r0 = CONV_HALO + c0 + k - half
            p = r0 % SUBLANES
            if shift_ref is None or p == 0:
                win = pad_ref[r0:r0 + CONV_CHUNK, :]
            else:
                win = shift_ref[p - 1, r0 - p:r0 - p + CONV_CHUNK, :]
            acc = acc + win * w_ref[k:k + 1, :]
        emit(c0, acc)


def _conv_kernel(s, scb_ref, scc_ref, scx_ref, cfa_ref, cfb2_ref, scw_ref, cfw_ref, cfb_ref, lng_ref, lnb_ref,
                 ob_ref, od_ref, pad_ref, shift_ref):
    zeros = jnp.zeros((CONV_HALO, BR_W), F32)
    pad_ref[0:CONV_HALO, :] = zeros
    pad_ref[CONV_HALO + s:2 * CONV_HALO + s, :] = zeros

    pad_ref[CONV_HALO:CONV_HALO + s, :] = scc_ref[...].astype(F32) * scx_ref[...].astype(F32)

    def emit_b(c0, acc):
        ob_ref[c0:c0 + CONV_CHUNK, :] = (scb_ref[c0:c0 + CONV_CHUNK, :].astype(F32) * acc).astype(ob_ref.dtype)

    _dwconv_from_pad(pad_ref, scw_ref, SC_K, s, emit_b)

    pad_ref[CONV_HALO:CONV_HALO + s, :] = cfa_ref[...].astype(F32) * _sigmoid(cfb2_ref[...].astype(F32))

    def emit_d(c0, acc):
        u = acc + cfb_ref[...]
        mu = jnp.mean(u, axis=-1, keepdims=True)
        d = u - mu
        var = jnp.mean(d * d, axis=-1, keepdims=True)
        y = d * lax.rsqrt(var + EPS) * lng_ref[...] + lnb_ref[...]
        od_ref[c0:c0 + CONV_CHUNK, :] = _silu(y).astype(od_ref.dtype)

    _dwconv_from_pad(pad_ref, cfw_ref, CF_K, s, emit_d, shift_ref)


def _convs(p, latent, W):
    if latent:
        nb, s, row_off = DEC_BATCH, DEC_SEQ, N_CTX_TOK
    else:
        nb, s, row_off = BATCH, SEQ, 0
    return pl.pallas_call(
        functools.partial(_conv_kernel, s),
        out_shape=[jax.ShapeDtypeStruct((nb * s, BR_W), BF16)] * 2,
        grid=(nb,),
        in_specs=(_p_blocks(P_SC, 3, s, lambda b: row_off // s + b)
                  + _p_blocks(P_CF, 2, s, lambda b: row_off // s + b)
                  + [_const_spec((SC_K, BR_W)), _const_spec((CF_K, BR_W)),
                     _const_spec((1, BR_W)), _const_spec((1, BR_W)), _const_spec((1, BR_W))]),
        out_specs=[pl.BlockSpec((s, BR_W), lambda b: (b, 0))] * 2,
        scratch_shapes=[pltpu.VMEM((s + 2 * CONV_HALO, BR_W), F32),
                        pltpu.VMEM((SUBLANES - 1, s + 2 * CONV_HALO, BR_W), F32)],
        compiler_params=_cparams(("arbitrary",)),
        name="convs_latent" if latent else "convs_context",
    )(p, p, p, p, p, W['sc_w'], W['cf_w'], W['cf_b'], W['cf_ln_g'], W['cf_ln_b'])


def _merge_kernel(nc, nx, *refs):
    x_refs = refs[:nx]
    (mod_ref, oac_ref, oal_ref, obc_ref, obl_ref, occ_ref, ocl_ref, odc_ref, odl_ref) = refs[nx:nx + 9]
    gate_refs = refs[nx + 9:nx + 9 + 2 * N_BRANCH]
    wa_ref, wb_ref, wo_ref, y_ref = refs[nx + 9 + 2 * N_BRANCH:]
    is_ctx = pl.program_id(0) < nc

    def gate(n):
        return jnp.concatenate([_sigmoid(gate_refs[2 * n][...]), _sigmoid(gate_refs[2 * n + 1][...])],
                               axis=1).astype(F32)

    merged = gate(0) * jnp.dot(_ctx_or_lat(is_ctx, oac_ref, oal_ref), wa_ref[...], preferred_element_type=F32)
    for n, (oc_ref, ol_ref) in enumerate(((obc_ref, obl_ref), (occ_ref, ocl_ref), (odc_ref, odl_ref))):
        merged = merged + gate(n + 1) * jnp.dot(_ctx_or_lat(is_ctx, oc_ref, ol_ref), wb_ref[n],
                                                preferred_element_type=F32)
    y_ref[...] = _x_load(x_refs) + mod_ref[2:3, :] * _dot(merged, wo_ref[...])


def _merge(x, mod3, p, o_a, o_b, o_c, o_d, W):
    tm = 512
    gate_spec = lambda k: pl.BlockSpec((tm, GATE_BLK), lambda i: (i, P_GATE // GATE_BLK + k))
    n_gate_blk = N_BRANCH * D_MODEL // GATE_BLK
    return pl.pallas_call(
        functools.partial(_merge_kernel, N_CTX_TOK // tm, len(x)),
        out_shape=jax.ShapeDtypeStruct((N_TOK, D_MODEL), F32),
        grid=(N_TOK // tm,),
        in_specs=(_x_specs(x, tm)
                  + [pl.BlockSpec((None, 6, D_MODEL), lambda i: (_mod_row(i * tm), 0, 0))]
                  + _pair_specs(tm, MLA_PAD_W) + _pair_specs(tm, BR_W) + _pair_specs(tm, BR_W) + _pair_specs(tm, BR_W)
                  + [gate_spec(k) for k in range(n_gate_blk)]
                  + [_const_spec((MLA_PAD_W, D_MODEL)), _const_spec((3, BR_W, D_MODEL)),
                     _const_spec((D_MODEL, D_MODEL))]),
        out_specs=pl.BlockSpec((tm, D_MODEL), lambda i: (i, 0)),
        compiler_params=_cparams(("arbitrary",)),
        name="merge",
    )(*x, mod3, *o_a, *o_b, *o_c, *o_d, *([p] * n_gate_blk), W['w_br_a'], W['w_br_bcd'], W['w_o'])


def _norm2(x, g, mod_ref):
    y = x * lax.rsqrt(jnp.mean(x * x, axis=-1, keepdims=True) + EPS) * g
    return y * (1.0 + mod_ref[4:5, :]) + mod_ref[3:4, :]


def _ffn_kernel(x_ref, mod_ref, g_ref, wg_ref, wu_ref, wd_ref, y_ref, h_scr, acc_scr):
    f = pl.program_id(1)

    @pl.when(f == 0)
    def _():
        h_scr[...] = _norm2(x_ref[...], g_ref[...], mod_ref).astype(BF16)
        acc_scr[...] = jnp.zeros_like(acc_scr)

    h = h_scr[...]
    a = _silu(_dot(h, wg_ref[...])) * _dot(h, wu_ref[...])
    acc_scr[...] += _dot(a, wd_ref[...])

    @pl.when(f == pl.num_programs(1) - 1)
    def _():
        y_ref[...] = x_ref[...] + mod_ref[5:6, :] * acc_scr[...]


def _ffn(x_all, mod3, g_norm2, w_g, w_u, w_d, l, j):
    tm, tf = 1024, 512
    return pl.pallas_call(
        _ffn_kernel,
        out_shape=jax.ShapeDtypeStruct((N_TOK, D_MODEL), F32),
        grid=(N_TOK // tm, D_FF // tf),
        in_specs=[pl.BlockSpec((tm, D_MODEL), lambda i, f: (i, 0)),
                  pl.BlockSpec((None, 6, D_MODEL), lambda i, f: (_mod_row(i * tm), 0, 0)),
                  pl.BlockSpec((None, 1, D_MODEL), lambda i, f: (l, 0, 0)),
                  pl.BlockSpec((None, D_MODEL, tf), lambda i, f: (j, 0, f)),
                  pl.BlockSpec((None, D_MODEL, tf), lambda i, f: (j, 0, f)),
                  pl.BlockSpec((None, tf, D_MODEL), lambda i, f: (j, f, 0))],
        out_specs=pl.BlockSpec((tm, D_MODEL), lambda i, f: (i, 0)),
        scratch_shapes=[pltpu.VMEM((tm, D_MODEL), BF16), pltpu.VMEM((tm, D_MODEL), F32)],
        compiler_params=_cparams(("arbitrary", "arbitrary")),
        name="ffn_dense",
    )(x_all, mod3, g_norm2.reshape(DEPTH, 1, D_MODEL), w_g, w_u, w_d)


ROUTER_LANES = 128
ROUTE_TM = 512
MOE_TM = 1024
MOE_SUB = 256
MOE_TF = 512
RUN_ALIGN = 8
MOE_TILES = -(-(2 * N_TOK + (N_TOK // ROUTE_TM) * N_EXPERTS * RUN_ALIGN + N_EXPERTS * MOE_TM) // MOE_TM)
MOE_ROWS = MOE_TILES * MOE_TM
R_I1, R_I2, R_W1, R_W2, R_RANK1, R_RANK2 = range(6)


def _route_top2(h, wr):
    h_hi = h.astype(BF16)
    h_lo = (h - h_hi.astype(F32)).astype(BF16)
    w_hi = wr.astype(BF16)
    w_lo = (wr - w_hi.astype(F32)).astype(BF16)
    logits = (jnp.dot(h_hi, w_hi, preferred_element_type=F32) + jnp.dot(h_lo, w_hi, preferred_element_type=F32)
              + jnp.dot(h_hi, w_lo, preferred_element_type=F32))
    lane = lax.broadcasted_iota(jnp.int32, logits.shape, 1).astype(F32)
    neg = jnp.float32(-jnp.inf)
    logits = jnp.where(lane < N_EXPERTS, logits, neg)
    m1 = jnp.max(logits, axis=-1, keepdims=True)
    i1 = jnp.min(jnp.where(logits == m1, lane, float(ROUTER_LANES)), axis=-1, keepdims=True)
    rest = jnp.where(lane == i1, neg, logits)
    m2 = jnp.max(rest, axis=-1, keepdims=True)
    i2 = jnp.min(jnp.where(rest == m2, lane, float(ROUTER_LANES)), axis=-1, keepdims=True)
    e2 = jnp.exp(m2 - m1)
    return lane, i1, i2, 1.0 / (1.0 + e2), e2 / (1.0 + e2)


def _route_kernel(x_ref, mod_ref, g_ref, wr_ref, route_ref, cnt_ref, tcarry_ref, carry_scr):
    @pl.when(pl.program_id(0) == 0)
    def _():
        carry_scr[...] = jnp.zeros_like(carry_scr)

    tcarry_ref[...] = carry_scr[...]

    h = _norm2(x_ref[...], g_ref[...], mod_ref)
    lane, i1, i2, w1, w2 = _route_top2(h, wr_ref[...])
    tm = h.shape[0]
    oh1 = lane == i1
    oh2 = lane == i2
    oh = jnp.where(oh1, 1.0, 0.0) + jnp.where(oh2, 1.0, 0.0)
    r = lax.broadcasted_iota(jnp.int32, (tm, tm), 0)
    c = lax.broadcasted_iota(jnp.int32, (tm, tm), 1)
    lower = jnp.where(r > c, 1.0, 0.0).astype(BF16)
    before = jnp.dot(lower, oh.astype(BF16), preferred_element_type=F32)
    rank1 = jnp.sum(jnp.where(oh1, before, 0.0), axis=-1, keepdims=True)
    rank2 = jnp.sum(jnp.where(oh2, before, 0.0), axis=-1, keepdims=True)
    carry_scr[...] += jnp.sum(oh, axis=0, keepdims=True)
    rec = jnp.zeros_like(lane)
    for k, v in ((R_I1, i1), (R_I2, i2), (R_W1, w1), (R_W2, w2), (R_RANK1, rank1), (R_RANK2, rank2)):
        rec = jnp.where(lane == float(k), v, rec)
    route_ref[...] = rec
    cnt_ref[...] = carry_scr[...]


def _moe_route(x_all, mod3, g_norm2, w_router_p, l, j):
    tm = ROUTE_TM
    return pl.pallas_call(
        _route_kernel,
        out_shape=[jax.ShapeDtypeStruct((N_TOK, ROUTER_LANES), F32),
                   jax.ShapeDtypeStruct((1, ROUTER_LANES), F32),
                   jax.ShapeDtypeStruct((N_TOK // tm, 1, ROUTER_LANES), F32)],
        grid=(N_TOK // tm,),
        in_specs=[pl.BlockSpec((tm, D_MODEL), lambda i: (i, 0)),
                  pl.BlockSpec((None, 6, D_MODEL), lambda i: (_mod_row(i * tm), 0, 0)),
                  pl.BlockSpec((None, 1, D_MODEL), lambda i: (l, 0, 0)),
                  pl.BlockSpec((None, D_MODEL, ROUTER_LANES), lambda i: (j, 0, 0))],
        out_specs=[pl.BlockSpec((tm, ROUTER_LANES), lambda i: (i, 0)),
                   pl.BlockSpec((1, ROUTER_LANES), lambda i: (0, 0)),
                   pl.BlockSpec((None, 1, ROUTER_LANES), lambda i: (i, 0, 0))],
        scratch_shapes=[pltpu.VMEM((1, ROUTER_LANES), F32)],
        compiler_params=_cparams(("arbitrary",)),
        name="moe_route",
    )(x_all, mod3, g_norm2.reshape(DEPTH, 1, D_MODEL), w_router_p)


ZERO_ROWS = 512
N_GAPS = N_EXPERTS + 1


def _bit_chunks(n_units, max_units, make, wait=False):
    for b in range(max_units.bit_length()):
        units = 1 << b

        @pl.when((n_units & units) != 0)
        def _():
            cp = make(pl.multiple_of((n_units & (units - 1)) * RUN_ALIGN, RUN_ALIGN), units * RUN_ALIGN)
            if wait:
                cp.wait()
            else:
                cp.start()


def _dispatch_kernel(n_ref, lo_ref, dst_ref, gap0_ref, gapn_ref, x_ref, mod_ref, g_ref, route_ref, lovec_ref,
                     xs_ref, z_scr, zero_scr, sems, zsem):
    i = pl.program_id(0)
    tm = x_ref.shape[0]
    h = _norm2(x_ref[...], g_ref[...], mod_ref).astype(BF16)
    rec = route_ref[...]
    lane = lax.broadcasted_iota(jnp.int32, rec.shape, 1).astype(F32)
    lo_row = lovec_ref[...]

    def local_pos(i_lane, r_lane):
        lp = (jnp.sum(jnp.where(lane == rec[:, i_lane:i_lane + 1], lo_row, 0.0), axis=-1, keepdims=True)
              + rec[:, r_lane:r_lane + 1])
        return jnp.transpose(jnp.broadcast_to(lp, (tm, ROUTER_LANES)))[0:1, :]

    slot = lax.broadcasted_iota(jnp.int32, (z_scr.shape[1], tm), 0).astype(F32)
    perm = jnp.where((slot == local_pos(R_I1, R_RANK1)) | (slot == local_pos(R_I2, R_RANK2)), 1.0, 0.0)
    buf = i % 2
    z_scr[buf] = jnp.dot(perm.astype(BF16), h, preferred_element_type=F32)

    def run_copies(tile, b, wait):
        for e in range(N_EXPERTS):
            k = tile * N_EXPERTS + e
            src0 = lo_ref[k]
            dst0 = dst_ref[k]

            def make(off, rows, src0=src0, dst0=dst0):
                return pltpu.make_async_copy(
                    z_scr.at[b, pl.ds(pl.multiple_of(src0 + off, RUN_ALIGN), rows), :],
                    xs_ref.at[pl.ds(pl.multiple_of(dst0 + off, RUN_ALIGN), rows), :], sems.at[b])

            _bit_chunks(n_ref[k], tm // RUN_ALIGN, make, wait)

    run_copies(i, buf, False)

    @pl.when(i > 0)
    def _():
        run_copies(i - 1, 1 - buf, True)

    @pl.when(i == pl.num_programs(0) - 1)
    def _():
        run_copies(i, buf, True)

    @pl.when(i == pl.num_programs(0) - 1)
    def _():
        zero_scr[...] = jnp.zeros_like(zero_scr)
        for wait in (False, True):
            for g in range(N_GAPS):
                start = gap0_ref[g]
                n = gapn_ref[g]
                max_rows = MOE_TM if g < N_EXPERTS else MOE_ROWS - 2 * N_TOK
                for c in range(max_rows // ZERO_ROWS):
                    @pl.when(n >= (c + 1) * ZERO_ROWS)
                    def _():
                        cp = pltpu.make_async_copy(
                            zero_scr,
                            xs_ref.at[pl.ds(pl.multiple_of(start + c * ZERO_ROWS, RUN_ALIGN), ZERO_ROWS), :], zsem)
                        if wait:
                            cp.wait()
                        else:
                            cp.start()

                def make(off, rows, start=start, n=n):
                    tail0 = start + (n // ZERO_ROWS) * ZERO_ROWS
                    return pltpu.make_async_copy(
                        zero_scr.at[pl.ds(0, rows), :],
                        xs_ref.at[pl.ds(pl.multiple_of(tail0 + off, RUN_ALIGN), rows), :], zsem)

                _bit_chunks((n % ZERO_ROWS) // RUN_ALIGN, ZERO_ROWS // RUN_ALIGN - 1, make, wait)


def _moe_dispatch(n_tile, lo_tile, dst_tile, gap0, gapn, x_all, mod3, g_norm2, route, lo_vec, l):
    tm = ROUTE_TM
    spec = lambda shape, fn: pl.BlockSpec(shape, lambda i, *_: fn(i))
    return pl.pallas_call(
        _dispatch_kernel,
        out_shape=jax.ShapeDtypeStruct((MOE_ROWS, D_MODEL), F32),
        grid_spec=pltpu.PrefetchScalarGridSpec(
            num_scalar_prefetch=5,
            grid=(N_TOK // tm,),
            in_specs=[spec((tm, D_MODEL), lambda i: (i, 0)),
                      spec((None, 6, D_MODEL), lambda i: (_mod_row(i * tm), 0, 0)),
                      spec((None, 1, D_MODEL), lambda i: (l, 0, 0)),
                      spec((tm, ROUTER_LANES), lambda i: (i, 0)),
                      spec((None, 1, ROUTER_LANES), lambda i: (i, 0, 0))],
            out_specs=pl.BlockSpec(memory_space=pl.ANY),
            scratch_shapes=[pltpu.VMEM((2, 2 * tm + N_EXPERTS * RUN_ALIGN, D_MODEL), F32),
                            pltpu.VMEM((ZERO_ROWS, D_MODEL), F32),
                            pltpu.SemaphoreType.DMA((2,)), pltpu.SemaphoreType.DMA],
        ),
        compiler_params=_cparams(("arbitrary",)),
        name="moe_dispatch",
    )(n_tile, lo_tile, dst_tile, gap0, gapn, x_all, mod3, g_norm2.reshape(DEPTH, 1, D_MODEL), route, lo_vec)


def _gmm_kernel(te_ref, tr_ref, xs_ref, wg_ref, wu_ref, wd_ref, y_ref, wg_scr, wu_scr, wd_scr):
    del te_ref
    g = pl.program_id(0)
    f = pl.program_id(1)
    rows = tr_ref[g]

    @pl.when(f == 0)
    def _():
        y_ref[...] = jnp.zeros_like(y_ref)

    def sub_tile(s, wg, wu, wd):
        sl = slice(s * MOE_SUB, (s + 1) * MOE_SUB)
        x = xs_ref[sl, :].astype(BF16)
        a = (_silu(jnp.dot(x, wg, preferred_element_type=F32)) * jnp.dot(x, wu, preferred_element_type=F32))
        y_ref[sl, :] += jnp.dot(a.astype(BF16), wd, preferred_element_type=F32)

    @pl.when(rows == MOE_TM)
    def _():
        x = xs_ref[...].astype(BF16)
        a = _silu(_dot(x, wg_ref[...])) * _dot(x, wu_ref[...])
        y_ref[...] += _dot(a, wd_ref[...])

    @pl.when((rows > 0) & (rows < MOE_TM))
    def _():
        wg = wg_ref[...].astype(BF16)
        wu = wu_ref[...].astype(BF16)
        wd = wd_ref[...].astype(BF16)
        wg_scr[...] = wg
        wu_scr[...] = wu
        wd_scr[...] = wd
        sub_tile(0, wg, wu, wd)

    for s in range(1, MOE_TM // MOE_SUB):
        @pl.when((rows > s * MOE_SUB) & (rows < MOE_TM))
        def _():
            sub_tile(s, wg_scr[...], wu_scr[...], wd_scr[...])


def _moe_gmm(tile_expert, tile_rows, xs, w_g, w_u, w_d, j):
    tm, tf = MOE_TM, MOE_TF
    nf = D_FF // tf

    def f_eff(g, f, tr):
        return jnp.where(tr[g] > 0, f, nf - 1)

    return pl.pallas_call(
        _gmm_kernel,
        out_shape=jax.ShapeDtypeStruct((MOE_ROWS, D_MODEL), F32),
        grid_spec=pltpu.PrefetchScalarGridSpec(
            num_scalar_prefetch=2,
            grid=(MOE_TILES, nf),
            in_specs=[pl.BlockSpec((tm, D_MODEL), lambda g, f, te, tr: (g, 0)),
                      pl.BlockSpec((None, None, D_MODEL, tf), lambda g, f, te, tr: (j, te[g], 0, f_eff(g, f, tr))),
                      pl.BlockSpec((None, None, D_MODEL, tf), lambda g, f, te, tr: (j, te[g], 0, f_eff(g, f, tr))),
                      pl.BlockSpec((None, None, tf, D_MODEL), lambda g, f, te, tr: (j, te[g], f_eff(g, f, tr), 0))],
            out_specs=pl.BlockSpec((tm, D_MODEL), lambda g, f, te, tr: (g, 0)),
            scratch_shapes=[pltpu.VMEM((D_MODEL, tf), BF16), pltpu.VMEM((D_MODEL, tf), BF16),
                            pltpu.VMEM((tf, D_MODEL), BF16)],
        ),
        compiler_params=_cparams(("arbitrary", "arbitrary")),
        name="moe_experts",
    )(tile_expert, tile_rows, xs, w_g, w_u, w_d)


ZY_ROWS = 1152


def _combine_kernel(n_out, n_ref, lo_ref, dst_ref, x_ref, mod_ref, route_ref, lovec_ref, y_ref, *refs):
    o_refs, (zy_scr, sems) = refs[:n_out], refs[n_out:]
    i = pl.program_id(0)
    tm = x_ref.shape[0]

    def run_copies(tile, b, wait):
        for e in range(N_EXPERTS):
            k = tile * N_EXPERTS + e
            src0 = dst_ref[k]
            dst0 = lo_ref[k]

            def make(off, rows, src0=src0, dst0=dst0):
                return pltpu.make_async_copy(
                    y_ref.at[pl.ds(pl.multiple_of(src0 + off, RUN_ALIGN), rows), :],
                    zy_scr.at[b, pl.ds(pl.multiple_of(dst0 + off, RUN_ALIGN), rows), :], sems.at[b])

            _bit_chunks(n_ref[k], tm // RUN_ALIGN, make, wait)

    @pl.when(i == 0)
    def _():
        zy_scr[...] = jnp.zeros_like(zy_scr)
        run_copies(0, 0, False)

    buf = i % 2

    @pl.when(i + 1 < pl.num_programs(0))
    def _():
        run_copies(i + 1, 1 - buf, False)

    run_copies(i, buf, True)

    rec = route_ref[...]
    lane = lax.broadcasted_iota(jnp.int32, rec.shape, 1).astype(F32)
    lo_row = lovec_ref[...]

    def local_pos(i_lane, r_lane):
        return (jnp.sum(jnp.where(lane == rec[:, i_lane:i_lane + 1], lo_row, 0.0), axis=-1, keepdims=True)
                + rec[:, r_lane:r_lane + 1])

    slot = lax.broadcasted_iota(jnp.int32, (tm, ZY_ROWS), 1).astype(F32)
    wsel = (jnp.where(slot == local_pos(R_I1, R_RANK1), rec[:, R_W1:R_W1 + 1], 0.0)
            + jnp.where(slot == local_pos(R_I2, R_RANK2), rec[:, R_W2:R_W2 + 1], 0.0))
    w_hi = wsel.astype(BF16)
    w_lo = (wsel - w_hi.astype(F32)).astype(BF16)
    z = zy_scr[buf].astype(BF16)
    mix = jnp.dot(w_hi, z, preferred_element_type=F32) + jnp.dot(w_lo, z, preferred_element_type=F32)
    _x_store(o_refs, x_ref[...] + mod_ref[5:6, :] * mix)


def _moe_combine(n_tile, lo_tile, dst_tile, x_all, mod3, route, lo_vec, y, split_out):
    tm = ROUTE_TM
    spec = lambda shape, fn: pl.BlockSpec(shape, lambda i, *_: fn(i))
    return pl.pallas_call(
        functools.partial(_combine_kernel, 2 if split_out else 1),
        out_shape=_x_shapes(split_out),
        grid_spec=pltpu.PrefetchScalarGridSpec(
            num_scalar_prefetch=3,
            grid=(N_TOK // tm,),
            in_specs=[spec((tm, D_MODEL), lambda i: (i, 0)),
                      spec((None, 6, D_MODEL), lambda i: (_mod_row(i * tm), 0, 0)),
                      spec((tm, ROUTER_LANES), lambda i: (i, 0)),
                      spec((None, 1, ROUTER_LANES), lambda i: (i, 0, 0)),
                      pl.BlockSpec(memory_space=pl.ANY)],
            out_specs=_x_specs((None,) * (2 if split_out else 1), tm),
            scratch_shapes=[pltpu.VMEM((2, ZY_ROWS, D_MODEL), F32), pltpu.SemaphoreType.DMA((2,))],
        ),
        compiler_params=_cparams(("arbitrary",)),
        name="moe_combine",
    )(n_tile, lo_tile, dst_tile, x_all, mod3, route, lo_vec, y)


def _moe(x_all, mod3, g_norm2, w_router_p, w_g, w_u, w_d, l, j, split_out):
    route, cnt, tcarry = _moe_route(x_all, mod3, g_norm2, w_router_p, l, j)
    cnt = cnt[0, :N_EXPERTS].astype(jnp.int32)
    carry = tcarry[:, 0, :N_EXPERTS].astype(jnp.int32)
    n_tile = jnp.concatenate([carry[1:], cnt[None, :]], axis=0) - carry
    n_tile = (n_tile + RUN_ALIGN - 1) // RUN_ALIGN * RUN_ALIGN
    lo_tile = jnp.cumsum(n_tile, axis=1) - n_tile
    carry = jnp.cumsum(n_tile, axis=0) - n_tile
    cnt = jnp.sum(n_tile, axis=0)
    padded = (cnt + MOE_TM - 1) // MOE_TM * MOE_TM
    ends = jnp.cumsum(padded)
    offs = ends - padded
    dst_tile = offs[None, :] + carry
    experts = jnp.arange(N_EXPERTS, dtype=jnp.int32)
    gap0 = jnp.concatenate([offs + cnt, ends[-1:]])
    gapn = jnp.concatenate([padded - cnt, MOE_ROWS - ends[-1:]])
    lo_vec = jnp.pad(lo_tile.astype(F32), ((0, 0), (0, ROUTER_LANES - N_EXPERTS)))[:, None, :]

    tile_start = jnp.arange(MOE_TILES, dtype=jnp.int32) * MOE_TM
    last_tile = jnp.maximum(ends[-1] - MOE_TM, 0)
    owner_start = jnp.minimum(tile_start, last_tile)
    tile_expert = jnp.minimum(jnp.sum(owner_start[:, None] >= ends[None, :], axis=1), N_EXPERTS - 1).astype(jnp.int32)
    group_end = jnp.sum(jnp.where(tile_expert[:, None] == experts[None, :], (offs + cnt)[None, :], 0), axis=1)
    tile_rows = jnp.where(tile_start < ends[-1], jnp.clip(group_end - tile_start, 0, MOE_TM), 0).astype(jnp.int32)

    i32 = lambda a: a.reshape(-1).astype(jnp.int32)
    runs = (i32(n_tile // RUN_ALIGN), i32(lo_tile), i32(dst_tile))
    xs = _moe_dispatch(*runs, i32(gap0), i32(gapn), x_all, mod3, g_norm2, route, lo_vec, l)
    y = _moe_gmm(tile_expert, tile_rows, xs, w_g, w_u, w_d, j)
    return tuple(_moe_combine(*runs, x_all, mod3, route, lo_vec, y, split_out))


def _pad_heads(w, per_head, lo):
    lead = w.shape[:-1]
    w = w.reshape(lead + (MLA_HEADS, per_head))
    w = jnp.pad(w, [(0, 0)] * len(lead) + [(0, 0), (lo, HEAD_PAD - lo - per_head)])
    return w.reshape(lead + (MLA_PAD_W,))


def _layer_weights(l, w_in, g_qa, w_uq, g_kva, w_ukv, g_mla_q, g_mla_k, sc_w, g_na_q, g_na_k,
                   cf_w, cf_b, cf_ln_g, cf_ln_b, w_br, w_o):
    s3 = MLA_Q_LORA + MLA_KV_LORA + MLA_ROPE
    w_in_p = jnp.concatenate([
        w_in[l, :, :s3], jnp.zeros((D_MODEL, P_SC - s3), F32), w_in[l, :, s3:]], axis=1).astype(BF16)
    ukv = w_ukv[l].reshape(MLA_KV_LORA, MLA_HEADS, MLA_NOPE + MLA_V)
    W = dict(
        w_in_p=w_in_p,
        g_qa=g_qa[l][None], g_kva=g_kva[l][None],
        w_uq_p=_pad_heads(w_uq[l], MLA_QK, 0).astype(BF16),
        w_ukv_k=_pad_heads(ukv[:, :, :MLA_NOPE].reshape(MLA_KV_LORA, -1), MLA_NOPE, 0).astype(BF16),
        w_ukv_v=_pad_heads(ukv[:, :, MLA_NOPE:].reshape(MLA_KV_LORA, -1), MLA_V, 0).astype(BF16),
        g_mla_q_p=jnp.pad(g_mla_q[l], (0, HEAD_PAD - MLA_QK))[None],
        g_mla_k_p=jnp.pad(g_mla_k[l], (0, HEAD_PAD - MLA_QK))[None],
        g_na_q_t=jnp.tile(g_na_q[l], NA_HEADS)[None], g_na_k_t=jnp.tile(g_na_k[l], NA_HEADS)[None],
        sc_w=sc_w[l], cf_w=cf_w[l], cf_b=cf_b[l][None], cf_ln_g=cf_ln_g[l][None], cf_ln_b=cf_ln_b[l][None],
        w_br_a=jnp.pad(w_br[l, 0].reshape(MLA_HEADS, MLA_V, D_MODEL),
                       ((0, 0), (0, HEAD_PAD - MLA_V), (0, 0))).reshape(MLA_PAD_W, D_MODEL).astype(BF16),
        w_br_bcd=w_br[l, 1:].astype(BF16),
        w_o=w_o[l].astype(BF16),
    )
    return W


def kernel(x_prompt, x_sample, cache_mla_ckv, cache_mla_kpe, cache_na_k, cache_na_v, c, c_ctx, w_ada, b_ada, g_norm1, w_in, g_qa, w_uq, g_kva, w_ukv, g_mla_q, g_mla_k, sc_w, g_na_q, g_na_k, na_rpb, cf_w, cf_b, cf_ln_g, cf_ln_b, w_br, w_o, g_norm2, w_ff_gate, w_ff_up, w_ff_down, w_router, w_e_gate, w_e_up, w_e_down):
    x = (x_prompt.reshape(N_CTX_TOK, D_MODEL), x_sample.reshape(N_LAT_TOK, D_MODEL))
    cvec = jnp.concatenate([c_ctx[None, :], c, jnp.zeros((MOD_ROWS - 1 - DEC_BATCH, D_MODEL), F32)], axis=0)
    cache_na_k2 = cache_na_k.reshape(DEC_BATCH, DEPTH, PAST_LEN, BR_W)
    cache_na_v2 = cache_na_v.reshape(DEC_BATCH, DEPTH, PAST_LEN, BR_W)
    w_router_p = jnp.pad(w_router, ((0, 0), (0, 0), (0, ROUTER_LANES - N_EXPERTS)))

    ckv_l, kpe_l, nak_l, nav_l = [], [], [], []
    for l in range(DEPTH):
        W = _layer_weights(l, w_in, g_qa, w_uq, g_kva, w_ukv, g_mla_q, g_mla_k, sc_w, g_na_q, g_na_k,
                           cf_w, cf_b, cf_ln_g, cf_ln_b, w_br, w_o)
        mod3 = _modulation(cvec, w_ada, b_ada, l).reshape(MOD_ROWS, 6, D_MODEL)
        p = _in_projection(x, mod3, g_norm1, W['w_in_p'], l)

        oa_c, ckv_new, kpe_new = _mla(p, l, False, None, None, W)
        kpe_g = jnp.pad(cache_mla_kpe[:, l], ((0, 0), (0, 0), (MLA_NOPE, HEAD_PAD - MLA_QK)))
        (oa_l,) = _mla(p, l, True, cache_mla_ckv, kpe_g, W)
        oc_c, nak_new, nav_new = _na_context(p, W)
        oc_l = _na_latent(p, l, cache_na_k2, cache_na_v2, _na_bias_table(na_rpb[l]), W)
        ob_c, od_c = _convs(p, False, W)
        ob_l, od_l = _convs(p, True, W)

        x_all = _merge(x, mod3, p, (oa_c, oa_l), (ob_c, ob_l), (oc_c, oc_l), (od_c, od_l), W)

        last = l == DEPTH - 1
        if l % 2 == 0:
            x = (_ffn(x_all, mod3, g_norm2, w_ff_gate, w_ff_up, w_ff_down, l, l // 2),)
        else:
            x = _moe(x_all, mod3, g_norm2, w_router_p, w_e_gate, w_e_up, w_e_down, l, l // 2, last)

        ckv_l.append(ckv_new)
        kpe_l.append(kpe_new)
        nak_l.append(nak_new.reshape(BATCH, SEQ, NA_HEADS, NA_HD))
        nav_l.append(nav_new.reshape(BATCH, SEQ, NA_HEADS, NA_HD))

    if len(x) == 1:
        x = (x[0][:N_CTX_TOK], x[0][N_CTX_TOK:])
    y_prompt = x[0].reshape(BATCH, SEQ, D_MODEL)
    y_sample = x[1].reshape(DEC_BATCH, DEC_SEQ, D_MODEL)
    return (y_prompt, y_sample, jnp.stack(ckv_l, axis=1), jnp.stack(kpe_l, axis=1),
            jnp.stack(nak_l, axis=1), jnp.stack(nav_l, axis=1))
```

```python
import functools

import numpy as np
import jax
import jax.numpy as jnp
from jax import lax
from jax.experimental import pallas as pl
from jax.experimental.pallas import tpu as pltpu

F32 = jnp.float32
BF16 = jnp.bfloat16

D_MODEL = 1024
BATCH = 16
SEQ = 256
DEPTH = 2
DEC_BATCH = 8
DEC_SEQ = 1024
PAST_LEN = 512
GRID_W = 64
N_BRANCH = 4
BR_W = 256
MLA_HEADS = 4
MLA_NOPE = 64
MLA_ROPE = 32
MLA_QK = 96
MLA_V = 64
MLA_Q_LORA = 256
MLA_KV_LORA = 128
SC_K = 3
NA_HEADS = 4
NA_HD = 64
NA_WIN_R = 8
NA_WIN_C = 16
CF_K = 31
D_FF = 3584
N_EXPERTS = 8
ROPE_THETA = 10000.0
EPS = 1e-6

N_CTX_TOK = BATCH * SEQ
N_LAT_TOK = DEC_BATCH * DEC_SEQ
N_TOK = N_CTX_TOK + N_LAT_TOK
MOD_ROWS = 16

HEAD_PAD = 128
MLA_PAD_W = MLA_HEADS * HEAD_PAD

P_QA = 0
P_KVA = 256
P_SC = 512
P_NA = 1280
P_CF = 2048
P_GATE = 2560
GATE_BLK = 512
P_COLS = 6656

VMEM_LIMIT = 56 * 1024 * 1024

NA_TQ = 256
NA_WIN_ROWS = 12
NA_WIN_KEYS = NA_WIN_ROWS * GRID_W


def _cparams(sem):
    return pltpu.CompilerParams(dimension_semantics=sem, vmem_limit_bytes=VMEM_LIMIT)


def _const_spec(shape):
    nd = len(shape)
    return pl.BlockSpec(shape, lambda *_: (0,) * nd)


def _mod_row(tok_start):
    return jnp.where(tok_start < N_CTX_TOK, 0, 1 + (tok_start - N_CTX_TOK) // DEC_SEQ)


def _sigmoid(x):
    return 0.5 * jnp.tanh(0.5 * x) + 0.5


def _silu(x):
    return x * _sigmoid(x)


def _ctx_or_lat(is_ctx, c_ref, l_ref):
    return jnp.where(is_ctx, c_ref[...], l_ref[...])


def _dot(a, b):
    return jnp.dot(a.astype(BF16), b.astype(BF16), preferred_element_type=F32)


def _dot_nt(a, b):
    return lax.dot_general(a.astype(BF16), b.astype(BF16), (((1,), (1,)), ((), ())),
                           preferred_element_type=F32)


def _mod_kernel(c_ref, w_ref, b_ref, o_ref):
    o_ref[...] = _dot(_silu(c_ref[...]), w_ref[...]) + b_ref[...]


def _modulation(cvec, w_ada, b_ada, l):
    tn = 1024
    return pl.pallas_call(
        _mod_kernel,
        out_shape=jax.ShapeDtypeStruct((MOD_ROWS, 6 * D_MODEL), F32),
        grid=(6 * D_MODEL // tn,),
        in_specs=[
            _const_spec((MOD_ROWS, D_MODEL)),
            pl.BlockSpec((None, D_MODEL, tn), lambda j: (l, 0, j)),
            pl.BlockSpec((None, 1, tn), lambda j: (l, 0, j)),
        ],
        out_specs=pl.BlockSpec((MOD_ROWS, tn), lambda j: (0, j)),
        compiler_params=_cparams(("arbitrary",)),
        name="modulation",
    )(cvec, w_ada, b_ada.reshape(DEPTH, 1, 6 * D_MODEL))


def _pair_specs(tm, width, buffers=None):
    nc = N_CTX_TOK // tm
    mode = {} if buffers is None else dict(pipeline_mode=pl.Buffered(buffers))
    return [pl.BlockSpec((tm, width), lambda i, *_: (jnp.minimum(i, nc - 1), 0), **mode),
            pl.BlockSpec((tm, width), lambda i, *_: (jnp.maximum(i - nc, 0), 0), **mode)]


def _pair_shapes(width, dtype):
    return [jax.ShapeDtypeStruct((N_CTX_TOK, width), dtype), jax.ShapeDtypeStruct((N_LAT_TOK, width), dtype)]


def _store_pair(is_ctx, c_ref, l_ref, val):
    @pl.when(is_ctx)
    def _():
        c_ref[...] = val

    @pl.when(jnp.logical_not(is_ctx))
    def _():
        l_ref[...] = val


def _x_specs(x, tm, buffers=None):
    if len(x) == 2:
        return _pair_specs(tm, D_MODEL, buffers)
    mode = {} if buffers is None else dict(pipeline_mode=pl.Buffered(buffers))
    return [pl.BlockSpec((tm, D_MODEL), lambda i, *_: (i, 0), **mode)]


def _x_shapes(split):
    return _pair_shapes(D_MODEL, F32) if split else [jax.ShapeDtypeStruct((N_TOK, D_MODEL), F32)]


def _x_load(x_refs):
    if len(x_refs) == 1:
        return x_refs[0][...]
    tm = x_refs[0].shape[0]
    return _ctx_or_lat(pl.program_id(0) < N_CTX_TOK // tm, *x_refs)


def _x_store(y_refs, val):
    if len(y_refs) == 1:
        y_refs[0][...] = val
    else:
        tm = y_refs[0].shape[0]
        _store_pair(pl.program_id(0) < N_CTX_TOK // tm, *y_refs, val)


def _inproj_kernel(nx, *refs):
    x_refs, (mod_ref, g_ref, w_ref, o_ref, h_scr) = refs[:nx], refs[nx:]

    @pl.when(pl.program_id(1) == 0)
    def _():
        x = _x_load(x_refs)
        y = x * lax.rsqrt(jnp.mean(x * x, axis=-1, keepdims=True) + EPS) * g_ref[...]
        h_scr[...] = (y * (1.0 + mod_ref[1:2, :]) + mod_ref[0:1, :]).astype(BF16)

    o_ref[...] = jnp.dot(h_scr[...], w_ref[...], preferred_element_type=F32).astype(o_ref.dtype)


def _in_projection(x, mod3, g_norm1, w_in_p, l):
    tm, tn = 512, P_COLS // 2
    return pl.pallas_call(
        functools.partial(_inproj_kernel, len(x)),
        out_shape=jax.ShapeDtypeStruct((N_TOK, P_COLS), BF16),
        grid=(N_TOK // tm, P_COLS // tn),
        in_specs=_x_specs(x, tm) + [
            pl.BlockSpec((None, 6, D_MODEL), lambda i, j: (_mod_row(i * tm), 0, 0)),
            pl.BlockSpec((None, 1, D_MODEL), lambda i, j: (l, 0, 0)),
            pl.BlockSpec((D_MODEL, tn), lambda i, j: (0, j)),
        ],
        out_specs=pl.BlockSpec((tm, tn), lambda i, j: (i, j)),
        scratch_shapes=[pltpu.VMEM((tm, D_MODEL), BF16)],
        compiler_params=_cparams(("arbitrary", "arbitrary")),
        name="in_projection",
    )(*x, mod3, g_norm1.reshape(DEPTH, 1, D_MODEL), w_in_p)


def _rope_tables():
    t = np.arange(DEC_SEQ)
    nf = MLA_ROPE // 4
    inv = (np.float32(ROPE_THETA) ** (-np.arange(nf, dtype=np.float32) / np.float32(nf))).astype(np.float32)
    ang_r = (t // GRID_W).astype(np.float32)[:, None] * inv[None, :]
    ang_c = (t % GRID_W).astype(np.float32)[:, None] * inv[None, :]
    c = np.zeros((DEC_SEQ, HEAD_PAD), np.float32)
    s1 = np.zeros((DEC_SEQ, HEAD_PAD), np.float32)
    s2 = np.zeros((DEC_SEQ, HEAD_PAD), np.float32)
    c[:, :MLA_NOPE] = 1.0
    for base, ang in ((MLA_NOPE, ang_r), (MLA_NOPE + 2 * nf, ang_c)):
        c[:, base:base + nf] = np.cos(ang)
        c[:, base + nf:base + 2 * nf] = np.cos(ang)
        s1[:, base:base + nf] = -np.sin(ang)
        s2[:, base + nf:base + 2 * nf] = np.sin(ang)
    return c, s1, s2


def _rope(x, c, s1, s2):
    nf = MLA_ROPE // 4
    return x * c + pltpu.roll(x, HEAD_PAD - nf, 1) * s1 + pltpu.roll(x, nf, 1) * s2


def _head_norm(xh, g):
    ms = jnp.sum(xh * xh, axis=-1, keepdims=True) * (1.0 / MLA_QK)
    return xh * lax.rsqrt(ms + EPS) * g


def _mla_kernel(latent, tq, *refs):
    if latent:
        (pqa_ref, pkva_ref, cckv_ref, ckpe_ref, rc_ref, rs1_ref, rs2_ref,
         gqa_ref, wuq_ref, gkva_ref, wk_ref, wv_ref, gq_ref, gk_ref,
         o_ref, k_scr, v_scr) = refs
    else:
        (pqa_ref, pkva_ref,
         gqa_ref, wuq_ref, gkva_ref, wk_ref, wv_ref, gq_ref, gk_ref,
         o_ref, ckv_ref, kpe_ref, k_scr, v_scr) = refs
    qi = pl.program_id(1)
    n_past = PAST_LEN if latent else 0

    vlane = lax.broadcasted_iota(jnp.int32, (1, MLA_PAD_W), 1)
    v_ones = jnp.where(vlane % HEAD_PAD == MLA_V, 1.0, 0.0)

    def put_kv(ckvn, kpe_g, row0, rope):
        n = ckvn.shape[0]
        kk = _dot(ckvn, wk_ref[...])
        v_scr[row0:row0 + n, :] = (_dot(ckvn, wv_ref[...]) + v_ones).astype(BF16)
        for h in range(MLA_HEADS):
            sl = slice(h * HEAD_PAD, (h + 1) * HEAD_PAD)
            kh = _head_norm(kk[:, sl] + kpe_g, gk_ref[...])
            if rope:
                kh = _rope(kh, rc_ref[...], rs1_ref[...], rs2_ref[...])
            k_scr[row0:row0 + n, sl] = kh.astype(BF16)

    @pl.when(qi == 0)
    def _():
        if latent:
            put_kv(cckv_ref[...], ckpe_ref[...], 0, False)
        kva = pkva_ref[...].astype(F32)
        ckv = kva[:, :MLA_KV_LORA]
        ckvn = ckv * lax.rsqrt(jnp.mean(ckv * ckv, axis=-1, keepdims=True) + EPS) * gkva_ref[...]
        if not latent:
            ckv_ref[...] = ckvn
            kpe_ref[...] = kva[:, MLA_KV_LORA:MLA_KV_LORA + MLA_ROPE]
        put_kv(ckvn, pltpu.roll(kva[:, MLA_KV_LORA:], MLA_NOPE, 1), n_past, latent)

    qa = pqa_ref[...].astype(F32)
    qan = qa * lax.rsqrt(jnp.mean(qa * qa, axis=-1, keepdims=True) + EPS) * gqa_ref[...]
    q = _dot(qan, wuq_ref[...])
    scale = MLA_QK ** -0.5
    for h in range(MLA_HEADS):
        sl = slice(h * HEAD_PAD, (h + 1) * HEAD_PAD)
        qh = _head_norm(q[:, sl], gq_ref[...])
        if latent:
            rows = pl.ds(pl.multiple_of(qi * tq, tq), tq)
            qh = _rope(qh, rc_ref[rows, :], rs1_ref[rows, :], rs2_ref[rows, :])
        s = _dot_nt(qh * scale, k_scr[:, sl])
        m = jnp.max(s, axis=-1, keepdims=True)
        e = jnp.exp((s - m).astype(BF16))
        o = jnp.dot(e, v_scr[:, sl], preferred_element_type=F32)
        o_ref[:, sl] = (o / o[:, MLA_V:MLA_V + 1]).astype(o_ref.dtype)


def _mla(p, l, latent, cache_ckv, cache_kpe_g, W):
    if latent:
        nb, s, tq, row_off = DEC_BATCH, DEC_SEQ, 256, N_CTX_TOK
    else:
        nb, s, tq, row_off = BATCH, SEQ, 256, 0
    nq = s // tq
    sk = s + (PAST_LEN if latent else 0)
    in_specs = [
        pl.BlockSpec((tq, MLA_Q_LORA), lambda b, qi: (row_off // tq + b * nq + qi, P_QA // MLA_Q_LORA)),
        pl.BlockSpec((s, 256), lambda b, qi: (row_off // s + b, P_KVA // 256)),
    ]
    args = [p, p]
    if latent:
        in_specs += [
            pl.BlockSpec((None, None, PAST_LEN, MLA_KV_LORA), lambda b, qi: (b, l, 0, 0)),
            pl.BlockSpec((None, PAST_LEN, HEAD_PAD), lambda b, qi: (b, 0, 0)),
            _const_spec((DEC_SEQ, HEAD_PAD)), _const_spec((DEC_SEQ, HEAD_PAD)), _const_spec((DEC_SEQ, HEAD_PAD)),
        ]
        args += [cache_ckv, cache_kpe_g] + [jnp.asarray(t) for t in _rope_tables()]
    in_specs += [
        _const_spec((1, MLA_Q_LORA)), _const_spec((MLA_Q_LORA, MLA_PAD_W)), _const_spec((1, MLA_KV_LORA)),
        _const_spec((MLA_KV_LORA, MLA_PAD_W)), _const_spec((MLA_KV_LORA, MLA_PAD_W)),
        _const_spec((1, HEAD_PAD)), _const_spec((1, HEAD_PAD)),
    ]
    args += [W['g_qa'], W['w_uq_p'], W['g_kva'], W['w_ukv_k'], W['w_ukv_v'], W['g_mla_q_p'], W['g_mla_k_p']]
    out_shape = [jax.ShapeDtypeStruct((nb * s, MLA_PAD_W), BF16)]
    out_specs = [pl.BlockSpec((tq, MLA_PAD_W), lambda b, qi: (b * nq + qi, 0))]
    if not latent:
        out_shape.append(jax.ShapeDtypeStruct((nb, s, MLA_KV_LORA), F32))
        out_specs.append(pl.BlockSpec((None, s, MLA_KV_LORA), lambda b, qi: (b, 0, 0)))
        out_shape.append(jax.ShapeDtypeStruct((nb, s, MLA_ROPE), F32))
        out_specs.append(pl.BlockSpec((None, s, MLA_ROPE), lambda b, qi: (b, 0, 0)))
    return pl.pallas_call(
        functools.partial(_mla_kernel, latent, tq),
        out_shape=out_shape,
        grid=(nb, nq),
        in_specs=in_specs,
        out_specs=out_specs,
        scratch_shapes=[pltpu.VMEM((sk, MLA_PAD_W), BF16), pltpu.VMEM((sk, MLA_PAD_W), BF16)],
        compiler_params=_cparams(("arbitrary", "arbitrary")),
        name="mla_latent" if latent else "mla_context",
    )(*args)


def _head_masks(width):
    lane = lax.broadcasted_iota(jnp.int32, (1, width), 1)
    return [(lane >= h * NA_HD) & (lane < (h + 1) * NA_HD) for h in range(NA_HEADS)]


def _group_norm64(x, g, masks):
    x2 = x * x
    inv = jnp.zeros_like(x)
    for m in masks:
        ms = jnp.sum(jnp.where(m, x2, 0.0), axis=-1, keepdims=True) * (1.0 / NA_HD)
        inv = jnp.where(m, lax.rsqrt(ms + EPS), inv)
    return x * inv * g


def _p_blocks(col0, n, rows, row_fn):
    def spec(k):
        return pl.BlockSpec((rows, BR_W), lambda *g: (row_fn(*g), col0 // BR_W + k))
    return [spec(k) for k in range(n)]


def _na_ctx_kernel(pq_ref, pk_ref, pv_ref, gq_ref, gk_ref, o_ref, k_ref, v_ref):
    masks = _head_masks(BR_W)
    qn = _group_norm64(pq_ref[...].astype(F32), gq_ref[...], masks)
    kn = _group_norm64(pk_ref[...].astype(F32), gk_ref[...], masks)
    v = pv_ref[...].astype(F32)
    k_ref[...] = kn
    v_ref[...] = v
    scale = NA_HD ** -0.5
    acc = jnp.zeros((pq_ref.shape[0], BR_W), F32)
    for m in masks:
        s = _dot_nt(jnp.where(m, qn, 0.0), kn) * scale
        mx = jnp.max(s, axis=-1, keepdims=True)
        e = jnp.exp(s - mx)
        den = jnp.sum(e, axis=-1, keepdims=True)
        acc = acc + jnp.where(m, _dot(e, v) / den, 0.0)
    o_ref[...] = acc.astype(o_ref.dtype)


def _na_context(p, W):
    s = SEQ
    return pl.pallas_call(
        _na_ctx_kernel,
        out_shape=[jax.ShapeDtypeStruct((N_CTX_TOK, BR_W), BF16),
                   jax.ShapeDtypeStruct((N_CTX_TOK, BR_W), F32),
                   jax.ShapeDtypeStruct((N_CTX_TOK, BR_W), F32)],
        grid=(BATCH,),
        in_specs=_p_blocks(P_NA, 3, s, lambda b: b) + [_const_spec((1, BR_W)), _const_spec((1, BR_W))],
        out_specs=[pl.BlockSpec((s, BR_W), lambda b: (b, 0))] * 3,
        compiler_params=_cparams(("arbitrary",)),
        name="na_context",
    )(p, p, p, W['g_na_q_t'], W['g_na_k_t'])


def _na_lat_kernel(pq_ref, pk_ref, pv_ref, ck_ref, cv_ref, bias_ref, gq_ref, gk_ref, o_ref,
                   q_scr, k_scr, v_scr, kc_scr, vc_scr):
    j = pl.program_id(1)
    masks = _head_masks(BR_W)

    zero = jnp.zeros((), BF16)
    one = jnp.ones((), BF16)
    lane = lax.broadcasted_iota(jnp.int32, (1, BR_W), 1)
    den_lanes = [((h + 1) % NA_HEADS) * NA_HD for h in range(NA_HEADS)]

    @pl.when(j == 0)
    def _():
        scale = NA_HD ** -0.5
        q_scr[...] = (_group_norm64(pq_ref[...].astype(F32), gq_ref[...], masks) * scale).astype(BF16)
        k_scr[...] = _group_norm64(pk_ref[...].astype(F32), gk_ref[...], masks).astype(BF16)
        kc_scr[...] = ck_ref[...].astype(BF16)
        v = pv_ref[...]
        vc = cv_ref[...].astype(BF16)
        for h, m in enumerate(masks):
            v_scr[h] = jnp.where(lane == den_lanes[h], one, jnp.where(m, v, zero))
            vc_scr[h] = jnp.where(lane == den_lanes[h], one, jnp.where(m, vc, zero))

    win0 = pl.multiple_of(jnp.where(j < 2, 0, DEC_SEQ - NA_WIN_KEYS), 256)
    q = q_scr[pl.ds(pl.multiple_of(j * NA_TQ, NA_TQ), NA_TQ), :]
    kw = k_scr[pl.ds(win0, NA_WIN_KEYS), :]
    kc = kc_scr[...]
    acc = jnp.zeros((NA_TQ, BR_W), F32)
    for h, m in enumerate(masks):
        qm = jnp.where(m, q, zero)
        s_loc = _dot_nt(qm, kw) + bias_ref[h]
        s_ctx = _dot_nt(qm, kc)
        mx = jnp.maximum(jnp.max(s_loc, axis=-1, keepdims=True), jnp.max(s_ctx, axis=-1, keepdims=True))
        e_loc = jnp.exp((s_loc - mx).astype(BF16))
        e_ctx = jnp.exp((s_ctx - mx).astype(BF16))
        o = (jnp.dot(e_loc, v_scr[h, pl.ds(win0, NA_WIN_KEYS), :], preferred_element_type=F32)
             + jnp.dot(e_ctx, vc_scr[h], preferred_element_type=F32))
        acc = acc + jnp.where(m, o / o[:, den_lanes[h]:den_lanes[h] + 1], 0.0)
    o_ref[...] = acc.astype(o_ref.dtype)


def _na_bias_table(rpb):
    n_dr, n_dc = 2 * NA_WIN_R - 1, 2 * NA_WIN_C - 1
    cols = np.arange(GRID_W)
    dc = np.clip(cols[None, :] - cols[:, None], -(NA_WIN_C - 1), NA_WIN_C - 1) + (NA_WIN_C - 1)
    cstart = np.clip(cols - NA_WIN_C // 2, 0, GRID_W - NA_WIN_C)
    col_ok = (cols[None, :] >= cstart[:, None]) & (cols[None, :] < cstart[:, None] + NA_WIN_C)
    place = (dc.reshape(-1)[None, :] == np.arange(n_dc)[:, None]).astype(np.float32)
    blocks = jnp.dot(rpb.reshape(NA_HEADS * n_dr, n_dc), place, precision=lax.Precision.HIGHEST)
    blocks = jnp.where(col_ok.reshape(-1)[None, :], blocks, -jnp.inf).reshape(NA_HEADS, n_dr, GRID_W, GRID_W)
    neg = jnp.full((NA_HEADS, 1, GRID_W, GRID_W), -jnp.inf, F32)
    blocks = jnp.concatenate([blocks, neg], axis=1)

    nt = DEC_SEQ // NA_TQ
    n_rows = DEC_SEQ // GRID_W
    rows_per_tile = NA_TQ // GRID_W
    sel = np.full((nt, rows_per_tile, NA_WIN_ROWS), n_dr, np.int32)
    for j in range(nt):
        win_row0 = 0 if j < nt // 2 else n_rows - NA_WIN_ROWS
        for rq in range(rows_per_tile):
            r = j * rows_per_tile + rq
            start = min(max(r - NA_WIN_R // 2, 0), n_rows - NA_WIN_R)
            for kr in range(NA_WIN_ROWS):
                if start <= win_row0 + kr < start + NA_WIN_R:
                    sel[j, rq, kr] = win_row0 + kr - r + (NA_WIN_R - 1)
    tiles = []
    for j in range(nt):
        rows = [jnp.concatenate([blocks[:, sel[j, rq, kr]] for kr in range(NA_WIN_ROWS)], axis=-1)
                for rq in range(rows_per_tile)]
        tiles.append(jnp.concatenate(rows, axis=1))
    return jnp.stack(tiles, axis=1)


def _na_latent(p, l, cache_k, cache_v, bias, W):
    s = DEC_SEQ
    nt = s // NA_TQ
    row_off = N_CTX_TOK
    return pl.pallas_call(
        _na_lat_kernel,
        out_shape=jax.ShapeDtypeStruct((N_LAT_TOK, BR_W), BF16),
        grid=(DEC_BATCH, nt),
        in_specs=_p_blocks(P_NA, 3, s, lambda b, j: row_off // s + b) + [
                  pl.BlockSpec((None, None, PAST_LEN, BR_W), lambda b, j: (b, l, 0, 0)),
                  pl.BlockSpec((None, None, PAST_LEN, BR_W), lambda b, j: (b, l, 0, 0)),
                  pl.BlockSpec((NA_HEADS, None, NA_TQ, NA_WIN_KEYS), lambda b, j: (0, j, 0, 0)),
                  _const_spec((1, BR_W)), _const_spec((1, BR_W))],
        out_specs=pl.BlockSpec((NA_TQ, BR_W), lambda b, j: (b * nt + j, 0)),
        scratch_shapes=[pltpu.VMEM((s, BR_W), BF16), pltpu.VMEM((s, BR_W), BF16),
                        pltpu.VMEM((NA_HEADS, s, BR_W), BF16),
                        pltpu.VMEM((PAST_LEN, BR_W), BF16), pltpu.VMEM((NA_HEADS, PAST_LEN, BR_W), BF16)],
        compiler_params=_cparams(("arbitrary", "arbitrary")),
        name="na_latent",
    )(p, p, p, cache_k, cache_v, bias, W['g_na_q_t'], W['g_na_k_t'])


CONV_HALO = 16
CONV_CHUNK = 128


SUBLANES = 8


def _dwconv_from_pad(pad_ref, w_ref, ksize, s, emit, shift_ref=None):
    half = ksize // 2
    if shift_ref is not None:
        n_rows = s + 2 * CONV_HALO - SUBLANES
        for p in range(1, SUBLANES):
            for c0 in range(0, n_rows, CONV_CHUNK):
                n = min(CONV_CHUNK, n_rows - c0)
                shift_ref[p - 1, c0:c0 + n, :] = pad_ref[c0 + p:c0 + p + n, :]
    for c0 in range(0, s, CONV_CHUNK):
        acc = jnp.zeros((CONV_CHUNK, BR_W), F32)
        for k in range(ksize):
            r0 = CONV_HALO + c0 + k - half
            p = r0 % SUBLANES
            if shift_ref is None or p == 0:
                win = pad_ref[r0:r0 + CONV_CHUNK, :]
            else:
                win = shift_ref[p - 1, r0 - p:r0 - p + CONV_CHUNK, :]
            acc = acc + win * w_ref[k:k + 1, :]
        emit(c0, acc)


def _conv_kernel(s, scb_ref, scc_ref, scx_ref, cfa_ref, cfb2_ref, scw_ref, cfw_ref, cfb_ref, lng_ref, lnb_ref,
                 ob_ref, od_ref, pad_ref, shift_ref):
    zeros = jnp.zeros((CONV_HALO, BR_W), F32)
    pad_ref[0:CONV_HALO, :] = zeros
    pad_ref[CONV_HALO + s:2 * CONV_HALO + s, :] = zeros

    pad_ref[CONV_HALO:CONV_HALO + s, :] = scc_ref[...].astype(F32) * scx_ref[...].astype(F32)

    def emit_b(c0, acc):
        ob_ref[c0:c0 + CONV_CHUNK, :] = (scb_ref[c0:c0 + CONV_CHUNK, :].astype(F32) * acc).astype(ob_ref.dtype)

    _dwconv_from_pad(pad_ref, scw_ref, SC_K, s, emit_b)

    pad_ref[CONV_HALO:CONV_HALO + s, :] = cfa_ref[...].astype(F32) * _sigmoid(cfb2_ref[...].astype(F32))

    def emit_d(c0, acc):
        u = acc + cfb_ref[...]
        mu = jnp.mean(u, axis=-1, keepdims=True)
        d = u - mu
        var = jnp.mean(d * d, axis=-1, keepdims=True)
        y = d * lax.rsqrt(var + EPS) * lng_ref[...] + lnb_ref[...]
        od_ref[c0:c0 + CONV_CHUNK, :] = _silu(y).astype(od_ref.dtype)

    _dwconv_from_pad(pad_ref, cfw_ref, CF_K, s, emit_d, shift_ref)


def _convs(p, latent, W):
    if latent:
        nb, s, row_off = DEC_BATCH, DEC_SEQ, N_CTX_TOK
    else:
        nb, s, row_off = BATCH, SEQ, 0
    return pl.pallas_call(
        functools.partial(_conv_kernel, s),
        out_shape=[jax.ShapeDtypeStruct((nb * s, BR_W), BF16)] * 2,
        grid=(nb,),
        in_specs=(_p_blocks(P_SC, 3, s, lambda b: row_off // s + b)
                  + _p_blocks(P_CF, 2, s, lambda b: row_off // s + b)
                  + [_const_spec((SC_K, BR_W)), _const_spec((CF_K, BR_W)),
                     _const_spec((1, BR_W)), _const_spec((1, BR_W)), _const_spec((1, BR_W))]),
        out_specs=[pl.BlockSpec((s, BR_W), lambda b: (b, 0))] * 2,
        scratch_shapes=[pltpu.VMEM((s + 2 * CONV_HALO, BR_W), F32),
                        pltpu.VMEM((SUBLANES - 1, s + 2 * CONV_HALO, BR_W), F32)],
        compiler_params=_cparams(("arbitrary",)),
        name="convs_latent" if latent else "convs_context",
    )(p, p, p, p, p, W['sc_w'], W['cf_w'], W['cf_b'], W['cf_ln_g'], W['cf_ln_b'])


def _merge_kernel(nc, nx, *refs):
    x_refs = refs[:nx]
    (mod_ref, oac_ref, oal_ref, obc_ref, obl_ref, occ_ref, ocl_ref, odc_ref, odl_ref) = refs[nx:nx + 9]
    gate_refs = refs[nx + 9:nx + 9 + 2 * N_BRANCH]
    wa_ref, wb_ref, wo_ref, y_ref = refs[nx + 9 + 2 * N_BRANCH:]
    is_ctx = pl.program_id(0) < nc

    def gate(n):
        return jnp.concatenate([_sigmoid(gate_refs[2 * n][...]), _sigmoid(gate_refs[2 * n + 1][...])],
                               axis=1).astype(F32)

    merged = gate(0) * jnp.dot(_ctx_or_lat(is_ctx, oac_ref, oal_ref), wa_ref[...], preferred_element_type=F32)
    for n, (oc_ref, ol_ref) in enumerate(((obc_ref, obl_ref), (occ_ref, ocl_ref), (odc_ref, odl_ref))):
        merged = merged + gate(n + 1) * jnp.dot(_ctx_or_lat(is_ctx, oc_ref, ol_ref), wb_ref[n],
                                                preferred_element_type=F32)
    y_ref[...] = _x_load(x_refs) + mod_ref[2:3, :] * _dot(merged, wo_ref[...])


def _merge(x, mod3, p, o_a, o_b, o_c, o_d, W):
    tm = 512
    gate_spec = lambda k: pl.BlockSpec((tm, GATE_BLK), lambda i: (i, P_GATE // GATE_BLK + k))
    n_gate_blk = N_BRANCH * D_MODEL // GATE_BLK
    return pl.pallas_call(
        functools.partial(_merge_kernel, N_CTX_TOK // tm, len(x)),
        out_shape=jax.ShapeDtypeStruct((N_TOK, D_MODEL), F32),
        grid=(N_TOK // tm,),
        in_specs=(_x_specs(x, tm)
                  + [pl.BlockSpec((None, 6, D_MODEL), lambda i: (_mod_row(i * tm), 0, 0))]
                  + _pair_specs(tm, MLA_PAD_W) + _pair_specs(tm, BR_W) + _pair_specs(tm, BR_W) + _pair_specs(tm, BR_W)
                  + [gate_spec(k) for k in range(n_gate_blk)]
                  + [_const_spec((MLA_PAD_W, D_MODEL)), _const_spec((3, BR_W, D_MODEL)),
                     _const_spec((D_MODEL, D_MODEL))]),
        out_specs=pl.BlockSpec((tm, D_MODEL), lambda i: (i, 0)),
        compiler_params=_cparams(("arbitrary",)),
        name="merge",
    )(*x, mod3, *o_a, *o_b, *o_c, *o_d, *([p] * n_gate_blk), W['w_br_a'], W['w_br_bcd'], W['w_o'])


def _norm2(x, g, mod_ref):
    y = x * lax.rsqrt(jnp.mean(x * x, axis=-1, keepdims=True) + EPS) * g
    return y * (1.0 + mod_ref[4:5, :]) + mod_ref[3:4, :]


def _ffn_kernel(x_ref, mod_ref, g_ref, wg_ref, wu_ref, wd_ref, y_ref, h_scr, acc_scr):
    f = pl.program_id(1)

    @pl.when(f == 0)
    def _():
        h_scr[...] = _norm2(x_ref[...], g_ref[...], mod_ref).astype(BF16)
        acc_scr[...] = jnp.zeros_like(acc_scr)

    h = h_scr[...]
    a = _silu(_dot(h, wg_ref[...])) * _dot(h, wu_ref[...])
    acc_scr[...] += _dot(a, wd_ref[...])

    @pl.when(f == pl.num_programs(1) - 1)
    def _():
        y_ref[...] = x_ref[...] + mod_ref[5:6, :] * acc_scr[...]


def _ffn(x_all, mod3, g_norm2, w_g, w_u, w_d, l, j):
    tm, tf = 1024, 512
    return pl.pallas_call(
        _ffn_kernel,
        out_shape=jax.ShapeDtypeStruct((N_TOK, D_MODEL), F32),
        grid=(N_TOK // tm, D_FF // tf),
        in_specs=[pl.BlockSpec((tm, D_MODEL), lambda i, f: (i, 0)),
                  pl.BlockSpec((None, 6, D_MODEL), lambda i, f: (_mod_row(i * tm), 0, 0)),
                  pl.BlockSpec((None, 1, D_MODEL), lambda i, f: (l, 0, 0)),
                  pl.BlockSpec((None, D_MODEL, tf), lambda i, f: (j, 0, f)),
                  pl.BlockSpec((None, D_MODEL, tf), lambda i, f: (j, 0, f)),
                  pl.BlockSpec((None, tf, D_MODEL), lambda i, f: (j, f, 0))],
        out_specs=pl.BlockSpec((tm, D_MODEL), lambda i, f: (i, 0)),
        scratch_shapes=[pltpu.VMEM((tm, D_MODEL), BF16), pltpu.VMEM((tm, D_MODEL), F32)],
        compiler_params=_cparams(("arbitrary", "arbitrary")),
        name="ffn_dense",
    )(x_all, mod3, g_norm2.reshape(DEPTH, 1, D_MODEL), w_g, w_u, w_d)


ROUTER_LANES = 128
ROUTE_TM = 512
MOE_TM = 1024
MOE_SUB = 256
MOE_TF = 512
RUN_ALIGN = 8
MOE_TILES = -(-(2 * N_TOK + (N_TOK // ROUTE_TM) * N_EXPERTS * RUN_ALIGN + N_EXPERTS * MOE_TM) // MOE_TM)
MOE_ROWS = MOE_TILES * MOE_TM
R_I1, R_I2, R_W1, R_W2, R_RANK1, R_RANK2 = range(6)


def _route_top2(h, wr):
    h_hi = h.astype(BF16)
    h_lo = (h - h_hi.astype(F32)).astype(BF16)
    w_hi = wr.astype(BF16)
    w_lo = (wr - w_hi.astype(F32)).astype(BF16)
    logits = (jnp.dot(h_hi, w_hi, preferred_element_type=F32) + jnp.dot(h_lo, w_hi, preferred_element_type=F32)
              + jnp.dot(h_hi, w_lo, preferred_element_type=F32))
    lane = lax.broadcasted_iota(jnp.int32, logits.shape, 1).astype(F32)
    neg = jnp.float32(-jnp.inf)
    logits = jnp.where(lane < N_EXPERTS, logits, neg)
    m1 = jnp.max(logits, axis=-1, keepdims=True)
    i1 = jnp.min(jnp.where(logits == m1, lane, float(ROUTER_LANES)), axis=-1, keepdims=True)
    rest = jnp.where(lane == i1, neg, logits)
    m2 = jnp.max(rest, axis=-1, keepdims=True)
    i2 = jnp.min(jnp.where(rest == m2, lane, float(ROUTER_LANES)), axis=-1, keepdims=True)
    e2 = jnp.exp(m2 - m1)
    return lane, i1, i2, 1.0 / (1.0 + e2), e2 / (1.0 + e2)


def _route_kernel(x_ref, mod_ref, g_ref, wr_ref, route_ref, cnt_ref, tcarry_ref, carry_scr):
    @pl.when(pl.program_id(0) == 0)
    def _():
        carry_scr[...] = jnp.zeros_like(carry_scr)

    tcarry_ref[...] = carry_scr[...]

    h = _norm2(x_ref[...], g_ref[...], mod_ref)
    lane, i1, i2, w1, w2 = _route_top2(h, wr_ref[...])
    tm = h.shape[0]
    oh1 = lane == i1
    oh2 = lane == i2
    oh = jnp.where(oh1, 1.0, 0.0) + jnp.where(oh2, 1.0, 0.0)
    r = lax.broadcasted_iota(jnp.int32, (tm, tm), 0)
    c = lax.broadcasted_iota(jnp.int32, (tm, tm), 1)
    lower = jnp.where(r > c, 1.0, 0.0).astype(BF16)
    before = jnp.dot(lower, oh.astype(BF16), preferred_element_type=F32)
    rank1 = jnp.sum(jnp.where(oh1, before, 0.0), axis=-1, keepdims=True)
    rank2 = jnp.sum(jnp.where(oh2, before, 0.0), axis=-1, keepdims=True)
    carry_scr[...] += jnp.sum(oh, axis=0, keepdims=True)
    rec = jnp.zeros_like(lane)
    for k, v in ((R_I1, i1), (R_I2, i2), (R_W1, w1), (R_W2, w2), (R_RANK1, rank1), (R_RANK2, rank2)):
        rec = jnp.where(lane == float(k), v, rec)
    route_ref[...] = rec
    cnt_ref[...] = carry_scr[...]


def _moe_route(x_all, mod3, g_norm2, w_router_p, l, j):
    tm = ROUTE_TM
    return pl.pallas_call(
        _route_kernel,
        out_shape=[jax.ShapeDtypeStruct((N_TOK, ROUTER_LANES), F32),
                   jax.ShapeDtypeStruct((1, ROUTER_LANES), F32),
                   jax.ShapeDtypeStruct((N_TOK // tm, 1, ROUTER_LANES), F32)],
        grid=(N_TOK // tm,),
        in_specs=[pl.BlockSpec((tm, D_MODEL), lambda i: (i, 0)),
                  pl.BlockSpec((None, 6, D_MODEL), lambda i: (_mod_row(i * tm), 0, 0)),
                  pl.BlockSpec((None, 1, D_MODEL), lambda i: (l, 0, 0)),
                  pl.BlockSpec((None, D_MODEL, ROUTER_LANES), lambda i: (j, 0, 0))],
        out_specs=[pl.BlockSpec((tm, ROUTER_LANES), lambda i: (i, 0)),
                   pl.BlockSpec((1, ROUTER_LANES), lambda i: (0, 0)),
                   pl.BlockSpec((None, 1, ROUTER_LANES), lambda i: (i, 0, 0))],
        scratch_shapes=[pltpu.VMEM((1, ROUTER_LANES), F32)],
        compiler_params=_cparams(("arbitrary",)),
        name="moe_route",
    )(x_all, mod3, g_norm2.reshape(DEPTH, 1, D_MODEL), w_router_p)


ZERO_ROWS = 512
N_GAPS = N_EXPERTS + 1


def _bit_chunks(n_units, max_units, make, wait=False):
    for b in range(max_units.bit_length()):
        units = 1 << b

        @pl.when((n_units & units) != 0)
        def _():
            cp = make(pl.multiple_of((n_units & (units - 1)) * RUN_ALIGN, RUN_ALIGN), units * RUN_ALIGN)
            if wait:
                cp.wait()
            else:
                cp.start()


def _dispatch_kernel(n_ref, lo_ref, dst_ref, gap0_ref, gapn_ref, x_ref, mod_ref, g_ref, route_ref, lovec_ref,
                     xs_ref, z_scr, zero_scr, sems, zsem):
    i = pl.program_id(0)
    tm = x_ref.shape[0]
    h = _norm2(x_ref[...], g_ref[...], mod_ref).astype(BF16)
    rec = route_ref[...]
    lane = lax.broadcasted_iota(jnp.int32, rec.shape, 1).astype(F32)
    lo_row = lovec_ref[...]

    def local_pos(i_lane, r_lane):
        lp = (jnp.sum(jnp.where(lane == rec[:, i_lane:i_lane + 1], lo_row, 0.0), axis=-1, keepdims=True)
              + rec[:, r_lane:r_lane + 1])
        return jnp.transpose(jnp.broadcast_to(lp, (tm, ROUTER_LANES)))[0:1, :]

    slot = lax.broadcasted_iota(jnp.int32, (z_scr.shape[1], tm), 0).astype(F32)
    perm = jnp.where((slot == local_pos(R_I1, R_RANK1)) | (slot == local_pos(R_I2, R_RANK2)), 1.0, 0.0)
    buf = i % 2
    z_scr[buf] = jnp.dot(perm.astype(BF16), h, preferred_element_type=F32)

    def run_copies(tile, b, wait):
        for e in range(N_EXPERTS):
            k = tile * N_EXPERTS + e
            src0 = lo_ref[k]
            dst0 = dst_ref[k]

            def make(off, rows, src0=src0, dst0=dst0):
                return pltpu.make_async_copy(
                    z_scr.at[b, pl.ds(pl.multiple_of(src0 + off, RUN_ALIGN), rows), :],
                    xs_ref.at[pl.ds(pl.multiple_of(dst0 + off, RUN_ALIGN), rows), :], sems.at[b])

            _bit_chunks(n_ref[k], tm // RUN_ALIGN, make, wait)

    run_copies(i, buf, False)

    @pl.when(i > 0)
    def _():
        run_copies(i - 1, 1 - buf, True)

    @pl.when(i == pl.num_programs(0) - 1)
    def _():
        run_copies(i, buf, True)

    @pl.when(i == pl.num_programs(0) - 1)
    def _():
        zero_scr[...] = jnp.zeros_like(zero_scr)
        for wait in (False, True):
            for g in range(N_GAPS):
                start = gap0_ref[g]
                n = gapn_ref[g]
                max_rows = MOE_TM if g < N_EXPERTS else MOE_ROWS - 2 * N_TOK
                for c in range(max_rows // ZERO_ROWS):
                    @pl.when(n >= (c + 1) * ZERO_ROWS)
                    def _():
                        cp = pltpu.make_async_copy(
                            zero_scr,
                            xs_ref.at[pl.ds(pl.multiple_of(start + c * ZERO_ROWS, RUN_ALIGN), ZERO_ROWS), :], zsem)
                        if wait:
                            cp.wait()
                        else:
                            cp.start()

                def make(off, rows, start=start, n=n):
                    tail0 = start + (n // ZERO_ROWS) * ZERO_ROWS
                    return pltpu.make_async_copy(
                        zero_scr.at[pl.ds(0, rows), :],
                        xs_ref.at[pl.ds(pl.multiple_of(tail0 + off, RUN_ALIGN), rows), :], zsem)

                _bit_chunks((n % ZERO_ROWS) // RUN_ALIGN, ZERO_ROWS // RUN_ALIGN - 1, make, wait)


def _moe_dispatch(n_tile, lo_tile, dst_tile, gap0, gapn, x_all, mod3, g_norm2, route, lo_vec, l):
    tm = ROUTE_TM
    spec = lambda shape, fn: pl.BlockSpec(shape, lambda i, *_: fn(i))
    return pl.pallas_call(
        _dispatch_kernel,
        out_shape=jax.ShapeDtypeStruct((MOE_ROWS, D_MODEL), F32),
        grid_spec=pltpu.PrefetchScalarGridSpec(
            num_scalar_prefetch=5,
            grid=(N_TOK // tm,),
            in_specs=[spec((tm, D_MODEL), lambda i: (i, 0)),
                      spec((None, 6, D_MODEL), lambda i: (_mod_row(i * tm), 0, 0)),
                      spec((None, 1, D_MODEL), lambda i: (l, 0, 0)),
                      spec((tm, ROUTER_LANES), lambda i: (i, 0)),
                      spec((None, 1, ROUTER_LANES), lambda i: (i, 0, 0))],
            out_specs=pl.BlockSpec(memory_space=pl.ANY),
            scratch_shapes=[pltpu.VMEM((2, 2 * tm + N_EXPERTS * RUN_ALIGN, D_MODEL), F32),
                            pltpu.VMEM((ZERO_ROWS, D_MODEL), F32),
                            pltpu.SemaphoreType.DMA((2,)), pltpu.SemaphoreType.DMA],
        ),
        compiler_params=_cparams(("arbitrary",)),
        name="moe_dispatch",
    )(n_tile, lo_tile, dst_tile, gap0, gapn, x_all, mod3, g_norm2.reshape(DEPTH, 1, D_MODEL), route, lo_vec)


def _gmm_kernel(te_ref, tr_ref, xs_ref, wg_ref, wu_ref, wd_ref, y_ref, wg_scr, wu_scr, wd_scr):
    del te_ref
    g = pl.program_id(0)
    f = pl.program_id(1)
    rows = tr_ref[g]

    @pl.when(f == 0)
    def _():
        y_ref[...] = jnp.zeros_like(y_ref)

    def sub_tile(s, wg, wu, wd):
        sl = slice(s * MOE_SUB, (s + 1) * MOE_SUB)
        x = xs_ref[sl, :].astype(BF16)
        a = (_silu(jnp.dot(x, wg, preferred_element_type=F32)) * jnp.dot(x, wu, preferred_element_type=F32))
        y_ref[sl, :] += jnp.dot(a.astype(BF16), wd, preferred_element_type=F32)

    @pl.when(rows == MOE_TM)
    def _():
        x = xs_ref[...].astype(BF16)
        a = _silu(_dot(x, wg_ref[...])) * _dot(x, wu_ref[...])
        y_ref[...] += _dot(a, wd_ref[...])

    @pl.when((rows > 0) & (rows < MOE_TM))
    def _():
        wg = wg_ref[...].astype(BF16)
        wu = wu_ref[...].astype(BF16)
        wd = wd_ref[...].astype(BF16)
        wg_scr[...] = wg
        wu_scr[...] = wu
        wd_scr[...] = wd
        sub_tile(0, wg, wu, wd)

    for s in range(1, MOE_TM // MOE_SUB):
        @pl.when((rows > s * MOE_SUB) & (rows < MOE_TM))
        def _():
            sub_tile(s, wg_scr[...], wu_scr[...], wd_scr[...])


def _moe_gmm(tile_expert, tile_rows, xs, w_g, w_u, w_d, j):
    tm, tf = MOE_TM, MOE_TF
    nf = D_FF // tf

    def f_eff(g, f, tr):
        return jnp.where(tr[g] > 0, f, nf - 1)

    return pl.pallas_call(
        _gmm_kernel,
        out_shape=jax.ShapeDtypeStruct((MOE_ROWS, D_MODEL), F32),
        grid_spec=pltpu.PrefetchScalarGridSpec(
            num_scalar_prefetch=2,
            grid=(MOE_TILES, nf),
            in_specs=[pl.BlockSpec((tm, D_MODEL), lambda g, f, te, tr: (g, 0)),
                      pl.BlockSpec((None, None, D_MODEL, tf), lambda g, f, te, tr: (j, te[g], 0, f_eff(g, f, tr))),
                      pl.BlockSpec((None, None, D_MODEL, tf), lambda g, f, te, tr: (j, te[g], 0, f_eff(g, f, tr))),
                      pl.BlockSpec((None, None, tf, D_MODEL), lambda g, f, te, tr: (j, te[g], f_eff(g, f, tr), 0))],
            out_specs=pl.BlockSpec((tm, D_MODEL), lambda g, f, te, tr: (g, 0)),
            scratch_shapes=[pltpu.VMEM((D_MODEL, tf), BF16), pltpu.VMEM((D_MODEL, tf), BF16),
                            pltpu.VMEM((tf, D_MODEL), BF16)],
        ),
        compiler_params=_cparams(("arbitrary", "arbitrary")),
        name="moe_experts",
    )(tile_expert, tile_rows, xs, w_g, w_u, w_d)


ZY_ROWS = 1152


def _combine_kernel(n_out, n_ref, lo_ref, dst_ref, x_ref, mod_ref, route_ref, lovec_ref, y_ref, *refs):
    o_refs, (zy_scr, sems) = refs[:n_out], refs[n_out:]
    i = pl.program_id(0)
    tm = x_ref.shape[0]

    def run_copies(tile, b, wait):
        for e in range(N_EXPERTS):
            k = tile * N_EXPERTS + e
            src0 = dst_ref[k]
            dst0 = lo_ref[k]

            def make(off, rows, src0=src0, dst0=dst0):
                return pltpu.make_async_copy(
                    y_ref.at[pl.ds(pl.multiple_of(src0 + off, RUN_ALIGN), rows), :],
                    zy_scr.at[b, pl.ds(pl.multiple_of(dst0 + off, RUN_ALIGN), rows), :], sems.at[b])

            _bit_chunks(n_ref[k], tm // RUN_ALIGN, make, wait)

    @pl.when(i == 0)
    def _():
        zy_scr[...] = jnp.zeros_like(zy_scr)
        run_copies(0, 0, False)

    buf = i % 2

    @pl.when(i + 1 < pl.num_programs(0))
    def _():
        run_copies(i + 1, 1 - buf, False)

    run_copies(i, buf, True)

    rec = route_ref[...]
    lane = lax.broadcasted_iota(jnp.int32, rec.shape, 1).astype(F32)
    lo_row = lovec_ref[...]

    def local_pos(i_lane, r_lane):
        return (jnp.sum(jnp.where(lane == rec[:, i_lane:i_lane + 1], lo_row, 0.0), axis=-1, keepdims=True)
                + rec[:, r_lane:r_lane + 1])

    slot = lax.broadcasted_iota(jnp.int32, (tm, ZY_ROWS), 1).astype(F32)
    wsel = (jnp.where(slot == local_pos(R_I1, R_RANK1), rec[:, R_W1:R_W1 + 1], 0.0)
            + jnp.where(slot == local_pos(R_I2, R_RANK2), rec[:, R_W2:R_W2 + 1], 0.0))
    w_hi = wsel.astype(BF16)
    w_lo = (wsel - w_hi.astype(F32)).astype(BF16)
    z = zy_scr[buf].astype(BF16)
    mix = jnp.dot(w_hi, z, preferred_element_type=F32) + jnp.dot(w_lo, z, preferred_element_type=F32)
    _x_store(o_refs, x_ref[...] + mod_ref[5:6, :] * mix)


def _moe_combine(n_tile, lo_tile, dst_tile, x_all, mod3, route, lo_vec, y, split_out):
    tm = ROUTE_TM
    spec = lambda shape, fn: pl.BlockSpec(shape, lambda i, *_: fn(i))
    return pl.pallas_call(
        functools.partial(_combine_kernel, 2 if split_out else 1),
        out_shape=_x_shapes(split_out),
        grid_spec=pltpu.PrefetchScalarGridSpec(
            num_scalar_prefetch=3,
            grid=(N_TOK // tm,),
            in_specs=[spec((tm, D_MODEL), lambda i: (i, 0)),
                      spec((None, 6, D_MODEL), lambda i: (_mod_row(i * tm), 0, 0)),
                      spec((tm, ROUTER_LANES), lambda i: (i, 0)),
                      spec((None, 1, ROUTER_LANES), lambda i: (i, 0, 0)),
                      pl.BlockSpec(memory_space=pl.ANY)],
            out_specs=_x_specs((None,) * (2 if split_out else 1), tm),
            scratch_shapes=[pltpu.VMEM((2, ZY_ROWS, D_MODEL), F32), pltpu.SemaphoreType.DMA((2,))],
        ),
        compiler_params=_cparams(("arbitrary",)),
        name="moe_combine",
    )(n_tile, lo_tile, dst_tile, x_all, mod3, route, lo_vec, y)


def _moe(x_all, mod3, g_norm2, w_router_p, w_g, w_u, w_d, l, j, split_out):
    route, cnt, tcarry = _moe_route(x_all, mod3, g_norm2, w_router_p, l, j)
    cnt = cnt[0, :N_EXPERTS].astype(jnp.int32)
    carry = tcarry[:, 0, :N_EXPERTS].astype(jnp.int32)
    n_tile = jnp.concatenate([carry[1:], cnt[None, :]], axis=0) - carry
    n_tile = (n_tile + RUN_ALIGN - 1) // RUN_ALIGN * RUN_ALIGN
    lo_tile = jnp.cumsum(n_tile, axis=1) - n_tile
    carry = jnp.cumsum(n_tile, axis=0) - n_tile
    cnt = jnp.sum(n_tile, axis=0)
    padded = (cnt + MOE_TM - 1) // MOE_TM * MOE_TM
    ends = jnp.cumsum(padded)
    offs = ends - padded
    dst_tile = offs[None, :] + carry
    experts = jnp.arange(N_EXPERTS, dtype=jnp.int32)
    gap0 = jnp.concatenate([offs + cnt, ends[-1:]])
    gapn = jnp.concatenate([padded - cnt, MOE_ROWS - ends[-1:]])
    lo_vec = jnp.pad(lo_tile.astype(F32), ((0, 0), (0, ROUTER_LANES - N_EXPERTS)))[:, None, :]

    tile_start = jnp.arange(MOE_TILES, dtype=jnp.int32) * MOE_TM
    last_tile = jnp.maximum(ends[-1] - MOE_TM, 0)
    owner_start = jnp.minimum(tile_start, last_tile)
    tile_expert = jnp.minimum(jnp.sum(owner_start[:, None] >= ends[None, :], axis=1), N_EXPERTS - 1).astype(jnp.int32)
    group_end = jnp.sum(jnp.where(tile_expert[:, None] == experts[None, :], (offs + cnt)[None, :], 0), axis=1)
    tile_rows = jnp.where(tile_start < ends[-1], jnp.clip(group_end - tile_start, 0, MOE_TM), 0).astype(jnp.int32)

    i32 = lambda a: a.reshape(-1).astype(jnp.int32)
    runs = (i32(n_tile // RUN_ALIGN), i32(lo_tile), i32(dst_tile))
    xs = _moe_dispatch(*runs, i32(gap0), i32(gapn), x_all, mod3, g_norm2, route, lo_vec, l)
    y = _moe_gmm(tile_expert, tile_rows, xs, w_g, w_u, w_d, j)
    return tuple(_moe_combine(*runs, x_all, mod3, route, lo_vec, y, split_out))


def _pad_heads(w, per_head, lo):
    lead = w.shape[:-1]
    w = w.reshape(lead + (MLA_HEADS, per_head))
    w = jnp.pad(w, [(0, 0)] * len(lead) + [(0, 0), (lo, HEAD_PAD - lo - per_head)])
    return w.reshape(lead + (MLA_PAD_W,))


W_IN_HEAD = MLA_Q_LORA + MLA_KV_LORA + MLA_ROPE
W_IN_COLS = W_IN_HEAD + 8 * BR_W + N_BRANCH * D_MODEL


def _w_in_layout_kernel(w_ref, o_ref):
    w = w_ref[...]
    o_ref[:, :W_IN_HEAD] = w[:, :W_IN_HEAD].astype(BF16)
    o_ref[:, W_IN_HEAD:P_SC] = jnp.zeros((w.shape[0], P_SC - W_IN_HEAD), BF16)
    o_ref[:, P_SC:] = w[:, W_IN_HEAD:].astype(BF16)


def _w_in_layout(w_in, l):
    tk = 128
    return pl.pallas_call(
        _w_in_layout_kernel,
        out_shape=jax.ShapeDtypeStruct((D_MODEL, P_COLS), BF16),
        grid=(D_MODEL // tk,),
        in_specs=[pl.BlockSpec((None, tk, W_IN_COLS), lambda k: (l, k, 0))],
        out_specs=pl.BlockSpec((tk, P_COLS), lambda k: (k, 0)),
        compiler_params=_cparams(("arbitrary",)),
        name="w_in_layout",
    )(w_in)


def _layer_weights(l, w_in, g_qa, w_uq, g_kva, w_ukv, g_mla_q, g_mla_k, sc_w, g_na_q, g_na_k,
                   cf_w, cf_b, cf_ln_g, cf_ln_b, w_br, w_o):
    w_in_p = _w_in_layout(w_in, l)
    ukv = w_ukv[l].reshape(MLA_KV_LORA, MLA_HEADS, MLA_NOPE + MLA_V)
    W = dict(
        w_in_p=w_in_p,
        g_qa=g_qa[l][None], g_kva=g_kva[l][None],
        w_uq_p=_pad_heads(w_uq[l], MLA_QK, 0).astype(BF16),
        w_ukv_k=_pad_heads(ukv[:, :, :MLA_NOPE].reshape(MLA_KV_LORA, -1), MLA_NOPE, 0).astype(BF16),
        w_ukv_v=_pad_heads(ukv[:, :, MLA_NOPE:].reshape(MLA_KV_LORA, -1), MLA_V, 0).astype(BF16),
        g_mla_q_p=jnp.pad(g_mla_q[l], (0, HEAD_PAD - MLA_QK))[None],
        g_mla_k_p=jnp.pad(g_mla_k[l], (0, HEAD_PAD - MLA_QK))[None],
        g_na_q_t=jnp.tile(g_na_q[l], NA_HEADS)[None], g_na_k_t=jnp.tile(g_na_k[l], NA_HEADS)[None],
        sc_w=sc_w[l], cf_w=cf_w[l], cf_b=cf_b[l][None], cf_ln_g=cf_ln_g[l][None], cf_ln_b=cf_ln_b[l][None],
        w_br_a=jnp.pad(w_br[l, 0].reshape(MLA_HEADS, MLA_V, D_MODEL),
                       ((0, 0), (0, HEAD_PAD - MLA_V), (0, 0))).reshape(MLA_PAD_W, D_MODEL).astype(BF16),
        w_br_bcd=w_br[l, 1:].astype(BF16),
        w_o=w_o[l].astype(BF16),
    )
    return W


def kernel(x_prompt, x_sample, cache_mla_ckv, cache_mla_kpe, cache_na_k, cache_na_v, c, c_ctx, w_ada, b_ada, g_norm1, w_in, g_qa, w_uq, g_kva, w_ukv, g_mla_q, g_mla_k, sc_w, g_na_q, g_na_k, na_rpb, cf_w, cf_b, cf_ln_g, cf_ln_b, w_br, w_o, g_norm2, w_ff_gate, w_ff_up, w_ff_down, w_router, w_e_gate, w_e_up, w_e_down):
    x = (x_prompt.reshape(N_CTX_TOK, D_MODEL), x_sample.reshape(N_LAT_TOK, D_MODEL))
    cvec = jnp.concatenate([c_ctx[None, :], c, jnp.zeros((MOD_ROWS - 1 - DEC_BATCH, D_MODEL), F32)], axis=0)
    cache_na_k2 = cache_na_k.reshape(DEC_BATCH, DEPTH, PAST_LEN, BR_W)
    cache_na_v2 = cache_na_v.reshape(DEC_BATCH, DEPTH, PAST_LEN, BR_W)
    w_router_p = jnp.pad(w_router, ((0, 0), (0, 0), (0, ROUTER_LANES - N_EXPERTS)))

    ckv_l, kpe_l, nak_l, nav_l = [], [], [], []
    for l in range(DEPTH):
        W = _layer_weights(l, w_in, g_qa, w_uq, g_kva, w_ukv, g_mla_q, g_mla_k, sc_w, g_na_q, g_na_k,
                           cf_w, cf_b, cf_ln_g, cf_ln_b, w_br, w_o)
        mod3 = _modulation(cvec, w_ada, b_ada, l).reshape(MOD_ROWS, 6, D_MODEL)
        p = _in_projection(x, mod3, g_norm1, W['w_in_p'], l)

        oa_c, ckv_new, kpe_new = _mla(p, l, False, None, None, W)
        kpe_g = jnp.pad(cache_mla_kpe[:, l], ((0, 0), (0, 0), (MLA_NOPE, HEAD_PAD - MLA_QK)))
        (oa_l,) = _mla(p, l, True, cache_mla_ckv, kpe_g, W)
        oc_c, nak_new, nav_new = _na_context(p, W)
        oc_l = _na_latent(p, l, cache_na_k2, cache_na_v2, _na_bias_table(na_rpb[l]), W)
        ob_c, od_c = _convs(p, False, W)
        ob_l, od_l = _convs(p, True, W)

        x_all = _merge(x, mod3, p, (oa_c, oa_l), (ob_c, ob_l), (oc_c, oc_l), (od_c, od_l), W)

        last = l == DEPTH - 1
        if l % 2 == 0:
            x = (_ffn(x_all, mod3, g_norm2, w_ff_gate, w_ff_up, w_ff_down, l, l // 2),)
        else:
            x = _moe(x_all, mod3, g_norm2, w_router_p, w_e_gate, w_e_up, w_e_down, l, l // 2, last)

        ckv_l.append(ckv_new)
        kpe_l.append(kpe_new)
        nak_l.append(nak_new.reshape(BATCH, SEQ, NA_HEADS, NA_HD))
        nav_l.append(nav_new.reshape(BATCH, SEQ, NA_HEADS, NA_HD))

    if len(x) == 1:
        x = (x[0][:N_CTX_TOK], x[0][N_CTX_TOK:])
    y_prompt = x[0].reshape(BATCH, SEQ, D_MODEL)
    y_sample = x[1].reshape(DEC_BATCH, DEC_SEQ, D_MODEL)
    return (y_prompt, y_sample, jnp.stack(ckv_l, axis=1), jnp.stack(kpe_l, axis=1),
            jnp.stack(nak_l, axis=1), jnp.stack(nav_l, axis=1))
```

```python
import functools

import numpy as np
import jax
import jax.numpy as jnp
from jax import lax
from jax.experimental import pallas as pl
from jax.experimental.pallas import tpu as pltpu

F32 = jnp.float32
BF16 = jnp.bfloat16

D_MODEL = 1024
BATCH = 16
SEQ = 256
DEPTH = 2
DEC_BATCH = 8
DEC_SEQ = 1024
PAST_LEN = 512
GRID_W = 64
N_BRANCH = 4
BR_W = 256
MLA_HEADS = 4
MLA_NOPE = 64
MLA_ROPE = 32
MLA_QK = 96
MLA_V = 64
MLA_Q_LORA = 256
MLA_KV_LORA = 128
SC_K = 3
NA_HEADS = 4
NA_HD = 64
NA_WIN_R = 8
NA_WIN_C = 16
CF_K = 31
D_FF = 3584
N_EXPERTS = 8
ROPE_THETA = 10000.0
EPS = 1e-6

N_CTX_TOK = BATCH * SEQ
N_LAT_TOK = DEC_BATCH * DEC_SEQ
N_TOK = N_CTX_TOK + N_LAT_TOK
MOD_ROWS = 16

HEAD_PAD = 128
MLA_PAD_W = MLA_HEADS * HEAD_PAD

P_QA = 0
P_KVA = 256
P_SC = 512
P_NA = 1280
P_CF = 2048
P_GATE = 2560
GATE_BLK = 512
P_COLS = 6656

VMEM_LIMIT = 56 * 1024 * 1024

NA_TQ = 256
NA_WIN_ROWS = 12
NA_WIN_KEYS = NA_WIN_ROWS * GRID_W


def _cparams(sem):
    return pltpu.CompilerParams(dimension_semantics=sem, vmem_limit_bytes=VMEM_LIMIT)


def _const_spec(shape):
    nd = len(shape)
    return pl.BlockSpec(shape, lambda *_: (0,) * nd)


def _mod_row(tok_start):
    return jnp.where(tok_start < N_CTX_TOK, 0, 1 + (tok_start - N_CTX_TOK) // DEC_SEQ)


def _sigmoid(x):
    return 0.5 * jnp.tanh(0.5 * x) + 0.5


def _silu(x):
    return x * _sigmoid(x)


def _ctx_or_lat(is_ctx, c_ref, l_ref):
    return jnp.where(is_ctx, c_ref[...], l_ref[...])


def _dot(a, b):
    return jnp.dot(a.astype(BF16), b.astype(BF16), preferred_element_type=F32)


def _dot_nt(a, b):
    return lax.dot_general(a.astype(BF16), b.astype(BF16), (((1,), (1,)), ((), ())),
                           preferred_element_type=F32)


def _mod_kernel(c_ref, w_ref, b_ref, o_ref):
    o_ref[...] = _dot(_silu(c_ref[...]), w_ref[...]) + b_ref[...]


def _modulation(cvec, w_ada, b_ada, l):
    tn = 1024
    return pl.pallas_call(
        _mod_kernel,
        out_shape=jax.ShapeDtypeStruct((MOD_ROWS, 6 * D_MODEL), F32),
        grid=(6 * D_MODEL // tn,),
        in_specs=[
            _const_spec((MOD_ROWS, D_MODEL)),
            pl.BlockSpec((None, D_MODEL, tn), lambda j: (l, 0, j)),
            pl.BlockSpec((None, 1, tn), lambda j: (l, 0, j)),
        ],
        out_specs=pl.BlockSpec((MOD_ROWS, tn), lambda j: (0, j)),
        compiler_params=_cparams(("arbitrary",)),
        name="modulation",
    )(cvec, w_ada, b_ada.reshape(DEPTH, 1, 6 * D_MODEL))


def _pair_specs(tm, width, buffers=None):
    nc = N_CTX_TOK // tm
    mode = {} if buffers is None else dict(pipeline_mode=pl.Buffered(buffers))
    return [pl.BlockSpec((tm, width), lambda i, *_: (jnp.minimum(i, nc - 1), 0), **mode),
            pl.BlockSpec((tm, width), lambda i, *_: (jnp.maximum(i - nc, 0), 0), **mode)]


def _pair_shapes(width, dtype):
    return [jax.ShapeDtypeStruct((N_CTX_TOK, width), dtype), jax.ShapeDtypeStruct((N_LAT_TOK, width), dtype)]


def _store_pair(is_ctx, c_ref, l_ref, val):
    @pl.when(is_ctx)
    def _():
        c_ref[...] = val

    @pl.when(jnp.logical_not(is_ctx))
    def _():
        l_ref[...] = val


def _x_specs(x, tm, buffers=None):
    if len(x) == 2:
        return _pair_specs(tm, D_MODEL, buffers)
    mode = {} if buffers is None else dict(pipeline_mode=pl.Buffered(buffers))
    return [pl.BlockSpec((tm, D_MODEL), lambda i, *_: (i, 0), **mode)]


def _x_shapes(split):
    return _pair_shapes(D_MODEL, F32) if split else [jax.ShapeDtypeStruct((N_TOK, D_MODEL), F32)]


def _x_load(x_refs):
    if len(x_refs) == 1:
        return x_refs[0][...]
    tm = x_refs[0].shape[0]
    return _ctx_or_lat(pl.program_id(0) < N_CTX_TOK // tm, *x_refs)


def _x_store(y_refs, val):
    if len(y_refs) == 1:
        y_refs[0][...] = val
    else:
        tm = y_refs[0].shape[0]
        _store_pair(pl.program_id(0) < N_CTX_TOK // tm, *y_refs, val)


def _inproj_kernel(nx, *refs):
    x_refs, (mod_ref, g_ref, w_ref, o_ref, h_scr) = refs[:nx], refs[nx:]

    @pl.when(pl.program_id(1) == 0)
    def _():
        x = _x_load(x_refs)
        y = x * lax.rsqrt(jnp.mean(x * x, axis=-1, keepdims=True) + EPS) * g_ref[...]
        h_scr[...] = (y * (1.0 + mod_ref[1:2, :]) + mod_ref[0:1, :]).astype(BF16)

    o_ref[...] = jnp.dot(h_scr[...], w_ref[...], preferred_element_type=F32).astype(o_ref.dtype)


def _in_projection(x, mod3, g_norm1, w_in_p, l):
    tm, tn = 1024, P_COLS // 4
    return pl.pallas_call(
        functools.partial(_inproj_kernel, len(x)),
        out_shape=jax.ShapeDtypeStruct((N_TOK, P_COLS), BF16),
        grid=(N_TOK // tm, P_COLS // tn),
        in_specs=_x_specs(x, tm) + [
            pl.BlockSpec((None, 6, D_MODEL), lambda i, j: (_mod_row(i * tm), 0, 0)),
            pl.BlockSpec((None, 1, D_MODEL), lambda i, j: (l, 0, 0)),
            pl.BlockSpec((D_MODEL, tn), lambda i, j: (0, j)),
        ],
        out_specs=pl.BlockSpec((tm, tn), lambda i, j: (i, j)),
        scratch_shapes=[pltpu.VMEM((tm, D_MODEL), BF16)],
        compiler_params=_cparams(("arbitrary", "arbitrary")),
        name="in_projection",
    )(*x, mod3, g_norm1.reshape(DEPTH, 1, D_MODEL), w_in_p)


def _rope_tables():
    t = np.arange(DEC_SEQ)
    nf = MLA_ROPE // 4
    inv = (np.float32(ROPE_THETA) ** (-np.arange(nf, dtype=np.float32) / np.float32(nf))).astype(np.float32)
    ang_r = (t // GRID_W).astype(np.float32)[:, None] * inv[None, :]
    ang_c = (t % GRID_W).astype(np.float32)[:, None] * inv[None, :]
    c = np.zeros((DEC_SEQ, HEAD_PAD), np.float32)
    s1 = np.zeros((DEC_SEQ, HEAD_PAD), np.float32)
    s2 = np.zeros((DEC_SEQ, HEAD_PAD), np.float32)
    c[:, :MLA_NOPE] = 1.0
    for base, ang in ((MLA_NOPE, ang_r), (MLA_NOPE + 2 * nf, ang_c)):
        c[:, base:base + nf] = np.cos(ang)
        c[:, base + nf:base + 2 * nf] = np.cos(ang)
        s1[:, base:base + nf] = -np.sin(ang)
        s2[:, base + nf:base + 2 * nf] = np.sin(ang)
    return c, s1, s2


def _rope(x, c, s1, s2):
    nf = MLA_ROPE // 4
    return x * c + pltpu.roll(x, HEAD_PAD - nf, 1) * s1 + pltpu.roll(x, nf, 1) * s2


def _head_norm(xh, g):
    ms = jnp.sum(xh * xh, axis=-1, keepdims=True) * (1.0 / MLA_QK)
    return xh * lax.rsqrt(ms + EPS) * g


def _mla_kernel(latent, tq, *refs):
    if latent:
        (pqa_ref, pkva_ref, cckv_ref, ckpe_ref, rc_ref, rs1_ref, rs2_ref,
         gqa_ref, wuq_ref, gkva_ref, wk_ref, wv_ref, gq_ref, gk_ref,
         o_ref, k_scr, v_scr) = refs
    else:
        (pqa_ref, pkva_ref,
         gqa_ref, wuq_ref, gkva_ref, wk_ref, wv_ref, gq_ref, gk_ref,
         o_ref, ckv_ref, kpe_ref, k_scr, v_scr) = refs
    qi = pl.program_id(1)
    n_past = PAST_LEN if latent else 0

    vlane = lax.broadcasted_iota(jnp.int32, (1, MLA_PAD_W), 1)
    v_ones = jnp.where(vlane % HEAD_PAD == MLA_V, 1.0, 0.0)

    def put_kv(ckvn, kpe_g, row0, rope):
        n = ckvn.shape[0]
        kk = _dot(ckvn, wk_ref[...])
        v_scr[row0:row0 + n, :] = (_dot(ckvn, wv_ref[...]) + v_ones).astype(BF16)
        for h in range(MLA_HEADS):
            sl = slice(h * HEAD_PAD, (h + 1) * HEAD_PAD)
            kh = _head_norm(kk[:, sl] + kpe_g, gk_ref[...])
            if rope:
                kh = _rope(kh, rc_ref[...], rs1_ref[...], rs2_ref[...])
            k_scr[row0:row0 + n, sl] = kh.astype(BF16)

    @pl.when(qi == 0)
    def _():
        if latent:
            put_kv(cckv_ref[...], ckpe_ref[...], 0, False)
        kva = pkva_ref[...].astype(F32)
        ckv = kva[:, :MLA_KV_LORA]
        ckvn = ckv * lax.rsqrt(jnp.mean(ckv * ckv, axis=-1, keepdims=True) + EPS) * gkva_ref[...]
        if not latent:
            ckv_ref[...] = ckvn
            kpe_ref[...] = kva[:, MLA_KV_LORA:MLA_KV_LORA + MLA_ROPE]
        put_kv(ckvn, pltpu.roll(kva[:, MLA_KV_LORA:], MLA_NOPE, 1), n_past, latent)

    qa = pqa_ref[...].astype(F32)
    qan = qa * lax.rsqrt(jnp.mean(qa * qa, axis=-1, keepdims=True) + EPS) * gqa_ref[...]
    q = _dot(qan, wuq_ref[...])
    scale = MLA_QK ** -0.5
    for h in range(MLA_HEADS):
        sl = slice(h * HEAD_PAD, (h + 1) * HEAD_PAD)
        qh = _head_norm(q[:, sl], gq_ref[...])
        if latent:
            rows = pl.ds(pl.multiple_of(qi * tq, tq), tq)
            qh = _rope(qh, rc_ref[rows, :], rs1_ref[rows, :], rs2_ref[rows, :])
        s = _dot_nt(qh * scale, k_scr[:, sl])
        m = jnp.max(s, axis=-1, keepdims=True)
        e = jnp.exp((s - m).astype(BF16))
        o = jnp.dot(e, v_scr[:, sl], preferred_element_type=F32)
        o_ref[:, sl] = (o / o[:, MLA_V:MLA_V + 1]).astype(o_ref.dtype)


def _mla(p, l, latent, cache_ckv, cache_kpe_g, W):
    if latent:
        nb, s, tq, row_off = DEC_BATCH, DEC_SEQ, 256, N_CTX_TOK
    else:
        nb, s, tq, row_off = BATCH, SEQ, 256, 0
    nq = s // tq
    sk = s + (PAST_LEN if latent else 0)
    in_specs = [
        pl.BlockSpec((tq, MLA_Q_LORA), lambda b, qi: (row_off // tq + b * nq + qi, P_QA // MLA_Q_LORA)),
        pl.BlockSpec((s, 256), lambda b, qi: (row_off // s + b, P_KVA // 256)),
    ]
    args = [p, p]
    if latent:
        in_specs += [
            pl.BlockSpec((None, None, PAST_LEN, MLA_KV_LORA), lambda b, qi: (b, l, 0, 0)),
            pl.BlockSpec((None, PAST_LEN, HEAD_PAD), lambda b, qi: (b, 0, 0)),
            _const_spec((DEC_SEQ, HEAD_PAD)), _const_spec((DEC_SEQ, HEAD_PAD)), _const_spec((DEC_SEQ, HEAD_PAD)),
        ]
        args += [cache_ckv, cache_kpe_g] + [jnp.asarray(t) for t in _rope_tables()]
    in_specs += [
        _const_spec((1, MLA_Q_LORA)), _const_spec((MLA_Q_LORA, MLA_PAD_W)), _const_spec((1, MLA_KV_LORA)),
        _const_spec((MLA_KV_LORA, MLA_PAD_W)), _const_spec((MLA_KV_LORA, MLA_PAD_W)),
        _const_spec((1, HEAD_PAD)), _const_spec((1, HEAD_PAD)),
    ]
    args += [W['g_qa'], W['w_uq_p'], W['g_kva'], W['w_ukv_k'], W['w_ukv_v'], W['g_mla_q_p'], W['g_mla_k_p']]
    out_shape = [jax.ShapeDtypeStruct((nb * s, MLA_PAD_W), BF16)]
    out_specs = [pl.BlockSpec((tq, MLA_PAD_W), lambda b, qi: (b * nq + qi, 0))]
    if not latent:
        out_shape.append(jax.ShapeDtypeStruct((nb, s, MLA_KV_LORA), F32))
        out_specs.append(pl.BlockSpec((None, s, MLA_KV_LORA), lambda b, qi: (b, 0, 0)))
        out_shape.append(jax.ShapeDtypeStruct((nb, s, MLA_ROPE), F32))
        out_specs.append(pl.BlockSpec((None, s, MLA_ROPE), lambda b, qi: (b, 0, 0)))
    return pl.pallas_call(
        functools.partial(_mla_kernel, latent, tq),
        out_shape=out_shape,
        grid=(nb, nq),
        in_specs=in_specs,
        out_specs=out_specs,
        scratch_shapes=[pltpu.VMEM((sk, MLA_PAD_W), BF16), pltpu.VMEM((sk, MLA_PAD_W), BF16)],
        compiler_params=_cparams(("arbitrary", "arbitrary")),
        name="mla_latent" if latent else "mla_context",
    )(*args)


def _head_masks(width):
    lane = lax.broadcasted_iota(jnp.int32, (1, width), 1)
    return [(lane >= h * NA_HD) & (lane < (h + 1) * NA_HD) for h in range(NA_HEADS)]


def _group_norm64(x, g, masks):
    x2 = x * x
    inv = jnp.zeros_like(x)
    for m in masks:
        ms = jnp.sum(jnp.where(m, x2, 0.0), axis=-1, keepdims=True) * (1.0 / NA_HD)
        inv = jnp.where(m, lax.rsqrt(ms + EPS), inv)
    return x * inv * g


def _p_blocks(col0, n, rows, row_fn):
    def spec(k):
        return pl.BlockSpec((rows, BR_W), lambda *g: (row_fn(*g), col0 // BR_W + k))
    return [spec(k) for k in range(n)]


def _na_ctx_kernel(pq_ref, pk_ref, pv_ref, gq_ref, gk_ref, o_ref, k_ref, v_ref):
    masks = _head_masks(BR_W)
    qn = _group_norm64(pq_ref[...].astype(F32), gq_ref[...], masks)
    kn = _group_norm64(pk_ref[...].astype(F32), gk_ref[...], masks)
    v = pv_ref[...].astype(F32)
    k_ref[...] = kn
    v_ref[...] = v
    scale = NA_HD ** -0.5
    acc = jnp.zeros((pq_ref.shape[0], BR_W), F32)
    for m in masks:
        s = _dot_nt(jnp.where(m, qn, 0.0), kn) * scale
        mx = jnp.max(s, axis=-1, keepdims=True)
        e = jnp.exp(s - mx)
        den = jnp.sum(e, axis=-1, keepdims=True)
        acc = acc + jnp.where(m, _dot(e, v) / den, 0.0)
    o_ref[...] = acc.astype(o_ref.dtype)


def _na_context(p, W):
    s = SEQ
    return pl.pallas_call(
        _na_ctx_kernel,
        out_shape=[jax.ShapeDtypeStruct((N_CTX_TOK, BR_W), BF16),
                   jax.ShapeDtypeStruct((N_CTX_TOK, BR_W), F32),
                   jax.ShapeDtypeStruct((N_CTX_TOK, BR_W), F32)],
        grid=(BATCH,),
        in_specs=_p_blocks(P_NA, 3, s, lambda b: b) + [_const_spec((1, BR_W)), _const_spec((1, BR_W))],
        out_specs=[pl.BlockSpec((s, BR_W), lambda b: (b, 0))] * 3,
        compiler_params=_cparams(("arbitrary",)),
        name="na_context",
    )(p, p, p, W['g_na_q_t'], W['g_na_k_t'])


def _na_lat_kernel(pq_ref, pk_ref, pv_ref, ck_ref, cv_ref, bias_ref, gq_ref, gk_ref, o_ref,
                   q_scr, k_scr, v_scr, kc_scr, vc_scr):
    j = pl.program_id(1)
    masks = _head_masks(BR_W)

    zero = jnp.zeros((), BF16)
    one = jnp.ones((), BF16)
    lane = lax.broadcasted_iota(jnp.int32, (1, BR_W), 1)
    den_lanes = [((h + 1) % NA_HEADS) * NA_HD for h in range(NA_HEADS)]

    @pl.when(j == 0)
    def _():
        scale = NA_HD ** -0.5
        q_scr[...] = (_group_norm64(pq_ref[...].astype(F32), gq_ref[...], masks) * scale).astype(BF16)
        k_scr[...] = _group_norm64(pk_ref[...].astype(F32), gk_ref[...], masks).astype(BF16)
        kc_scr[...] = ck_ref[...].astype(BF16)
        v = pv_ref[...]
        vc = cv_ref[...].astype(BF16)
        for h, m in enumerate(masks):
            v_scr[h] = jnp.where(lane == den_lanes[h], one, jnp.where(m, v, zero))
            vc_scr[h] = jnp.where(lane == den_lanes[h], one, jnp.where(m, vc, zero))

    win0 = pl.multiple_of(jnp.where(j < 2, 0, DEC_SEQ - NA_WIN_KEYS), 256)
    q = q_scr[pl.ds(pl.multiple_of(j * NA_TQ, NA_TQ), NA_TQ), :]
    kw = k_scr[pl.ds(win0, NA_WIN_KEYS), :]
    kc = kc_scr[...]
    acc = jnp.zeros((NA_TQ, BR_W), F32)
    for h, m in enumerate(masks):
        qm = jnp.where(m, q, zero)
        s_loc = _dot_nt(qm, kw) + bias_ref[h]
        s_ctx = _dot_nt(qm, kc)
        mx = jnp.maximum(jnp.max(s_loc, axis=-1, keepdims=True), jnp.max(s_ctx, axis=-1, keepdims=True))
        e_loc = jnp.exp((s_loc - mx).astype(BF16))
        e_ctx = jnp.exp((s_ctx - mx).astype(BF16))
        o = (jnp.dot(e_loc, v_scr[h, pl.ds(win0, NA_WIN_KEYS), :], preferred_element_type=F32)
             + jnp.dot(e_ctx, vc_scr[h], preferred_element_type=F32))
        acc = acc + jnp.where(m, o / o[:, den_lanes[h]:den_lanes[h] + 1], 0.0)
    o_ref[...] = acc.astype(o_ref.dtype)


def _na_bias_table(rpb):
    n_dr, n_dc = 2 * NA_WIN_R - 1, 2 * NA_WIN_C - 1
    cols = np.arange(GRID_W)
    dc = np.clip(cols[None, :] - cols[:, None], -(NA_WIN_C - 1), NA_WIN_C - 1) + (NA_WIN_C - 1)
    cstart = np.clip(cols - NA_WIN_C // 2, 0, GRID_W - NA_WIN_C)
    col_ok = (cols[None, :] >= cstart[:, None]) & (cols[None, :] < cstart[:, None] + NA_WIN_C)
    place = (dc.reshape(-1)[None, :] == np.arange(n_dc)[:, None]).astype(np.float32)
    blocks = jnp.dot(rpb.reshape(NA_HEADS * n_dr, n_dc), place, precision=lax.Precision.HIGHEST)
    blocks = jnp.where(col_ok.reshape(-1)[None, :], blocks, -jnp.inf).reshape(NA_HEADS, n_dr, GRID_W, GRID_W)
    neg = jnp.full((NA_HEADS, 1, GRID_W, GRID_W), -jnp.inf, F32)
    blocks = jnp.concatenate([blocks, neg], axis=1)

    nt = DEC_SEQ // NA_TQ
    n_rows = DEC_SEQ // GRID_W
    rows_per_tile = NA_TQ // GRID_W
    sel = np.full((nt, rows_per_tile, NA_WIN_ROWS), n_dr, np.int32)
    for j in range(nt):
        win_row0 = 0 if j < nt // 2 else n_rows - NA_WIN_ROWS
        for rq in range(rows_per_tile):
            r = j * rows_per_tile + rq
            start = min(max(r - NA_WIN_R // 2, 0), n_rows - NA_WIN_R)
            for kr in range(NA_WIN_ROWS):
                if start <= win_row0 + kr < start + NA_WIN_R:
                    sel[j, rq, kr] = win_row0 + kr - r + (NA_WIN_R - 1)
    tiles = []
    for j in range(nt):
        rows = [jnp.concatenate([blocks[:, sel[j, rq, kr]] for kr in range(NA_WIN_ROWS)], axis=-1)
                for rq in range(rows_per_tile)]
        tiles.append(jnp.concatenate(rows, axis=1))
    return jnp.stack(tiles, axis=1)


def _na_latent(p, l, cache_k, cache_v, bias, W):
    s = DEC_SEQ
    nt = s // NA_TQ
    row_off = N_CTX_TOK
    return pl.pallas_call(
        _na_lat_kernel,
        out_shape=jax.ShapeDtypeStruct((N_LAT_TOK, BR_W), BF16),
        grid=(DEC_BATCH, nt),
        in_specs=_p_blocks(P_NA, 3, s, lambda b, j: row_off // s + b) + [
                  pl.BlockSpec((None, None, PAST_LEN, BR_W), lambda b, j: (b, l, 0, 0)),
                  pl.BlockSpec((None, None, PAST_LEN, BR_W), lambda b, j: (b, l, 0, 0)),
                  pl.BlockSpec((NA_HEADS, None, NA_TQ, NA_WIN_KEYS), lambda b, j: (0, j, 0, 0)),
                  _const_spec((1, BR_W)), _const_spec((1, BR_W))],
        out_specs=pl.BlockSpec((NA_TQ, BR_W), lambda b, j: (b * nt + j, 0)),
        scratch_shapes=[pltpu.VMEM((s, BR_W), BF16), pltpu.VMEM((s, BR_W), BF16),
                        pltpu.VMEM((NA_HEADS, s, BR_W), BF16),
                        pltpu.VMEM((PAST_LEN, BR_W), BF16), pltpu.VMEM((NA_HEADS, PAST_LEN, BR_W), BF16)],
        compiler_params=_cparams(("arbitrary", "arbitrary")),
        name="na_latent",
    )(p, p, p, cache_k, cache_v, bias, W['g_na_q_t'], W['g_na_k_t'])


CONV_HALO = 16
CONV_CHUNK = 128


SUBLANES = 8


def _dwconv_from_pad(pad_ref, w_ref, ksize, s, emit, shift_ref=None):
    half = ksize // 2
    if shift_ref is not None:
        n_rows = s + 2 * CONV_HALO - SUBLANES
        for p in range(1, SUBLANES):
            for c0 in range(0, n_rows, CONV_CHUNK):
                n = min(CONV_CHUNK, n_rows - c0)
                shift_ref[p - 1, c0:c0 + n, :] = pad_ref[c0 + p:c0 + p + n, :]
    for c0 in range(0, s, CONV_CHUNK):
        acc = jnp.zeros((CONV_CHUNK, BR_W), F32)
        for k in range(ksize):
            r0 = CONV_HALO + c0 + k - half
            p = r0 % SUBLANES
            if shift_ref is None or p == 0:
                win = pad_ref[r0:r0 + CONV_CHUNK, :]
            else:
                win = shift_ref[p - 1, r0 - p:r0 - p + CONV_CHUNK, :]
            acc = acc + win * w_ref[k:k + 1, :]
        emit(c0, acc)


def _conv_kernel(s, scb_ref, scc_ref, scx_ref, cfa_ref, cfb2_ref, scw_ref, cfw_ref, cfb_ref, lng_ref, lnb_ref,
                 ob_ref, od_ref, pad_ref, shift_ref):
    zeros = jnp.zeros((CONV_HALO, BR_W), F32)
    pad_ref[0:CONV_HALO, :] = zeros
    pad_ref[CONV_HALO + s:2 * CONV_HALO + s, :] = zeros

    pad_ref[CONV_HALO:CONV_HALO + s, :] = scc_ref[...].astype(F32) * scx_ref[...].astype(F32)

    def emit_b(c0, acc):
        ob_ref[c0:c0 + CONV_CHUNK, :] = (scb_ref[c0:c0 + CONV_CHUNK, :].astype(F32) * acc).astype(ob_ref.dtype)

    _dwconv_from_pad(pad_ref, scw_ref, SC_K, s, emit_b)

    pad_ref[CONV_HALO:CONV_HALO + s, :] = cfa_ref[...].astype(F32) * _sigmoid(cfb2_ref[...].astype(F32))

    def emit_d(c0, acc):
        u = acc + cfb_ref[...]
        mu = jnp.mean(u, axis=-1, keepdims=True)
        d = u - mu
        var = jnp.mean(d * d, axis=-1, keepdims=True)
        y = d * lax.rsqrt(var + EPS) * lng_ref[...] + lnb_ref[...]
        od_ref[c0:c0 + CONV_CHUNK, :] = _silu(y).astype(od_ref.dtype)

    _dwconv_from_pad(pad_ref, cfw_ref, CF_K, s, emit_d, shift_ref)


def _convs(p, latent, W):
    if latent:
        nb, s, row_off = DEC_BATCH, DEC_SEQ, N_CTX_TOK
    else:
        nb, s, row_off = BATCH, SEQ, 0
    return pl.pallas_call(
        functools.partial(_conv_kernel, s),
        out_shape=[jax.ShapeDtypeStruct((nb * s, BR_W), BF16)] * 2,
        grid=(nb,),
        in_specs=(_p_blocks(P_SC, 3, s, lambda b: row_off // s + b)
                  + _p_blocks(P_CF, 2, s, lambda b: row_off // s + b)
                  + [_const_spec((SC_K, BR_W)), _const_spec((CF_K, BR_W)),
                     _const_spec((1, BR_W)), _const_spec((1, BR_W)), _const_spec((1, BR_W))]),
        out_specs=[pl.BlockSpec((s, BR_W), lambda b: (b, 0))] * 2,
        scratch_shapes=[pltpu.VMEM((s + 2 * CONV_HALO, BR_W), F32),
                        pltpu.VMEM((SUBLANES - 1, s + 2 * CONV_HALO, BR_W), F32)],
        compiler_params=_cparams(("arbitrary",)),
        name="convs_latent" if latent else "convs_context",
    )(p, p, p, p, p, W['sc_w'], W['cf_w'], W['cf_b'], W['cf_ln_g'], W['cf_ln_b'])


def _merge_kernel(nc, nx, *refs):
    x_refs = refs[:nx]
    (mod_ref, oac_ref, oal_ref, obc_ref, obl_ref, occ_ref, ocl_ref, odc_ref, odl_ref) = refs[nx:nx + 9]
    gate_refs = refs[nx + 9:nx + 9 + 2 * N_BRANCH]
    wa_ref, wb_ref, wo_ref, y_ref = refs[nx + 9 + 2 * N_BRANCH:]
    is_ctx = pl.program_id(0) < nc

    def gate(n):
        return jnp.concatenate([_sigmoid(gate_refs[2 * n][...]), _sigmoid(gate_refs[2 * n + 1][...])],
                               axis=1).astype(F32)

    merged = gate(0) * jnp.dot(_ctx_or_lat(is_ctx, oac_ref, oal_ref), wa_ref[...], preferred_element_type=F32)
    for n, (oc_ref, ol_ref) in enumerate(((obc_ref, obl_ref), (occ_ref, ocl_ref), (odc_ref, odl_ref))):
        merged = merged + gate(n + 1) * jnp.dot(_ctx_or_lat(is_ctx, oc_ref, ol_ref), wb_ref[n],
                                                preferred_element_type=F32)
    y_ref[...] = _x_load(x_refs) + mod_ref[2:3, :] * _dot(merged, wo_ref[...])


def _merge(x, mod3, p, o_a, o_b, o_c, o_d, W):
    tm = 512
    gate_spec = lambda k: pl.BlockSpec((tm, GATE_BLK), lambda i: (i, P_GATE // GATE_BLK + k))
    n_gate_blk = N_BRANCH * D_MODEL // GATE_BLK
    return pl.pallas_call(
        functools.partial(_merge_kernel, N_CTX_TOK // tm, len(x)),
        out_shape=jax.ShapeDtypeStruct((N_TOK, D_MODEL), F32),
        grid=(N_TOK // tm,),
        in_specs=(_x_specs(x, tm)
                  + [pl.BlockSpec((None, 6, D_MODEL), lambda i: (_mod_row(i * tm), 0, 0))]
                  + _pair_specs(tm, MLA_PAD_W) + _pair_specs(tm, BR_W) + _pair_specs(tm, BR_W) + _pair_specs(tm, BR_W)
                  + [gate_spec(k) for k in range(n_gate_blk)]
                  + [_const_spec((MLA_PAD_W, D_MODEL)), _const_spec((3, BR_W, D_MODEL)),
                     _const_spec((D_MODEL, D_MODEL))]),
        out_specs=pl.BlockSpec((tm, D_MODEL), lambda i: (i, 0)),
        compiler_params=_cparams(("arbitrary",)),
        name="merge",
    )(*x, mod3, *o_a, *o_b, *o_c, *o_d, *([p] * n_gate_blk), W['w_br_a'], W['w_br_bcd'], W['w_o'])


def _norm2(x, g, mod_ref):
    y = x * lax.rsqrt(jnp.mean(x * x, axis=-1, keepdims=True) + EPS) * g
    return y * (1.0 + mod_ref[4:5, :]) + mod_ref[3:4, :]


def _ffn_kernel(x_ref, mod_ref, g_ref, wg_ref, wu_ref, wd_ref, y_ref, h_scr, acc_scr):
    f = pl.program_id(1)

    @pl.when(f == 0)
    def _():
        h_scr[...] = _norm2(x_ref[...], g_ref[...], mod_ref).astype(BF16)
        acc_scr[...] = jnp.zeros_like(acc_scr)

    h = h_scr[...]
    a = _silu(_dot(h, wg_ref[...])) * _dot(h, wu_ref[...])
    acc_scr[...] += _dot(a, wd_ref[...])

    @pl.when(f == pl.num_programs(1) - 1)
    def _():
        y_ref[...] = x_ref[...] + mod_ref[5:6, :] * acc_scr[...]


def _ffn(x_all, mod3, g_norm2, w_g, w_u, w_d, l, j):
    tm, tf = 1024, 512
    return pl.pallas_call(
        _ffn_kernel,
        out_shape=jax.ShapeDtypeStruct((N_TOK, D_MODEL), F32),
        grid=(N_TOK // tm, D_FF // tf),
        in_specs=[pl.BlockSpec((tm, D_MODEL), lambda i, f: (i, 0)),
                  pl.BlockSpec((None, 6, D_MODEL), lambda i, f: (_mod_row(i * tm), 0, 0)),
                  pl.BlockSpec((None, 1, D_MODEL), lambda i, f: (l, 0, 0)),
                  pl.BlockSpec((None, D_MODEL, tf), lambda i, f: (j, 0, f)),
                  pl.BlockSpec((None, D_MODEL, tf), lambda i, f: (j, 0, f)),
                  pl.BlockSpec((None, tf, D_MODEL), lambda i, f: (j, f, 0))],
        out_specs=pl.BlockSpec((tm, D_MODEL), lambda i, f: (i, 0)),
        scratch_shapes=[pltpu.VMEM((tm, D_MODEL), BF16), pltpu.VMEM((tm, D_MODEL), F32)],
        compiler_params=_cparams(("arbitrary", "arbitrary")),
        name="ffn_dense",
    )(x_all, mod3, g_norm2.reshape(DEPTH, 1, D_MODEL), w_g, w_u, w_d)


ROUTER_LANES = 128
ROUTE_TM = 512
MOE_TM = 1024
MOE_SUB = 256
MOE_TF = 512
RUN_ALIGN = 8
MOE_TILES = -(-(2 * N_TOK + (N_TOK // ROUTE_TM) * N_EXPERTS * RUN_ALIGN + N_EXPERTS * MOE_TM) // MOE_TM)
MOE_ROWS = MOE_TILES * MOE_TM
R_I1, R_I2, R_W1, R_W2, R_RANK1, R_RANK2 = range(6)


def _route_top2(h, wr):
    h_hi = h.astype(BF16)
    h_lo = (h - h_hi.astype(F32)).astype(BF16)
    w_hi = wr.astype(BF16)
    w_lo = (wr - w_hi.astype(F32)).astype(BF16)
    logits = (jnp.dot(h_hi, w_hi, preferred_element_type=F32) + jnp.dot(h_lo, w_hi, preferred_element_type=F32)
              + jnp.dot(h_hi, w_lo, preferred_element_type=F32))
    lane = lax.broadcasted_iota(jnp.int32, logits.shape, 1).astype(F32)
    neg = jnp.float32(-jnp.inf)
    logits = jnp.where(lane < N_EXPERTS, logits, neg)
    m1 = jnp.max(logits, axis=-1, keepdims=True)
    i1 = jnp.min(jnp.where(logits == m1, lane, float(ROUTER_LANES)), axis=-1, keepdims=True)
    rest = jnp.where(lane == i1, neg, logits)
    m2 = jnp.max(rest, axis=-1, keepdims=True)
    i2 = jnp.min(jnp.where(rest == m2, lane, float(ROUTER_LANES)), axis=-1, keepdims=True)
    e2 = jnp.exp(m2 - m1)
    return lane, i1, i2, 1.0 / (1.0 + e2), e2 / (1.0 + e2)


def _route_kernel(x_ref, mod_ref, g_ref, wr_ref, route_ref, cnt_ref, tcarry_ref, carry_scr):
    @pl.when(pl.program_id(0) == 0)
    def _():
        carry_scr[...] = jnp.zeros_like(carry_scr)

    tcarry_ref[...] = carry_scr[...]

    h = _norm2(x_ref[...], g_ref[...], mod_ref)
    lane, i1, i2, w1, w2 = _route_top2(h, wr_ref[...])
    tm = h.shape[0]
    oh1 = lane == i1
    oh2 = lane == i2
    oh = jnp.where(oh1, 1.0, 0.0) + jnp.where(oh2, 1.0, 0.0)
    r = lax.broadcasted_iota(jnp.int32, (tm, tm), 0)
    c = lax.broadcasted_iota(jnp.int32, (tm, tm), 1)
    lower = jnp.where(r > c, 1.0, 0.0).astype(BF16)
    before = jnp.dot(lower, oh.astype(BF16), preferred_element_type=F32)
    rank1 = jnp.sum(jnp.where(oh1, before, 0.0), axis=-1, keepdims=True)
    rank2 = jnp.sum(jnp.where(oh2, before, 0.0), axis=-1, keepdims=True)
    carry_scr[...] += jnp.sum(oh, axis=0, keepdims=True)
    rec = jnp.zeros_like(lane)
    for k, v in ((R_I1, i1), (R_I2, i2), (R_W1, w1), (R_W2, w2), (R_RANK1, rank1), (R_RANK2, rank2)):
        rec = jnp.where(lane == float(k), v, rec)
    route_ref[...] = rec
    cnt_ref[...] = carry_scr[...]


def _moe_route(x_all, mod3, g_norm2, w_router_p, l, j):
    tm = ROUTE_TM
    return pl.pallas_call(
        _route_kernel,
        out_shape=[jax.ShapeDtypeStruct((N_TOK, ROUTER_LANES), F32),
                   jax.ShapeDtypeStruct((1, ROUTER_LANES), F32),
                   jax.ShapeDtypeStruct((N_TOK // tm, 1, ROUTER_LANES), F32)],
        grid=(N_TOK // tm,),
        in_specs=[pl.BlockSpec((tm, D_MODEL), lambda i: (i, 0)),
                  pl.BlockSpec((None, 6, D_MODEL), lambda i: (_mod_row(i * tm), 0, 0)),
                  pl.BlockSpec((None, 1, D_MODEL), lambda i: (l, 0, 0)),
                  pl.BlockSpec((None, D_MODEL, ROUTER_LANES), lambda i: (j, 0, 0))],
        out_specs=[pl.BlockSpec((tm, ROUTER_LANES), lambda i: (i, 0)),
                   pl.BlockSpec((1, ROUTER_LANES), lambda i: (0, 0)),
                   pl.BlockSpec((None, 1, ROUTER_LANES), lambda i: (i, 0, 0))],
        scratch_shapes=[pltpu.VMEM((1, ROUTER_LANES), F32)],
        compiler_params=_cparams(("arbitrary",)),
        name="moe_route",
    )(x_all, mod3, g_norm2.reshape(DEPTH, 1, D_MODEL), w_router_p)


ZERO_ROWS = 512
N_GAPS = N_EXPERTS + 1


def _bit_chunks(n_units, max_units, make, wait=False):
    for b in range(max_units.bit_length()):
        units = 1 << b

        @pl.when((n_units & units) != 0)
        def _():
            cp = make(pl.multiple_of((n_units & (units - 1)) * RUN_ALIGN, RUN_ALIGN), units * RUN_ALIGN)
            if wait:
                cp.wait()
            else:
                cp.start()


def _dispatch_kernel(n_ref, lo_ref, dst_ref, gap0_ref, gapn_ref, x_ref, mod_ref, g_ref, route_ref, lovec_ref,
                     xs_ref, z_scr, zero_scr, sems, zsem):
    i = pl.program_id(0)
    tm = x_ref.shape[0]
    h = _norm2(x_ref[...], g_ref[...], mod_ref).astype(BF16)
    rec = route_ref[...]
    lane = lax.broadcasted_iota(jnp.int32, rec.shape, 1).astype(F32)
    lo_row = lovec_ref[...]

    def local_pos(i_lane, r_lane):
        lp = (jnp.sum(jnp.where(lane == rec[:, i_lane:i_lane + 1], lo_row, 0.0), axis=-1, keepdims=True)
              + rec[:, r_lane:r_lane + 1])
        return jnp.transpose(jnp.broadcast_to(lp, (tm, ROUTER_LANES)))[0:1, :]

    slot = lax.broadcasted_iota(jnp.int32, (z_scr.shape[1], tm), 0).astype(F32)
    perm = jnp.where((slot == local_pos(R_I1, R_RANK1)) | (slot == local_pos(R_I2, R_RANK2)), 1.0, 0.0)
    buf = i % 2
    z_scr[buf] = jnp.dot(perm.astype(BF16), h, preferred_element_type=F32)

    def run_copies(tile, b, wait):
        for e in range(N_EXPERTS):
            k = tile * N_EXPERTS + e
            src0 = lo_ref[k]
            dst0 = dst_ref[k]

            def make(off, rows, src0=src0, dst0=dst0):
                return pltpu.make_async_copy(
                    z_scr.at[b, pl.ds(pl.multiple_of(src0 + off, RUN_ALIGN), rows), :],
                    xs_ref.at[pl.ds(pl.multiple_of(dst0 + off, RUN_ALIGN), rows), :], sems.at[b])

            _bit_chunks(n_ref[k], tm // RUN_ALIGN, make, wait)

    run_copies(i, buf, False)

    @pl.when(i > 0)
    def _():
        run_copies(i - 1, 1 - buf, True)

    @pl.when(i == pl.num_programs(0) - 1)
    def _():
        run_copies(i, buf, True)

    @pl.when(i == pl.num_programs(0) - 1)
    def _():
        zero_scr[...] = jnp.zeros_like(zero_scr)
        for wait in (False, True):
            for g in range(N_GAPS):
                start = gap0_ref[g]
                n = gapn_ref[g]
                max_rows = MOE_TM if g < N_EXPERTS else MOE_ROWS - 2 * N_TOK
                for c in range(max_rows // ZERO_ROWS):
                    @pl.when(n >= (c + 1) * ZERO_ROWS)
                    def _():
                        cp = pltpu.make_async_copy(
                            zero_scr,
                            xs_ref.at[pl.ds(pl.multiple_of(start + c * ZERO_ROWS, RUN_ALIGN), ZERO_ROWS), :], zsem)
                        if wait:
                            cp.wait()
                        else:
                            cp.start()

                def make(off, rows, start=start, n=n):
                    tail0 = start + (n // ZERO_ROWS) * ZERO_ROWS
                    return pltpu.make_async_copy(
                        zero_scr.at[pl.ds(0, rows), :],
                        xs_ref.at[pl.ds(pl.multiple_of(tail0 + off, RUN_ALIGN), rows), :], zsem)

                _bit_chunks((n % ZERO_ROWS) // RUN_ALIGN, ZERO_ROWS // RUN_ALIGN - 1, make, wait)


def _moe_dispatch(n_tile, lo_tile, dst_tile, gap0, gapn, x_all, mod3, g_norm2, route, lo_vec, l):
    tm = ROUTE_TM
    spec = lambda shape, fn: pl.BlockSpec(shape, lambda i, *_: fn(i))
    return pl.pallas_call(
        _dispatch_kernel,
        out_shape=jax.ShapeDtypeStruct((MOE_ROWS, D_MODEL), F32),
        grid_spec=pltpu.PrefetchScalarGridSpec(
            num_scalar_prefetch=5,
            grid=(N_TOK // tm,),
            in_specs=[spec((tm, D_MODEL), lambda i: (i, 0)),
                      spec((None, 6, D_MODEL), lambda i: (_mod_row(i * tm), 0, 0)),
                      spec((None, 1, D_MODEL), lambda i: (l, 0, 0)),
                      spec((tm, ROUTER_LANES), lambda i: (i, 0)),
                      spec((None, 1, ROUTER_LANES), lambda i: (i, 0, 0))],
            out_specs=pl.BlockSpec(memory_space=pl.ANY),
            scratch_shapes=[pltpu.VMEM((2, 2 * tm + N_EXPERTS * RUN_ALIGN, D_MODEL), F32),
                            pltpu.VMEM((ZERO_ROWS, D_MODEL), F32),
                            pltpu.SemaphoreType.DMA((2,)), pltpu.SemaphoreType.DMA],
        ),
        compiler_params=_cparams(("arbitrary",)),
        name="moe_dispatch",
    )(n_tile, lo_tile, dst_tile, gap0, gapn, x_all, mod3, g_norm2.reshape(DEPTH, 1, D_MODEL), route, lo_vec)


def _gmm_kernel(te_ref, tr_ref, xs_ref, wg_ref, wu_ref, wd_ref, y_ref, wg_scr, wu_scr, wd_scr):
    del te_ref
    g = pl.program_id(0)
    f = pl.program_id(1)
    rows = tr_ref[g]

    @pl.when(f == 0)
    def _():
        y_ref[...] = jnp.zeros_like(y_ref)

    def sub_tile(s, wg, wu, wd):
        sl = slice(s * MOE_SUB, (s + 1) * MOE_SUB)
        x = xs_ref[sl, :].astype(BF16)
        a = (_silu(jnp.dot(x, wg, preferred_element_type=F32)) * jnp.dot(x, wu, preferred_element_type=F32))
        y_ref[sl, :] += jnp.dot(a.astype(BF16), wd, preferred_element_type=F32)

    @pl.when(rows == MOE_TM)
    def _():
        x = xs_ref[...].astype(BF16)
        a = _silu(_dot(x, wg_ref[...])) * _dot(x, wu_ref[...])
        y_ref[...] += _dot(a, wd_ref[...])

    @pl.when((rows > 0) & (rows < MOE_TM))
    def _():
        wg = wg_ref[...].astype(BF16)
        wu = wu_ref[...].astype(BF16)
        wd = wd_ref[...].astype(BF16)
        wg_scr[...] = wg
        wu_scr[...] = wu
        wd_scr[...] = wd
        sub_tile(0, wg, wu, wd)

    for s in range(1, MOE_TM // MOE_SUB):
        @pl.when((rows > s * MOE_SUB) & (rows < MOE_TM))
        def _():
            sub_tile(s, wg_scr[...], wu_scr[...], wd_scr[...])


def _moe_gmm(tile_expert, tile_rows, xs, w_g, w_u, w_d, j):
    tm, tf = MOE_TM, MOE_TF
    nf = D_FF // tf

    def f_eff(g, f, tr):
        return jnp.where(tr[g] > 0, f, nf - 1)

    return pl.pallas_call(
        _gmm_kernel,
        out_shape=jax.ShapeDtypeStruct((MOE_ROWS, D_MODEL), F32),
        grid_spec=pltpu.PrefetchScalarGridSpec(
            num_scalar_prefetch=2,
            grid=(MOE_TILES, nf),
            in_specs=[pl.BlockSpec((tm, D_MODEL), lambda g, f, te, tr: (g, 0)),
                      pl.BlockSpec((None, None, D_MODEL, tf), lambda g, f, te, tr: (j, te[g], 0, f_eff(g, f, tr))),
                      pl.BlockSpec((None, None, D_MODEL, tf), lambda g, f, te, tr: (j, te[g], 0, f_eff(g, f, tr))),
                      pl.BlockSpec((None, None, tf, D_MODEL), lambda g, f, te, tr: (j, te[g], f_eff(g, f, tr), 0))],
            out_specs=pl.BlockSpec((tm, D_MODEL), lambda g, f, te, tr: (g, 0)),
            scratch_shapes=[pltpu.VMEM((D_MODEL, tf), BF16), pltpu.VMEM((D_MODEL, tf), BF16),
                            pltpu.VMEM((tf, D_MODEL), BF16)],
        ),
        compiler_params=_cparams(("arbitrary", "arbitrary")),
        name="moe_experts",
    )(tile_expert, tile_rows, xs, w_g, w_u, w_d)


ZY_ROWS = 1152


def _combine_kernel(n_out, n_ref, lo_ref, dst_ref, x_ref, mod_ref, route_ref, lovec_ref, y_ref, *refs):
    o_refs, (zy_scr, sems) = refs[:n_out], refs[n_out:]
    i = pl.program_id(0)
    tm = x_ref.shape[0]

    def run_copies(tile, b, wait):
        for e in range(N_EXPERTS):
            k = tile * N_EXPERTS + e
            src0 = dst_ref[k]
            dst0 = lo_ref[k]

            def make(off, rows, src0=src0, dst0=dst0):
                return pltpu.make_async_copy(
                    y_ref.at[pl.ds(pl.multiple_of(src0 + off, RUN_ALIGN), rows), :],
                    zy_scr.at[b, pl.ds(pl.multiple_of(dst0 + off, RUN_ALIGN), rows), :], sems.at[b])

            _bit_chunks(n_ref[k], tm // RUN_ALIGN, make, wait)

    @pl.when(i == 0)
    def _():
        zy_scr[...] = jnp.zeros_like(zy_scr)
        run_copies(0, 0, False)

    buf = i % 2

    @pl.when(i + 1 < pl.num_programs(0))
    def _():
        run_copies(i + 1, 1 - buf, False)

    run_copies(i, buf, True)

    rec = route_ref[...]
    lane = lax.broadcasted_iota(jnp.int32, rec.shape, 1).astype(F32)
    lo_row = lovec_ref[...]

    def local_pos(i_lane, r_lane):
        return (jnp.sum(jnp.where(lane == rec[:, i_lane:i_lane + 1], lo_row, 0.0), axis=-1, keepdims=True)
                + rec[:, r_lane:r_lane + 1])

    slot = lax.broadcasted_iota(jnp.int32, (tm, ZY_ROWS), 1).astype(F32)
    wsel = (jnp.where(slot == local_pos(R_I1, R_RANK1), rec[:, R_W1:R_W1 + 1], 0.0)
            + jnp.where(slot == local_pos(R_I2, R_RANK2), rec[:, R_W2:R_W2 + 1], 0.0))
    w_hi = wsel.astype(BF16)
    w_lo = (wsel - w_hi.astype(F32)).astype(BF16)
    z = zy_scr[buf].astype(BF16)
    mix = jnp.dot(w_hi, z, preferred_element_type=F32) + jnp.dot(w_lo, z, preferred_element_type=F32)
    _x_store(o_refs, x_ref[...] + mod_ref[5:6, :] * mix)


def _moe_combine(n_tile, lo_tile, dst_tile, x_all, mod3, route, lo_vec, y, split_out):
    tm = ROUTE_TM
    spec = lambda shape, fn: pl.BlockSpec(shape, lambda i, *_: fn(i))
    return pl.pallas_call(
        functools.partial(_combine_kernel, 2 if split_out else 1),
        out_shape=_x_shapes(split_out),
        grid_spec=pltpu.PrefetchScalarGridSpec(
            num_scalar_prefetch=3,
            grid=(N_TOK // tm,),
            in_specs=[spec((tm, D_MODEL), lambda i: (i, 0)),
                      spec((None, 6, D_MODEL), lambda i: (_mod_row(i * tm), 0, 0)),
                      spec((tm, ROUTER_LANES), lambda i: (i, 0)),
                      spec((None, 1, ROUTER_LANES), lambda i: (i, 0, 0)),
                      pl.BlockSpec(memory_space=pl.ANY)],
            out_specs=_x_specs((None,) * (2 if split_out else 1), tm),
            scratch_shapes=[pltpu.VMEM((2, ZY_ROWS, D_MODEL), F32), pltpu.SemaphoreType.DMA((2,))],
        ),
        compiler_params=_cparams(("arbitrary",)),
        name="moe_combine",
    )(n_tile, lo_tile, dst_tile, x_all, mod3, route, lo_vec, y)


def _moe(x_all, mod3, g_norm2, w_router_p, w_g, w_u, w_d, l, j, split_out):
    route, cnt, tcarry = _moe_route(x_all, mod3, g_norm2, w_router_p, l, j)
    cnt = cnt[0, :N_EXPERTS].astype(jnp.int32)
    carry = tcarry[:, 0, :N_EXPERTS].astype(jnp.int32)
    n_tile = jnp.concatenate([carry[1:], cnt[None, :]], axis=0) - carry
    n_tile = (n_tile + RUN_ALIGN - 1) // RUN_ALIGN * RUN_ALIGN
    lo_tile = jnp.cumsum(n_tile, axis=1) - n_tile
    carry = jnp.cumsum(n_tile, axis=0) - n_tile
    cnt = jnp.sum(n_tile, axis=0)
    padded = (cnt + MOE_TM - 1) // MOE_TM * MOE_TM
    ends = jnp.cumsum(padded)
    offs = ends - padded
    dst_tile = offs[None, :] + carry
    experts = jnp.arange(N_EXPERTS, dtype=jnp.int32)
    gap0 = jnp.concatenate([offs + cnt, ends[-1:]])
    gapn = jnp.concatenate([padded - cnt, MOE_ROWS - ends[-1:]])
    lo_vec = jnp.pad(lo_tile.astype(F32), ((0, 0), (0, ROUTER_LANES - N_EXPERTS)))[:, None, :]

    tile_start = jnp.arange(MOE_TILES, dtype=jnp.int32) * MOE_TM
    last_tile = jnp.maximum(ends[-1] - MOE_TM, 0)
    owner_start = jnp.minimum(tile_start, last_tile)
    tile_expert = jnp.minimum(jnp.sum(owner_start[:, None] >= ends[None, :], axis=1), N_EXPERTS - 1).astype(jnp.int32)
    group_end = jnp.sum(jnp.where(tile_expert[:, None] == experts[None, :], (offs + cnt)[None, :], 0), axis=1)
    tile_rows = jnp.where(tile_start < ends[-1], jnp.clip(group_end - tile_start, 0, MOE_TM), 0).astype(jnp.int32)

    i32 = lambda a: a.reshape(-1).astype(jnp.int32)
    runs = (i32(n_tile // RUN_ALIGN), i32(lo_tile), i32(dst_tile))
    xs = _moe_dispatch(*runs, i32(gap0), i32(gapn), x_all, mod3, g_norm2, route, lo_vec, l)
    y = _moe_gmm(tile_expert, tile_rows, xs, w_g, w_u, w_d, j)
    return tuple(_moe_combine(*runs, x_all, mod3, route, lo_vec, y, split_out))


def _pad_heads(w, per_head, lo):
    lead = w.shape[:-1]
    w = w.reshape(lead + (MLA_HEADS, per_head))
    w = jnp.pad(w, [(0, 0)] * len(lead) + [(0, 0), (lo, HEAD_PAD - lo - per_head)])
    return w.reshape(lead + (MLA_PAD_W,))


W_IN_HEAD = MLA_Q_LORA + MLA_KV_LORA + MLA_ROPE
W_IN_COLS = W_IN_HEAD + 8 * BR_W + N_BRANCH * D_MODEL


def _w_in_layout_kernel(w_ref, o_ref):
    w = w_ref[...]
    o_ref[:, :W_IN_HEAD] = w[:, :W_IN_HEAD].astype(BF16)
    o_ref[:, W_IN_HEAD:P_SC] = jnp.zeros((w.shape[0], P_SC - W_IN_HEAD), BF16)
    o_ref[:, P_SC:] = w[:, W_IN_HEAD:].astype(BF16)


def _w_in_layout(w_in, l):
    tk = 128
    return pl.pallas_call(
        _w_in_layout_kernel,
        out_shape=jax.ShapeDtypeStruct((D_MODEL, P_COLS), BF16),
        grid=(D_MODEL // tk,),
        in_specs=[pl.BlockSpec((None, tk, W_IN_COLS), lambda k: (l, k, 0))],
        out_specs=pl.BlockSpec((tk, P_COLS), lambda k: (k, 0)),
        compiler_params=_cparams(("arbitrary",)),
        name="w_in_layout",
    )(w_in)


def _layer_weights(l, w_in, g_qa, w_uq, g_kva, w_ukv, g_mla_q, g_mla_k, sc_w, g_na_q, g_na_k,
                   cf_w, cf_b, cf_ln_g, cf_ln_b, w_br, w_o):
    w_in_p = _w_in_layout(w_in, l)
    ukv = w_ukv[l].reshape(MLA_KV_LORA, MLA_HEADS, MLA_NOPE + MLA_V)
    W = dict(
        w_in_p=w_in_p,
        g_qa=g_qa[l][None], g_kva=g_kva[l][None],
        w_uq_p=_pad_heads(w_uq[l], MLA_QK, 0).astype(BF16),
        w_ukv_k=_pad_heads(ukv[:, :, :MLA_NOPE].reshape(MLA_KV_LORA, -1), MLA_NOPE, 0).astype(BF16),
        w_ukv_v=_pad_heads(ukv[:, :, MLA_NOPE:].reshape(MLA_KV_LORA, -1), MLA_V, 0).astype(BF16),
        g_mla_q_p=jnp.pad(g_mla_q[l], (0, HEAD_PAD - MLA_QK))[None],
        g_mla_k_p=jnp.pad(g_mla_k[l], (0, HEAD_PAD - MLA_QK))[None],
        g_na_q_t=jnp.tile(g_na_q[l], NA_HEADS)[None], g_na_k_t=jnp.tile(g_na_k[l], NA_HEADS)[None],
        sc_w=sc_w[l], cf_w=cf_w[l], cf_b=cf_b[l][None], cf_ln_g=cf_ln_g[l][None], cf_ln_b=cf_ln_b[l][None],
        w_br_a=jnp.pad(w_br[l, 0].reshape(MLA_HEADS, MLA_V, D_MODEL),
                       ((0, 0), (0, HEAD_PAD - MLA_V), (0, 0))).reshape(MLA_PAD_W, D_MODEL).astype(BF16),
        w_br_bcd=w_br[l, 1:].astype(BF16),
        w_o=w_o[l].astype(BF16),
    )
    return W


def kernel(x_prompt, x_sample, cache_mla_ckv, cache_mla_kpe, cache_na_k, cache_na_v, c, c_ctx, w_ada, b_ada, g_norm1, w_in, g_qa, w_uq, g_kva, w_ukv, g_mla_q, g_mla_k, sc_w, g_na_q, g_na_k, na_rpb, cf_w, cf_b, cf_ln_g, cf_ln_b, w_br, w_o, g_norm2, w_ff_gate, w_ff_up, w_ff_down, w_router, w_e_gate, w_e_up, w_e_down):
    x = (x_prompt.reshape(N_CTX_TOK, D_MODEL), x_sample.reshape(N_LAT_TOK, D_MODEL))
    cvec = jnp.concatenate([c_ctx[None, :], c, jnp.zeros((MOD_ROWS - 1 - DEC_BATCH, D_MODEL), F32)], axis=0)
    cache_na_k2 = cache_na_k.reshape(DEC_BATCH, DEPTH, PAST_LEN, BR_W)
    cache_na_v2 = cache_na_v.reshape(DEC_BATCH, DEPTH, PAST_LEN, BR_W)
    w_router_p = jnp.pad(w_router, ((0, 0), (0, 0), (0, ROUTER_LANES - N_EXPERTS)))

    ckv_l, kpe_l, nak_l, nav_l = [], [], [], []
    for l in range(DEPTH):
        W = _layer_weights(l, w_in, g_qa, w_uq, g_kva, w_ukv, g_mla_q, g_mla_k, sc_w, g_na_q, g_na_k,
                           cf_w, cf_b, cf_ln_g, cf_ln_b, w_br, w_o)
        mod3 = _modulation(cvec, w_ada, b_ada, l).reshape(MOD_ROWS, 6, D_MODEL)
        p = _in_projection(x, mod3, g_norm1, W['w_in_p'], l)

        oa_c, ckv_new, kpe_new = _mla(p, l, False, None, None, W)
        kpe_g = jnp.pad(cache_mla_kpe[:, l], ((0, 0), (0, 0), (MLA_NOPE, HEAD_PAD - MLA_QK)))
        (oa_l,) = _mla(p, l, True, cache_mla_ckv, kpe_g, W)
        oc_c, nak_new, nav_new = _na_context(p, W)
        oc_l = _na_latent(p, l, cache_na_k2, cache_na_v2, _na_bias_table(na_rpb[l]), W)
        ob_c, od_c = _convs(p, False, W)
        ob_l, od_l = _convs(p, True, W)

        x_all = _merge(x, mod3, p, (oa_c, oa_l), (ob_c, ob_l), (oc_c, oc_l), (od_c, od_l), W)

        last = l == DEPTH - 1
        if l % 2 == 0:
            x = (_ffn(x_all, mod3, g_norm2, w_ff_gate, w_ff_up, w_ff_down, l, l // 2),)
        else:
            x = _moe(x_all, mod3, g_norm2, w_router_p, w_e_gate, w_e_up, w_e_down, l, l // 2, last)

        ckv_l.append(ckv_new)
        kpe_l.append(kpe_new)
        nak_l.append(nak_new.reshape(BATCH, SEQ, NA_HEADS, NA_HD))
        nav_l.append(nav_new.reshape(BATCH, SEQ, NA_HEADS, NA_HD))

    if len(x) == 1:
        x = (x[0][:N_CTX_TOK], x[0][N_CTX_TOK:])
    y_prompt = x[0].reshape(BATCH, SEQ, D_MODEL)
    y_sample = x[1].reshape(DEC_BATCH, DEC_SEQ, D_MODEL)
    return (y_prompt, y_sample, jnp.stack(ckv_l, axis=1), jnp.stack(kpe_l, axis=1),
            jnp.stack(nak_l, axis=1), jnp.stack(nav_l, axis=1))
```

```python
import functools

import numpy as np
import jax
import jax.numpy as jnp
from jax import lax
from jax.experimental import pallas as pl
from jax.experimental.pallas import tpu as pltpu

F32 = jnp.float32
BF16 = jnp.bfloat16

D_MODEL = 1024
BATCH = 16
SEQ = 256
DEPTH = 2
DEC_BATCH = 8
DEC_SEQ = 1024
PAST_LEN = 512
GRID_W = 64
N_BRANCH = 4
BR_W = 256
MLA_HEADS = 4
MLA_NOPE = 64
MLA_ROPE = 32
MLA_QK = 96
MLA_V = 64
MLA_Q_LORA = 256
MLA_KV_LORA = 128
SC_K = 3
NA_HEADS = 4
NA_HD = 64
NA_WIN_R = 8
NA_WIN_C = 16
CF_K = 31
D_FF = 3584
N_EXPERTS = 8
ROPE_THETA = 10000.0
EPS = 1e-6

N_CTX_TOK = BATCH * SEQ
N_LAT_TOK = DEC_BATCH * DEC_SEQ
N_TOK = N_CTX_TOK + N_LAT_TOK
MOD_ROWS = 16

HEAD_PAD = 128
MLA_PAD_W = MLA_HEADS * HEAD_PAD

P_QA = 0
P_KVA = 256
P_SC = 512
P_NA = 1280
P_CF = 2048
P_GATE = 2560
GATE_BLK = 512
P_COLS = 6656

VMEM_LIMIT = 56 * 1024 * 1024

NA_TQ = 256
NA_WIN_ROWS = 12
NA_WIN_KEYS = NA_WIN_ROWS * GRID_W


def _cparams(sem):
    return pltpu.CompilerParams(dimension_semantics=sem, vmem_limit_bytes=VMEM_LIMIT)


def _const_spec(shape):
    nd = len(shape)
    return pl.BlockSpec(shape, lambda *_: (0,) * nd)


def _mod_row(tok_start):
    return jnp.where(tok_start < N_CTX_TOK, 0, 1 + (tok_start - N_CTX_TOK) // DEC_SEQ)


def _sigmoid(x):
    return 0.5 * jnp.tanh(0.5 * x) + 0.5


def _silu(x):
    return x * _sigmoid(x)


def _ctx_or_lat(is_ctx, c_ref, l_ref):
    return jnp.where(is_ctx, c_ref[...], l_ref[...])


def _dot(a, b):
    return jnp.dot(a.astype(BF16), b.astype(BF16), preferred_element_type=F32)


def _dot_nt(a, b):
    return lax.dot_general(a.astype(BF16), b.astype(BF16), (((1,), (1,)), ((), ())),
                           preferred_element_type=F32)


def _mod_kernel(c_ref, w_ref, b_ref, o_ref):
    o_ref[...] = _dot(_silu(c_ref[...]), w_ref[...]) + b_ref[...]


def _modulation(cvec, w_ada, b_ada, l):
    tn = 1024
    return pl.pallas_call(
        _mod_kernel,
        out_shape=jax.ShapeDtypeStruct((MOD_ROWS, 6 * D_MODEL), F32),
        grid=(6 * D_MODEL // tn,),
        in_specs=[
            _const_spec((MOD_ROWS, D_MODEL)),
            pl.BlockSpec((None, D_MODEL, tn), lambda j: (l, 0, j)),
            pl.BlockSpec((None, 1, tn), lambda j: (l, 0, j)),
        ],
        out_specs=pl.BlockSpec((MOD_ROWS, tn), lambda j: (0, j)),
        compiler_params=_cparams(("arbitrary",)),
        name="modulation",
    )(cvec, w_ada, b_ada.reshape(DEPTH, 1, 6 * D_MODEL))


def _pair_specs(tm, width):
    nc = N_CTX_TOK // tm
    return [pl.BlockSpec((tm, width), lambda i, *_: (jnp.minimum(i, nc - 1), 0)),
            pl.BlockSpec((tm, width), lambda i, *_: (jnp.maximum(i - nc, 0), 0))]


def _pair_shapes(width, dtype):
    return [jax.ShapeDtypeStruct((N_CTX_TOK, width), dtype), jax.ShapeDtypeStruct((N_LAT_TOK, width), dtype)]


def _store_pair(is_ctx, c_ref, l_ref, val):
    @pl.when(is_ctx)
    def _():
        c_ref[...] = val

    @pl.when(jnp.logical_not(is_ctx))
    def _():
        l_ref[...] = val


def _x_specs(x, tm):
    if len(x) == 2:
        return _pair_specs(tm, D_MODEL)
    return [pl.BlockSpec((tm, D_MODEL), lambda i, *_: (i, 0))]


def _x_shapes(split):
    return _pair_shapes(D_MODEL, F32) if split else [jax.ShapeDtypeStruct((N_TOK, D_MODEL), F32)]


def _x_load(x_refs):
    if len(x_refs) == 1:
        return x_refs[0][...]
    tm = x_refs[0].shape[0]
    return _ctx_or_lat(pl.program_id(0) < N_CTX_TOK // tm, *x_refs)


def _x_store(y_refs, val):
    if len(y_refs) == 1:
        y_refs[0][...] = val
    else:
        tm = y_refs[0].shape[0]
        _store_pair(pl.program_id(0) < N_CTX_TOK // tm, *y_refs, val)


def _inproj_kernel(nx, *refs):
    x_refs, (mod_ref, g_ref, w_ref, o_ref, h_scr) = refs[:nx], refs[nx:]

    @pl.when(pl.program_id(1) == 0)
    def _():
        x = _x_load(x_refs)
        y = x * lax.rsqrt(jnp.mean(x * x, axis=-1, keepdims=True) + EPS) * g_ref[...]
        h_scr[...] = (y * (1.0 + mod_ref[1:2, :]) + mod_ref[0:1, :]).astype(BF16)

    o_ref[...] = jnp.dot(h_scr[...], w_ref[...], preferred_element_type=F32).astype(o_ref.dtype)


def _in_projection(x, mod3, g_norm1, w_in_p, l):
    tm, tn = 1024, P_COLS // 4
    return pl.pallas_call(
        functools.partial(_inproj_kernel, len(x)),
        out_shape=jax.ShapeDtypeStruct((N_TOK, P_COLS), BF16),
        grid=(N_TOK // tm, P_COLS // tn),
        in_specs=_x_specs(x, tm) + [
            pl.BlockSpec((None, 6, D_MODEL), lambda i, j: (_mod_row(i * tm), 0, 0)),
            pl.BlockSpec((None, 1, D_MODEL), lambda i, j: (l, 0, 0)),
            pl.BlockSpec((D_MODEL, tn), lambda i, j: (0, j)),
        ],
        out_specs=pl.BlockSpec((tm, tn), lambda i, j: (i, j)),
        scratch_shapes=[pltpu.VMEM((tm, D_MODEL), BF16)],
        compiler_params=_cparams(("arbitrary", "arbitrary")),
        name="in_projection",
    )(*x, mod3, g_norm1.reshape(DEPTH, 1, D_MODEL), w_in_p)


def _rope_tables():
    t = np.arange(DEC_SEQ)
    nf = MLA_ROPE // 4
    inv = (np.float32(ROPE_THETA) ** (-np.arange(nf, dtype=np.float32) / np.float32(nf))).astype(np.float32)
    ang_r = (t // GRID_W).astype(np.float32)[:, None] * inv[None, :]
    ang_c = (t % GRID_W).astype(np.float32)[:, None] * inv[None, :]
    c = np.zeros((DEC_SEQ, HEAD_PAD), np.float32)
    s1 = np.zeros((DEC_SEQ, HEAD_PAD), np.float32)
    s2 = np.zeros((DEC_SEQ, HEAD_PAD), np.float32)
    c[:, :MLA_NOPE] = 1.0
    for base, ang in ((MLA_NOPE, ang_r), (MLA_NOPE + 2 * nf, ang_c)):
        c[:, base:base + nf] = np.cos(ang)
        c[:, base + nf:base + 2 * nf] = np.cos(ang)
        s1[:, base:base + nf] = -np.sin(ang)
        s2[:, base + nf:base + 2 * nf] = np.sin(ang)
    return c, s1, s2


def _rope(x, c, s1, s2):
    nf = MLA_ROPE // 4
    return x * c + pltpu.roll(x, HEAD_PAD - nf, 1) * s1 + pltpu.roll(x, nf, 1) * s2


def _head_norm(xh, g):
    ms = jnp.sum(xh * xh, axis=-1, keepdims=True) * (1.0 / MLA_QK)
    return xh * lax.rsqrt(ms + EPS) * g


def _mla_kernel(latent, tq, *refs):
    if latent:
        (pqa_ref, pkva_ref, cckv_ref, ckpe_ref, rc_ref, rs1_ref, rs2_ref,
         gqa_ref, wuq_ref, gkva_ref, wk_ref, wv_ref, gq_ref, gk_ref,
         o_ref, k_scr, v_scr) = refs
    else:
        (pqa_ref, pkva_ref,
         gqa_ref, wuq_ref, gkva_ref, wk_ref, wv_ref, gq_ref, gk_ref,
         o_ref, ckv_ref, kpe_ref, k_scr, v_scr) = refs
    qi = pl.program_id(1)
    n_past = PAST_LEN if latent else 0

    vlane = lax.broadcasted_iota(jnp.int32, (1, MLA_PAD_W), 1)
    v_ones = jnp.where(vlane % HEAD_PAD == MLA_V, 1.0, 0.0)

    def put_kv(ckvn, kpe_g, row0, rope):
        n = ckvn.shape[0]
        kk = _dot(ckvn, wk_ref[...])
        v_scr[row0:row0 + n, :] = (_dot(ckvn, wv_ref[...]) + v_ones).astype(BF16)
        for h in range(MLA_HEADS):
            sl = slice(h * HEAD_PAD, (h + 1) * HEAD_PAD)
            kh = _head_norm(kk[:, sl] + kpe_g, gk_ref[...])
            if rope:
                kh = _rope(kh, rc_ref[...], rs1_ref[...], rs2_ref[...])
            k_scr[row0:row0 + n, sl] = kh.astype(BF16)

    @pl.when(qi == 0)
    def _():
        if latent:
            put_kv(cckv_ref[...], ckpe_ref[...], 0, False)
        kva = pkva_ref[...].astype(F32)
        ckv = kva[:, :MLA_KV_LORA]
        ckvn = ckv * lax.rsqrt(jnp.mean(ckv * ckv, axis=-1, keepdims=True) + EPS) * gkva_ref[...]
        if not latent:
            ckv_ref[...] = ckvn
            kpe_ref[...] = kva[:, MLA_KV_LORA:MLA_KV_LORA + MLA_ROPE]
        put_kv(ckvn, pltpu.roll(kva[:, MLA_KV_LORA:], MLA_NOPE, 1), n_past, latent)

    qa = pqa_ref[...].astype(F32)
    qan = qa * lax.rsqrt(jnp.mean(qa * qa, axis=-1, keepdims=True) + EPS) * gqa_ref[...]
    q = _dot(qan, wuq_ref[...])
    scale = MLA_QK ** -0.5
    for h in range(MLA_HEADS):
        sl = slice(h * HEAD_PAD, (h + 1) * HEAD_PAD)
        qh = _head_norm(q[:, sl], gq_ref[...])
        if latent:
            rows = pl.ds(pl.multiple_of(qi * tq, tq), tq)
            qh = _rope(qh, rc_ref[rows, :], rs1_ref[rows, :], rs2_ref[rows, :])
        s = _dot_nt(qh * scale, k_scr[:, sl])
        m = jnp.max(s, axis=-1, keepdims=True)
        e = jnp.exp((s - m).astype(BF16))
        o = jnp.dot(e, v_scr[:, sl], preferred_element_type=F32)
        o_ref[:, sl] = (o / o[:, MLA_V:MLA_V + 1]).astype(o_ref.dtype)


def _mla(p, l, latent, cache_ckv, cache_kpe_g, W):
    if latent:
        nb, s, tq, row_off = DEC_BATCH, DEC_SEQ, 256, N_CTX_TOK
    else:
        nb, s, tq, row_off = BATCH, SEQ, 256, 0
    nq = s // tq
    sk = s + (PAST_LEN if latent else 0)
    in_specs = [
        pl.BlockSpec((tq, MLA_Q_LORA), lambda b, qi: (row_off // tq + b * nq + qi, P_QA // MLA_Q_LORA)),
        pl.BlockSpec((s, 256), lambda b, qi: (row_off // s + b, P_KVA // 256)),
    ]
    args = [p, p]
    if latent:
        in_specs += [
            pl.BlockSpec((None, None, PAST_LEN, MLA_KV_LORA), lambda b, qi: (b, l, 0, 0)),
            pl.BlockSpec((None, PAST_LEN, HEAD_PAD), lambda b, qi: (b, 0, 0)),
            _const_spec((DEC_SEQ, HEAD_PAD)), _const_spec((DEC_SEQ, HEAD_PAD)), _const_spec((DEC_SEQ, HEAD_PAD)),
        ]
        args += [cache_ckv, cache_kpe_g] + [jnp.asarray(t) for t in _rope_tables()]
    in_specs += [
        _const_spec((1, MLA_Q_LORA)), _const_spec((MLA_Q_LORA, MLA_PAD_W)), _const_spec((1, MLA_KV_LORA)),
        _const_spec((MLA_KV_LORA, MLA_PAD_W)), _const_spec((MLA_KV_LORA, MLA_PAD_W)),
        _const_spec((1, HEAD_PAD)), _const_spec((1, HEAD_PAD)),
    ]
    args += [W['g_qa'], W['w_uq_p'], W['g_kva'], W['w_ukv_k'], W['w_ukv_v'], W['g_mla_q_p'], W['g_mla_k_p']]
    out_shape = [jax.ShapeDtypeStruct((nb * s, MLA_PAD_W), BF16)]
    out_specs = [pl.BlockSpec((tq, MLA_PAD_W), lambda b, qi: (b * nq + qi, 0))]
    if not latent:
        out_shape.append(jax.ShapeDtypeStruct((nb, s, MLA_KV_LORA), F32))
        out_specs.append(pl.BlockSpec((None, s, MLA_KV_LORA), lambda b, qi: (b, 0, 0)))
        out_shape.append(jax.ShapeDtypeStruct((nb, s, MLA_ROPE), F32))
        out_specs.append(pl.BlockSpec((None, s, MLA_ROPE), lambda b, qi: (b, 0, 0)))
    return pl.pallas_call(
        functools.partial(_mla_kernel, latent, tq),
        out_shape=out_shape,
        grid=(nb, nq),
        in_specs=in_specs,
        out_specs=out_specs,
        scratch_shapes=[pltpu.VMEM((sk, MLA_PAD_W), BF16), pltpu.VMEM((sk, MLA_PAD_W), BF16)],
        compiler_params=_cparams(("arbitrary", "arbitrary")),
        name="mla_latent" if latent else "mla_context",
    )(*args)


def _head_masks(width):
    lane = lax.broadcasted_iota(jnp.int32, (1, width), 1)
    return [(lane >= h * NA_HD) & (lane < (h + 1) * NA_HD) for h in range(NA_HEADS)]


def _group_norm64(x, g, masks):
    x2 = x * x
    inv = jnp.zeros_like(x)
    for m in masks:
        ms = jnp.sum(jnp.where(m, x2, 0.0), axis=-1, keepdims=True) * (1.0 / NA_HD)
        inv = jnp.where(m, lax.rsqrt(ms + EPS), inv)
    return x * inv * g


def _p_blocks(col0, n, rows, row_fn):
    def spec(k):
        return pl.BlockSpec((rows, BR_W), lambda *g: (row_fn(*g), col0 // BR_W + k))
    return [spec(k) for k in range(n)]


def _na_ctx_kernel(pq_ref, pk_ref, pv_ref, gq_ref, gk_ref, o_ref, k_ref, v_ref):
    masks = _head_masks(BR_W)
    qn = _group_norm64(pq_ref[...].astype(F32), gq_ref[...], masks)
    kn = _group_norm64(pk_ref[...].astype(F32), gk_ref[...], masks)
    v = pv_ref[...].astype(F32)
    k_ref[...] = kn
    v_ref[...] = v
    scale = NA_HD ** -0.5
    acc = jnp.zeros((pq_ref.shape[0], BR_W), F32)
    for m in masks:
        s = _dot_nt(jnp.where(m, qn, 0.0), kn) * scale
        mx = jnp.max(s, axis=-1, keepdims=True)
        e = jnp.exp(s - mx)
        den = jnp.sum(e, axis=-1, keepdims=True)
        acc = acc + jnp.where(m, _dot(e, v) / den, 0.0)
    o_ref[...] = acc.astype(o_ref.dtype)


def _na_context(p, W):
    s = SEQ
    return pl.pallas_call(
        _na_ctx_kernel,
        out_shape=[jax.ShapeDtypeStruct((N_CTX_TOK, BR_W), BF16),
                   jax.ShapeDtypeStruct((N_CTX_TOK, BR_W), F32),
                   jax.ShapeDtypeStruct((N_CTX_TOK, BR_W), F32)],
        grid=(BATCH,),
        in_specs=_p_blocks(P_NA, 3, s, lambda b: b) + [_const_spec((1, BR_W)), _const_spec((1, BR_W))],
        out_specs=[pl.BlockSpec((s, BR_W), lambda b: (b, 0))] * 3,
        compiler_params=_cparams(("arbitrary",)),
        name="na_context",
    )(p, p, p, W['g_na_q_t'], W['g_na_k_t'])


def _na_lat_kernel(pq_ref, pk_ref, pv_ref, ck_ref, cv_ref, bias_ref, gq_ref, gk_ref, o_ref,
                   q_scr, k_scr, v_scr, kc_scr, vc_scr):
    j = pl.program_id(1)
    masks = _head_masks(BR_W)

    zero = jnp.zeros((), BF16)
    one = jnp.ones((), BF16)
    lane = lax.broadcasted_iota(jnp.int32, (1, BR_W), 1)
    den_lanes = [((h + 1) % NA_HEADS) * NA_HD for h in range(NA_HEADS)]

    @pl.when(j == 0)
    def _():
        scale = NA_HD ** -0.5
        q_scr[...] = (_group_norm64(pq_ref[...].astype(F32), gq_ref[...], masks) * scale).astype(BF16)
        k_scr[...] = _group_norm64(pk_ref[...].astype(F32), gk_ref[...], masks).astype(BF16)
        kc_scr[...] = ck_ref[...].astype(BF16)
        v = pv_ref[...]
        vc = cv_ref[...].astype(BF16)
        for h, m in enumerate(masks):
            v_scr[h] = jnp.where(lane == den_lanes[h], one, jnp.where(m, v, zero))
            vc_scr[h] = jnp.where(lane == den_lanes[h], one, jnp.where(m, vc, zero))

    win0 = pl.multiple_of(jnp.where(j < 2, 0, DEC_SEQ - NA_WIN_KEYS), 256)
    q = q_scr[pl.ds(pl.multiple_of(j * NA_TQ, NA_TQ), NA_TQ), :]
    kw = k_scr[pl.ds(win0, NA_WIN_KEYS), :]
    kc = kc_scr[...]
    acc = jnp.zeros((NA_TQ, BR_W), F32)
    for h, m in enumerate(masks):
        qm = jnp.where(m, q, zero)
        s_loc = _dot_nt(qm, kw) + bias_ref[h]
        s_ctx = _dot_nt(qm, kc)
        mx = jnp.maximum(jnp.max(s_loc, axis=-1, keepdims=True), jnp.max(s_ctx, axis=-1, keepdims=True))
        e_loc = jnp.exp((s_loc - mx).astype(BF16))
        e_ctx = jnp.exp((s_ctx - mx).astype(BF16))
        o = (jnp.dot(e_loc, v_scr[h, pl.ds(win0, NA_WIN_KEYS), :], preferred_element_type=F32)
             + jnp.dot(e_ctx, vc_scr[h], preferred_element_type=F32))
        acc = acc + jnp.where(m, o / o[:, den_lanes[h]:den_lanes[h] + 1], 0.0)
    o_ref[...] = acc.astype(o_ref.dtype)


def _na_bias_table(rpb):
    n_dr, n_dc = 2 * NA_WIN_R - 1, 2 * NA_WIN_C - 1
    cols = np.arange(GRID_W)
    dc = np.clip(cols[None, :] - cols[:, None], -(NA_WIN_C - 1), NA_WIN_C - 1) + (NA_WIN_C - 1)
    cstart = np.clip(cols - NA_WIN_C // 2, 0, GRID_W - NA_WIN_C)
    col_ok = (cols[None, :] >= cstart[:, None]) & (cols[None, :] < cstart[:, None] + NA_WIN_C)
    place = (dc.reshape(-1)[None, :] == np.arange(n_dc)[:, None]).astype(np.float32)
    blocks = jnp.dot(rpb.reshape(NA_HEADS * n_dr, n_dc), place, precision=lax.Precision.HIGHEST)
    blocks = jnp.where(col_ok.reshape(-1)[None, :], blocks, -jnp.inf).reshape(NA_HEADS, n_dr, GRID_W, GRID_W)
    neg = jnp.full((NA_HEADS, 1, GRID_W, GRID_W), -jnp.inf, F32)
    blocks = jnp.concatenate([blocks, neg], axis=1)

    nt = DEC_SEQ // NA_TQ
    n_rows = DEC_SEQ // GRID_W
    rows_per_tile = NA_TQ // GRID_W
    sel = np.full((nt, rows_per_tile, NA_WIN_ROWS), n_dr, np.int32)
    for j in range(nt):
        win_row0 = 0 if j < nt // 2 else n_rows - NA_WIN_ROWS
        for rq in range(rows_per_tile):
            r = j * rows_per_tile + rq
            start = min(max(r - NA_WIN_R // 2, 0), n_rows - NA_WIN_R)
            for kr in range(NA_WIN_ROWS):
                if start <= win_row0 + kr < start + NA_WIN_R:
                    sel[j, rq, kr] = win_row0 + kr - r + (NA_WIN_R - 1)
    tiles = []
    for j in range(nt):
        rows = [jnp.concatenate([blocks[:, sel[j, rq, kr]] for kr in range(NA_WIN_ROWS)], axis=-1)
                for rq in range(rows_per_tile)]
        tiles.append(jnp.concatenate(rows, axis=1))
    return jnp.stack(tiles, axis=1)


def _na_latent(p, l, cache_k, cache_v, bias, W):
    s = DEC_SEQ
    nt = s // NA_TQ
    row_off = N_CTX_TOK
    return pl.pallas_call(
        _na_lat_kernel,
        out_shape=jax.ShapeDtypeStruct((N_LAT_TOK, BR_W), BF16),
        grid=(DEC_BATCH, nt),
        in_specs=_p_blocks(P_NA, 3, s, lambda b, j: row_off // s + b) + [
                  pl.BlockSpec((None, None, PAST_LEN, BR_W), lambda b, j: (b, l, 0, 0)),
                  pl.BlockSpec((None, None, PAST_LEN, BR_W), lambda b, j: (b, l, 0, 0)),
                  pl.BlockSpec((NA_HEADS, None, NA_TQ, NA_WIN_KEYS), lambda b, j: (0, j, 0, 0)),
                  _const_spec((1, BR_W)), _const_spec((1, BR_W))],
        out_specs=pl.BlockSpec((NA_TQ, BR_W), lambda b, j: (b * nt + j, 0)),
        scratch_shapes=[pltpu.VMEM((s, BR_W), BF16), pltpu.VMEM((s, BR_W), BF16),
                        pltpu.VMEM((NA_HEADS, s, BR_W), BF16),
                        pltpu.VMEM((PAST_LEN, BR_W), BF16), pltpu.VMEM((NA_HEADS, PAST_LEN, BR_W), BF16)],
        compiler_params=_cparams(("arbitrary", "arbitrary")),
        name="na_latent",
    )(p, p, p, cache_k, cache_v, bias, W['g_na_q_t'], W['g_na_k_t'])


CONV_HALO = 16
CONV_CHUNK = 128


SUBLANES = 8


def _dwconv_from_pad(pad_ref, w_ref, ksize, s, emit, shift_ref=None):
    half = ksize // 2
    if shift_ref is not None:
        n_rows = s + 2 * CONV_HALO - SUBLANES
        for p in range(1, SUBLANES):
            for c0 in range(0, n_rows, CONV_CHUNK):
                n = min(CONV_CHUNK, n_rows - c0)
                shift_ref[p - 1, c0:c0 + n, :] = pad_ref[c0 + p:c0 + p + n, :]
    for c0 in range(0, s, CONV_CHUNK):
        acc = jnp.zeros((CONV_CHUNK, BR_W), F32)
        for k in range(ksize):
            r0 = CONV_HALO + c0 + k - half
            p = r0 % SUBLANES
            if shift_ref is None or p == 0:
                win = pad_ref[r0:r0 + CONV_CHUNK, :]
            else:
                win = shift_ref[p - 1, r0 - p:r0 - p + CONV_CHUNK, :]
            acc = acc + win * w_ref[k:k + 1, :]
        emit(c0, acc)


def _conv_kernel(s, scb_ref, scc_ref, scx_ref, cfa_ref, cfb2_ref, scw_ref, cfw_ref, cfb_ref, lng_ref, lnb_ref,
                 ob_ref, od_ref, pad_ref, shift_ref):
    zeros = jnp.zeros((CONV_HALO, BR_W), F32)
    pad_ref[0:CONV_HALO, :] = zeros
    pad_ref[CONV_HALO + s:2 * CONV_HALO + s, :] = zeros

    pad_ref[CONV_HALO:CONV_HALO + s, :] = scc_ref[...].astype(F32) * scx_ref[...].astype(F32)

    def emit_b(c0, acc):
        ob_ref[c0:c0 + CONV_CHUNK, :] = (scb_ref[c0:c0 + CONV_CHUNK, :].astype(F32) * acc).astype(ob_ref.dtype)

    _dwconv_from_pad(pad_ref, scw_ref, SC_K, s, emit_b)

    pad_ref[CONV_HALO:CONV_HALO + s, :] = cfa_ref[...].astype(F32) * _sigmoid(cfb2_ref[...].astype(F32))

    def emit_d(c0, acc):
        u = acc + cfb_ref[...]
        mu = jnp.mean(u, axis=-1, keepdims=True)
        d = u - mu
        var = jnp.mean(d * d, axis=-1, keepdims=True)
        y = d * lax.rsqrt(var + EPS) * lng_ref[...] + lnb_ref[...]
        od_ref[c0:c0 + CONV_CHUNK, :] = _silu(y).astype(od_ref.dtype)

    _dwconv_from_pad(pad_ref, cfw_ref, CF_K, s, emit_d, shift_ref)


def _convs(p, latent, W):
    if latent:
        nb, s, row_off = DEC_BATCH, DEC_SEQ, N_CTX_TOK
    else:
        nb, s, row_off = BATCH, SEQ, 0
    return pl.pallas_call(
        functools.partial(_conv_kernel, s),
        out_shape=[jax.ShapeDtypeStruct((nb * s, BR_W), BF16)] * 2,
        grid=(nb,),
        in_specs=(_p_blocks(P_SC, 3, s, lambda b: row_off // s + b)
                  + _p_blocks(P_CF, 2, s, lambda b: row_off // s + b)
                  + [_const_spec((SC_K, BR_W)), _const_spec((CF_K, BR_W)),
                     _const_spec((1, BR_W)), _const_spec((1, BR_W)), _const_spec((1, BR_W))]),
        out_specs=[pl.BlockSpec((s, BR_W), lambda b: (b, 0))] * 2,
        scratch_shapes=[pltpu.VMEM((s + 2 * CONV_HALO, BR_W), F32),
                        pltpu.VMEM((SUBLANES - 1, s + 2 * CONV_HALO, BR_W), F32)],
        compiler_params=_cparams(("arbitrary",)),
        name="convs_latent" if latent else "convs_context",
    )(p, p, p, p, p, W['sc_w'], W['cf_w'], W['cf_b'], W['cf_ln_g'], W['cf_ln_b'])


def _merge_kernel(nc, nx, *refs):
    x_refs = refs[:nx]
    (mod_ref, oac_ref, oal_ref, obc_ref, obl_ref, occ_ref, ocl_ref, odc_ref, odl_ref) = refs[nx:nx + 9]
    gate_refs = refs[nx + 9:nx + 9 + 2 * N_BRANCH]
    wa_ref, wb_ref, wo_ref, y_ref = refs[nx + 9 + 2 * N_BRANCH:]
    is_ctx = pl.program_id(0) < nc

    def gate(n):
        return jnp.concatenate([_sigmoid(gate_refs[2 * n][...]), _sigmoid(gate_refs[2 * n + 1][...])],
                               axis=1).astype(F32)

    merged = gate(0) * jnp.dot(_ctx_or_lat(is_ctx, oac_ref, oal_ref), wa_ref[...], preferred_element_type=F32)
    for n, (oc_ref, ol_ref) in enumerate(((obc_ref, obl_ref), (occ_ref, ocl_ref), (odc_ref, odl_ref))):
        merged = merged + gate(n + 1) * jnp.dot(_ctx_or_lat(is_ctx, oc_ref, ol_ref), wb_ref[n],
                                                preferred_element_type=F32)
    y_ref[...] = _x_load(x_refs) + mod_ref[2:3, :] * _dot(merged, wo_ref[...])


def _merge(x, mod3, p, o_a, o_b, o_c, o_d, W):
    tm = 512
    gate_spec = lambda k: pl.BlockSpec((tm, GATE_BLK), lambda i: (i, P_GATE // GATE_BLK + k))
    n_gate_blk = N_BRANCH * D_MODEL // GATE_BLK
    return pl.pallas_call(
        functools.partial(_merge_kernel, N_CTX_TOK // tm, len(x)),
        out_shape=jax.ShapeDtypeStruct((N_TOK, D_MODEL), F32),
        grid=(N_TOK // tm,),
        in_specs=(_x_specs(x, tm)
                  + [pl.BlockSpec((None, 6, D_MODEL), lambda i: (_mod_row(i * tm), 0, 0))]
                  + _pair_specs(tm, MLA_PAD_W) + _pair_specs(tm, BR_W) + _pair_specs(tm, BR_W) + _pair_specs(tm, BR_W)
                  + [gate_spec(k) for k in range(n_gate_blk)]
                  + [_const_spec((MLA_PAD_W, D_MODEL)), _const_spec((3, BR_W, D_MODEL)),
                     _const_spec((D_MODEL, D_MODEL))]),
        out_specs=pl.BlockSpec((tm, D_MODEL), lambda i: (i, 0)),
        compiler_params=_cparams(("arbitrary",)),
        name="merge",
    )(*x, mod3, *o_a, *o_b, *o_c, *o_d, *([p] * n_gate_blk), W['w_br_a'], W['w_br_bcd'], W['w_o'])


def _norm2(x, g, mod_ref):
    y = x * lax.rsqrt(jnp.mean(x * x, axis=-1, keepdims=True) + EPS) * g
    return y * (1.0 + mod_ref[4:5, :]) + mod_ref[3:4, :]


def _ffn_kernel(x_ref, mod_ref, g_ref, wg_ref, wu_ref, wd_ref, y_ref, h_scr, acc_scr):
    f = pl.program_id(1)

    @pl.when(f == 0)
    def _():
        h_scr[...] = _norm2(x_ref[...], g_ref[...], mod_ref).astype(BF16)
        acc_scr[...] = jnp.zeros_like(acc_scr)

    h = h_scr[...]
    a = _silu(_dot(h, wg_ref[...])) * _dot(h, wu_ref[...])
    acc_scr[...] += _dot(a, wd_ref[...])

    @pl.when(f == pl.num_programs(1) - 1)
    def _():
        y_ref[...] = x_ref[...] + mod_ref[5:6, :] * acc_scr[...]


def _ffn(x_all, mod3, g_norm2, w_g, w_u, w_d, l, j):
    tm, tf = 1024, 512
    return pl.pallas_call(
        _ffn_kernel,
        out_shape=jax.ShapeDtypeStruct((N_TOK, D_MODEL), F32),
        grid=(N_TOK // tm, D_FF // tf),
        in_specs=[pl.BlockSpec((tm, D_MODEL), lambda i, f: (i, 0)),
                  pl.BlockSpec((None, 6, D_MODEL), lambda i, f: (_mod_row(i * tm), 0, 0)),
                  pl.BlockSpec((None, 1, D_MODEL), lambda i, f: (l, 0, 0)),
                  pl.BlockSpec((None, D_MODEL, tf), lambda i, f: (j, 0, f)),
                  pl.BlockSpec((None, D_MODEL, tf), lambda i, f: (j, 0, f)),
                  pl.BlockSpec((None, tf, D_MODEL), lambda i, f: (j, f, 0))],
        out_specs=pl.BlockSpec((tm, D_MODEL), lambda i, f: (i, 0)),
        scratch_shapes=[pltpu.VMEM((tm, D_MODEL), BF16), pltpu.VMEM((tm, D_MODEL), F32)],
        compiler_params=_cparams(("arbitrary", "arbitrary")),
        name="ffn_dense",
    )(x_all, mod3, g_norm2.reshape(DEPTH, 1, D_MODEL), w_g, w_u, w_d)


ROUTER_LANES = 128
ROUTE_TM = 512
MOE_TM = 1024
MOE_SUB = 256
MOE_TF = 512
RUN_ALIGN = 8
MOE_TILES = -(-(2 * N_TOK + (N_TOK // ROUTE_TM) * N_EXPERTS * RUN_ALIGN + N_EXPERTS * MOE_TM) // MOE_TM)
MOE_ROWS = MOE_TILES * MOE_TM
R_I1, R_I2, R_W1, R_W2, R_RANK1, R_RANK2 = range(6)


def _route_top2(h, wr):
    h_hi = h.astype(BF16)
    h_lo = (h - h_hi.astype(F32)).astype(BF16)
    w_hi = wr.astype(BF16)
    w_lo = (wr - w_hi.astype(F32)).astype(BF16)
    logits = (jnp.dot(h_hi, w_hi, preferred_element_type=F32) + jnp.dot(h_lo, w_hi, preferred_element_type=F32)
              + jnp.dot(h_hi, w_lo, preferred_element_type=F32))
    lane = lax.broadcasted_iota(jnp.int32, logits.shape, 1).astype(F32)
    neg = jnp.float32(-jnp.inf)
    logits = jnp.where(lane < N_EXPERTS, logits, neg)
    m1 = jnp.max(logits, axis=-1, keepdims=True)
    i1 = jnp.min(jnp.where(logits == m1, lane, float(ROUTER_LANES)), axis=-1, keepdims=True)
    rest = jnp.where(lane == i1, neg, logits)
    m2 = jnp.max(rest, axis=-1, keepdims=True)
    i2 = jnp.min(jnp.where(rest == m2, lane, float(ROUTER_LANES)), axis=-1, keepdims=True)
    e2 = jnp.exp(m2 - m1)
    return lane, i1, i2, 1.0 / (1.0 + e2), e2 / (1.0 + e2)


def _route_kernel(x_ref, mod_ref, g_ref, wr_ref, route_ref, cnt_ref, tcarry_ref, carry_scr):
    @pl.when(pl.program_id(0) == 0)
    def _():
        carry_scr[...] = jnp.zeros_like(carry_scr)

    tcarry_ref[...] = carry_scr[...]

    h = _norm2(x_ref[...], g_ref[...], mod_ref)
    lane, i1, i2, w1, w2 = _route_top2(h, wr_ref[...])
    tm = h.shape[0]
    oh1 = lane == i1
    oh2 = lane == i2
    oh = jnp.where(oh1, 1.0, 0.0) + jnp.where(oh2, 1.0, 0.0)
    r = lax.broadcasted_iota(jnp.int32, (tm, tm), 0)
    c = lax.broadcasted_iota(jnp.int32, (tm, tm), 1)
    lower = jnp.where(r > c, 1.0, 0.0).astype(BF16)
    before = jnp.dot(lower, oh.astype(BF16), preferred_element_type=F32)
    rank1 = jnp.sum(jnp.where(oh1, before, 0.0), axis=-1, keepdims=True)
    rank2 = jnp.sum(jnp.where(oh2, before, 0.0), axis=-1, keepdims=True)
    carry_scr[...] += jnp.sum(oh, axis=0, keepdims=True)
    rec = jnp.zeros_like(lane)
    for k, v in ((R_I1, i1), (R_I2, i2), (R_W1, w1), (R_W2, w2), (R_RANK1, rank1), (R_RANK2, rank2)):
        rec = jnp.where(lane == float(k), v, rec)
    route_ref[...] = rec
    cnt_ref[...] = carry_scr[...]


def _moe_route(x_all, mod3, g_norm2, w_router_p, l, j):
    tm = ROUTE_TM
    return pl.pallas_call(
        _route_kernel,
        out_shape=[jax.ShapeDtypeStruct((N_TOK, ROUTER_LANES), F32),
                   jax.ShapeDtypeStruct((1, ROUTER_LANES), F32),
                   jax.ShapeDtypeStruct((N_TOK // tm, 1, ROUTER_LANES), F32)],
        grid=(N_TOK // tm,),
        in_specs=[pl.BlockSpec((tm, D_MODEL), lambda i: (i, 0)),
                  pl.BlockSpec((None, 6, D_MODEL), lambda i: (_mod_row(i * tm), 0, 0)),
                  pl.BlockSpec((None, 1, D_MODEL), lambda i: (l, 0, 0)),
                  pl.BlockSpec((None, D_MODEL, ROUTER_LANES), lambda i: (j, 0, 0))],
        out_specs=[pl.BlockSpec((tm, ROUTER_LANES), lambda i: (i, 0)),
                   pl.BlockSpec((1, ROUTER_LANES), lambda i: (0, 0)),
                   pl.BlockSpec((None, 1, ROUTER_LANES), lambda i: (i, 0, 0))],
        scratch_shapes=[pltpu.VMEM((1, ROUTER_LANES), F32)],
        compiler_params=_cparams(("arbitrary",)),
        name="moe_route",
    )(x_all, mod3, g_norm2.reshape(DEPTH, 1, D_MODEL), w_router_p)


ZERO_ROWS = 512
N_GAPS = N_EXPERTS + 1


def _bit_chunks(n_units, max_units, make, wait=False):
    for b in range(max_units.bit_length()):
        units = 1 << b

        @pl.when((n_units & units) != 0)
        def _():
            cp = make(pl.multiple_of((n_units & (units - 1)) * RUN_ALIGN, RUN_ALIGN), units * RUN_ALIGN)
            if wait:
                cp.wait()
            else:
                cp.start()


def _dispatch_kernel(n_ref, lo_ref, dst_ref, gap0_ref, gapn_ref, x_ref, mod_ref, g_ref, route_ref, lovec_ref,
                     xs_ref, z_scr, zero_scr, sems, zsem):
    i = pl.program_id(0)
    tm = x_ref.shape[0]
    h = _norm2(x_ref[...], g_ref[...], mod_ref).astype(BF16)
    rec = route_ref[...]
    lane = lax.broadcasted_iota(jnp.int32, rec.shape, 1).astype(F32)
    lo_row = lovec_ref[...]

    def local_pos(i_lane, r_lane):
        lp = (jnp.sum(jnp.where(lane == rec[:, i_lane:i_lane + 1], lo_row, 0.0), axis=-1, keepdims=True)
              + rec[:, r_lane:r_lane + 1])
        return jnp.transpose(jnp.broadcast_to(lp, (tm, ROUTER_LANES)))[0:1, :]

    slot = lax.broadcasted_iota(jnp.int32, (z_scr.shape[1], tm), 0).astype(F32)
    perm = jnp.where((slot == local_pos(R_I1, R_RANK1)) | (slot == local_pos(R_I2, R_RANK2)), 1.0, 0.0)
    buf = i % 2
    z_scr[buf] = jnp.dot(perm.astype(BF16), h, preferred_element_type=F32)

    def run_copies(tile, b, wait):
        for e in range(N_EXPERTS):
            k = tile * N_EXPERTS + e
            src0 = lo_ref[k]
            dst0 = dst_ref[k]

            def make(off, rows, src0=src0, dst0=dst0):
                return pltpu.make_async_copy(
                    z_scr.at[b, pl.ds(pl.multiple_of(src0 + off, RUN_ALIGN), rows), :],
                    xs_ref.at[pl.ds(pl.multiple_of(dst0 + off, RUN_ALIGN), rows), :], sems.at[b])

            _bit_chunks(n_ref[k], tm // RUN_ALIGN, make, wait)

    run_copies(i, buf, False)

    @pl.when(i > 0)
    def _():
        run_copies(i - 1, 1 - buf, True)

    @pl.when(i == pl.num_programs(0) - 1)
    def _():
        run_copies(i, buf, True)

    @pl.when(i == pl.num_programs(0) - 1)
    def _():
        zero_scr[...] = jnp.zeros_like(zero_scr)
        for wait in (False, True):
            for g in range(N_GAPS):
                start = gap0_ref[g]
                n = gapn_ref[g]
                max_rows = MOE_TM if g < N_EXPERTS else MOE_ROWS - 2 * N_TOK
                for c in range(max_rows // ZERO_ROWS):
                    @pl.when(n >= (c + 1) * ZERO_ROWS)
                    def _():
                        cp = pltpu.make_async_copy(
                            zero_scr,
                            xs_ref.at[pl.ds(pl.multiple_of(start + c * ZERO_ROWS, RUN_ALIGN), ZERO_ROWS), :], zsem)
                        if wait:
                            cp.wait()
                        else:
                            cp.start()

                def make(off, rows, start=start, n=n):
                    tail0 = start + (n // ZERO_ROWS) * ZERO_ROWS
                    return pltpu.make_async_copy(
                        zero_scr.at[pl.ds(0, rows), :],
                        xs_ref.at[pl.ds(pl.multiple_of(tail0 + off, RUN_ALIGN), rows), :], zsem)

                _bit_chunks((n % ZERO_ROWS) // RUN_ALIGN, ZERO_ROWS // RUN_ALIGN - 1, make, wait)


def _moe_dispatch(n_tile, lo_tile, dst_tile, gap0, gapn, x_all, mod3, g_norm2, route, lo_vec, l):
    tm = ROUTE_TM
    spec = lambda shape, fn: pl.BlockSpec(shape, lambda i, *_: fn(i))
    return pl.pallas_call(
        _dispatch_kernel,
        out_shape=jax.ShapeDtypeStruct((MOE_ROWS, D_MODEL), F32),
        grid_spec=pltpu.PrefetchScalarGridSpec(
            num_scalar_prefetch=5,
            grid=(N_TOK // tm,),
            in_specs=[spec((tm, D_MODEL), lambda i: (i, 0)),
                      spec((None, 6, D_MODEL), lambda i: (_mod_row(i * tm), 0, 0)),
                      spec((None, 1, D_MODEL), lambda i: (l, 0, 0)),
                      spec((tm, ROUTER_LANES), lambda i: (i, 0)),
                      spec((None, 1, ROUTER_LANES), lambda i: (i, 0, 0))],
            out_specs=pl.BlockSpec(memory_space=pl.ANY),
            scratch_shapes=[pltpu.VMEM((2, 2 * tm + N_EXPERTS * RUN_ALIGN, D_MODEL), F32),
                            pltpu.VMEM((ZERO_ROWS, D_MODEL), F32),
                            pltpu.SemaphoreType.DMA((2,)), pltpu.SemaphoreType.DMA],
        ),
        compiler_params=_cparams(("arbitrary",)),
        name="moe_dispatch",
    )(n_tile, lo_tile, dst_tile, gap0, gapn, x_all, mod3, g_norm2.reshape(DEPTH, 1, D_MODEL), route, lo_vec)


def _gmm_kernel(te_ref, tr_ref, xs_ref, wg_ref, wu_ref, wd_ref, y_ref, wg_scr, wu_scr, wd_scr):
    del te_ref
    g = pl.program_id(0)
    f = pl.program_id(1)
    rows = tr_ref[g]

    @pl.when(f == 0)
    def _():
        y_ref[...] = jnp.zeros_like(y_ref)

    def sub_tile(s, wg, wu, wd):
        sl = slice(s * MOE_SUB, (s + 1) * MOE_SUB)
        x = xs_ref[sl, :].astype(BF16)
        a = (_silu(jnp.dot(x, wg, preferred_element_type=F32)) * jnp.dot(x, wu, preferred_element_type=F32))
        y_ref[sl, :] += jnp.dot(a.astype(BF16), wd, preferred_element_type=F32)

    @pl.when(rows == MOE_TM)
    def _():
        x = xs_ref[...].astype(BF16)
        a = _silu(_dot(x, wg_ref[...])) * _dot(x, wu_ref[...])
        y_ref[...] += _dot(a, wd_ref[...])

    @pl.when((rows > 0) & (rows < MOE_TM))
    def _():
        wg = wg_ref[...].astype(BF16)
        wu = wu_ref[...].astype(BF16)
        wd = wd_ref[...].astype(BF16)
        wg_scr[...] = wg
        wu_scr[...] = wu
        wd_scr[...] = wd
        sub_tile(0, wg, wu, wd)

    for s in range(1, MOE_TM // MOE_SUB):
        @pl.when((rows > s * MOE_SUB) & (rows < MOE_TM))
        def _():
            sub_tile(s, wg_scr[...], wu_scr[...], wd_scr[...])


def _moe_gmm(tile_expert, tile_rows, xs, w_g, w_u, w_d, j):
    tm, tf = MOE_TM, MOE_TF
    nf = D_FF // tf

    def f_eff(g, f, tr):
        return jnp.where(tr[g] > 0, f, nf - 1)

    return pl.pallas_call(
        _gmm_kernel,
        out_shape=jax.ShapeDtypeStruct((MOE_ROWS, D_MODEL), F32),
        grid_spec=pltpu.PrefetchScalarGridSpec(
            num_scalar_prefetch=2,
            grid=(MOE_TILES, nf),
            in_specs=[pl.BlockSpec((tm, D_MODEL), lambda g, f, te, tr: (g, 0)),
                      pl.BlockSpec((None, None, D_MODEL, tf), lambda g, f, te, tr: (j, te[g], 0, f_eff(g, f, tr))),
                      pl.BlockSpec((None, None, D_MODEL, tf), lambda g, f, te, tr: (j, te[g], 0, f_eff(g, f, tr))),
                      pl.BlockSpec((None, None, tf, D_MODEL), lambda g, f, te, tr: (j, te[g], f_eff(g, f, tr), 0))],
            out_specs=pl.BlockSpec((tm, D_MODEL), lambda g, f, te, tr: (g, 0)),
            scratch_shapes=[pltpu.VMEM((D_MODEL, tf), BF16), pltpu.VMEM((D_MODEL, tf), BF16),
                            pltpu.VMEM((tf, D_MODEL), BF16)],
        ),
        compiler_params=_cparams(("arbitrary", "arbitrary")),
        name="moe_experts",
    )(tile_expert, tile_rows, xs, w_g, w_u, w_d)


ZY_ROWS = 1152


def _combine_kernel(n_out, n_ref, lo_ref, dst_ref, x_ref, mod_ref, route_ref, lovec_ref, y_ref, *refs):
    o_refs, (zy_scr, sems) = refs[:n_out], refs[n_out:]
    i = pl.program_id(0)
    tm = x_ref.shape[0]

    def run_copies(tile, b, wait):
        for e in range(N_EXPERTS):
            k = tile * N_EXPERTS + e
            src0 = dst_ref[k]
            dst0 = lo_ref[k]

            def make(off, rows, src0=src0, dst0=dst0):
                return pltpu.make_async_copy(
                    y_ref.at[pl.ds(pl.multiple_of(src0 + off, RUN_ALIGN), rows), :],
                    zy_scr.at[b, pl.ds(pl.multiple_of(dst0 + off, RUN_ALIGN), rows), :], sems.at[b])

            _bit_chunks(n_ref[k], tm // RUN_ALIGN, make, wait)

    @pl.when(i == 0)
    def _():
        zy_scr[...] = jnp.zeros_like(zy_scr)
        run_copies(0, 0, False)

    buf = i % 2

    @pl.when(i + 1 < pl.num_programs(0))
    def _():
        run_copies(i + 1, 1 - buf, False)

    run_copies(i, buf, True)

    rec = route_ref[...]
    lane = lax.broadcasted_iota(jnp.int32, rec.shape, 1).astype(F32)
    lo_row = lovec_ref[...]

    def local_pos(i_lane, r_lane):
        return (jnp.sum(jnp.where(lane == rec[:, i_lane:i_lane + 1], lo_row, 0.0), axis=-1, keepdims=True)
                + rec[:, r_lane:r_lane + 1])

    slot = lax.broadcasted_iota(jnp.int32, (tm, ZY_ROWS), 1).astype(F32)
    wsel = (jnp.where(slot == local_pos(R_I1, R_RANK1), rec[:, R_W1:R_W1 + 1], 0.0)
            + jnp.where(slot == local_pos(R_I2, R_RANK2), rec[:, R_W2:R_W2 + 1], 0.0))
    w_hi = wsel.astype(BF16)
    w_lo = (wsel - w_hi.astype(F32)).astype(BF16)
    z = zy_scr[buf].astype(BF16)
    mix = jnp.dot(w_hi, z, preferred_element_type=F32) + jnp.dot(w_lo, z, preferred_element_type=F32)
    _x_store(o_refs, x_ref[...] + mod_ref[5:6, :] * mix)


def _moe_combine(n_tile, lo_tile, dst_tile, x_all, mod3, route, lo_vec, y, split_out):
    tm = ROUTE_TM
    spec = lambda shape, fn: pl.BlockSpec(shape, lambda i, *_: fn(i))
    return pl.pallas_call(
        functools.partial(_combine_kernel, 2 if split_out else 1),
        out_shape=_x_shapes(split_out),
        grid_spec=pltpu.PrefetchScalarGridSpec(
            num_scalar_prefetch=3,
            grid=(N_TOK // tm,),
            in_specs=[spec((tm, D_MODEL), lambda i: (i, 0)),
                      spec((None, 6, D_MODEL), lambda i: (_mod_row(i * tm), 0, 0)),
                      spec((tm, ROUTER_LANES), lambda i: (i, 0)),
                      spec((None, 1, ROUTER_LANES), lambda i: (i, 0, 0)),
                      pl.BlockSpec(memory_space=pl.ANY)],
            out_specs=_x_specs((None,) * (2 if split_out else 1), tm),
            scratch_shapes=[pltpu.VMEM((2, ZY_ROWS, D_MODEL), F32), pltpu.SemaphoreType.DMA((2,))],
        ),
        compiler_params=_cparams(("arbitrary",)),
        name="moe_combine",
    )(n_tile, lo_tile, dst_tile, x_all, mod3, route, lo_vec, y)


def _moe(x_all, mod3, g_norm2, w_router_p, w_g, w_u, w_d, l, j, split_out):
    route, cnt, tcarry = _moe_route(x_all, mod3, g_norm2, w_router_p, l, j)
    cnt = cnt[0, :N_EXPERTS].astype(jnp.int32)
    carry = tcarry[:, 0, :N_EXPERTS].astype(jnp.int32)
    n_tile = jnp.concatenate([carry[1:], cnt[None, :]], axis=0) - carry
    n_tile = (n_tile + RUN_ALIGN - 1) // RUN_ALIGN * RUN_ALIGN
    lo_tile = jnp.cumsum(n_tile, axis=1) - n_tile
    carry = jnp.cumsum(n_tile, axis=0) - n_tile
    cnt = jnp.sum(n_tile, axis=0)
    padded = (cnt + MOE_TM - 1) // MOE_TM * MOE_TM
    ends = jnp.cumsum(padded)
    offs = ends - padded
    dst_tile = offs[None, :] + carry
    experts = jnp.arange(N_EXPERTS, dtype=jnp.int32)
    gap0 = jnp.concatenate([offs + cnt, ends[-1:]])
    gapn = jnp.concatenate([padded - cnt, MOE_ROWS - ends[-1:]])
    lo_vec = jnp.pad(lo_tile.astype(F32), ((0, 0), (0, ROUTER_LANES - N_EXPERTS)))[:, None, :]

    tile_start = jnp.arange(MOE_TILES, dtype=jnp.int32) * MOE_TM
    last_tile = jnp.maximum(ends[-1] - MOE_TM, 0)
    owner_start = jnp.minimum(tile_start, last_tile)
    tile_expert = jnp.minimum(jnp.sum(owner_start[:, None] >= ends[None, :], axis=1), N_EXPERTS - 1).astype(jnp.int32)
    group_end = jnp.sum(jnp.where(tile_expert[:, None] == experts[None, :], (offs + cnt)[None, :], 0), axis=1)
    tile_rows = jnp.where(tile_start < ends[-1], jnp.clip(group_end - tile_start, 0, MOE_TM), 0).astype(jnp.int32)

    i32 = lambda a: a.reshape(-1).astype(jnp.int32)
    runs = (i32(n_tile // RUN_ALIGN), i32(lo_tile), i32(dst_tile))
    xs = _moe_dispatch(*runs, i32(gap0), i32(gapn), x_all, mod3, g_norm2, route, lo_vec, l)
    y = _moe_gmm(tile_expert, tile_rows, xs, w_g, w_u, w_d, j)
    return tuple(_moe_combine(*runs, x_all, mod3, route, lo_vec, y, split_out))


def _pad_heads(w, per_head):
    lead = w.shape[:-1]
    w = w.reshape(lead + (MLA_HEADS, per_head))
    w = jnp.pad(w, [(0, 0)] * len(lead) + [(0, 0), (0, HEAD_PAD - per_head)])
    return w.reshape(lead + (MLA_PAD_W,))


W_IN_HEAD = MLA_Q_LORA + MLA_KV_LORA + MLA_ROPE
W_IN_COLS = W_IN_HEAD + 8 * BR_W + N_BRANCH * D_MODEL


def _w_in_layout_kernel(w_ref, o_ref):
    w = w_ref[...]
    o_ref[:, :W_IN_HEAD] = w[:, :W_IN_HEAD].astype(BF16)
    o_ref[:, W_IN_HEAD:P_SC] = jnp.zeros((w.shape[0], P_SC - W_IN_HEAD), BF16)
    o_ref[:, P_SC:] = w[:, W_IN_HEAD:].astype(BF16)


def _w_in_layout(w_in, l):
    tk = 128
    return pl.pallas_call(
        _w_in_layout_kernel,
        out_shape=jax.ShapeDtypeStruct((D_MODEL, P_COLS), BF16),
        grid=(D_MODEL // tk,),
        in_specs=[pl.BlockSpec((None, tk, W_IN_COLS), lambda k: (l, k, 0))],
        out_specs=pl.BlockSpec((tk, P_COLS), lambda k: (k, 0)),
        compiler_params=_cparams(("arbitrary",)),
        name="w_in_layout",
    )(w_in)


def _layer_weights(l, w_in, g_qa, w_uq, g_kva, w_ukv, g_mla_q, g_mla_k, sc_w, g_na_q, g_na_k,
                   cf_w, cf_b, cf_ln_g, cf_ln_b, w_br, w_o):
    w_in_p = _w_in_layout(w_in, l)
    ukv = w_ukv[l].reshape(MLA_KV_LORA, MLA_HEADS, MLA_NOPE + MLA_V)
    W = dict(
        w_in_p=w_in_p,
        g_qa=g_qa[l][None], g_kva=g_kva[l][None],
        w_uq_p=_pad_heads(w_uq[l], MLA_QK).astype(BF16),
        w_ukv_k=_pad_heads(ukv[:, :, :MLA_NOPE].reshape(MLA_KV_LORA, -1), MLA_NOPE).astype(BF16),
        w_ukv_v=_pad_heads(ukv[:, :, MLA_NOPE:].reshape(MLA_KV_LORA, -1), MLA_V).astype(BF16),
        g_mla_q_p=jnp.pad(g_mla_q[l], (0, HEAD_PAD - MLA_QK))[None],
        g_mla_k_p=jnp.pad(g_mla_k[l], (0, HEAD_PAD - MLA_QK))[None],
        g_na_q_t=jnp.tile(g_na_q[l], NA_HEADS)[None], g_na_k_t=jnp.tile(g_na_k[l], NA_HEADS)[None],
        sc_w=sc_w[l], cf_w=cf_w[l], cf_b=cf_b[l][None], cf_ln_g=cf_ln_g[l][None], cf_ln_b=cf_ln_b[l][None],
        w_br_a=jnp.pad(w_br[l, 0].reshape(MLA_HEADS, MLA_V, D_MODEL),
                       ((0, 0), (0, HEAD_PAD - MLA_V), (0, 0))).reshape(MLA_PAD_W, D_MODEL).astype(BF16),
        w_br_bcd=w_br[l, 1:].astype(BF16),
        w_o=w_o[l].astype(BF16),
    )
    return W


def kernel(x_prompt, x_sample, cache_mla_ckv, cache_mla_kpe, cache_na_k, cache_na_v, c, c_ctx, w_ada, b_ada, g_norm1, w_in, g_qa, w_uq, g_kva, w_ukv, g_mla_q, g_mla_k, sc_w, g_na_q, g_na_k, na_rpb, cf_w, cf_b, cf_ln_g, cf_ln_b, w_br, w_o, g_norm2, w_ff_gate, w_ff_up, w_ff_down, w_router, w_e_gate, w_e_up, w_e_down):
    x = (x_prompt.reshape(N_CTX_TOK, D_MODEL), x_sample.reshape(N_LAT_TOK, D_MODEL))
    cvec = jnp.concatenate([c_ctx[None, :], c, jnp.zeros((MOD_ROWS - 1 - DEC_BATCH, D_MODEL), F32)], axis=0)
    cache_na_k2 = cache_na_k.reshape(DEC_BATCH, DEPTH, PAST_LEN, BR_W)
    cache_na_v2 = cache_na_v.reshape(DEC_BATCH, DEPTH, PAST_LEN, BR_W)
    w_router_p = jnp.pad(w_router, ((0, 0), (0, 0), (0, ROUTER_LANES - N_EXPERTS)))

    ckv_l, kpe_l, nak_l, nav_l = [], [], [], []
    for l in range(DEPTH):
        W = _layer_weights(l, w_in, g_qa, w_uq, g_kva, w_ukv, g_mla_q, g_mla_k, sc_w, g_na_q, g_na_k,
                           cf_w, cf_b, cf_ln_g, cf_ln_b, w_br, w_o)
        mod3 = _modulation(cvec, w_ada, b_ada, l).reshape(MOD_ROWS, 6, D_MODEL)
        p = _in_projection(x, mod3, g_norm1, W['w_in_p'], l)

        oa_c, ckv_new, kpe_new = _mla(p, l, False, None, None, W)
        kpe_g = jnp.pad(cache_mla_kpe[:, l], ((0, 0), (0, 0), (MLA_NOPE, HEAD_PAD - MLA_QK)))
        (oa_l,) = _mla(p, l, True, cache_mla_ckv, kpe_g, W)
        oc_c, nak_new, nav_new = _na_context(p, W)
        oc_l = _na_latent(p, l, cache_na_k2, cache_na_v2, _na_bias_table(na_rpb[l]), W)
        ob_c, od_c = _convs(p, False, W)
        ob_l, od_l = _convs(p, True, W)

        x_all = _merge(x, mod3, p, (oa_c, oa_l), (ob_c, ob_l), (oc_c, oc_l), (od_c, od_l), W)

        last = l == DEPTH - 1
        if l % 2 == 0:
            x = (_ffn(x_all, mod3, g_norm2, w_ff_gate, w_ff_up, w_ff_down, l, l // 2),)
        else:
            x = _moe(x_all, mod3, g_norm2, w_router_p, w_e_gate, w_e_up, w_e_down, l, l // 2, last)

        ckv_l.append(ckv_new)
        kpe_l.append(kpe_new)
        nak_l.append(nak_new.reshape(BATCH, SEQ, NA_HEADS, NA_HD))
        nav_l.append(nav_new.reshape(BATCH, SEQ, NA_HEADS, NA_HD))

    if len(x) == 1:
        x = (x[0][:N_CTX_TOK], x[0][N_CTX_TOK:])
    y_prompt = x[0].reshape(BATCH, SEQ, D_MODEL)
    y_sample = x[1].reshape(DEC_BATCH, DEC_SEQ, D_MODEL)
    return (y_prompt, y_sample, jnp.stack(ckv_l, axis=1), jnp.stack(kpe_l, axis=1),
            jnp.stack(nak_l, axis=1), jnp.stack(nav_l, axis=1))
```

```python
import functools

import numpy as np
import jax
import jax.numpy as jnp
from jax import lax
from jax.experimental import pallas as pl
from jax.experimental.pallas import tpu as pltpu

F32 = jnp.float32
BF16 = jnp.bfloat16

D_MODEL = 1024
BATCH = 16
SEQ = 256
DEPTH = 2
DEC_BATCH = 8
DEC_SEQ = 1024
PAST_LEN = 512
GRID_W = 64
N_BRANCH = 4
BR_W = 256
MLA_HEADS = 4
MLA_NOPE = 64
MLA_ROPE = 32
MLA_QK = 96
MLA_V = 64
MLA_Q_LORA = 256
MLA_KV_LORA = 128
SC_K = 3
NA_HEADS = 4
NA_HD = 64
NA_WIN_R = 8
NA_WIN_C = 16
CF_K = 31
D_FF = 3584
N_EXPERTS = 8
ROPE_THETA = 10000.0
EPS = 1e-6

N_CTX_TOK = BATCH * SEQ
N_LAT_TOK = DEC_BATCH * DEC_SEQ
N_TOK = N_CTX_TOK + N_LAT_TOK
MOD_ROWS = 16

HEAD_PAD = 128
MLA_PAD_W = MLA_HEADS * HEAD_PAD

P_QA = 0
P_KVA = 256
P_SC = 512
P_NA = 1280
P_CF = 2048
P_GATE = 2560
GATE_BLK = 512
P_COLS = 6656

VMEM_LIMIT = 56 * 1024 * 1024

NA_TQ = 256
NA_WIN_ROWS = 12
NA_WIN_KEYS = NA_WIN_ROWS * GRID_W


def _cparams(sem):
    return pltpu.CompilerParams(dimension_semantics=sem, vmem_limit_bytes=VMEM_LIMIT)


def _const_spec(shape):
    nd = len(shape)
    return pl.BlockSpec(shape, lambda *_: (0,) * nd)


def _mod_row(tok_start):
    return jnp.where(tok_start < N_CTX_TOK, 0, 1 + (tok_start - N_CTX_TOK) // DEC_SEQ)


def _sigmoid(x):
    return 0.5 * jnp.tanh(0.5 * x) + 0.5


def _silu(x):
    return x * _sigmoid(x)


def _ctx_or_lat(is_ctx, c_ref, l_ref):
    return jnp.where(is_ctx, c_ref[...], l_ref[...])


def _dot(a, b):
    return jnp.dot(a.astype(BF16), b.astype(BF16), preferred_element_type=F32)


def _dot_nt(a, b):
    return lax.dot_general(a.astype(BF16), b.astype(BF16), (((1,), (1,)), ((), ())),
                           preferred_element_type=F32)


def _mod_kernel(c_ref, w_ref, b_ref, o_ref):
    o_ref[...] = _dot(_silu(c_ref[...]), w_ref[...]) + b_ref[...]


def _modulation(cvec, w_ada, b_ada, l):
    tn = 1024
    return pl.pallas_call(
        _mod_kernel,
        out_shape=jax.ShapeDtypeStruct((MOD_ROWS, 6 * D_MODEL), F32),
        grid=(6 * D_MODEL // tn,),
        in_specs=[
            _const_spec((MOD_ROWS, D_MODEL)),
            pl.BlockSpec((None, D_MODEL, tn), lambda j: (l, 0, j)),
            pl.BlockSpec((None, 1, tn), lambda j: (l, 0, j)),
        ],
        out_specs=pl.BlockSpec((MOD_ROWS, tn), lambda j: (0, j)),
        compiler_params=_cparams(("arbitrary",)),
        name="modulation",
    )(cvec, w_ada, b_ada.reshape(DEPTH, 1, 6 * D_MODEL))


def _pair_specs(tm, width):
    nc = N_CTX_TOK // tm
    return [pl.BlockSpec((tm, width), lambda i, *_: (jnp.minimum(i, nc - 1), 0)),
            pl.BlockSpec((tm, width), lambda i, *_: (jnp.maximum(i - nc, 0), 0))]


def _pair_shapes(width, dtype):
    return [jax.ShapeDtypeStruct((N_CTX_TOK, width), dtype), jax.ShapeDtypeStruct((N_LAT_TOK, width), dtype)]


def _store_pair(is_ctx, c_ref, l_ref, val):
    @pl.when(is_ctx)
    def _():
        c_ref[...] = val

    @pl.when(jnp.logical_not(is_ctx))
    def _():
        l_ref[...] = val


def _x_specs(x, tm):
    if len(x) == 2:
        return _pair_specs(tm, D_MODEL)
    return [pl.BlockSpec((tm, D_MODEL), lambda i, *_: (i, 0))]


def _x_shapes(split):
    return _pair_shapes(D_MODEL, F32) if split else [jax.ShapeDtypeStruct((N_TOK, D_MODEL), F32)]


def _x_load(x_refs):
    if len(x_refs) == 1:
        return x_refs[0][...]
    tm = x_refs[0].shape[0]
    return _ctx_or_lat(pl.program_id(0) < N_CTX_TOK // tm, *x_refs)


def _x_store(y_refs, val):
    if len(y_refs) == 1:
        y_refs[0][...] = val
    else:
        tm = y_refs[0].shape[0]
        _store_pair(pl.program_id(0) < N_CTX_TOK // tm, *y_refs, val)


def _inproj_kernel(nx, *refs):
    x_refs, (mod_ref, g_ref, w_ref, o_ref, h_scr) = refs[:nx], refs[nx:]

    @pl.when(pl.program_id(1) == 0)
    def _():
        x = _x_load(x_refs)
        y = x * lax.rsqrt(jnp.mean(x * x, axis=-1, keepdims=True) + EPS) * g_ref[...]
        h_scr[...] = (y * (1.0 + mod_ref[1:2, :]) + mod_ref[0:1, :]).astype(BF16)

    o_ref[...] = jnp.dot(h_scr[...], w_ref[...], preferred_element_type=F32).astype(o_ref.dtype)


def _in_projection(x, mod3, g_norm1, w_in_p, l):
    tm, tn = 1024, P_COLS // 4
    return pl.pallas_call(
        functools.partial(_inproj_kernel, len(x)),
        out_shape=jax.ShapeDtypeStruct((N_TOK, P_COLS), BF16),
        grid=(N_TOK // tm, P_COLS // tn),
        in_specs=_x_specs(x, tm) + [
            pl.BlockSpec((None, 6, D_MODEL), lambda i, j: (_mod_row(i * tm), 0, 0)),
            pl.BlockSpec((None, 1, D_MODEL), lambda i, j: (l, 0, 0)),
            pl.BlockSpec((D_MODEL, tn), lambda i, j: (0, j)),
        ],
        out_specs=pl.BlockSpec((tm, tn), lambda i, j: (i, j)),
        scratch_shapes=[pltpu.VMEM((tm, D_MODEL), BF16)],
        compiler_params=_cparams(("arbitrary", "arbitrary")),
        name="in_projection",
    )(*x, mod3, g_norm1.reshape(DEPTH, 1, D_MODEL), w_in_p)


def _rope_tables():
    t = np.arange(DEC_SEQ)
    nf = MLA_ROPE // 4
    inv = (np.float32(ROPE_THETA) ** (-np.arange(nf, dtype=np.float32) / np.float32(nf))).astype(np.float32)
    ang_r = (t // GRID_W).astype(np.float32)[:, None] * inv[None, :]
    ang_c = (t % GRID_W).astype(np.float32)[:, None] * inv[None, :]
    c = np.zeros((DEC_SEQ, HEAD_PAD), np.float32)
    s1 = np.zeros((DEC_SEQ, HEAD_PAD), np.float32)
    s2 = np.zeros((DEC_SEQ, HEAD_PAD), np.float32)
    c[:, :MLA_NOPE] = 1.0
    for base, ang in ((MLA_NOPE, ang_r), (MLA_NOPE + 2 * nf, ang_c)):
        c[:, base:base + nf] = np.cos(ang)
        c[:, base + nf:base + 2 * nf] = np.cos(ang)
        s1[:, base:base + nf] = -np.sin(ang)
        s2[:, base + nf:base + 2 * nf] = np.sin(ang)
    return c, s1, s2


def _rope(x, c, s1, s2):
    nf = MLA_ROPE // 4
    return x * c + pltpu.roll(x, HEAD_PAD - nf, 1) * s1 + pltpu.roll(x, nf, 1) * s2


def _head_norm(xh, g):
    ms = jnp.sum(xh * xh, axis=-1, keepdims=True) * (1.0 / MLA_QK)
    return xh * lax.rsqrt(ms + EPS) * g


def _mla_kernel(latent, tq, *refs):
    if latent:
        (pqa_ref, pkva_ref, cckv_ref, ckpe_ref, rc_ref, rs1_ref, rs2_ref,
         gqa_ref, wuq_ref, gkva_ref, wk_ref, wv_ref, gq_ref, gk_ref,
         o_ref, k_scr, v_scr) = refs
    else:
        (pqa_ref, pkva_ref,
         gqa_ref, wuq_ref, gkva_ref, wk_ref, wv_ref, gq_ref, gk_ref,
         o_ref, ckv_ref, kpe_ref, k_scr, v_scr) = refs
    qi = pl.program_id(1)
    n_past = PAST_LEN if latent else 0

    vlane = lax.broadcasted_iota(jnp.int32, (1, MLA_PAD_W), 1)
    v_ones = jnp.where(vlane % HEAD_PAD == MLA_V, 1.0, 0.0)

    def put_kv(ckvn, kpe_g, row0, rope):
        n = ckvn.shape[0]
        kk = _dot(ckvn, wk_ref[...])
        v_scr[row0:row0 + n, :] = (_dot(ckvn, wv_ref[...]) + v_ones).astype(BF16)
        for h in range(MLA_HEADS):
            sl = slice(h * HEAD_PAD, (h + 1) * HEAD_PAD)
            kh = _head_norm(kk[:, sl] + kpe_g, gk_ref[...])
            if rope:
                kh = _rope(kh, rc_ref[...], rs1_ref[...], rs2_ref[...])
            k_scr[row0:row0 + n, sl] = kh.astype(BF16)

    @pl.when(qi == 0)
    def _():
        if latent:
            put_kv(cckv_ref[...], ckpe_ref[...], 0, False)
        kva = pkva_ref[...].astype(F32)
        ckv = kva[:, :MLA_KV_LORA]
        ckvn = ckv * lax.rsqrt(jnp.mean(ckv * ckv, axis=-1, keepdims=True) + EPS) * gkva_ref[...]
        if not latent:
            ckv_ref[...] = ckvn
            kpe_ref[...] = kva[:, MLA_KV_LORA:MLA_KV_LORA + MLA_ROPE]
        put_kv(ckvn, pltpu.roll(kva[:, MLA_KV_LORA:], MLA_NOPE, 1), n_past, latent)

    qa = pqa_ref[...].astype(F32)
    qan = qa * lax.rsqrt(jnp.mean(qa * qa, axis=-1, keepdims=True) + EPS) * gqa_ref[...]
    q = _dot(qan, wuq_ref[...])
    scale = MLA_QK ** -0.5
    heads = []
    for h in range(MLA_HEADS):
        sl = slice(h * HEAD_PAD, (h + 1) * HEAD_PAD)
        qh = _head_norm(q[:, sl], gq_ref[...])
        if latent:
            rows = pl.ds(pl.multiple_of(qi * tq, tq), tq)
            qh = _rope(qh, rc_ref[rows, :], rs1_ref[rows, :], rs2_ref[rows, :])
        s = _dot_nt(qh * scale, k_scr[:, sl])
        m = jnp.max(s, axis=-1, keepdims=True)
        e = jnp.exp((s - m).astype(BF16))
        o = jnp.dot(e, v_scr[:, sl], preferred_element_type=F32)
        heads.append(o / o[:, MLA_V:MLA_V + 1])
    low = lax.broadcasted_iota(jnp.int32, (1, HEAD_PAD), 1) < MLA_V
    pairs = [jnp.where(low, heads[h], pltpu.roll(heads[h + 1], MLA_V, 1)) for h in range(0, MLA_HEADS, 2)]
    o_ref[...] = jnp.concatenate(pairs, axis=1).astype(o_ref.dtype)


def _mla(p, l, latent, cache_ckv, cache_kpe_g, W):
    if latent:
        nb, s, tq, row_off = DEC_BATCH, DEC_SEQ, 256, N_CTX_TOK
    else:
        nb, s, tq, row_off = BATCH, SEQ, 256, 0
    nq = s // tq
    sk = s + (PAST_LEN if latent else 0)
    in_specs = [
        pl.BlockSpec((tq, MLA_Q_LORA), lambda b, qi: (row_off // tq + b * nq + qi, P_QA // MLA_Q_LORA)),
        pl.BlockSpec((s, 256), lambda b, qi: (row_off // s + b, P_KVA // 256)),
    ]
    args = [p, p]
    if latent:
        in_specs += [
            pl.BlockSpec((None, None, PAST_LEN, MLA_KV_LORA), lambda b, qi: (b, l, 0, 0)),
            pl.BlockSpec((None, PAST_LEN, HEAD_PAD), lambda b, qi: (b, 0, 0)),
            _const_spec((DEC_SEQ, HEAD_PAD)), _const_spec((DEC_SEQ, HEAD_PAD)), _const_spec((DEC_SEQ, HEAD_PAD)),
        ]
        args += [cache_ckv, cache_kpe_g] + [jnp.asarray(t) for t in _rope_tables()]
    in_specs += [
        _const_spec((1, MLA_Q_LORA)), _const_spec((MLA_Q_LORA, MLA_PAD_W)), _const_spec((1, MLA_KV_LORA)),
        _const_spec((MLA_KV_LORA, MLA_PAD_W)), _const_spec((MLA_KV_LORA, MLA_PAD_W)),
        _const_spec((1, HEAD_PAD)), _const_spec((1, HEAD_PAD)),
    ]
    args += [W['g_qa'], W['w_uq_p'], W['g_kva'], W['w_ukv_k'], W['w_ukv_v'], W['g_mla_q_p'], W['g_mla_k_p']]
    out_shape = [jax.ShapeDtypeStruct((nb * s, BR_W), BF16)]
    out_specs = [pl.BlockSpec((tq, BR_W), lambda b, qi: (b * nq + qi, 0))]
    if not latent:
        out_shape.append(jax.ShapeDtypeStruct((nb, s, MLA_KV_LORA), F32))
        out_specs.append(pl.BlockSpec((None, s, MLA_KV_LORA), lambda b, qi: (b, 0, 0)))
        out_shape.append(jax.ShapeDtypeStruct((nb, s, MLA_ROPE), F32))
        out_specs.append(pl.BlockSpec((None, s, MLA_ROPE), lambda b, qi: (b, 0, 0)))
    return pl.pallas_call(
        functools.partial(_mla_kernel, latent, tq),
        out_shape=out_shape,
        grid=(nb, nq),
        in_specs=in_specs,
        out_specs=out_specs,
        scratch_shapes=[pltpu.VMEM((sk, MLA_PAD_W), BF16), pltpu.VMEM((sk, MLA_PAD_W), BF16)],
        compiler_params=_cparams(("arbitrary", "arbitrary")),
        name="mla_latent" if latent else "mla_context",
    )(*args)


def _head_masks(width):
    lane = lax.broadcasted_iota(jnp.int32, (1, width), 1)
    return [(lane >= h * NA_HD) & (lane < (h + 1) * NA_HD) for h in range(NA_HEADS)]


def _group_norm64(x, g, masks):
    x2 = x * x
    inv = jnp.zeros_like(x)
    for m in masks:
        ms = jnp.sum(jnp.where(m, x2, 0.0), axis=-1, keepdims=True) * (1.0 / NA_HD)
        inv = jnp.where(m, lax.rsqrt(ms + EPS), inv)
    return x * inv * g


def _p_blocks(col0, n, rows, row_fn):
    def spec(k):
        return pl.BlockSpec((rows, BR_W), lambda *g: (row_fn(*g), col0 // BR_W + k))
    return [spec(k) for k in range(n)]


def _na_ctx_kernel(pq_ref, pk_ref, pv_ref, gq_ref, gk_ref, o_ref, k_ref, v_ref):
    masks = _head_masks(BR_W)
    qn = _group_norm64(pq_ref[...].astype(F32), gq_ref[...], masks)
    kn = _group_norm64(pk_ref[...].astype(F32), gk_ref[...], masks)
    v = pv_ref[...].astype(F32)
    k_ref[...] = kn
    v_ref[...] = v
    scale = NA_HD ** -0.5
    acc = jnp.zeros((pq_ref.shape[0], BR_W), F32)
    for m in masks:
        s = _dot_nt(jnp.where(m, qn, 0.0), kn) * scale
        mx = jnp.max(s, axis=-1, keepdims=True)
        e = jnp.exp(s - mx)
        den = jnp.sum(e, axis=-1, keepdims=True)
        acc = acc + jnp.where(m, _dot(e, v) / den, 0.0)
    o_ref[...] = acc.astype(o_ref.dtype)


def _na_context(p, W):
    s = SEQ
    return pl.pallas_call(
        _na_ctx_kernel,
        out_shape=[jax.ShapeDtypeStruct((N_CTX_TOK, BR_W), BF16),
                   jax.ShapeDtypeStruct((N_CTX_TOK, BR_W), F32),
                   jax.ShapeDtypeStruct((N_CTX_TOK, BR_W), F32)],
        grid=(BATCH,),
        in_specs=_p_blocks(P_NA, 3, s, lambda b: b) + [_const_spec((1, BR_W)), _const_spec((1, BR_W))],
        out_specs=[pl.BlockSpec((s, BR_W), lambda b: (b, 0))] * 3,
        compiler_params=_cparams(("arbitrary",)),
        name="na_context",
    )(p, p, p, W['g_na_q_t'], W['g_na_k_t'])


def _na_lat_kernel(pq_ref, pk_ref, pv_ref, ck_ref, cv_ref, bias_ref, gq_ref, gk_ref, o_ref,
                   q_scr, k_scr, v_scr, kc_scr, vc_scr):
    j = pl.program_id(1)
    masks = _head_masks(BR_W)

    zero = jnp.zeros((), BF16)
    one = jnp.ones((), BF16)
    lane = lax.broadcasted_iota(jnp.int32, (1, BR_W), 1)
    den_lanes = [((h + 1) % NA_HEADS) * NA_HD for h in range(NA_HEADS)]

    @pl.when(j == 0)
    def _():
        scale = NA_HD ** -0.5
        q_scr[...] = (_group_norm64(pq_ref[...].astype(F32), gq_ref[...], masks) * scale).astype(BF16)
        k_scr[...] = _group_norm64(pk_ref[...].astype(F32), gk_ref[...], masks).astype(BF16)
        kc_scr[...] = ck_ref[...].astype(BF16)
        v = pv_ref[...]
        vc = cv_ref[...].astype(BF16)
        for h, m in enumerate(masks):
            v_scr[h] = jnp.where(lane == den_lanes[h], one, jnp.where(m, v, zero))
            vc_scr[h] = jnp.where(lane == den_lanes[h], one, jnp.where(m, vc, zero))

    win0 = pl.multiple_of(jnp.where(j < 2, 0, DEC_SEQ - NA_WIN_KEYS), 256)
    q = q_scr[pl.ds(pl.multiple_of(j * NA_TQ, NA_TQ), NA_TQ), :]
    kw = k_scr[pl.ds(win0, NA_WIN_KEYS), :]
    kc = kc_scr[...]
    acc = jnp.zeros((NA_TQ, BR_W), F32)
    for h, m in enumerate(masks):
        qm = jnp.where(m, q, zero)
        s_loc = _dot_nt(qm, kw) + bias_ref[h]
        s_ctx = _dot_nt(qm, kc)
        mx = jnp.maximum(jnp.max(s_loc, axis=-1, keepdims=True), jnp.max(s_ctx, axis=-1, keepdims=True))
        e_loc = jnp.exp((s_loc - mx).astype(BF16))
        e_ctx = jnp.exp((s_ctx - mx).astype(BF16))
        o = (jnp.dot(e_loc, v_scr[h, pl.ds(win0, NA_WIN_KEYS), :], preferred_element_type=F32)
             + jnp.dot(e_ctx, vc_scr[h], preferred_element_type=F32))
        acc = acc + jnp.where(m, o / o[:, den_lanes[h]:den_lanes[h] + 1], 0.0)
    o_ref[...] = acc.astype(o_ref.dtype)


def _na_bias_table(rpb):
    n_dr, n_dc = 2 * NA_WIN_R - 1, 2 * NA_WIN_C - 1
    cols = np.arange(GRID_W)
    dc = np.clip(cols[None, :] - cols[:, None], -(NA_WIN_C - 1), NA_WIN_C - 1) + (NA_WIN_C - 1)
    cstart = np.clip(cols - NA_WIN_C // 2, 0, GRID_W - NA_WIN_C)
    col_ok = (cols[None, :] >= cstart[:, None]) & (cols[None, :] < cstart[:, None] + NA_WIN_C)
    place = (dc.reshape(-1)[None, :] == np.arange(n_dc)[:, None]).astype(np.float32)
    blocks = jnp.dot(rpb.reshape(NA_HEADS * n_dr, n_dc), place, precision=lax.Precision.HIGHEST)
    blocks = jnp.where(col_ok.reshape(-1)[None, :], blocks, -jnp.inf).reshape(NA_HEADS, n_dr, GRID_W, GRID_W)
    neg = jnp.full((NA_HEADS, 1, GRID_W, GRID_W), -jnp.inf, F32)
    blocks = jnp.concatenate([blocks, neg], axis=1)

    nt = DEC_SEQ // NA_TQ
    n_rows = DEC_SEQ // GRID_W
    rows_per_tile = NA_TQ // GRID_W
    sel = np.full((nt, rows_per_tile, NA_WIN_ROWS), n_dr, np.int32)
    for j in range(nt):
        win_row0 = 0 if j < nt // 2 else n_rows - NA_WIN_ROWS
        for rq in range(rows_per_tile):
            r = j * rows_per_tile + rq
            start = min(max(r - NA_WIN_R // 2, 0), n_rows - NA_WIN_R)
            for kr in range(NA_WIN_ROWS):
                if start <= win_row0 + kr < start + NA_WIN_R:
                    sel[j, rq, kr] = win_row0 + kr - r + (NA_WIN_R - 1)
    tiles = []
    for j in range(nt):
        rows = [jnp.concatenate([blocks[:, sel[j, rq, kr]] for kr in range(NA_WIN_ROWS)], axis=-1)
                for rq in range(rows_per_tile)]
        tiles.append(jnp.concatenate(rows, axis=1))
    return jnp.stack(tiles, axis=1)


def _na_latent(p, l, cache_k, cache_v, bias, W):
    s = DEC_SEQ
    nt = s // NA_TQ
    row_off = N_CTX_TOK
    return pl.pallas_call(
        _na_lat_kernel,
        out_shape=jax.ShapeDtypeStruct((N_LAT_TOK, BR_W), BF16),
        grid=(DEC_BATCH, nt),
        in_specs=_p_blocks(P_NA, 3, s, lambda b, j: row_off // s + b) + [
                  pl.BlockSpec((None, None, PAST_LEN, BR_W), lambda b, j: (b, l, 0, 0)),
                  pl.BlockSpec((None, None, PAST_LEN, BR_W), lambda b, j: (b, l, 0, 0)),
                  pl.BlockSpec((NA_HEADS, None, NA_TQ, NA_WIN_KEYS), lambda b, j: (0, j, 0, 0)),
                  _const_spec((1, BR_W)), _const_spec((1, BR_W))],
        out_specs=pl.BlockSpec((NA_TQ, BR_W), lambda b, j: (b * nt + j, 0)),
        scratch_shapes=[pltpu.VMEM((s, BR_W), BF16), pltpu.VMEM((s, BR_W), BF16),
                        pltpu.VMEM((NA_HEADS, s, BR_W), BF16),
                        pltpu.VMEM((PAST_LEN, BR_W), BF16), pltpu.VMEM((NA_HEADS, PAST_LEN, BR_W), BF16)],
        compiler_params=_cparams(("arbitrary", "arbitrary")),
        name="na_latent",
    )(p, p, p, cache_k, cache_v, bias, W['g_na_q_t'], W['g_na_k_t'])


CONV_HALO = 16
CONV_CHUNK = 128


SUBLANES = 8


def _dwconv_from_pad(pad_ref, w_ref, ksize, s, emit, shift_ref=None):
    half = ksize // 2
    if shift_ref is not None:
        n_rows = s + 2 * CONV_HALO - SUBLANES
        for p in range(1, SUBLANES):
            for c0 in range(0, n_rows, CONV_CHUNK):
                n = min(CONV_CHUNK, n_rows - c0)
                shift_ref[p - 1, c0:c0 + n, :] = pad_ref[c0 + p:c0 + p + n, :]
    for c0 in range(0, s, CONV_CHUNK):
        acc = jnp.zeros((CONV_CHUNK, BR_W), F32)
        for k in range(ksize):
            r0 = CONV_HALO + c0 + k - half
            p = r0 % SUBLANES
            if shift_ref is None or p == 0:
                win = pad_ref[r0:r0 + CONV_CHUNK, :]
            else:
                win = shift_ref[p - 1, r0 - p:r0 - p + CONV_CHUNK, :]
            acc = acc + win * w_ref[k:k + 1, :]
        emit(c0, acc)


def _conv_kernel(s, scb_ref, scc_ref, scx_ref, cfa_ref, cfb2_ref, scw_ref, cfw_ref, cfb_ref, lng_ref, lnb_ref,
                 ob_ref, od_ref, pad_ref, shift_ref):
    zeros = jnp.zeros((CONV_HALO, BR_W), F32)
    pad_ref[0:CONV_HALO, :] = zeros
    pad_ref[CONV_HALO + s:2 * CONV_HALO + s, :] = zeros

    pad_ref[CONV_HALO:CONV_HALO + s, :] = scc_ref[...].astype(F32) * scx_ref[...].astype(F32)

    def emit_b(c0, acc):
        ob_ref[c0:c0 + CONV_CHUNK, :] = (scb_ref[c0:c0 + CONV_CHUNK, :].astype(F32) * acc).astype(ob_ref.dtype)

    _dwconv_from_pad(pad_ref, scw_ref, SC_K, s, emit_b)

    pad_ref[CONV_HALO:CONV_HALO + s, :] = cfa_ref[...].astype(F32) * _sigmoid(cfb2_ref[...].astype(F32))

    def emit_d(c0, acc):
        u = acc + cfb_ref[...]
        mu = jnp.mean(u, axis=-1, keepdims=True)
        d = u - mu
        var = jnp.mean(d * d, axis=-1, keepdims=True)
        y = d * lax.rsqrt(var + EPS) * lng_ref[...] + lnb_ref[...]
        od_ref[c0:c0 + CONV_CHUNK, :] = _silu(y).astype(od_ref.dtype)

    _dwconv_from_pad(pad_ref, cfw_ref, CF_K, s, emit_d, shift_ref)


def _convs(p, latent, W):
    if latent:
        nb, s, row_off = DEC_BATCH, DEC_SEQ, N_CTX_TOK
    else:
        nb, s, row_off = BATCH, SEQ, 0
    return pl.pallas_call(
        functools.partial(_conv_kernel, s),
        out_shape=[jax.ShapeDtypeStruct((nb * s, BR_W), BF16)] * 2,
        grid=(nb,),
        in_specs=(_p_blocks(P_SC, 3, s, lambda b: row_off // s + b)
                  + _p_blocks(P_CF, 2, s, lambda b: row_off // s + b)
                  + [_const_spec((SC_K, BR_W)), _const_spec((CF_K, BR_W)),
                     _const_spec((1, BR_W)), _const_spec((1, BR_W)), _const_spec((1, BR_W))]),
        out_specs=[pl.BlockSpec((s, BR_W), lambda b: (b, 0))] * 2,
        scratch_shapes=[pltpu.VMEM((s + 2 * CONV_HALO, BR_W), F32),
                        pltpu.VMEM((SUBLANES - 1, s + 2 * CONV_HALO, BR_W), F32)],
        compiler_params=_cparams(("arbitrary",)),
        name="convs_latent" if latent else "convs_context",
    )(p, p, p, p, p, W['sc_w'], W['cf_w'], W['cf_b'], W['cf_ln_g'], W['cf_ln_b'])


def _merge_kernel(nc, nx, *refs):
    x_refs = refs[:nx]
    (mod_ref, oac_ref, oal_ref, obc_ref, obl_ref, occ_ref, ocl_ref, odc_ref, odl_ref) = refs[nx:nx + 9]
    gate_refs = refs[nx + 9:nx + 9 + 2 * N_BRANCH]
    wb_ref, wo_ref, y_ref = refs[nx + 9 + 2 * N_BRANCH:]
    is_ctx = pl.program_id(0) < nc

    def gate(n):
        return jnp.concatenate([_sigmoid(gate_refs[2 * n][...]), _sigmoid(gate_refs[2 * n + 1][...])],
                               axis=1).astype(F32)

    merged = None
    branches = ((oac_ref, oal_ref), (obc_ref, obl_ref), (occ_ref, ocl_ref), (odc_ref, odl_ref))
    for n, (oc_ref, ol_ref) in enumerate(branches):
        term = gate(n) * jnp.dot(_ctx_or_lat(is_ctx, oc_ref, ol_ref), wb_ref[n], preferred_element_type=F32)
        merged = term if merged is None else merged + term
    y_ref[...] = _x_load(x_refs) + mod_ref[2:3, :] * _dot(merged, wo_ref[...])


def _merge(x, mod3, p, o_a, o_b, o_c, o_d, W):
    tm = 512
    gate_spec = lambda k: pl.BlockSpec((tm, GATE_BLK), lambda i: (i, P_GATE // GATE_BLK + k))
    n_gate_blk = N_BRANCH * D_MODEL // GATE_BLK
    return pl.pallas_call(
        functools.partial(_merge_kernel, N_CTX_TOK // tm, len(x)),
        out_shape=jax.ShapeDtypeStruct((N_TOK, D_MODEL), F32),
        grid=(N_TOK // tm,),
        in_specs=(_x_specs(x, tm)
                  + [pl.BlockSpec((None, 6, D_MODEL), lambda i: (_mod_row(i * tm), 0, 0))]
                  + _pair_specs(tm, BR_W) + _pair_specs(tm, BR_W) + _pair_specs(tm, BR_W) + _pair_specs(tm, BR_W)
                  + [gate_spec(k) for k in range(n_gate_blk)]
                  + [_const_spec((N_BRANCH, BR_W, D_MODEL)), _const_spec((D_MODEL, D_MODEL))]),
        out_specs=pl.BlockSpec((tm, D_MODEL), lambda i: (i, 0)),
        compiler_params=_cparams(("arbitrary",)),
        name="merge",
    )(*x, mod3, *o_a, *o_b, *o_c, *o_d, *([p] * n_gate_blk), W['w_br'], W['w_o'])


def _norm2(x, g, mod_ref):
    y = x * lax.rsqrt(jnp.mean(x * x, axis=-1, keepdims=True) + EPS) * g
    return y * (1.0 + mod_ref[4:5, :]) + mod_ref[3:4, :]


def _ffn_kernel(x_ref, mod_ref, g_ref, wg_ref, wu_ref, wd_ref, y_ref, h_scr, acc_scr):
    f = pl.program_id(1)

    @pl.when(f == 0)
    def _():
        h_scr[...] = _norm2(x_ref[...], g_ref[...], mod_ref).astype(BF16)
        acc_scr[...] = jnp.zeros_like(acc_scr)

    h = h_scr[...]
    a = _silu(_dot(h, wg_ref[...])) * _dot(h, wu_ref[...])
    acc_scr[...] += _dot(a, wd_ref[...])

    @pl.when(f == pl.num_programs(1) - 1)
    def _():
        y_ref[...] = x_ref[...] + mod_ref[5:6, :] * acc_scr[...]


def _ffn(x_all, mod3, g_norm2, w_g, w_u, w_d, l, j):
    tm, tf = 1024, 512
    return pl.pallas_call(
        _ffn_kernel,
        out_shape=jax.ShapeDtypeStruct((N_TOK, D_MODEL), F32),
        grid=(N_TOK // tm, D_FF // tf),
        in_specs=[pl.BlockSpec((tm, D_MODEL), lambda i, f: (i, 0)),
                  pl.BlockSpec((None, 6, D_MODEL), lambda i, f: (_mod_row(i * tm), 0, 0)),
                  pl.BlockSpec((None, 1, D_MODEL), lambda i, f: (l, 0, 0)),
                  pl.BlockSpec((None, D_MODEL, tf), lambda i, f: (j, 0, f)),
                  pl.BlockSpec((None, D_MODEL, tf), lambda i, f: (j, 0, f)),
                  pl.BlockSpec((None, tf, D_MODEL), lambda i, f: (j, f, 0))],
        out_specs=pl.BlockSpec((tm, D_MODEL), lambda i, f: (i, 0)),
        scratch_shapes=[pltpu.VMEM((tm, D_MODEL), BF16), pltpu.VMEM((tm, D_MODEL), F32)],
        compiler_params=_cparams(("arbitrary", "arbitrary")),
        name="ffn_dense",
    )(x_all, mod3, g_norm2.reshape(DEPTH, 1, D_MODEL), w_g, w_u, w_d)


ROUTER_LANES = 128
ROUTE_TM = 512
MOE_TM = 1024
MOE_SUB = 256
MOE_TF = 512
RUN_ALIGN = 8
MOE_TILES = -(-(2 * N_TOK + (N_TOK // ROUTE_TM) * N_EXPERTS * RUN_ALIGN + N_EXPERTS * MOE_TM) // MOE_TM)
MOE_ROWS = MOE_TILES * MOE_TM
R_I1, R_I2, R_W1, R_W2, R_RANK1, R_RANK2 = range(6)


def _route_top2(h, wr):
    h_hi = h.astype(BF16)
    h_lo = (h - h_hi.astype(F32)).astype(BF16)
    w_hi = wr.astype(BF16)
    w_lo = (wr - w_hi.astype(F32)).astype(BF16)
    logits = (jnp.dot(h_hi, w_hi, preferred_element_type=F32) + jnp.dot(h_lo, w_hi, preferred_element_type=F32)
              + jnp.dot(h_hi, w_lo, preferred_element_type=F32))
    lane = lax.broadcasted_iota(jnp.int32, logits.shape, 1).astype(F32)
    neg = jnp.float32(-jnp.inf)
    logits = jnp.where(lane < N_EXPERTS, logits, neg)
    m1 = jnp.max(logits, axis=-1, keepdims=True)
    i1 = jnp.min(jnp.where(logits == m1, lane, float(ROUTER_LANES)), axis=-1, keepdims=True)
    rest = jnp.where(lane == i1, neg, logits)
    m2 = jnp.max(rest, axis=-1, keepdims=True)
    i2 = jnp.min(jnp.where(rest == m2, lane, float(ROUTER_LANES)), axis=-1, keepdims=True)
    e2 = jnp.exp(m2 - m1)
    return lane, i1, i2, 1.0 / (1.0 + e2), e2 / (1.0 + e2)


def _route_kernel(x_ref, mod_ref, g_ref, wr_ref, route_ref, cnt_ref, tcarry_ref, carry_scr):
    @pl.when(pl.program_id(0) == 0)
    def _():
        carry_scr[...] = jnp.zeros_like(carry_scr)

    tcarry_ref[...] = carry_scr[...]

    h = _norm2(x_ref[...], g_ref[...], mod_ref)
    lane, i1, i2, w1, w2 = _route_top2(h, wr_ref[...])
    tm = h.shape[0]
    oh1 = lane == i1
    oh2 = lane == i2
    oh = jnp.where(oh1, 1.0, 0.0) + jnp.where(oh2, 1.0, 0.0)
    r = lax.broadcasted_iota(jnp.int32, (tm, tm), 0)
    c = lax.broadcasted_iota(jnp.int32, (tm, tm), 1)
    lower = jnp.where(r > c, 1.0, 0.0).astype(BF16)
    before = jnp.dot(lower, oh.astype(BF16), preferred_element_type=F32)
    rank1 = jnp.sum(jnp.where(oh1, before, 0.0), axis=-1, keepdims=True)
    rank2 = jnp.sum(jnp.where(oh2, before, 0.0), axis=-1, keepdims=True)
    carry_scr[...] += jnp.sum(oh, axis=0, keepdims=True)
    rec = jnp.zeros_like(lane)
    for k, v in ((R_I1, i1), (R_I2, i2), (R_W1, w1), (R_W2, w2), (R_RANK1, rank1), (R_RANK2, rank2)):
        rec = jnp.where(lane == float(k), v, rec)
    route_ref[...] = rec
    cnt_ref[...] = carry_scr[...]


def _moe_route(x_all, mod3, g_norm2, w_router_p, l, j):
    tm = ROUTE_TM
    return pl.pallas_call(
        _route_kernel,
        out_shape=[jax.ShapeDtypeStruct((N_TOK, ROUTER_LANES), F32),
                   jax.ShapeDtypeStruct((1, ROUTER_LANES), F32),
                   jax.ShapeDtypeStruct((N_TOK // tm, 1, ROUTER_LANES), F32)],
        grid=(N_TOK // tm,),
        in_specs=[pl.BlockSpec((tm, D_MODEL), lambda i: (i, 0)),
                  pl.BlockSpec((None, 6, D_MODEL), lambda i: (_mod_row(i * tm), 0, 0)),
                  pl.BlockSpec((None, 1, D_MODEL), lambda i: (l, 0, 0)),
                  pl.BlockSpec((None, D_MODEL, ROUTER_LANES), lambda i: (j, 0, 0))],
        out_specs=[pl.BlockSpec((tm, ROUTER_LANES), lambda i: (i, 0)),
                   pl.BlockSpec((1, ROUTER_LANES), lambda i: (0, 0)),
                   pl.BlockSpec((None, 1, ROUTER_LANES), lambda i: (i, 0, 0))],
        scratch_shapes=[pltpu.VMEM((1, ROUTER_LANES), F32)],
        compiler_params=_cparams(("arbitrary",)),
        name="moe_route",
    )(x_all, mod3, g_norm2.reshape(DEPTH, 1, D_MODEL), w_router_p)


ZERO_ROWS = 512
N_GAPS = N_EXPERTS + 1


def _bit_chunks(n_units, max_units, make, wait=False):
    for b in range(max_units.bit_length()):
        units = 1 << b

        @pl.when((n_units & units) != 0)
        def _():
            cp = make(pl.multiple_of((n_units & (units - 1)) * RUN_ALIGN, RUN_ALIGN), units * RUN_ALIGN)
            if wait:
                cp.wait()
            else:
                cp.start()


def _dispatch_kernel(n_ref, lo_ref, dst_ref, gap0_ref, gapn_ref, x_ref, mod_ref, g_ref, route_ref, lovec_ref,
                     xs_ref, z_scr, zero_scr, sems, zsem):
    i = pl.program_id(0)
    tm = x_ref.shape[0]
    h = _norm2(x_ref[...], g_ref[...], mod_ref).astype(BF16)
    rec = route_ref[...]
    lane = lax.broadcasted_iota(jnp.int32, rec.shape, 1).astype(F32)
    lo_row = lovec_ref[...]

    def local_pos(i_lane, r_lane):
        lp = (jnp.sum(jnp.where(lane == rec[:, i_lane:i_lane + 1], lo_row, 0.0), axis=-1, keepdims=True)
              + rec[:, r_lane:r_lane + 1])
        return jnp.transpose(jnp.broadcast_to(lp, (tm, ROUTER_LANES)))[0:1, :]

    slot = lax.broadcasted_iota(jnp.int32, (z_scr.shape[1], tm), 0).astype(F32)
    perm = jnp.where((slot == local_pos(R_I1, R_RANK1)) | (slot == local_pos(R_I2, R_RANK2)), 1.0, 0.0)
    buf = i % 2
    z_scr[buf] = jnp.dot(perm.astype(BF16), h, preferred_element_type=F32)

    def run_copies(tile, b, wait):
        for e in range(N_EXPERTS):
            k = tile * N_EXPERTS + e
            src0 = lo_ref[k]
            dst0 = dst_ref[k]

            def make(off, rows, src0=src0, dst0=dst0):
                return pltpu.make_async_copy(
                    z_scr.at[b, pl.ds(pl.multiple_of(src0 + off, RUN_ALIGN), rows), :],
                    xs_ref.at[pl.ds(pl.multiple_of(dst0 + off, RUN_ALIGN), rows), :], sems.at[b])

            _bit_chunks(n_ref[k], tm // RUN_ALIGN, make, wait)

    run_copies(i, buf, False)

    @pl.when(i > 0)
    def _():
        run_copies(i - 1, 1 - buf, True)

    @pl.when(i == pl.num_programs(0) - 1)
    def _():
        run_copies(i, buf, True)

    @pl.when(i == pl.num_programs(0) - 1)
    def _():
        zero_scr[...] = jnp.zeros_like(zero_scr)
        for wait in (False, True):
            for g in range(N_GAPS):
                start = gap0_ref[g]
                n = gapn_ref[g]
                max_rows = MOE_TM if g < N_EXPERTS else MOE_ROWS - 2 * N_TOK
                for c in range(max_rows // ZERO_ROWS):
                    @pl.when(n >= (c + 1) * ZERO_ROWS)
                    def _():
                        cp = pltpu.make_async_copy(
                            zero_scr,
                            xs_ref.at[pl.ds(pl.multiple_of(start + c * ZERO_ROWS, RUN_ALIGN), ZERO_ROWS), :], zsem)
                        if wait:
                            cp.wait()
                        else:
                            cp.start()

                def make(off, rows, start=start, n=n):
                    tail0 = start + (n // ZERO_ROWS) * ZERO_ROWS
                    return pltpu.make_async_copy(
                        zero_scr.at[pl.ds(0, rows), :],
                        xs_ref.at[pl.ds(pl.multiple_of(tail0 + off, RUN_ALIGN), rows), :], zsem)

                _bit_chunks((n % ZERO_ROWS) // RUN_ALIGN, ZERO_ROWS // RUN_ALIGN - 1, make, wait)


def _moe_dispatch(n_tile, lo_tile, dst_tile, gap0, gapn, x_all, mod3, g_norm2, route, lo_vec, l):
    tm = ROUTE_TM
    spec = lambda shape, fn: pl.BlockSpec(shape, lambda i, *_: fn(i))
    return pl.pallas_call(
        _dispatch_kernel,
        out_shape=jax.ShapeDtypeStruct((MOE_ROWS, D_MODEL), F32),
        grid_spec=pltpu.PrefetchScalarGridSpec(
            num_scalar_prefetch=5,
            grid=(N_TOK // tm,),
            in_specs=[spec((tm, D_MODEL), lambda i: (i, 0)),
                      spec((None, 6, D_MODEL), lambda i: (_mod_row(i * tm), 0, 0)),
                      spec((None, 1, D_MODEL), lambda i: (l, 0, 0)),
                      spec((tm, ROUTER_LANES), lambda i: (i, 0)),
                      spec((None, 1, ROUTER_LANES), lambda i: (i, 0, 0))],
            out_specs=pl.BlockSpec(memory_space=pl.ANY),
            scratch_shapes=[pltpu.VMEM((2, 2 * tm + N_EXPERTS * RUN_ALIGN, D_MODEL), F32),
                            pltpu.VMEM((ZERO_ROWS, D_MODEL), F32),
                            pltpu.SemaphoreType.DMA((2,)), pltpu.SemaphoreType.DMA],
        ),
        compiler_params=_cparams(("arbitrary",)),
        name="moe_dispatch",
    )(n_tile, lo_tile, dst_tile, gap0, gapn, x_all, mod3, g_norm2.reshape(DEPTH, 1, D_MODEL), route, lo_vec)


def _gmm_kernel(te_ref, tr_ref, xs_ref, wg_ref, wu_ref, wd_ref, y_ref, wg_scr, wu_scr, wd_scr):
    del te_ref
    g = pl.program_id(0)
    f = pl.program_id(1)
    rows = tr_ref[g]

    @pl.when(f == 0)
    def _():
        y_ref[...] = jnp.zeros_like(y_ref)

    def sub_tile(s, wg, wu, wd):
        sl = slice(s * MOE_SUB, (s + 1) * MOE_SUB)
        x = xs_ref[sl, :].astype(BF16)
        a = (_silu(jnp.dot(x, wg, preferred_element_type=F32)) * jnp.dot(x, wu, preferred_element_type=F32))
        y_ref[sl, :] += jnp.dot(a.astype(BF16), wd, preferred_element_type=F32)

    @pl.when(rows == MOE_TM)
    def _():
        x = xs_ref[...].astype(BF16)
        a = _silu(_dot(x, wg_ref[...])) * _dot(x, wu_ref[...])
        y_ref[...] += _dot(a, wd_ref[...])

    @pl.when((rows > 0) & (rows < MOE_TM))
    def _():
        wg = wg_ref[...].astype(BF16)
        wu = wu_ref[...].astype(BF16)
        wd = wd_ref[...].astype(BF16)
        wg_scr[...] = wg
        wu_scr[...] = wu
        wd_scr[...] = wd
        sub_tile(0, wg, wu, wd)

    for s in range(1, MOE_TM // MOE_SUB):
        @pl.when((rows > s * MOE_SUB) & (rows < MOE_TM))
        def _():
            sub_tile(s, wg_scr[...], wu_scr[...], wd_scr[...])


def _moe_gmm(tile_expert, tile_rows, xs, w_g, w_u, w_d, j):
    tm, tf = MOE_TM, MOE_TF
    nf = D_FF // tf

    def f_eff(g, f, tr):
        return jnp.where(tr[g] > 0, f, nf - 1)

    return pl.pallas_call(
        _gmm_kernel,
        out_shape=jax.ShapeDtypeStruct((MOE_ROWS, D_MODEL), F32),
        grid_spec=pltpu.PrefetchScalarGridSpec(
            num_scalar_prefetch=2,
            grid=(MOE_TILES, nf),
            in_specs=[pl.BlockSpec((tm, D_MODEL), lambda g, f, te, tr: (g, 0)),
                      pl.BlockSpec((None, None, D_MODEL, tf), lambda g, f, te, tr: (j, te[g], 0, f_eff(g, f, tr))),
                      pl.BlockSpec((None, None, D_MODEL, tf), lambda g, f, te, tr: (j, te[g], 0, f_eff(g, f, tr))),
                      pl.BlockSpec((None, None, tf, D_MODEL), lambda g, f, te, tr: (j, te[g], f_eff(g, f, tr), 0))],
            out_specs=pl.BlockSpec((tm, D_MODEL), lambda g, f, te, tr: (g, 0)),
            scratch_shapes=[pltpu.VMEM((D_MODEL, tf), BF16), pltpu.VMEM((D_MODEL, tf), BF16),
                            pltpu.VMEM((tf, D_MODEL), BF16)],
        ),
        compiler_params=_cparams(("arbitrary", "arbitrary")),
        name="moe_experts",
    )(tile_expert, tile_rows, xs, w_g, w_u, w_d)


ZY_ROWS = 1152


def _combine_kernel(n_out, n_ref, lo_ref, dst_ref, x_ref, mod_ref, route_ref, lovec_ref, y_ref, *refs):
    o_refs, (zy_scr, sems) = refs[:n_out], refs[n_out:]
    i = pl.program_id(0)
    tm = x_ref.shape[0]

    def run_copies(tile, b, wait):
        for e in range(N_EXPERTS):
            k = tile * N_EXPERTS + e
            src0 = dst_ref[k]
            dst0 = lo_ref[k]

            def make(off, rows, src0=src0, dst0=dst0):
                return pltpu.make_async_copy(
                    y_ref.at[pl.ds(pl.multiple_of(src0 + off, RUN_ALIGN), rows), :],
                    zy_scr.at[b, pl.ds(pl.multiple_of(dst0 + off, RUN_ALIGN), rows), :], sems.at[b])

            _bit_chunks(n_ref[k], tm // RUN_ALIGN, make, wait)

    @pl.when(i == 0)
    def _():
        zy_scr[...] = jnp.zeros_like(zy_scr)
        run_copies(0, 0, False)

    buf = i % 2

    @pl.when(i + 1 < pl.num_programs(0))
    def _():
        run_copies(i + 1, 1 - buf, False)

    run_copies(i, buf, True)

    rec = route_ref[...]
    lane = lax.broadcasted_iota(jnp.int32, rec.shape, 1).astype(F32)
    lo_row = lovec_ref[...]

    def local_pos(i_lane, r_lane):
        return (jnp.sum(jnp.where(lane == rec[:, i_lane:i_lane + 1], lo_row, 0.0), axis=-1, keepdims=True)
                + rec[:, r_lane:r_lane + 1])

    slot = lax.broadcasted_iota(jnp.int32, (tm, ZY_ROWS), 1).astype(F32)
    wsel = (jnp.where(slot == local_pos(R_I1, R_RANK1), rec[:, R_W1:R_W1 + 1], 0.0)
            + jnp.where(slot == local_pos(R_I2, R_RANK2), rec[:, R_W2:R_W2 + 1], 0.0))
    w_hi = wsel.astype(BF16)
    w_lo = (wsel - w_hi.astype(F32)).astype(BF16)
    z = zy_scr[buf].astype(BF16)
    mix = jnp.dot(w_hi, z, preferred_element_type=F32) + jnp.dot(w_lo, z, preferred_element_type=F32)
    _x_store(o_refs, x_ref[...] + mod_ref[5:6, :] * mix)


def _moe_combine(n_tile, lo_tile, dst_tile, x_all, mod3, route, lo_vec, y, split_out):
    tm = ROUTE_TM
    spec = lambda shape, fn: pl.BlockSpec(shape, lambda i, *_: fn(i))
    return pl.pallas_call(
        functools.partial(_combine_kernel, 2 if split_out else 1),
        out_shape=_x_shapes(split_out),
        grid_spec=pltpu.PrefetchScalarGridSpec(
            num_scalar_prefetch=3,
            grid=(N_TOK // tm,),
            in_specs=[spec((tm, D_MODEL), lambda i: (i, 0)),
                      spec((None, 6, D_MODEL), lambda i: (_mod_row(i * tm), 0, 0)),
                      spec((tm, ROUTER_LANES), lambda i: (i, 0)),
                      spec((None, 1, ROUTER_LANES), lambda i: (i, 0, 0)),
                      pl.BlockSpec(memory_space=pl.ANY)],
            out_specs=_x_specs((None,) * (2 if split_out else 1), tm),
            scratch_shapes=[pltpu.VMEM((2, ZY_ROWS, D_MODEL), F32), pltpu.SemaphoreType.DMA((2,))],
        ),
        compiler_params=_cparams(("arbitrary",)),
        name="moe_combine",
    )(n_tile, lo_tile, dst_tile, x_all, mod3, route, lo_vec, y)


def _moe(x_all, mod3, g_norm2, w_router_p, w_g, w_u, w_d, l, j, split_out):
    route, cnt, tcarry = _moe_route(x_all, mod3, g_norm2, w_router_p, l, j)
    cnt = cnt[0, :N_EXPERTS].astype(jnp.int32)
    carry = tcarry[:, 0, :N_EXPERTS].astype(jnp.int32)
    n_tile = jnp.concatenate([carry[1:], cnt[None, :]], axis=0) - carry
    n_tile = (n_tile + RUN_ALIGN - 1) // RUN_ALIGN * RUN_ALIGN
    lo_tile = jnp.cumsum(n_tile, axis=1) - n_tile
    carry = jnp.cumsum(n_tile, axis=0) - n_tile
    cnt = jnp.sum(n_tile, axis=0)
    padded = (cnt + MOE_TM - 1) // MOE_TM * MOE_TM
    ends = jnp.cumsum(padded)
    offs = ends - padded
    dst_tile = offs[None, :] + carry
    experts = jnp.arange(N_EXPERTS, dtype=jnp.int32)
    gap0 = jnp.concatenate([offs + cnt, ends[-1:]])
    gapn = jnp.concatenate([padded - cnt, MOE_ROWS - ends[-1:]])
    lo_vec = jnp.pad(lo_tile.astype(F32), ((0, 0), (0, ROUTER_LANES - N_EXPERTS)))[:, None, :]

    tile_start = jnp.arange(MOE_TILES, dtype=jnp.int32) * MOE_TM
    last_tile = jnp.maximum(ends[-1] - MOE_TM, 0)
    owner_start = jnp.minimum(tile_start, last_tile)
    tile_expert = jnp.minimum(jnp.sum(owner_start[:, None] >= ends[None, :], axis=1), N_EXPERTS - 1).astype(jnp.int32)
    group_end = jnp.sum(jnp.where(tile_expert[:, None] == experts[None, :], (offs + cnt)[None, :], 0), axis=1)
    tile_rows = jnp.where(tile_start < ends[-1], jnp.clip(group_end - tile_start, 0, MOE_TM), 0).astype(jnp.int32)

    i32 = lambda a: a.reshape(-1).astype(jnp.int32)
    runs = (i32(n_tile // RUN_ALIGN), i32(lo_tile), i32(dst_tile))
    xs = _moe_dispatch(*runs, i32(gap0), i32(gapn), x_all, mod3, g_norm2, route, lo_vec, l)
    y = _moe_gmm(tile_expert, tile_rows, xs, w_g, w_u, w_d, j)
    return tuple(_moe_combine(*runs, x_all, mod3, route, lo_vec, y, split_out))


def _pad_heads(w, per_head):
    lead = w.shape[:-1]
    w = w.reshape(lead + (MLA_HEADS, per_head))
    w = jnp.pad(w, [(0, 0)] * len(lead) + [(0, 0), (0, HEAD_PAD - per_head)])
    return w.reshape(lead + (MLA_PAD_W,))


W_IN_HEAD = MLA_Q_LORA + MLA_KV_LORA + MLA_ROPE
W_IN_COLS = W_IN_HEAD + 8 * BR_W + N_BRANCH * D_MODEL


def _w_in_layout_kernel(w_ref, o_ref):
    w = w_ref[...]
    o_ref[:, :W_IN_HEAD] = w[:, :W_IN_HEAD].astype(BF16)
    o_ref[:, W_IN_HEAD:P_SC] = jnp.zeros((w.shape[0], P_SC - W_IN_HEAD), BF16)
    o_ref[:, P_SC:] = w[:, W_IN_HEAD:].astype(BF16)


def _w_in_layout(w_in, l):
    tk = 128
    return pl.pallas_call(
        _w_in_layout_kernel,
        out_shape=jax.ShapeDtypeStruct((D_MODEL, P_COLS), BF16),
        grid=(D_MODEL // tk,),
        in_specs=[pl.BlockSpec((None, tk, W_IN_COLS), lambda k: (l, k, 0))],
        out_specs=pl.BlockSpec((tk, P_COLS), lambda k: (k, 0)),
        compiler_params=_cparams(("arbitrary",)),
        name="w_in_layout",
    )(w_in)


def _layer_weights(l, w_in, g_qa, w_uq, g_kva, w_ukv, g_mla_q, g_mla_k, sc_w, g_na_q, g_na_k,
                   cf_w, cf_b, cf_ln_g, cf_ln_b, w_br, w_o):
    w_in_p = _w_in_layout(w_in, l)
    ukv = w_ukv[l].reshape(MLA_KV_LORA, MLA_HEADS, MLA_NOPE + MLA_V)
    W = dict(
        w_in_p=w_in_p,
        g_qa=g_qa[l][None], g_kva=g_kva[l][None],
        w_uq_p=_pad_heads(w_uq[l], MLA_QK).astype(BF16),
        w_ukv_k=_pad_heads(ukv[:, :, :MLA_NOPE].reshape(MLA_KV_LORA, -1), MLA_NOPE).astype(BF16),
        w_ukv_v=_pad_heads(ukv[:, :, MLA_NOPE:].reshape(MLA_KV_LORA, -1), MLA_V).astype(BF16),
        g_mla_q_p=jnp.pad(g_mla_q[l], (0, HEAD_PAD - MLA_QK))[None],
        g_mla_k_p=jnp.pad(g_mla_k[l], (0, HEAD_PAD - MLA_QK))[None],
        g_na_q_t=jnp.tile(g_na_q[l], NA_HEADS)[None], g_na_k_t=jnp.tile(g_na_k[l], NA_HEADS)[None],
        sc_w=sc_w[l], cf_w=cf_w[l], cf_b=cf_b[l][None], cf_ln_g=cf_ln_g[l][None], cf_ln_b=cf_ln_b[l][None],
        w_br=w_br[l].astype(BF16),
        w_o=w_o[l].astype(BF16),
    )
    return W


def kernel(x_prompt, x_sample, cache_mla_ckv, cache_mla_kpe, cache_na_k, cache_na_v, c, c_ctx, w_ada, b_ada, g_norm1, w_in, g_qa, w_uq, g_kva, w_ukv, g_mla_q, g_mla_k, sc_w, g_na_q, g_na_k, na_rpb, cf_w, cf_b, cf_ln_g, cf_ln_b, w_br, w_o, g_norm2, w_ff_gate, w_ff_up, w_ff_down, w_router, w_e_gate, w_e_up, w_e_down):
    x = (x_prompt.reshape(N_CTX_TOK, D_MODEL), x_sample.reshape(N_LAT_TOK, D_MODEL))
    cvec = jnp.concatenate([c_ctx[None, :], c, jnp.zeros((MOD_ROWS - 1 - DEC_BATCH, D_MODEL), F32)], axis=0)
    cache_na_k2 = cache_na_k.reshape(DEC_BATCH, DEPTH, PAST_LEN, BR_W)
    cache_na_v2 = cache_na_v.reshape(DEC_BATCH, DEPTH, PAST_LEN, BR_W)
    w_router_p = jnp.pad(w_router, ((0, 0), (0, 0), (0, ROUTER_LANES - N_EXPERTS)))

    ckv_l, kpe_l, nak_l, nav_l = [], [], [], []
    for l in range(DEPTH):
        W = _layer_weights(l, w_in, g_qa, w_uq, g_kva, w_ukv, g_mla_q, g_mla_k, sc_w, g_na_q, g_na_k,
                           cf_w, cf_b, cf_ln_g, cf_ln_b, w_br, w_o)
        mod3 = _modulation(cvec, w_ada, b_ada, l).reshape(MOD_ROWS, 6, D_MODEL)
        p = _in_projection(x, mod3, g_norm1, W['w_in_p'], l)

        oa_c, ckv_new, kpe_new = _mla(p, l, False, None, None, W)
        kpe_g = jnp.pad(cache_mla_kpe[:, l], ((0, 0), (0, 0), (MLA_NOPE, HEAD_PAD - MLA_QK)))
        (oa_l,) = _mla(p, l, True, cache_mla_ckv, kpe_g, W)
        oc_c, nak_new, nav_new = _na_context(p, W)
        oc_l = _na_latent(p, l, cache_na_k2, cache_na_v2, _na_bias_table(na_rpb[l]), W)
        ob_c, od_c = _convs(p, False, W)
        ob_l, od_l = _convs(p, True, W)

        x_all = _merge(x, mod3, p, (oa_c, oa_l), (ob_c, ob_l), (oc_c, oc_l), (od_c, od_l), W)

        last = l == DEPTH - 1
        if l % 2 == 0:
            x = (_ffn(x_all, mod3, g_norm2, w_ff_gate, w_ff_up, w_ff_down, l, l // 2),)
        else:
            x = _moe(x_all, mod3, g_norm2, w_router_p, w_e_gate, w_e_up, w_e_down, l, l // 2, last)

        ckv_l.append(ckv_new)
        kpe_l.append(kpe_new)
        nak_l.append(nak_new.reshape(BATCH, SEQ, NA_HEADS, NA_HD))
        nav_l.append(nav_new.reshape(BATCH, SEQ, NA_HEADS, NA_HD))

    if len(x) == 1:
        x = (x[0][:N_CTX_TOK], x[0][N_CTX_TOK:])
    y_prompt = x[0].reshape(BATCH, SEQ, D_MODEL)
    y_sample = x[1].reshape(DEC_BATCH, DEC_SEQ, D_MODEL)
    return (y_prompt, y_sample, jnp.stack(ckv_l, axis=1), jnp.stack(kpe_l, axis=1),
            jnp.stack(nak_l, axis=1), jnp.stack(nav_l, axis=1))
```

```python
import functools

import numpy as np
import jax
import jax.numpy as jnp
from jax import lax
from jax.experimental import pallas as pl
from jax.experimental.pallas import tpu as pltpu

F32 = jnp.float32
BF16 = jnp.bfloat16

D_MODEL = 1024
BATCH = 16
SEQ = 256
DEPTH = 2
DEC_BATCH = 8
DEC_SEQ = 1024
PAST_LEN = 512
GRID_W = 64
N_BRANCH = 4
BR_W = 256
MLA_HEADS = 4
MLA_NOPE = 64
MLA_ROPE = 32
MLA_QK = 96
MLA_V = 64
MLA_Q_LORA = 256
MLA_KV_LORA = 128
SC_K = 3
NA_HEADS = 4
NA_HD = 64
NA_WIN_R = 8
NA_WIN_C = 16
CF_K = 31
D_FF = 3584
N_EXPERTS = 8
ROPE_THETA = 10000.0
EPS = 1e-6

N_CTX_TOK = BATCH * SEQ
N_LAT_TOK = DEC_BATCH * DEC_SEQ
N_TOK = N_CTX_TOK + N_LAT_TOK
MOD_ROWS = 16

HEAD_PAD = 128
MLA_PAD_W = MLA_HEADS * HEAD_PAD

P_QA = 0
P_KVA = 256
P_SC = 512
P_NA = 1280
P_CF = 2048
P_GATE = 2560
GATE_BLK = 512
P_COLS = 6656

VMEM_LIMIT = 56 * 1024 * 1024

NA_TQ = 256
NA_WIN_ROWS = 12
NA_WIN_KEYS = NA_WIN_ROWS * GRID_W


def _cparams(sem):
    return pltpu.CompilerParams(dimension_semantics=sem, vmem_limit_bytes=VMEM_LIMIT)


def _const_spec(shape):
    nd = len(shape)
    return pl.BlockSpec(shape, lambda *_: (0,) * nd)


def _mod_row(tok_start):
    return jnp.where(tok_start < N_CTX_TOK, 0, 1 + (tok_start - N_CTX_TOK) // DEC_SEQ)


def _sigmoid(x):
    return 0.5 * jnp.tanh(0.5 * x) + 0.5


def _silu(x):
    return x * _sigmoid(x)


def _ctx_or_lat(is_ctx, c_ref, l_ref):
    return jnp.where(is_ctx, c_ref[...], l_ref[...])


def _dot(a, b):
    return jnp.dot(a.astype(BF16), b.astype(BF16), preferred_element_type=F32)


def _dot_nt(a, b):
    return lax.dot_general(a.astype(BF16), b.astype(BF16), (((1,), (1,)), ((), ())),
                           preferred_element_type=F32)


def _mod_kernel(c_ref, w_ref, b_ref, o_ref):
    o_ref[...] = _dot(_silu(c_ref[...]), w_ref[...]) + b_ref[...]


def _modulation(cvec, w_ada, b_ada, l):
    tn = 1024
    return pl.pallas_call(
        _mod_kernel,
        out_shape=jax.ShapeDtypeStruct((MOD_ROWS, 6 * D_MODEL), F32),
        grid=(6 * D_MODEL // tn,),
        in_specs=[
            _const_spec((MOD_ROWS, D_MODEL)),
            pl.BlockSpec((None, D_MODEL, tn), lambda j: (l, 0, j)),
            pl.BlockSpec((None, 1, tn), lambda j: (l, 0, j)),
        ],
        out_specs=pl.BlockSpec((MOD_ROWS, tn), lambda j: (0, j)),
        compiler_params=_cparams(("arbitrary",)),
        name="modulation",
    )(cvec, w_ada, b_ada.reshape(DEPTH, 1, 6 * D_MODEL))


def _pair_specs(tm, width):
    nc = N_CTX_TOK // tm
    return [pl.BlockSpec((tm, width), lambda i, *_: (jnp.minimum(i, nc - 1), 0)),
            pl.BlockSpec((tm, width), lambda i, *_: (jnp.maximum(i - nc, 0), 0))]


def _pair_shapes(width, dtype):
    return [jax.ShapeDtypeStruct((N_CTX_TOK, width), dtype), jax.ShapeDtypeStruct((N_LAT_TOK, width), dtype)]


def _store_pair(is_ctx, c_ref, l_ref, val):
    @pl.when(is_ctx)
    def _():
        c_ref[...] = val

    @pl.when(jnp.logical_not(is_ctx))
    def _():
        l_ref[...] = val


def _x_specs(x, tm):
    if len(x) == 2:
        return _pair_specs(tm, D_MODEL)
    return [pl.BlockSpec((tm, D_MODEL), lambda i, *_: (i, 0))]


def _x_shapes(split):
    return _pair_shapes(D_MODEL, F32) if split else [jax.ShapeDtypeStruct((N_TOK, D_MODEL), F32)]


def _x_load(x_refs):
    if len(x_refs) == 1:
        return x_refs[0][...]
    tm = x_refs[0].shape[0]
    return _ctx_or_lat(pl.program_id(0) < N_CTX_TOK // tm, *x_refs)


def _x_store(y_refs, val):
    if len(y_refs) == 1:
        y_refs[0][...] = val
    else:
        tm = y_refs[0].shape[0]
        _store_pair(pl.program_id(0) < N_CTX_TOK // tm, *y_refs, val)


def _inproj_kernel(nx, *refs):
    x_refs, (mod_ref, g_ref, w_ref, o_ref, h_scr) = refs[:nx], refs[nx:]

    @pl.when(pl.program_id(1) == 0)
    def _():
        x = _x_load(x_refs)
        y = x * lax.rsqrt(jnp.mean(x * x, axis=-1, keepdims=True) + EPS) * g_ref[...]
        h_scr[...] = (y * (1.0 + mod_ref[1:2, :]) + mod_ref[0:1, :]).astype(BF16)

    o_ref[...] = jnp.dot(h_scr[...], w_ref[...], preferred_element_type=F32).astype(o_ref.dtype)


def _in_projection(x, mod3, g_norm1, w_in_p, l):
    tm, tn = 1024, P_COLS // 4
    return pl.pallas_call(
        functools.partial(_inproj_kernel, len(x)),
        out_shape=jax.ShapeDtypeStruct((N_TOK, P_COLS), BF16),
        grid=(N_TOK // tm, P_COLS // tn),
        in_specs=_x_specs(x, tm) + [
            pl.BlockSpec((None, 6, D_MODEL), lambda i, j: (_mod_row(i * tm), 0, 0)),
            pl.BlockSpec((None, 1, D_MODEL), lambda i, j: (l, 0, 0)),
            pl.BlockSpec((D_MODEL, tn), lambda i, j: (0, j)),
        ],
        out_specs=pl.BlockSpec((tm, tn), lambda i, j: (i, j)),
        scratch_shapes=[pltpu.VMEM((tm, D_MODEL), BF16)],
        compiler_params=_cparams(("arbitrary", "arbitrary")),
        name="in_projection",
    )(*x, mod3, g_norm1.reshape(DEPTH, 1, D_MODEL), w_in_p)


def _rope_tables():
    t = np.arange(DEC_SEQ)
    nf = MLA_ROPE // 4
    inv = (np.float32(ROPE_THETA) ** (-np.arange(nf, dtype=np.float32) / np.float32(nf))).astype(np.float32)
    ang_r = (t // GRID_W).astype(np.float32)[:, None] * inv[None, :]
    ang_c = (t % GRID_W).astype(np.float32)[:, None] * inv[None, :]
    c = np.zeros((DEC_SEQ, HEAD_PAD), np.float32)
    s1 = np.zeros((DEC_SEQ, HEAD_PAD), np.float32)
    s2 = np.zeros((DEC_SEQ, HEAD_PAD), np.float32)
    c[:, :MLA_NOPE] = 1.0
    for base, ang in ((MLA_NOPE, ang_r), (MLA_NOPE + 2 * nf, ang_c)):
        c[:, base:base + nf] = np.cos(ang)
        c[:, base + nf:base + 2 * nf] = np.cos(ang)
        s1[:, base:base + nf] = -np.sin(ang)
        s2[:, base + nf:base + 2 * nf] = np.sin(ang)
    return c, s1, s2


def _rope(x, c, s1, s2):
    nf = MLA_ROPE // 4
    return x * c + pltpu.roll(x, HEAD_PAD - nf, 1) * s1 + pltpu.roll(x, nf, 1) * s2


def _head_norm(xh, g):
    ms = jnp.sum(xh * xh, axis=-1, keepdims=True) * (1.0 / MLA_QK)
    return xh * lax.rsqrt(ms + EPS) * g


def _mla_kernel(latent, tq, *refs):
    if latent:
        (pqa_ref, pkva_ref, cckv_ref, ckpe_ref, rc_ref, rs1_ref, rs2_ref,
         gqa_ref, wuq_ref, gkva_ref, wk_ref, wv_ref, gq_ref, gk_ref,
         o_ref, k_scr, v_scr) = refs
    else:
        (pqa_ref, pkva_ref,
         gqa_ref, wuq_ref, gkva_ref, wk_ref, wv_ref, gq_ref, gk_ref,
         o_ref, ckv_ref, kpe_ref, k_scr, v_scr) = refs
    qi = pl.program_id(1)
    n_past = PAST_LEN if latent else 0

    vlane = lax.broadcasted_iota(jnp.int32, (1, MLA_PAD_W), 1)
    v_ones = jnp.where(vlane % HEAD_PAD == MLA_V, 1.0, 0.0)

    def put_kv(ckvn, kpe_g, row0, rope):
        n = ckvn.shape[0]
        kk = _dot(ckvn, wk_ref[...])
        v_scr[row0:row0 + n, :] = (_dot(ckvn, wv_ref[...]) + v_ones).astype(BF16)
        for h in range(MLA_HEADS):
            sl = slice(h * HEAD_PAD, (h + 1) * HEAD_PAD)
            kh = _head_norm(kk[:, sl] + kpe_g, gk_ref[...])
            if rope:
                kh = _rope(kh, rc_ref[...], rs1_ref[...], rs2_ref[...])
            k_scr[row0:row0 + n, sl] = kh.astype(BF16)

    @pl.when(qi == 0)
    def _():
        if latent:
            put_kv(cckv_ref[...], ckpe_ref[...], 0, False)
        kva = pkva_ref[...].astype(F32)
        ckv = kva[:, :MLA_KV_LORA]
        ckvn = ckv * lax.rsqrt(jnp.mean(ckv * ckv, axis=-1, keepdims=True) + EPS) * gkva_ref[...]
        if not latent:
            ckv_ref[...] = ckvn
            kpe_ref[...] = kva[:, MLA_KV_LORA:MLA_KV_LORA + MLA_ROPE]
        put_kv(ckvn, pltpu.roll(kva[:, MLA_KV_LORA:], MLA_NOPE, 1), n_past, latent)

    qa = pqa_ref[...].astype(F32)
    qan = qa * lax.rsqrt(jnp.mean(qa * qa, axis=-1, keepdims=True) + EPS) * gqa_ref[...]
    q = _dot(qan, wuq_ref[...])
    scale = MLA_QK ** -0.5
    for h in range(MLA_HEADS):
        sl = slice(h * HEAD_PAD, (h + 1) * HEAD_PAD)
        qh = _head_norm(q[:, sl], gq_ref[...])
        if latent:
            rows = pl.ds(pl.multiple_of(qi * tq, tq), tq)
            qh = _rope(qh, rc_ref[rows, :], rs1_ref[rows, :], rs2_ref[rows, :])
        s = _dot_nt(qh * scale, k_scr[:, sl])
        m = jnp.max(s, axis=-1, keepdims=True)
        e = jnp.exp((s - m).astype(BF16))
        o = jnp.dot(e, v_scr[:, sl], preferred_element_type=F32)
        o_ref[:, sl] = (o / o[:, MLA_V:MLA_V + 1]).astype(o_ref.dtype)


def _mla(p, l, latent, cache_ckv, cache_kpe_g, W):
    if latent:
        nb, s, tq, row_off = DEC_BATCH, DEC_SEQ, 256, N_CTX_TOK
    else:
        nb, s, tq, row_off = BATCH, SEQ, 256, 0
    nq = s // tq
    sk = s + (PAST_LEN if latent else 0)
    in_specs = [
        pl.BlockSpec((tq, MLA_Q_LORA), lambda b, qi: (row_off // tq + b * nq + qi, P_QA // MLA_Q_LORA)),
        pl.BlockSpec((s, 256), lambda b, qi: (row_off // s + b, P_KVA // 256)),
    ]
    args = [p, p]
    if latent:
        in_specs += [
            pl.BlockSpec((None, None, PAST_LEN, MLA_KV_LORA), lambda b, qi: (b, l, 0, 0)),
            pl.BlockSpec((None, PAST_LEN, HEAD_PAD), lambda b, qi: (b, 0, 0)),
            _const_spec((DEC_SEQ, HEAD_PAD)), _const_spec((DEC_SEQ, HEAD_PAD)), _const_spec((DEC_SEQ, HEAD_PAD)),
        ]
        args += [cache_ckv, cache_kpe_g] + [jnp.asarray(t) for t in _rope_tables()]
    in_specs += [
        _const_spec((1, MLA_Q_LORA)), _const_spec((MLA_Q_LORA, MLA_PAD_W)), _const_spec((1, MLA_KV_LORA)),
        _const_spec((MLA_KV_LORA, MLA_PAD_W)), _const_spec((MLA_KV_LORA, MLA_PAD_W)),
        _const_spec((1, HEAD_PAD)), _const_spec((1, HEAD_PAD)),
    ]
    args += [W['g_qa'], W['w_uq_p'], W['g_kva'], W['w_ukv_k'], W['w_ukv_v'], W['g_mla_q_p'], W['g_mla_k_p']]
    out_shape = [jax.ShapeDtypeStruct((nb * s, MLA_PAD_W), BF16)]
    out_specs = [pl.BlockSpec((tq, MLA_PAD_W), lambda b, qi: (b * nq + qi, 0))]
    if not latent:
        out_shape.append(jax.ShapeDtypeStruct((nb, s, MLA_KV_LORA), F32))
        out_specs.append(pl.BlockSpec((None, s, MLA_KV_LORA), lambda b, qi: (b, 0, 0)))
        out_shape.append(jax.ShapeDtypeStruct((nb, s, MLA_ROPE), F32))
        out_specs.append(pl.BlockSpec((None, s, MLA_ROPE), lambda b, qi: (b, 0, 0)))
    return pl.pallas_call(
        functools.partial(_mla_kernel, latent, tq),
        out_shape=out_shape,
        grid=(nb, nq),
        in_specs=in_specs,
        out_specs=out_specs,
        scratch_shapes=[pltpu.VMEM((sk, MLA_PAD_W), BF16), pltpu.VMEM((sk, MLA_PAD_W), BF16)],
        compiler_params=_cparams(("arbitrary", "arbitrary")),
        name="mla_latent" if latent else "mla_context",
    )(*args)


def _head_masks(width):
    lane = lax.broadcasted_iota(jnp.int32, (1, width), 1)
    return [(lane >= h * NA_HD) & (lane < (h + 1) * NA_HD) for h in range(NA_HEADS)]


def _group_norm64(x, g, masks):
    x2 = x * x
    inv = jnp.zeros_like(x)
    for m in masks:
        ms = jnp.sum(jnp.where(m, x2, 0.0), axis=-1, keepdims=True) * (1.0 / NA_HD)
        inv = jnp.where(m, lax.rsqrt(ms + EPS), inv)
    return x * inv * g


def _p_blocks(col0, n, rows, row_fn):
    def spec(k):
        return pl.BlockSpec((rows, BR_W), lambda *g: (row_fn(*g), col0 // BR_W + k))
    return [spec(k) for k in range(n)]


def _na_ctx_kernel(pq_ref, pk_ref, pv_ref, gq_ref, gk_ref, o_ref, k_ref, v_ref):
    masks = _head_masks(BR_W)
    qn = _group_norm64(pq_ref[...].astype(F32), gq_ref[...], masks)
    kn = _group_norm64(pk_ref[...].astype(F32), gk_ref[...], masks)
    v = pv_ref[...].astype(F32)
    k_ref[...] = kn
    v_ref[...] = v
    scale = NA_HD ** -0.5
    acc = jnp.zeros((pq_ref.shape[0], BR_W), F32)
    for m in masks:
        s = _dot_nt(jnp.where(m, qn, 0.0), kn) * scale
        mx = jnp.max(s, axis=-1, keepdims=True)
        e = jnp.exp(s - mx)
        den = jnp.sum(e, axis=-1, keepdims=True)
        acc = acc + jnp.where(m, _dot(e, v) / den, 0.0)
    o_ref[...] = acc.astype(o_ref.dtype)


def _na_context(p, W):
    s = SEQ
    return pl.pallas_call(
        _na_ctx_kernel,
        out_shape=[jax.ShapeDtypeStruct((N_CTX_TOK, BR_W), BF16),
                   jax.ShapeDtypeStruct((N_CTX_TOK, BR_W), F32),
                   jax.ShapeDtypeStruct((N_CTX_TOK, BR_W), F32)],
        grid=(BATCH,),
        in_specs=_p_blocks(P_NA, 3, s, lambda b: b) + [_const_spec((1, BR_W)), _const_spec((1, BR_W))],
        out_specs=[pl.BlockSpec((s, BR_W), lambda b: (b, 0))] * 3,
        compiler_params=_cparams(("arbitrary",)),
        name="na_context",
    )(p, p, p, W['g_na_q_t'], W['g_na_k_t'])


def _na_lat_kernel(pq_ref, pk_ref, pv_ref, ck_ref, cv_ref, bias_ref, gq_ref, gk_ref, o_ref,
                   q_scr, k_scr, v_scr, kc_scr, vc_scr):
    j = pl.program_id(1)
    masks = _head_masks(BR_W)

    zero = jnp.zeros((), BF16)
    one = jnp.ones((), BF16)
    lane = lax.broadcasted_iota(jnp.int32, (1, BR_W), 1)
    den_lanes = [((h + 1) % NA_HEADS) * NA_HD for h in range(NA_HEADS)]

    @pl.when(j == 0)
    def _():
        scale = NA_HD ** -0.5
        q_scr[...] = (_group_norm64(pq_ref[...].astype(F32), gq_ref[...], masks) * scale).astype(BF16)
        k_scr[...] = _group_norm64(pk_ref[...].astype(F32), gk_ref[...], masks).astype(BF16)
        kc_scr[...] = ck_ref[...].astype(BF16)
        v = pv_ref[...]
        vc = cv_ref[...].astype(BF16)
        for h, m in enumerate(masks):
            v_scr[h] = jnp.where(lane == den_lanes[h], one, jnp.where(m, v, zero))
            vc_scr[h] = jnp.where(lane == den_lanes[h], one, jnp.where(m, vc, zero))

    win0 = pl.multiple_of(jnp.where(j < 2, 0, DEC_SEQ - NA_WIN_KEYS), 256)
    q = q_scr[pl.ds(pl.multiple_of(j * NA_TQ, NA_TQ), NA_TQ), :]
    kw = k_scr[pl.ds(win0, NA_WIN_KEYS), :]
    kc = kc_scr[...]
    acc = jnp.zeros((NA_TQ, BR_W), F32)
    for h, m in enumerate(masks):
        qm = jnp.where(m, q, zero)
        s_loc = _dot_nt(qm, kw) + bias_ref[h]
        s_ctx = _dot_nt(qm, kc)
        mx = jnp.maximum(jnp.max(s_loc, axis=-1, keepdims=True), jnp.max(s_ctx, axis=-1, keepdims=True))
        e_loc = jnp.exp((s_loc - mx).astype(BF16))
        e_ctx = jnp.exp((s_ctx - mx).astype(BF16))
        o = (jnp.dot(e_loc, v_scr[h, pl.ds(win0, NA_WIN_KEYS), :], preferred_element_type=F32)
             + jnp.dot(e_ctx, vc_scr[h], preferred_element_type=F32))
        acc = acc + jnp.where(m, o / o[:, den_lanes[h]:den_lanes[h] + 1], 0.0)
    o_ref[...] = acc.astype(o_ref.dtype)


def _na_bias_table(rpb):
    n_dr, n_dc = 2 * NA_WIN_R - 1, 2 * NA_WIN_C - 1
    cols = np.arange(GRID_W)
    dc = np.clip(cols[None, :] - cols[:, None], -(NA_WIN_C - 1), NA_WIN_C - 1) + (NA_WIN_C - 1)
    cstart = np.clip(cols - NA_WIN_C // 2, 0, GRID_W - NA_WIN_C)
    col_ok = (cols[None, :] >= cstart[:, None]) & (cols[None, :] < cstart[:, None] + NA_WIN_C)
    place = (dc.reshape(-1)[None, :] == np.arange(n_dc)[:, None]).astype(np.float32)
    blocks = jnp.dot(rpb.reshape(NA_HEADS * n_dr, n_dc), place, precision=lax.Precision.HIGHEST)
    blocks = jnp.where(col_ok.reshape(-1)[None, :], blocks, -jnp.inf).reshape(NA_HEADS, n_dr, GRID_W, GRID_W)
    neg = jnp.full((NA_HEADS, 1, GRID_W, GRID_W), -jnp.inf, F32)
    blocks = jnp.concatenate([blocks, neg], axis=1)

    nt = DEC_SEQ // NA_TQ
    n_rows = DEC_SEQ // GRID_W
    rows_per_tile = NA_TQ // GRID_W
    sel = np.full((nt, rows_per_tile, NA_WIN_ROWS), n_dr, np.int32)
    for j in range(nt):
        win_row0 = 0 if j < nt // 2 else n_rows - NA_WIN_ROWS
        for rq in range(rows_per_tile):
            r = j * rows_per_tile + rq
            start = min(max(r - NA_WIN_R // 2, 0), n_rows - NA_WIN_R)
            for kr in range(NA_WIN_ROWS):
                if start <= win_row0 + kr < start + NA_WIN_R:
                    sel[j, rq, kr] = win_row0 + kr - r + (NA_WIN_R - 1)
    tiles = []
    for j in range(nt):
        rows = [jnp.concatenate([blocks[:, sel[j, rq, kr]] for kr in range(NA_WIN_ROWS)], axis=-1)
                for rq in range(rows_per_tile)]
        tiles.append(jnp.concatenate(rows, axis=1))
    return jnp.stack(tiles, axis=1)


def _na_latent(p, l, cache_k, cache_v, bias, W):
    s = DEC_SEQ
    nt = s // NA_TQ
    row_off = N_CTX_TOK
    return pl.pallas_call(
        _na_lat_kernel,
        out_shape=jax.ShapeDtypeStruct((N_LAT_TOK, BR_W), BF16),
        grid=(DEC_BATCH, nt),
        in_specs=_p_blocks(P_NA, 3, s, lambda b, j: row_off // s + b) + [
                  pl.BlockSpec((None, None, PAST_LEN, BR_W), lambda b, j: (b, l, 0, 0)),
                  pl.BlockSpec((None, None, PAST_LEN, BR_W), lambda b, j: (b, l, 0, 0)),
                  pl.BlockSpec((NA_HEADS, None, NA_TQ, NA_WIN_KEYS), lambda b, j: (0, j, 0, 0)),
                  _const_spec((1, BR_W)), _const_spec((1, BR_W))],
        out_specs=pl.BlockSpec((NA_TQ, BR_W), lambda b, j: (b * nt + j, 0)),
        scratch_shapes=[pltpu.VMEM((s, BR_W), BF16), pltpu.VMEM((s, BR_W), BF16),
                        pltpu.VMEM((NA_HEADS, s, BR_W), BF16),
                        pltpu.VMEM((PAST_LEN, BR_W), BF16), pltpu.VMEM((NA_HEADS, PAST_LEN, BR_W), BF16)],
        compiler_params=_cparams(("arbitrary", "arbitrary")),
        name="na_latent",
    )(p, p, p, cache_k, cache_v, bias, W['g_na_q_t'], W['g_na_k_t'])


CONV_HALO = 16
CONV_CHUNK = 128


SUBLANES = 8


def _dwconv_from_pad(pad_ref, w_ref, ksize, s, emit, shift_ref=None):
    half = ksize // 2
    if shift_ref is not None:
        n_rows = s + 2 * CONV_HALO - SUBLANES
        for p in range(1, SUBLANES):
            for c0 in range(0, n_rows, CONV_CHUNK):
                n = min(CONV_CHUNK, n_rows - c0)
                shift_ref[p - 1, c0:c0 + n, :] = pad_ref[c0 + p:c0 + p + n, :]
    for c0 in range(0, s, CONV_CHUNK):
        acc = jnp.zeros((CONV_CHUNK, BR_W), F32)
        for k in range(ksize):
            r0 = CONV_HALO + c0 + k - half
            p = r0 % SUBLANES
            if shift_ref is None or p == 0:
                win = pad_ref[r0:r0 + CONV_CHUNK, :]
            else:
                win = shift_ref[p - 1, r0 - p:r0 - p + CONV_CHUNK, :]
            acc = acc + win * w_ref[k:k + 1, :]
        emit(c0, acc)


def _conv_kernel(s, scb_ref, scc_ref, scx_ref, cfa_ref, cfb2_ref, scw_ref, cfw_ref, cfb_ref, lng_ref, lnb_ref,
                 ob_ref, od_ref, pad_ref, shift_ref):
    zeros = jnp.zeros((CONV_HALO, BR_W), F32)
    pad_ref[0:CONV_HALO, :] = zeros
    pad_ref[CONV_HALO + s:2 * CONV_HALO + s, :] = zeros

    pad_ref[CONV_HALO:CONV_HALO + s, :] = scc_ref[...].astype(F32) * scx_ref[...].astype(F32)

    def emit_b(c0, acc):
        ob_ref[c0:c0 + CONV_CHUNK, :] = (scb_ref[c0:c0 + CONV_CHUNK, :].astype(F32) * acc).astype(ob_ref.dtype)

    _dwconv_from_pad(pad_ref, scw_ref, SC_K, s, emit_b)

    pad_ref[CONV_HALO:CONV_HALO + s, :] = cfa_ref[...].astype(F32) * _sigmoid(cfb2_ref[...].astype(F32))

    def emit_d(c0, acc):
        u = acc + cfb_ref[...]
        mu = jnp.mean(u, axis=-1, keepdims=True)
        d = u - mu
        var = jnp.mean(d * d, axis=-1, keepdims=True)
        y = d * lax.rsqrt(var + EPS) * lng_ref[...] + lnb_ref[...]
        od_ref[c0:c0 + CONV_CHUNK, :] = _silu(y).astype(od_ref.dtype)

    _dwconv_from_pad(pad_ref, cfw_ref, CF_K, s, emit_d, shift_ref)


def _convs(p, latent, W):
    if latent:
        nb, s, row_off = DEC_BATCH, DEC_SEQ, N_CTX_TOK
    else:
        nb, s, row_off = BATCH, SEQ, 0
    return pl.pallas_call(
        functools.partial(_conv_kernel, s),
        out_shape=[jax.ShapeDtypeStruct((nb * s, BR_W), BF16)] * 2,
        grid=(nb,),
        in_specs=(_p_blocks(P_SC, 3, s, lambda b: row_off // s + b)
                  + _p_blocks(P_CF, 2, s, lambda b: row_off // s + b)
                  + [_const_spec((SC_K, BR_W)), _const_spec((CF_K, BR_W)),
                     _const_spec((1, BR_W)), _const_spec((1, BR_W)), _const_spec((1, BR_W))]),
        out_specs=[pl.BlockSpec((s, BR_W), lambda b: (b, 0))] * 2,
        scratch_shapes=[pltpu.VMEM((s + 2 * CONV_HALO, BR_W), F32),
                        pltpu.VMEM((SUBLANES - 1, s + 2 * CONV_HALO, BR_W), F32)],
        compiler_params=_cparams(("arbitrary",)),
        name="convs_latent" if latent else "convs_context",
    )(p, p, p, p, p, W['sc_w'], W['cf_w'], W['cf_b'], W['cf_ln_g'], W['cf_ln_b'])


def _merge_kernel(nc, nx, *refs):
    x_refs = refs[:nx]
    (mod_ref, oac_ref, oal_ref, obc_ref, obl_ref, occ_ref, ocl_ref, odc_ref, odl_ref) = refs[nx:nx + 9]
    gate_refs = refs[nx + 9:nx + 9 + 2 * N_BRANCH]
    wa_ref, wb_ref, wo_ref, y_ref = refs[nx + 9 + 2 * N_BRANCH:]
    is_ctx = pl.program_id(0) < nc

    def gate(n):
        return jnp.concatenate([_sigmoid(gate_refs[2 * n][...]), _sigmoid(gate_refs[2 * n + 1][...])],
                               axis=1).astype(F32)

    merged = gate(0) * jnp.dot(_ctx_or_lat(is_ctx, oac_ref, oal_ref), wa_ref[...], preferred_element_type=F32)
    for n, (oc_ref, ol_ref) in enumerate(((obc_ref, obl_ref), (occ_ref, ocl_ref), (odc_ref, odl_ref))):
        merged = merged + gate(n + 1) * jnp.dot(_ctx_or_lat(is_ctx, oc_ref, ol_ref), wb_ref[n],
                                                preferred_element_type=F32)
    y_ref[...] = _x_load(x_refs) + mod_ref[2:3, :] * _dot(merged, wo_ref[...])


def _merge(x, mod3, p, o_a, o_b, o_c, o_d, W):
    tm = 512
    gate_spec = lambda k: pl.BlockSpec((tm, GATE_BLK), lambda i: (i, P_GATE // GATE_BLK + k))
    n_gate_blk = N_BRANCH * D_MODEL // GATE_BLK
    return pl.pallas_call(
        functools.partial(_merge_kernel, N_CTX_TOK // tm, len(x)),
        out_shape=jax.ShapeDtypeStruct((N_TOK, D_MODEL), F32),
        grid=(N_TOK // tm,),
        in_specs=(_x_specs(x, tm)
                  + [pl.BlockSpec((None, 6, D_MODEL), lambda i: (_mod_row(i * tm), 0, 0))]
                  + _pair_specs(tm, MLA_PAD_W) + _pair_specs(tm, BR_W) + _pair_specs(tm, BR_W) + _pair_specs(tm, BR_W)
                  + [gate_spec(k) for k in range(n_gate_blk)]
                  + [_const_spec((MLA_PAD_W, D_MODEL)), _const_spec((3, BR_W, D_MODEL)),
                     _const_spec((D_MODEL, D_MODEL))]),
        out_specs=pl.BlockSpec((tm, D_MODEL), lambda i: (i, 0)),
        compiler_params=_cparams(("arbitrary",)),
        name="merge",
    )(*x, mod3, *o_a, *o_b, *o_c, *o_d, *([p] * n_gate_blk), W['w_br_a'], W['w_br_bcd'], W['w_o'])


def _norm2(x, g, mod_ref):
    y = x * lax.rsqrt(jnp.mean(x * x, axis=-1, keepdims=True) + EPS) * g
    return y * (1.0 + mod_ref[4:5, :]) + mod_ref[3:4, :]


def _ffn_kernel(x_ref, mod_ref, g_ref, wg_ref, wu_ref, wd_ref, y_ref, h_scr, acc_scr):
    f = pl.program_id(1)

    @pl.when(f == 0)
    def _():
        h_scr[...] = _norm2(x_ref[...], g_ref[...], mod_ref).astype(BF16)
        acc_scr[...] = jnp.zeros_like(acc_scr)

    h = h_scr[...]
    a = _silu(_dot(h, wg_ref[...])) * _dot(h, wu_ref[...])
    acc_scr[...] += _dot(a, wd_ref[...])

    @pl.when(f == pl.num_programs(1) - 1)
    def _():
        y_ref[...] = x_ref[...] + mod_ref[5:6, :] * acc_scr[...]


def _ffn(x_all, mod3, g_norm2, w_g, w_u, w_d, l, j):
    tm, tf = 1024, 512
    return pl.pallas_call(
        _ffn_kernel,
        out_shape=jax.ShapeDtypeStruct((N_TOK, D_MODEL), F32),
        grid=(N_TOK // tm, D_FF // tf),
        in_specs=[pl.BlockSpec((tm, D_MODEL), lambda i, f: (i, 0)),
                  pl.BlockSpec((None, 6, D_MODEL), lambda i, f: (_mod_row(i * tm), 0, 0)),
                  pl.BlockSpec((None, 1, D_MODEL), lambda i, f: (l, 0, 0)),
                  pl.BlockSpec((None, D_MODEL, tf), lambda i, f: (j, 0, f)),
                  pl.BlockSpec((None, D_MODEL, tf), lambda i, f: (j, 0, f)),
                  pl.BlockSpec((None, tf, D_MODEL), lambda i, f: (j, f, 0))],
        out_specs=pl.BlockSpec((tm, D_MODEL), lambda i, f: (i, 0)),
        scratch_shapes=[pltpu.VMEM((tm, D_MODEL), BF16), pltpu.VMEM((tm, D_MODEL), F32)],
        compiler_params=_cparams(("arbitrary", "arbitrary")),
        name="ffn_dense",
    )(x_all, mod3, g_norm2.reshape(DEPTH, 1, D_MODEL), w_g, w_u, w_d)


ROUTER_LANES = 128
ROUTE_TM = 512
MOE_TM = 1024
MOE_SUB = 256
MOE_TF = 512
RUN_ALIGN = 8
MOE_TILES = -(-(2 * N_TOK + (N_TOK // ROUTE_TM) * N_EXPERTS * RUN_ALIGN + N_EXPERTS * MOE_TM) // MOE_TM)
MOE_ROWS = MOE_TILES * MOE_TM
R_I1, R_I2, R_W1, R_W2, R_RANK1, R_RANK2 = range(6)


def _route_top2(h, wr):
    h_hi = h.astype(BF16)
    h_lo = (h - h_hi.astype(F32)).astype(BF16)
    w_hi = wr.astype(BF16)
    w_lo = (wr - w_hi.astype(F32)).astype(BF16)
    logits = (jnp.dot(h_hi, w_hi, preferred_element_type=F32) + jnp.dot(h_lo, w_hi, preferred_element_type=F32)
              + jnp.dot(h_hi, w_lo, preferred_element_type=F32))
    lane = lax.broadcasted_iota(jnp.int32, logits.shape, 1).astype(F32)
    neg = jnp.float32(-jnp.inf)
    logits = jnp.where(lane < N_EXPERTS, logits, neg)
    m1 = jnp.max(logits, axis=-1, keepdims=True)
    i1 = jnp.min(jnp.where(logits == m1, lane, float(ROUTER_LANES)), axis=-1, keepdims=True)
    rest = jnp.where(lane == i1, neg, logits)
    m2 = jnp.max(rest, axis=-1, keepdims=True)
    i2 = jnp.min(jnp.where(rest == m2, lane, float(ROUTER_LANES)), axis=-1, keepdims=True)
    e2 = jnp.exp(m2 - m1)
    return lane, i1, i2, 1.0 / (1.0 + e2), e2 / (1.0 + e2)


def _route_kernel(x_ref, mod_ref, g_ref, wr_ref, route_ref, cnt_ref, tcarry_ref, carry_scr):
    @pl.when(pl.program_id(0) == 0)
    def _():
        carry_scr[...] = jnp.zeros_like(carry_scr)

    tcarry_ref[...] = carry_scr[...]

    h = _norm2(x_ref[...], g_ref[...], mod_ref)
    lane, i1, i2, w1, w2 = _route_top2(h, wr_ref[...])
    tm = h.shape[0]
    oh1 = lane == i1
    oh2 = lane == i2
    oh = jnp.where(oh1, 1.0, 0.0) + jnp.where(oh2, 1.0, 0.0)
    r = lax.broadcasted_iota(jnp.int32, (tm, tm), 0)
    c = lax.broadcasted_iota(jnp.int32, (tm, tm), 1)
    lower = jnp.where(r > c, 1.0, 0.0).astype(BF16)
    before = jnp.dot(lower, oh.astype(BF16), preferred_element_type=F32)
    rank1 = jnp.sum(jnp.where(oh1, before, 0.0), axis=-1, keepdims=True)
    rank2 = jnp.sum(jnp.where(oh2, before, 0.0), axis=-1, keepdims=True)
    carry_scr[...] += jnp.sum(oh, axis=0, keepdims=True)
    rec = jnp.zeros_like(lane)
    for k, v in ((R_I1, i1), (R_I2, i2), (R_W1, w1), (R_W2, w2), (R_RANK1, rank1), (R_RANK2, rank2)):
        rec = jnp.where(lane == float(k), v, rec)
    route_ref[...] = rec
    cnt_ref[...] = carry_scr[...]


def _moe_route(x_all, mod3, g_norm2, w_router_p, l, j):
    tm = ROUTE_TM
    return pl.pallas_call(
        _route_kernel,
        out_shape=[jax.ShapeDtypeStruct((N_TOK, ROUTER_LANES), F32),
                   jax.ShapeDtypeStruct((1, ROUTER_LANES), F32),
                   jax.ShapeDtypeStruct((N_TOK // tm, 1, ROUTER_LANES), F32)],
        grid=(N_TOK // tm,),
        in_specs=[pl.BlockSpec((tm, D_MODEL), lambda i: (i, 0)),
                  pl.BlockSpec((None, 6, D_MODEL), lambda i: (_mod_row(i * tm), 0, 0)),
                  pl.BlockSpec((None, 1, D_MODEL), lambda i: (l, 0, 0)),
                  pl.BlockSpec((None, D_MODEL, ROUTER_LANES), lambda i: (j, 0, 0))],
        out_specs=[pl.BlockSpec((tm, ROUTER_LANES), lambda i: (i, 0)),
                   pl.BlockSpec((1, ROUTER_LANES), lambda i: (0, 0)),
                   pl.BlockSpec((None, 1, ROUTER_LANES), lambda i: (i, 0, 0))],
        scratch_shapes=[pltpu.VMEM((1, ROUTER_LANES), F32)],
        compiler_params=_cparams(("arbitrary",)),
        name="moe_route",
    )(x_all, mod3, g_norm2.reshape(DEPTH, 1, D_MODEL), w_router_p)


ZERO_ROWS = 512
N_GAPS = N_EXPERTS + 1


def _bit_chunks(n_units, max_units, make, wait=False):
    for b in range(max_units.bit_length()):
        units = 1 << b

        @pl.when((n_units & units) != 0)
        def _():
            cp = make(pl.multiple_of((n_units & (units - 1)) * RUN_ALIGN, RUN_ALIGN), units * RUN_ALIGN)
            if wait:
                cp.wait()
            else:
                cp.start(priority=b % 2)


def _dispatch_kernel(n_ref, lo_ref, dst_ref, gap0_ref, gapn_ref, x_ref, mod_ref, g_ref, route_ref, lovec_ref,
                     xs_ref, z_scr, zero_scr, sems, zsem):
    i = pl.program_id(0)
    tm = x_ref.shape[0]
    h = _norm2(x_ref[...], g_ref[...], mod_ref).astype(BF16)
    rec = route_ref[...]
    lane = lax.broadcasted_iota(jnp.int32, rec.shape, 1).astype(F32)
    lo_row = lovec_ref[...]

    def local_pos(i_lane, r_lane):
        lp = (jnp.sum(jnp.where(lane == rec[:, i_lane:i_lane + 1], lo_row, 0.0), axis=-1, keepdims=True)
              + rec[:, r_lane:r_lane + 1])
        return jnp.transpose(jnp.broadcast_to(lp, (tm, ROUTER_LANES)))[0:1, :]

    slot = lax.broadcasted_iota(jnp.int32, (z_scr.shape[1], tm), 0).astype(F32)
    perm = jnp.where((slot == local_pos(R_I1, R_RANK1)) | (slot == local_pos(R_I2, R_RANK2)), 1.0, 0.0)
    buf = i % 2
    z_scr[buf] = jnp.dot(perm.astype(BF16), h, preferred_element_type=F32)

    def run_copies(tile, b, wait):
        for e in range(N_EXPERTS):
            k = tile * N_EXPERTS + e
            src0 = lo_ref[k]
            dst0 = dst_ref[k]

            def make(off, rows, src0=src0, dst0=dst0):
                return pltpu.make_async_copy(
                    z_scr.at[b, pl.ds(pl.multiple_of(src0 + off, RUN_ALIGN), rows), :],
                    xs_ref.at[pl.ds(pl.multiple_of(dst0 + off, RUN_ALIGN), rows), :], sems.at[b])

            _bit_chunks(n_ref[k], tm // RUN_ALIGN, make, wait)

    run_copies(i, buf, False)

    @pl.when(i > 0)
    def _():
        run_copies(i - 1, 1 - buf, True)

    @pl.when(i == pl.num_programs(0) - 1)
    def _():
        run_copies(i, buf, True)

    @pl.when(i == pl.num_programs(0) - 1)
    def _():
        zero_scr[...] = jnp.zeros_like(zero_scr)
        for wait in (False, True):
            for g in range(N_GAPS):
                start = gap0_ref[g]
                n = gapn_ref[g]
                max_rows = MOE_TM if g < N_EXPERTS else MOE_ROWS - 2 * N_TOK
                for c in range(max_rows // ZERO_ROWS):
                    @pl.when(n >= (c + 1) * ZERO_ROWS)
                    def _():
                        cp = pltpu.make_async_copy(
                            zero_scr,
                            xs_ref.at[pl.ds(pl.multiple_of(start + c * ZERO_ROWS, RUN_ALIGN), ZERO_ROWS), :], zsem)
                        if wait:
                            cp.wait()
                        else:
                            cp.start()

                def make(off, rows, start=start, n=n):
                    tail0 = start + (n // ZERO_ROWS) * ZERO_ROWS
                    return pltpu.make_async_copy(
                        zero_scr.at[pl.ds(0, rows), :],
                        xs_ref.at[pl.ds(pl.multiple_of(tail0 + off, RUN_ALIGN), rows), :], zsem)

                _bit_chunks((n % ZERO_ROWS) // RUN_ALIGN, ZERO_ROWS // RUN_ALIGN - 1, make, wait)


def _moe_dispatch(n_tile, lo_tile, dst_tile, gap0, gapn, x_all, mod3, g_norm2, route, lo_vec, l):
    tm = ROUTE_TM
    spec = lambda shape, fn: pl.BlockSpec(shape, lambda i, *_: fn(i))
    return pl.pallas_call(
        _dispatch_kernel,
        out_shape=jax.ShapeDtypeStruct((MOE_ROWS, D_MODEL), F32),
        grid_spec=pltpu.PrefetchScalarGridSpec(
            num_scalar_prefetch=5,
            grid=(N_TOK // tm,),
            in_specs=[spec((tm, D_MODEL), lambda i: (i, 0)),
                      spec((None, 6, D_MODEL), lambda i: (_mod_row(i * tm), 0, 0)),
                      spec((None, 1, D_MODEL), lambda i: (l, 0, 0)),
                      spec((tm, ROUTER_LANES), lambda i: (i, 0)),
                      spec((None, 1, ROUTER_LANES), lambda i: (i, 0, 0))],
            out_specs=pl.BlockSpec(memory_space=pl.ANY),
            scratch_shapes=[pltpu.VMEM((2, 2 * tm + N_EXPERTS * RUN_ALIGN, D_MODEL), F32),
                            pltpu.VMEM((ZERO_ROWS, D_MODEL), F32),
                            pltpu.SemaphoreType.DMA((2,)), pltpu.SemaphoreType.DMA],
        ),
        compiler_params=_cparams(("arbitrary",)),
        name="moe_dispatch",
    )(n_tile, lo_tile, dst_tile, gap0, gapn, x_all, mod3, g_norm2.reshape(DEPTH, 1, D_MODEL), route, lo_vec)


def _gmm_kernel(te_ref, tr_ref, xs_ref, wg_ref, wu_ref, wd_ref, y_ref, wg_scr, wu_scr, wd_scr):
    del te_ref
    g = pl.program_id(0)
    f = pl.program_id(1)
    rows = tr_ref[g]

    @pl.when(f == 0)
    def _():
        y_ref[...] = jnp.zeros_like(y_ref)

    def sub_tile(s, wg, wu, wd):
        sl = slice(s * MOE_SUB, (s + 1) * MOE_SUB)
        x = xs_ref[sl, :].astype(BF16)
        a = (_silu(jnp.dot(x, wg, preferred_element_type=F32)) * jnp.dot(x, wu, preferred_element_type=F32))
        y_ref[sl, :] += jnp.dot(a.astype(BF16), wd, preferred_element_type=F32)

    @pl.when(rows == MOE_TM)
    def _():
        x = xs_ref[...].astype(BF16)
        a = _silu(_dot(x, wg_ref[...])) * _dot(x, wu_ref[...])
        y_ref[...] += _dot(a, wd_ref[...])

    @pl.when((rows > 0) & (rows < MOE_TM))
    def _():
        wg = wg_ref[...].astype(BF16)
        wu = wu_ref[...].astype(BF16)
        wd = wd_ref[...].astype(BF16)
        wg_scr[...] = wg
        wu_scr[...] = wu
        wd_scr[...] = wd
        sub_tile(0, wg, wu, wd)

    for s in range(1, MOE_TM // MOE_SUB):
        @pl.when((rows > s * MOE_SUB) & (rows < MOE_TM))
        def _():
            sub_tile(s, wg_scr[...], wu_scr[...], wd_scr[...])


def _moe_gmm(tile_expert, tile_rows, xs, w_g, w_u, w_d, j):
    tm, tf = MOE_TM, MOE_TF
    nf = D_FF // tf

    def f_eff(g, f, tr):
        return jnp.where(tr[g] > 0, f, nf - 1)

    return pl.pallas_call(
        _gmm_kernel,
        out_shape=jax.ShapeDtypeStruct((MOE_ROWS, D_MODEL), F32),
        grid_spec=pltpu.PrefetchScalarGridSpec(
            num_scalar_prefetch=2,
            grid=(MOE_TILES, nf),
            in_specs=[pl.BlockSpec((tm, D_MODEL), lambda g, f, te, tr: (g, 0)),
                      pl.BlockSpec((None, None, D_MODEL, tf), lambda g, f, te, tr: (j, te[g], 0, f_eff(g, f, tr))),
                      pl.BlockSpec((None, None, D_MODEL, tf), lambda g, f, te, tr: (j, te[g], 0, f_eff(g, f, tr))),
                      pl.BlockSpec((None, None, tf, D_MODEL), lambda g, f, te, tr: (j, te[g], f_eff(g, f, tr), 0))],
            out_specs=pl.BlockSpec((tm, D_MODEL), lambda g, f, te, tr: (g, 0)),
            scratch_shapes=[pltpu.VMEM((D_MODEL, tf), BF16), pltpu.VMEM((D_MODEL, tf), BF16),
                            pltpu.VMEM((tf, D_MODEL), BF16)],
        ),
        compiler_params=_cparams(("arbitrary", "arbitrary")),
        name="moe_experts",
    )(tile_expert, tile_rows, xs, w_g, w_u, w_d)


ZY_ROWS = 1152


def _combine_kernel(n_out, n_ref, lo_ref, dst_ref, x_ref, mod_ref, route_ref, lovec_ref, y_ref, *refs):
    o_refs, (zy_scr, sems) = refs[:n_out], refs[n_out:]
    i = pl.program_id(0)
    tm = x_ref.shape[0]

    def run_copies(tile, b, wait):
        for e in range(N_EXPERTS):
            k = tile * N_EXPERTS + e
            src0 = dst_ref[k]
            dst0 = lo_ref[k]

            def make(off, rows, src0=src0, dst0=dst0):
                return pltpu.make_async_copy(
                    y_ref.at[pl.ds(pl.multiple_of(src0 + off, RUN_ALIGN), rows), :],
                    zy_scr.at[b, pl.ds(pl.multiple_of(dst0 + off, RUN_ALIGN), rows), :], sems.at[b])

            _bit_chunks(n_ref[k], tm // RUN_ALIGN, make, wait)

    @pl.when(i == 0)
    def _():
        zy_scr[...] = jnp.zeros_like(zy_scr)
        run_copies(0, 0, False)

    buf = i % 2

    @pl.when(i + 1 < pl.num_programs(0))
    def _():
        run_copies(i + 1, 1 - buf, False)

    run_copies(i, buf, True)

    rec = route_ref[...]
    lane = lax.broadcasted_iota(jnp.int32, rec.shape, 1).astype(F32)
    lo_row = lovec_ref[...]

    def local_pos(i_lane, r_lane):
        return (jnp.sum(jnp.where(lane == rec[:, i_lane:i_lane + 1], lo_row, 0.0), axis=-1, keepdims=True)
                + rec[:, r_lane:r_lane + 1])

    slot = lax.broadcasted_iota(jnp.int32, (tm, ZY_ROWS), 1).astype(F32)
    wsel = (jnp.where(slot == local_pos(R_I1, R_RANK1), rec[:, R_W1:R_W1 + 1], 0.0)
            + jnp.where(slot == local_pos(R_I2, R_RANK2), rec[:, R_W2:R_W2 + 1], 0.0))
    w_hi = wsel.astype(BF16)
    w_lo = (wsel - w_hi.astype(F32)).astype(BF16)
    z = zy_scr[buf].astype(BF16)
    mix = jnp.dot(w_hi, z, preferred_element_type=F32) + jnp.dot(w_lo, z, preferred_element_type=F32)
    _x_store(o_refs, x_ref[...] + mod_ref[5:6, :] * mix)


def _moe_combine(n_tile, lo_tile, dst_tile, x_all, mod3, route, lo_vec, y, split_out):
    tm = ROUTE_TM
    spec = lambda shape, fn: pl.BlockSpec(shape, lambda i, *_: fn(i))
    return pl.pallas_call(
        functools.partial(_combine_kernel, 2 if split_out else 1),
        out_shape=_x_shapes(split_out),
        grid_spec=pltpu.PrefetchScalarGridSpec(
            num_scalar_prefetch=3,
            grid=(N_TOK // tm,),
            in_specs=[spec((tm, D_MODEL), lambda i: (i, 0)),
                      spec((None, 6, D_MODEL), lambda i: (_mod_row(i * tm), 0, 0)),
                      spec((tm, ROUTER_LANES), lambda i: (i, 0)),
                      spec((None, 1, ROUTER_LANES), lambda i: (i, 0, 0)),
                      pl.BlockSpec(memory_space=pl.ANY)],
            out_specs=_x_specs((None,) * (2 if split_out else 1), tm),
            scratch_shapes=[pltpu.VMEM((2, ZY_ROWS, D_MODEL), F32), pltpu.SemaphoreType.DMA((2,))],
        ),
        compiler_params=_cparams(("arbitrary",)),
        name="moe_combine",
    )(n_tile, lo_tile, dst_tile, x_all, mod3, route, lo_vec, y)


def _moe(x_all, mod3, g_norm2, w_router_p, w_g, w_u, w_d, l, j, split_out):
    route, cnt, tcarry = _moe_route(x_all, mod3, g_norm2, w_router_p, l, j)
    cnt = cnt[0, :N_EXPERTS].astype(jnp.int32)
    carry = tcarry[:, 0, :N_EXPERTS].astype(jnp.int32)
    n_tile = jnp.concatenate([carry[1:], cnt[None, :]], axis=0) - carry
    n_tile = (n_tile + RUN_ALIGN - 1) // RUN_ALIGN * RUN_ALIGN
    lo_tile = jnp.cumsum(n_tile, axis=1) - n_tile
    carry = jnp.cumsum(n_tile, axis=0) - n_tile
    cnt = jnp.sum(n_tile, axis=0)
    padded = (cnt + MOE_TM - 1) // MOE_TM * MOE_TM
    ends = jnp.cumsum(padded)
    offs = ends - padded
    dst_tile = offs[None, :] + carry
    experts = jnp.arange(N_EXPERTS, dtype=jnp.int32)
    gap0 = jnp.concatenate([offs + cnt, ends[-1:]])
    gapn = jnp.concatenate([padded - cnt, MOE_ROWS - ends[-1:]])
    lo_vec = jnp.pad(lo_tile.astype(F32), ((0, 0), (0, ROUTER_LANES - N_EXPERTS)))[:, None, :]

    tile_start = jnp.arange(MOE_TILES, dtype=jnp.int32) * MOE_TM
    last_tile = jnp.maximum(ends[-1] - MOE_TM, 0)
    owner_start = jnp.minimum(tile_start, last_tile)
    tile_expert = jnp.minimum(jnp.sum(owner_start[:, None] >= ends[None, :], axis=1), N_EXPERTS - 1).astype(jnp.int32)
    group_end = jnp.sum(jnp.where(tile_expert[:, None] == experts[None, :], (offs + cnt)[None, :], 0), axis=1)
    tile_rows = jnp.where(tile_start < ends[-1], jnp.clip(group_end - tile_start, 0, MOE_TM), 0).astype(jnp.int32)

    i32 = lambda a: a.reshape(-1).astype(jnp.int32)
    runs = (i32(n_tile // RUN_ALIGN), i32(lo_tile), i32(dst_tile))
    xs = _moe_dispatch(*runs, i32(gap0), i32(gapn), x_all, mod3, g_norm2, route, lo_vec, l)
    y = _moe_gmm(tile_expert, tile_rows, xs, w_g, w_u, w_d, j)
    return tuple(_moe_combine(*runs, x_all, mod3, route, lo_vec, y, split_out))


def _pad_heads(w, per_head):
    lead = w.shape[:-1]
    w = w.reshape(lead + (MLA_HEADS, per_head))
    w = jnp.pad(w, [(0, 0)] * len(lead) + [(0, 0), (0, HEAD_PAD - per_head)])
    return w.reshape(lead + (MLA_PAD_W,))


W_IN_HEAD = MLA_Q_LORA + MLA_KV_LORA + MLA_ROPE
W_IN_COLS = W_IN_HEAD + 8 * BR_W + N_BRANCH * D_MODEL


def _w_in_layout_kernel(w_ref, o_ref):
    w = w_ref[...]
    o_ref[:, :W_IN_HEAD] = w[:, :W_IN_HEAD].astype(BF16)
    o_ref[:, W_IN_HEAD:P_SC] = jnp.zeros((w.shape[0], P_SC - W_IN_HEAD), BF16)
    o_ref[:, P_SC:] = w[:, W_IN_HEAD:].astype(BF16)


def _w_in_layout(w_in, l):
    tk = 128
    return pl.pallas_call(
        _w_in_layout_kernel,
        out_shape=jax.ShapeDtypeStruct((D_MODEL, P_COLS), BF16),
        grid=(D_MODEL // tk,),
        in_specs=[pl.BlockSpec((None, tk, W_IN_COLS), lambda k: (l, k, 0))],
        out_specs=pl.BlockSpec((tk, P_COLS), lambda k: (k, 0)),
        compiler_params=_cparams(("arbitrary",)),
        name="w_in_layout",
    )(w_in)


def _layer_weights(l, w_in, g_qa, w_uq, g_kva, w_ukv, g_mla_q, g_mla_k, sc_w, g_na_q, g_na_k,
                   cf_w, cf_b, cf_ln_g, cf_ln_b, w_br, w_o):
    w_in_p = _w_in_layout(w_in, l)
    ukv = w_ukv[l].reshape(MLA_KV_LORA, MLA_HEADS, MLA_NOPE + MLA_V)
    W = dict(
        w_in_p=w_in_p,
        g_qa=g_qa[l][None], g_kva=g_kva[l][None],
        w_uq_p=_pad_heads(w_uq[l], MLA_QK).astype(BF16),
        w_ukv_k=_pad_heads(ukv[:, :, :MLA_NOPE].reshape(MLA_KV_LORA, -1), MLA_NOPE).astype(BF16),
        w_ukv_v=_pad_heads(ukv[:, :, MLA_NOPE:].reshape(MLA_KV_LORA, -1), MLA_V).astype(BF16),
        g_mla_q_p=jnp.pad(g_mla_q[l], (0, HEAD_PAD - MLA_QK))[None],
        g_mla_k_p=jnp.pad(g_mla_k[l], (0, HEAD_PAD - MLA_QK))[None],
        g_na_q_t=jnp.tile(g_na_q[l], NA_HEADS)[None], g_na_k_t=jnp.tile(g_na_k[l], NA_HEADS)[None],
        sc_w=sc_w[l], cf_w=cf_w[l], cf_b=cf_b[l][None], cf_ln_g=cf_ln_g[l][None], cf_ln_b=cf_ln_b[l][None],
        w_br_a=jnp.pad(w_br[l, 0].reshape(MLA_HEADS, MLA_V, D_MODEL),
                       ((0, 0), (0, HEAD_PAD - MLA_V), (0, 0))).reshape(MLA_PAD_W, D_MODEL).astype(BF16),
        w_br_bcd=w_br[l, 1:].astype(BF16),
        w_o=w_o[l].astype(BF16),
    )
    return W


def kernel(x_prompt, x_sample, cache_mla_ckv, cache_mla_kpe, cache_na_k, cache_na_v, c, c_ctx, w_ada, b_ada, g_norm1, w_in, g_qa, w_uq, g_kva, w_ukv, g_mla_q, g_mla_k, sc_w, g_na_q, g_na_k, na_rpb, cf_w, cf_b, cf_ln_g, cf_ln_b, w_br, w_o, g_norm2, w_ff_gate, w_ff_up, w_ff_down, w_router, w_e_gate, w_e_up, w_e_down):
    x = (x_prompt.reshape(N_CTX_TOK, D_MODEL), x_sample.reshape(N_LAT_TOK, D_MODEL))
    cvec = jnp.concatenate([c_ctx[None, :], c, jnp.zeros((MOD_ROWS - 1 - DEC_BATCH, D_MODEL), F32)], axis=0)
    cache_na_k2 = cache_na_k.reshape(DEC_BATCH, DEPTH, PAST_LEN, BR_W)
    cache_na_v2 = cache_na_v.reshape(DEC_BATCH, DEPTH, PAST_LEN, BR_W)
    w_router_p = jnp.pad(w_router, ((0, 0), (0, 0), (0, ROUTER_LANES - N_EXPERTS)))

    ckv_l, kpe_l, nak_l, nav_l = [], [], [], []
    for l in range(DEPTH):
        W = _layer_weights(l, w_in, g_qa, w_uq, g_kva, w_ukv, g_mla_q, g_mla_k, sc_w, g_na_q, g_na_k,
                           cf_w, cf_b, cf_ln_g, cf_ln_b, w_br, w_o)
        mod3 = _modulation(cvec, w_ada, b_ada, l).reshape(MOD_ROWS, 6, D_MODEL)
        p = _in_projection(x, mod3, g_norm1, W['w_in_p'], l)

        oa_c, ckv_new, kpe_new = _mla(p, l, False, None, None, W)
        kpe_g = jnp.pad(cache_mla_kpe[:, l], ((0, 0), (0, 0), (MLA_NOPE, HEAD_PAD - MLA_QK)))
        (oa_l,) = _mla(p, l, True, cache_mla_ckv, kpe_g, W)
        oc_c, nak_new, nav_new = _na_context(p, W)
        oc_l = _na_latent(p, l, cache_na_k2, cache_na_v2, _na_bias_table(na_rpb[l]), W)
        ob_c, od_c = _convs(p, False, W)
        ob_l, od_l = _convs(p, True, W)

        x_all = _merge(x, mod3, p, (oa_c, oa_l), (ob_c, ob_l), (oc_c, oc_l), (od_c, od_l), W)

        last = l == DEPTH - 1
        if l % 2 == 0:
            x = (_ffn(x_all, mod3, g_norm2, w_ff_gate, w_ff_up, w_ff_down, l, l // 2),)
        else:
            x = _moe(x_all, mod3, g_norm2, w_router_p, w_e_gate, w_e_up, w_e_down, l, l // 2, last)

        ckv_l.append(ckv_new)
        kpe_l.append(kpe_new)
        nak_l.append(nak_new.reshape(BATCH, SEQ, NA_HEADS, NA_HD))
        nav_l.append(nav_new.reshape(BATCH, SEQ, NA_HEADS, NA_HD))

    if len(x) == 1:
        x = (x[0][:N_CTX_TOK], x[0][N_CTX_TOK:])
    y_prompt = x[0].reshape(BATCH, SEQ, D_MODEL)
    y_sample = x[1].reshape(DEC_BATCH, DEC_SEQ, D_MODEL)
    return (y_prompt, y_sample, jnp.stack(ckv_l, axis=1), jnp.stack(kpe_l, axis=1),
            jnp.stack(nak_l, axis=1), jnp.stack(nav_l, axis=1))
```
